```python
import numpy as np
import jax
import jax.numpy as jnp
from jax import lax

D_MODEL = 2048
BATCH = 4
SEQ = 2048
DEPTH = 2

HEAD_DIM = 128
ROPE_THETA = 10000.0
NORM_EPS = 1e-6
Q_CHUNK = 64
DENSE_Q_BLOCK = 128
NEG_BIG = -1e30
TINY = 1e-30
D_FF = 5632
N_BRANCHES = 3

MOBA_HEADS = 4
MOBA_BLOCK = 256
MOBA_TOPK = 3

MLA_HEADS = 6
MLA_Q_RANK = 512
MLA_KV_RANK = 256
MLA_NOPE = 128
MLA_ROPE = 64
MLA_V = 128

NSA_HEADS = 6
NSA_KV_GROUPS = 2
NSA_CMP_LEN = 32
NSA_CMP_STRIDE = 16
NSA_CMP_HIDDEN = 256
NSA_SEL_BLOCK = 64
NSA_SEL_TOPK = 16
NSA_WINDOW = 512
FORCE_SCORE = 1e9

MOBA_W = MOBA_HEADS * HEAD_DIM
NSA_Q_W = NSA_HEADS * HEAD_DIM
NSA_KV_W = NSA_KV_GROUPS * HEAD_DIM
MLA_QH_W = MLA_HEADS * (MLA_NOPE + MLA_ROPE)
MLA_KVH_W = MLA_HEADS * (MLA_NOPE + MLA_V)

IN_SPLITS = (
    MOBA_W, MOBA_W, MOBA_W,
    MLA_Q_RANK, MLA_KV_RANK, MLA_ROPE,
    NSA_Q_W,
    NSA_KV_W, NSA_KV_W,
    NSA_KV_W, NSA_KV_W,
    NSA_KV_W, NSA_KV_W,
    NSA_HEADS * 3,
    N_BRANCHES * D_MODEL,
)
IN_COLS = int(sum(IN_SPLITS))
IN_SPLIT_POINTS = [int(v) for v in np.cumsum(IN_SPLITS)[:-1]]

kernel_name = "hybrid_moba_mla_nsa_macaron"


def rmsnorm(x, g):
    xf = x.astype(jnp.float32)
    y = xf * lax.rsqrt(jnp.mean(xf * xf, axis=-1, keepdims=True) + NORM_EPS)
    return (y * g.astype(jnp.float32)).astype(x.dtype)


def swiglu(h, w_gate, w_up, w_down):
    return (jax.nn.silu(h @ w_gate) * (h @ w_up)) @ w_down


def rope_tables(pos, dim):
    inv = 1.0 / (ROPE_THETA ** (jnp.arange(0, dim, 2, dtype=jnp.float32) / dim))
    ang = pos.astype(jnp.float32)[:, None] * inv[None, :]
    ang = jnp.concatenate([ang, ang], axis=-1)
    return jnp.cos(ang), jnp.sin(ang)


def apply_rope(x, cos, sin):
    xf = x.astype(jnp.float32)
    x1, x2 = jnp.split(xf, 2, axis=-1)
    rot = jnp.concatenate([-x2, x1], axis=-1)
    return (xf * cos + rot * sin).astype(x.dtype)


def masked_softmax(s, mask, scale):
    s = jnp.where(mask, s.astype(jnp.float32) * scale, NEG_BIG)
    s = s - jnp.max(s, axis=-1, keepdims=True)
    e = jnp.where(mask, jnp.exp(s), 0.0)
    return e / jnp.maximum(jnp.sum(e, axis=-1, keepdims=True), TINY)


def to_heads(t, n_heads):
    b, s, _ = t.shape
    return t.reshape(b, s, n_heads, -1).transpose(0, 2, 1, 3)


def moba_attention(q, k, v, cos, sin):
    B, H, S, dh = q.shape
    q = apply_rope(q, cos, sin)
    k = apply_rope(k, cos, sin)
    n_blk = -(-S // MOBA_BLOCK)
    pad = n_blk * MOBA_BLOCK - S
    k_pad = jnp.pad(k, ((0, 0), (0, 0), (0, pad), (0, 0)))
    v_pad = jnp.pad(v, ((0, 0), (0, 0), (0, pad), (0, 0)))
    k_blk = k_pad.reshape(B, H, n_blk, MOBA_BLOCK, dh)
    v_blk = v_pad.reshape(B, H, n_blk, MOBA_BLOCK, dh)
    k_mean = jnp.mean(k_blk.astype(jnp.float32), axis=3)
    top = min(MOBA_TOPK, n_blk)
    scale = dh ** -0.5
    b_idx = jnp.arange(B)[:, None, None, None]
    h_idx = jnp.arange(H)[None, :, None, None]
    blk_ids = jnp.arange(n_blk)

    def chunk(c):
        start = c * Q_CHUNK
        t = start + jnp.arange(Q_CHUNK)
        own = start // MOBA_BLOCK
        qc = lax.dynamic_slice_in_dim(q, start, Q_CHUNK, axis=2)
        gate = jnp.einsum('bhqd,bhnd->bhqn', qc.astype(jnp.float32), k_mean)
        gate = jnp.where(blk_ids < own, gate, -jnp.inf)
        _, idx = lax.top_k(gate, top)
        sel_ok = idx < own
        k_sel = k_blk[b_idx, h_idx, idx]
        v_sel = v_blk[b_idx, h_idx, idx]
        s_sel = jnp.einsum('bhqd,bhqnkd->bhqnk', qc, k_sel).reshape(B, H, Q_CHUNK, top * MOBA_BLOCK)
        m_sel = jnp.broadcast_to(sel_ok[..., None], (B, H, Q_CHUNK, top, MOBA_BLOCK)).reshape(
            B, H, Q_CHUNK, top * MOBA_BLOCK)
        k_own = lax.dynamic_slice_in_dim(k_pad, own * MOBA_BLOCK, MOBA_BLOCK, axis=2)
        v_own = lax.dynamic_slice_in_dim(v_pad, own * MOBA_BLOCK, MOBA_BLOCK, axis=2)
        s_own = jnp.einsum('bhqd,bhkd->bhqk', qc, k_own)
        own_pos = own * MOBA_BLOCK + jnp.arange(MOBA_BLOCK)
        m_own = jnp.broadcast_to(own_pos[None, :] <= t[:, None], (B, H, Q_CHUNK, MOBA_BLOCK))
        p = masked_softmax(jnp.concatenate([s_sel, s_own], axis=-1),
                           jnp.concatenate([m_sel, m_own], axis=-1), scale).astype(v.dtype)
        p_sel = p[..., :top * MOBA_BLOCK].reshape(B, H, Q_CHUNK, top, MOBA_BLOCK)
        p_own = p[..., top * MOBA_BLOCK:]
        return (jnp.einsum('bhqnk,bhqnkd->bhqd', p_sel, v_sel)
                + jnp.einsum('bhqk,bhkd->bhqd', p_own, v_own))

    out = lax.map(chunk, jnp.arange(S // Q_CHUNK))
    return out.transpose(1, 0, 3, 2, 4).reshape(B, S, H * dh)


def mla_attention(c_q, c_kv, k_rope, q_norm, w_uq, kv_norm, w_ukv, cos, sin):
    B, S, _ = c_q.shape
    q = (rmsnorm(c_q, q_norm) @ w_uq).reshape(B, S, MLA_HEADS, MLA_NOPE + MLA_ROPE).transpose(0, 2, 1, 3)
    q_nope = q[..., :MLA_NOPE]
    q_rope = apply_rope(q[..., MLA_NOPE:], cos, sin)
    kv = (rmsnorm(c_kv, kv_norm) @ w_ukv).reshape(B, S, MLA_HEADS, MLA_NOPE + MLA_V).transpose(0, 2, 1, 3)
    k_nope = kv[..., :MLA_NOPE]
    v = kv[..., MLA_NOPE:]
    k_rope = apply_rope(k_rope, cos, sin)
    scale = (MLA_NOPE + MLA_ROPE) ** -0.5
    k_pos = jnp.arange(S)

    def block(c):
        start = c * DENSE_Q_BLOCK
        t = start + jnp.arange(DENSE_Q_BLOCK)
        qn = lax.dynamic_slice_in_dim(q_nope, start, DENSE_Q_BLOCK, axis=2)
        qr = lax.dynamic_slice_in_dim(q_rope, start, DENSE_Q_BLOCK, axis=2)
        s = jnp.einsum('bhqd,bhkd->bhqk', qn, k_nope) + jnp.einsum('bhqd,bkd->bhqk', qr, k_rope)
        p = masked_softmax(s, k_pos[None, :] <= t[:, None], scale).astype(v.dtype)
        return jnp.einsum('bhqk,bhkd->bhqd', p, v)

    out = lax.map(block, jnp.arange(S // DENSE_Q_BLOCK))
    return out.transpose(1, 0, 3, 2, 4).reshape(B, S, MLA_HEADS * MLA_V)


def nsa_compress(blocks, pos, w1, w2):
    B, G, N, L, dh = blocks.shape
    flat = (blocks + pos).reshape(B, G, N, L * dh)
    return jax.nn.gelu(flat @ w1) @ w2


def nsa_attention(q, kc, vc, ks, vs, kw, vw, gate_logits,
                  pos_k, w1_k, w2_k, pos_v, w1_v, w2_v, cos, sin):
    B, S, _ = q.shape
    G, R, dh = NSA_KV_GROUPS, NSA_HEADS // NSA_KV_GROUPS, HEAD_DIM
    q = apply_rope(q.reshape(B, S, G, R, dh).transpose(0, 2, 3, 1, 4), cos, sin)

    def kv_heads(t):
        return t.reshape(B, S, G, dh).transpose(0, 2, 1, 3)

    kc, vc, vs, vw = kv_heads(kc), kv_heads(vc), kv_heads(vs), kv_heads(vw)
    ks = apply_rope(kv_heads(ks), cos, sin)
    kw = apply_rope(kv_heads(kw), cos, sin)

    n_cmp = (S - NSA_CMP_LEN) // NSA_CMP_STRIDE + 1
    cmp_idx = np.arange(n_cmp)[:, None] * NSA_CMP_STRIDE + np.arange(NSA_CMP_LEN)[None, :]
    cmp_end = jnp.asarray(cmp_idx[:, -1])
    cos_c, sin_c = rope_tables(cmp_end, dh)
    k_cmp = apply_rope(nsa_compress(kc[:, :, cmp_idx], pos_k, w1_k, w2_k), cos_c, sin_c)
    v_cmp = nsa_compress(vc[:, :, cmp_idx], pos_v, w1_v, w2_v)

    n_sel = S // NSA_SEL_BLOCK
    top = min(NSA_SEL_TOPK, n_sel)
    sel_start = np.arange(n_sel) * NSA_SEL_BLOCK
    overlap = jnp.asarray(((cmp_idx[:, 0][:, None] <= sel_start[None, :] + NSA_SEL_BLOCK - 1)
                           & (cmp_idx[:, -1][:, None] >= sel_start[None, :])).astype(np.float32))
    ks_blk = ks.reshape(B, G, n_sel, NSA_SEL_BLOCK, dh)
    vs_blk = vs.reshape(B, G, n_sel, NSA_SEL_BLOCK, dh)

    kw_pad = jnp.pad(kw, ((0, 0), (0, 0), (NSA_WINDOW, 0), (0, 0)))
    vw_pad = jnp.pad(vw, ((0, 0), (0, 0), (NSA_WINDOW, 0), (0, 0)))

    gates = jax.nn.sigmoid(gate_logits.reshape(B, S, G, R, 3).transpose(0, 2, 3, 1, 4))
    scale = dh ** -0.5
    b_idx = jnp.arange(B)[:, None, None, None]
    g_idx = jnp.arange(G)[None, :, None, None]
    blk_ids = jnp.arange(n_sel)
    win_len = NSA_WINDOW + Q_CHUNK

    def chunk(c):
        start = c * Q_CHUNK
        t = start + jnp.arange(Q_CHUNK)
        qc = lax.dynamic_slice_in_dim(q, start, Q_CHUNK, axis=3)
        s_c = jnp.einsum('bgrqd,bgnd->bgrqn', qc, k_cmp)
        p_c = masked_softmax(s_c, cmp_end[None, :] <= t[:, None], scale)
        o_c = jnp.einsum('bgrqn,bgnd->bgrqd', p_c.astype(v_cmp.dtype), v_cmp)
        imp = jnp.einsum('bgrqn,nj->bgqj', p_c, overlap)
        cur = t // NSA_SEL_BLOCK
        forced = ((blk_ids[None, :] == 0) | (blk_ids[None, :] == cur[:, None])
                  | (blk_ids[None, :] == cur[:, None] - 1))
        imp = jnp.where(forced, FORCE_SCORE, imp)
        imp = jnp.where(blk_ids[None, :] <= cur[:, None], imp, -jnp.inf)
        _, idx = lax.top_k(imp, top)
        k_sel = ks_blk[b_idx, g_idx, idx]
        v_sel = vs_blk[b_idx, g_idx, idx].reshape(B, G, Q_CHUNK, top * NSA_SEL_BLOCK, dh)
        tok = (idx[..., None] * NSA_SEL_BLOCK + jnp.arange(NSA_SEL_BLOCK)).reshape(
            B, G, Q_CHUNK, top * NSA_SEL_BLOCK)
        m_s = (tok <= t[None, None, :, None])[:, :, None]
        s_s = jnp.einsum('bgrqd,bgqnkd->bgrqnk', qc, k_sel).reshape(B, G, R, Q_CHUNK, top * NSA_SEL_BLOCK)
        p_s = masked_softmax(s_s, m_s, scale)
        o_s = jnp.einsum('bgrqm,bgqmd->bgrqd', p_s.astype(v_sel.dtype), v_sel)
        k_win = lax.dynamic_slice_in_dim(kw_pad, start, win_len, axis=2)
        v_win = lax.dynamic_slice_in_dim(vw_pad, start, win_len, axis=2)
        w_pos = start - NSA_WINDOW + jnp.arange(win_len)
        m_w = ((w_pos[None, :] <= t[:, None]) & (w_pos[None, :] > t[:, None] - NSA_WINDOW)
               & (w_pos[None, :] >= 0))
        s_w = jnp.einsum('bgrqd,bgkd->bgrqk', qc, k_win)
        p_w = masked_softmax(s_w, m_w, scale)
        o_w = jnp.einsum('bgrqk,bgkd->bgrqd', p_w.astype(v_win.dtype), v_win)
        g = lax.dynamic_slice_in_dim(gates, start, Q_CHUNK, axis=3)
        return g[..., 0:1] * o_c + g[..., 1:2] * o_s + g[..., 2:3] * o_w

    out = lax.map(chunk, jnp.arange(S // Q_CHUNK))
    return out.transpose(1, 0, 4, 2, 3, 5).reshape(B, S, NSA_HEADS * dh)


def setup_inputs(seed: int = 0) -> dict:
    key = jax.random.key(seed)
    ks = jax.random.split(key, 26)

    def normal(k, shape, scale):
        return jax.random.normal(k, shape, jnp.float32) * scale

    def dense(k, fan_in, fan_out):
        return normal(k, (DEPTH, fan_in, fan_out), fan_in ** -0.5)

    def gain(k, shape):
        return 1.0 + normal(k, shape, 0.02)

    cmp_in = NSA_CMP_LEN * HEAD_DIM
    return {
        "x": normal(ks[0], (BATCH, SEQ, D_MODEL), 1.0),
        "ffn1_norm": gain(ks[1], (DEPTH, D_MODEL)),
        "ffn1_w_gate": dense(ks[2], D_MODEL, D_FF),
        "ffn1_w_up": dense(ks[3], D_MODEL, D_FF),
        "ffn1_w_down": dense(ks[4], D_FF, D_MODEL),
        "mix_norm": gain(ks[5], (DEPTH, D_MODEL)),
        "w_in": dense(ks[6], D_MODEL, IN_COLS),
        "mla_q_norm": gain(ks[7], (DEPTH, MLA_Q_RANK)),
        "mla_w_uq": dense(ks[8], MLA_Q_RANK, MLA_QH_W),
        "mla_kv_norm": gain(ks[9], (DEPTH, MLA_KV_RANK)),
        "mla_w_ukv": dense(ks[10], MLA_KV_RANK, MLA_KVH_W),
        "nsa_cmp_pos_k": normal(ks[11], (DEPTH, NSA_CMP_LEN, HEAD_DIM), 0.1),
        "nsa_cmp_w1_k": dense(ks[12], cmp_in, NSA_CMP_HIDDEN),
        "nsa_cmp_w2_k": dense(ks[13], NSA_CMP_HIDDEN, HEAD_DIM),
        "nsa_cmp_pos_v": normal(ks[14], (DEPTH, NSA_CMP_LEN, HEAD_DIM), 0.1),
        "nsa_cmp_w1_v": dense(ks[15], cmp_in, NSA_CMP_HIDDEN),
        "nsa_cmp_w2_v": dense(ks[16], NSA_CMP_HIDDEN, HEAD_DIM),
        "w_branch_moba": dense(ks[17], MOBA_W, D_MODEL),
        "w_branch_mla": dense(ks[18], MLA_HEADS * MLA_V, D_MODEL),
        "w_branch_nsa": dense(ks[19], NSA_Q_W, D_MODEL),
        "w_out": dense(ks[20], D_MODEL, D_MODEL),
        "ffn2_norm": gain(ks[21], (DEPTH, D_MODEL)),
        "ffn2_w_gate": dense(ks[22], D_MODEL, D_FF),
        "ffn2_w_up": dense(ks[23], D_MODEL, D_FF),
        "ffn2_w_down": dense(ks[24], D_FF, D_MODEL),
        "final_norm": gain(ks[25], (D_MODEL,)),
    }


def reference(x, ffn1_norm, ffn1_w_gate, ffn1_w_up, ffn1_w_down, mix_norm, w_in,
              mla_q_norm, mla_w_uq, mla_kv_norm, mla_w_ukv,
              nsa_cmp_pos_k, nsa_cmp_w1_k, nsa_cmp_w2_k,
              nsa_cmp_pos_v, nsa_cmp_w1_v, nsa_cmp_w2_v,
              w_branch_moba, w_branch_mla, w_branch_nsa, w_out,
              ffn2_norm, ffn2_w_gate, ffn2_w_up, ffn2_w_down, final_norm):
    B, S, D = x.shape
    pos = jnp.arange(S)
    cos_h, sin_h = rope_tables(pos, HEAD_DIM)
    cos_r, sin_r = rope_tables(pos, MLA_ROPE)

    for i in range(DEPTH):
        h = rmsnorm(x, ffn1_norm[i])
        x = x + 0.5 * swiglu(h, ffn1_w_gate[i], ffn1_w_up[i], ffn1_w_down[i])

        h = rmsnorm(x, mix_norm[i])
        z = h @ w_in[i]
        (a_q, a_k, a_v, b_cq, b_ckv, b_kr, c_q, c_kc, c_vc, c_ks, c_vs, c_kw, c_vw,
         c_gate, merge_logits) = jnp.split(z, IN_SPLIT_POINTS, axis=-1)

        y_a = moba_attention(to_heads(a_q, MOBA_HEADS), to_heads(a_k, MOBA_HEADS),
                             to_heads(a_v, MOBA_HEADS), cos_h, sin_h)
        y_b = mla_attention(b_cq, b_ckv, b_kr, mla_q_norm[i], mla_w_uq[i],
                            mla_kv_norm[i], mla_w_ukv[i], cos_r, sin_r)
        y_c = nsa_attention(c_q, c_kc, c_vc, c_ks, c_vs, c_kw, c_vw, c_gate,
                            nsa_cmp_pos_k[i], nsa_cmp_w1_k[i], nsa_cmp_w2_k[i],
                            nsa_cmp_pos_v[i], nsa_cmp_w1_v[i], nsa_cmp_w2_v[i], cos_h, sin_h)

        g = jax.nn.sigmoid(merge_logits.reshape(B, S, N_BRANCHES, D))
        merged = (g[:, :, 0] * (y_a @ w_branch_moba[i])
                  + g[:, :, 1] * (y_b @ w_branch_mla[i])
                  + g[:, :, 2] * (y_c @ w_branch_nsa[i]))
        x = x + merged @ w_out[i]

        h = rmsnorm(x, ffn2_norm[i])
        x = x + 0.5 * swiglu(h, ffn2_w_gate[i], ffn2_w_up[i], ffn2_w_down[i])

    return rmsnorm(x, final_norm)
```

```python
import functools

import numpy as np
import jax
import jax.numpy as jnp
from jax import lax
from jax.experimental import pallas as pl
from jax.experimental.pallas import tpu as pltpu

F32 = jnp.float32
BF16 = jnp.bfloat16

HEAD_DIM = 128
ROPE_THETA = 10000.0
NORM_EPS = 1e-6
NEG_BIG = -1e30
TINY = 1e-30
N_BRANCHES = 3

MOBA_HEADS = 4
MOBA_BLOCK = 256
MOBA_TOPK = 3

MLA_HEADS = 6
MLA_Q_RANK = 512
MLA_KV_RANK = 256
MLA_NOPE = 128
MLA_ROPE = 64
MLA_V = 128

NSA_HEADS = 6
NSA_KV_GROUPS = 2
NSA_REP = NSA_HEADS // NSA_KV_GROUPS
NSA_CMP_LEN = 32
NSA_CMP_STRIDE = 16
NSA_CMP_HIDDEN = 256
NSA_SEL_BLOCK = 64
NSA_SEL_TOPK = 16
NSA_WINDOW = 512
FORCE_SCORE = 1e9

LANES = 128
VMEM_LIMIT = 56 * 2**20

OFF_MERGE = 0
OFF_CQ = 6144
OFF_BCKV = 6912
OFF_BCQ = 7168
OFF_AQ = 7680
OFF_AK = 8192
OFF_AV = 8704
OFF_CKC = 9216
OFF_CVC = 9472
OFF_CKS = 9728
OFF_CVS = 9984
OFF_CKW = 10240
OFF_CVW = 10496
OFF_BKR = 10752
OFF_CGATE = 10880
Z_COLS = 11264


def _cparams(sem):
    return pltpu.CompilerParams(dimension_semantics=sem, vmem_limit_bytes=VMEM_LIMIT)


def _rmsnorm(x, g):
    ms = jnp.mean(x * x, axis=-1, keepdims=True)
    return x * lax.rsqrt(ms + NORM_EPS) * g


def _rope_full(x, cos, sin_signed):
    return x * cos + pltpu.roll(x, HEAD_DIM // 2, axis=1) * sin_signed


def _rope_half(x, cos, sin_signed):
    lane = lax.broadcasted_iota(jnp.int32, x.shape, 1)
    q = MLA_ROPE // 2
    swapped = jnp.where(lane < q, pltpu.roll(x, LANES - q, axis=1), pltpu.roll(x, q, axis=1))
    return x * cos + swapped * sin_signed


def _dot_nt(a, b):
    return lax.dot_general(a, b, (((1,), (1,)), ((), ())), preferred_element_type=F32)


def _dot_nn(a, b):
    return jnp.dot(a, b, preferred_element_type=F32)


def _split(a):
    hi = a.astype(BF16)
    lo = (a - hi.astype(F32)).astype(BF16)
    return hi, lo


def _dot3_nt(a, b):
    ah, al = _split(a)
    bh, bl = _split(b)
    return _dot_nt(ah, bh) + _dot_nt(ah, bl) + _dot_nt(al, bh)


def _dot3_nn(a, b):
    ah, al = _split(a)
    bh, bl = _split(b)
    return _dot_nn(ah, bh) + _dot_nn(ah, bl) + _dot_nn(al, bh)


def _softmax_t(s, mask, scale):
    s = jnp.where(mask, s * scale, NEG_BIG)
    m = jnp.max(s, axis=0, keepdims=True)
    e = jnp.where(mask, jnp.exp(s - m), 0.0)
    l = jnp.maximum(jnp.sum(e, axis=0, keepdims=True), TINY)
    return e, l


def _rank_t(v, nb):
    blk = lax.broadcasted_iota(jnp.int32, v.shape, 0)
    rank = jnp.zeros(v.shape, F32)
    for j in range(nb):
        vj = v[j:j + 1, :]
        ge = (vj >= v).astype(F32)
        gt = (vj > v).astype(F32)
        rank = rank + jnp.where(blk > j, ge, gt)
    return rank


def _ffn_kernel(x_ref, g_ref, wg_ref, wu_ref, wd_ref, *rest, n_ff, final):
    if final:
        fg_ref, o_ref, h_ref, acc_ref = rest
    else:
        o_ref, h_ref, acc_ref = rest
    j = pl.program_id(1)

    @pl.when(j == 0)
    def _():
        h_ref[...] = _rmsnorm(x_ref[...], g_ref[...]).astype(BF16)
        acc_ref[...] = jnp.zeros_like(acc_ref)

    h = h_ref[...]
    a = _dot_nn(h, wg_ref[...])
    u = _dot_nn(h, wu_ref[...])
    act = (jax.nn.silu(a) * u).astype(BF16)
    acc_ref[...] += _dot_nn(act, wd_ref[...])

    @pl.when(j == n_ff - 1)
    def _():
        y = x_ref[...] + 0.5 * acc_ref[...]
        if final:
            y = _rmsnorm(y, fg_ref[...])
        o_ref[...] = y


def _ffn(x2d, g, wg, wu, wd, final_g=None, tm=512, tf=512):
    t, d = x2d.shape
    f = wg.shape[1]
    n_ff = f // tf
    final = final_g is not None
    in_specs = [
        pl.BlockSpec((tm, d), lambda i, j: (i, 0)),
        pl.BlockSpec((1, d), lambda i, j: (0, 0)),
        pl.BlockSpec((d, tf), lambda i, j: (0, j)),
        pl.BlockSpec((d, tf), lambda i, j: (0, j)),
        pl.BlockSpec((tf, d), lambda i, j: (j, 0)),
    ]
    args = [x2d, g.reshape(1, d), wg, wu, wd]
    if final:
        in_specs.append(pl.BlockSpec((1, d), lambda i, j: (0, 0)))
        args.append(final_g.reshape(1, d))
    return pl.pallas_call(
        functools.partial(_ffn_kernel, n_ff=n_ff, final=final),
        grid=(t // tm, n_ff),
        in_specs=in_specs,
        out_specs=pl.BlockSpec((tm, d), lambda i, j: (i, 0)),
        out_shape=jax.ShapeDtypeStruct((t, d), F32),
        scratch_shapes=[pltpu.VMEM((tm, d), BF16), pltpu.VMEM((tm, d), F32)],
        compiler_params=_cparams(("parallel", "arbitrary")),
        name="ffn",
    )(*args)


def _norm_matmul_kernel(x_ref, g_ref, w_ref, o_ref, h_ref):
    @pl.when(pl.program_id(1) == 0)
    def _():
        h_ref[...] = _rmsnorm(x_ref[...], g_ref[...]).astype(BF16)

    o_ref[...] = _dot_nn(h_ref[...], w_ref[...]).astype(o_ref.dtype)


def _norm_matmul(x2d, col_blk, g, w, tm, tn, name):
    t = x2d.shape[0]
    k, n = w.shape
    return pl.pallas_call(
        _norm_matmul_kernel,
        grid=(t // tm, n // tn),
        in_specs=[
            pl.BlockSpec((tm, k), lambda i, j: (i, col_blk)),
            pl.BlockSpec((1, k), lambda i, j: (0, 0)),
            pl.BlockSpec((k, tn), lambda i, j: (0, j)),
        ],
        out_specs=pl.BlockSpec((tm, tn), lambda i, j: (i, j)),
        out_shape=jax.ShapeDtypeStruct((t, n), F32),
        scratch_shapes=[pltpu.VMEM((tm, k), BF16)],
        compiler_params=_cparams(("parallel", "arbitrary")),
        name=name,
    )(x2d, g.reshape(1, k), w)


def _moba_kernel(q_ref, k_ref, v_ref, cosf_ref, sinf_ref, cosq_ref, sinq_ref, o_ref,
                 kr_ref, km_ref, vt_ref, *, n_blk, scale):
    qi = pl.program_id(2)
    tq = q_ref.shape[1]

    @pl.when(qi == 0)
    def _():
        kr = _rope_full(k_ref[0], cosf_ref[...], sinf_ref[...])
        kr_ref[...] = kr.astype(BF16)
        km_ref[...] = jnp.concatenate(
            [jnp.mean(kr[j * MOBA_BLOCK:(j + 1) * MOBA_BLOCK], axis=0, keepdims=True) for j in range(n_blk)],
            axis=0)
        vt_ref[...] = v_ref[0].T.astype(BF16)

    qr = _rope_full(q_ref[0], cosq_ref[...], sinq_ref[...])
    t = qi * tq + lax.broadcasted_iota(jnp.int32, (1, tq), 1)
    own = t // MOBA_BLOCK

    gate = _dot3_nt(km_ref[...], qr)
    bidx = lax.broadcasted_iota(jnp.int32, (n_blk, tq), 0)
    past = bidx < own
    gate = jnp.where(past, gate, -jnp.inf)
    sel = jnp.where(past & (_rank_t(gate, n_blk) < MOBA_TOPK), 1.0, 0.0)

    s = _dot_nt(kr_ref[...], qr.astype(BF16))
    masks = []
    for j in range(n_blk):
        kpos = j * MOBA_BLOCK + lax.broadcasted_iota(jnp.int32, (MOBA_BLOCK, tq), 0)
        masks.append((sel[j:j + 1, :] > 0.5) | ((own == j) & (kpos <= t)))
    e, l = _softmax_t(s, jnp.concatenate(masks, axis=0), scale)
    o_t = _dot_nn(vt_ref[...], e.astype(BF16)) / l
    o_ref[0] = o_t.T.astype(o_ref.dtype)


def _moba(z3, cos, sin_s, tq=256):
    b, s, _ = z3.shape
    n_blk = s // MOBA_BLOCK
    qb, kb, vb = OFF_AQ // LANES, OFF_AK // LANES, OFF_AV // LANES
    return pl.pallas_call(
        functools.partial(_moba_kernel, n_blk=n_blk, scale=HEAD_DIM ** -0.5),
        grid=(b, MOBA_HEADS, s // tq),
        in_specs=[
            pl.BlockSpec((1, tq, LANES), lambda bi, h, qi: (bi, qi, qb + h)),
            pl.BlockSpec((1, s, LANES), lambda bi, h, qi: (bi, 0, kb + h)),
            pl.BlockSpec((1, s, LANES), lambda bi, h, qi: (bi, 0, vb + h)),
            pl.BlockSpec((s, LANES), lambda bi, h, qi: (0, 0)),
            pl.BlockSpec((s, LANES), lambda bi, h, qi: (0, 0)),
            pl.BlockSpec((tq, LANES), lambda bi, h, qi: (qi, 0)),
            pl.BlockSpec((tq, LANES), lambda bi, h, qi: (qi, 0)),
        ],
        out_specs=pl.BlockSpec((1, tq, LANES), lambda bi, h, qi: (bi, qi, h)),
        out_shape=jax.ShapeDtypeStruct((b, s, MOBA_HEADS * HEAD_DIM), BF16),
        scratch_shapes=[pltpu.VMEM((s, LANES), BF16), pltpu.VMEM((n_blk, LANES), F32),
                        pltpu.VMEM((LANES, s), BF16)],
        compiler_params=_cparams(("parallel", "parallel", "arbitrary")),
        name="moba",
    )(z3, z3, z3, cos, sin_s, cos, sin_s)


def _mla_kernel(qn_ref, qr_ref, kn_ref, v_ref, kr_ref, cosf_ref, sinf_ref, cosq_ref, sinq_ref, o_ref,
                kf_ref, vt_ref, *, scale):
    qi = pl.program_id(2)
    tq = qn_ref.shape[1]
    s_len = kn_ref.shape[1]

    @pl.when(qi == 0)
    def _():
        kf_ref[:, :LANES] = kn_ref[0].astype(BF16)
        kf_ref[:, LANES:] = _rope_half(kr_ref[0], cosf_ref[...], sinf_ref[...]).astype(BF16)
        vt_ref[...] = v_ref[0].T.astype(BF16)

    q = jnp.concatenate([qn_ref[0], _rope_half(qr_ref[0], cosq_ref[...], sinq_ref[...])], axis=1).astype(BF16)
    s = _dot_nt(kf_ref[...], q)
    kpos = lax.broadcasted_iota(jnp.int32, (s_len, tq), 0)
    t = qi * tq + lax.broadcasted_iota(jnp.int32, (1, tq), 1)
    e, l = _softmax_t(s, kpos <= t, scale)
    o_t = _dot_nn(vt_ref[...], e.astype(BF16)) / l
    o_ref[0] = o_t.T.astype(o_ref.dtype)


def _mla(q3, kv3, z3, cos, sin_s, tq=256):
    b, s, _ = q3.shape
    krb = OFF_BKR // LANES
    return pl.pallas_call(
        functools.partial(_mla_kernel, scale=(MLA_NOPE + MLA_ROPE) ** -0.5),
        grid=(b, MLA_HEADS, s // tq),
        in_specs=[
            pl.BlockSpec((1, tq, LANES), lambda bi, h, qi: (bi, qi, h)),
            pl.BlockSpec((1, tq, LANES), lambda bi, h, qi: (bi, qi, MLA_HEADS + h)),
            pl.BlockSpec((1, s, LANES), lambda bi, h, qi: (bi, 0, h)),
            pl.BlockSpec((1, s, LANES), lambda bi, h, qi: (bi, 0, MLA_HEADS + h)),
            pl.BlockSpec((1, s, LANES), lambda bi, h, qi: (bi, 0, krb)),
            pl.BlockSpec((s, LANES), lambda bi, h, qi: (0, 0)),
            pl.BlockSpec((s, LANES), lambda bi, h, qi: (0, 0)),
            pl.BlockSpec((tq, LANES), lambda bi, h, qi: (qi, 0)),
            pl.BlockSpec((tq, LANES), lambda bi, h, qi: (qi, 0)),
        ],
        out_specs=pl.BlockSpec((1, tq, LANES), lambda bi, h, qi: (bi, qi, h)),
        out_shape=jax.ShapeDtypeStruct((b, s, MLA_HEADS * MLA_V), BF16),
        scratch_shapes=[pltpu.VMEM((s, 2 * LANES), BF16), pltpu.VMEM((LANES, s), BF16)],
        compiler_params=_cparams(("parallel", "parallel", "arbitrary")),
        name="mla",
    )(q3, q3, kv3, kv3, z3, cos, sin_s, cos, sin_s)


def _cmp_kernel(x_ref, w1_ref, w2_ref, pos_ref, cos_ref, sin_ref, o_ref):
    kv = pl.program_id(0)
    x = x_ref[0, 0, 0]
    half = x.shape[1]
    n16 = x.shape[0]
    w1 = w1_ref[0]
    first = _dot3_nn(x, w1[:half])
    second = _dot3_nn(x, w1[half:])
    bias = _dot3_nn(jnp.broadcast_to(pos_ref[0], (8, 2 * half)), w1)[0:1]
    hid = first + pltpu.roll(second, n16 - 1, axis=0) + bias
    y = _dot3_nn(jax.nn.gelu(hid), w2_ref[0])
    roped = _rope_full(y, cos_ref[...], sin_ref[...])
    o_ref[0, 0, 0] = jnp.where(kv == 0, roped, y)


def _nsa_compress(x16, w1, w2, posflat, cos_c, sin_c):
    _, b, g, n16, width = x16.shape
    hid = w1.shape[2]
    return pl.pallas_call(
        _cmp_kernel,
        grid=(2, b, g),
        in_specs=[
            pl.BlockSpec((1, 1, 1, n16, width), lambda kv, bi, gi: (kv, bi, gi, 0, 0)),
            pl.BlockSpec((1, 2 * width, hid), lambda kv, bi, gi: (kv, 0, 0)),
            pl.BlockSpec((1, hid, HEAD_DIM), lambda kv, bi, gi: (kv, 0, 0)),
            pl.BlockSpec((1, 1, 2 * width), lambda kv, bi, gi: (kv, 0, 0)),
            pl.BlockSpec((n16, HEAD_DIM), lambda kv, bi, gi: (0, 0)),
            pl.BlockSpec((n16, HEAD_DIM), lambda kv, bi, gi: (0, 0)),
        ],
        out_specs=pl.BlockSpec((1, 1, 1, n16, HEAD_DIM), lambda kv, bi, gi: (kv, bi, gi, 0, 0)),
        out_shape=jax.ShapeDtypeStruct((2, b, g, n16, HEAD_DIM), F32),
        compiler_params=_cparams(("parallel", "parallel", "parallel")),
        name="nsa_compress",
    )(x16, w1, w2, posflat, cos_c, sin_c)


def _nsa_kernel(q_ref, ks_ref, vs_ref, kw_ref, vw_ref, kc_ref, vc_ref, gate_ref,
                cosf_ref, sinf_ref, cosq_ref, sinq_ref, ovl_ref, exp_ref, o_ref,
                ksr_ref, vst_ref, kw3_ref, vwt3_ref, vct_ref, *, n_sel, scale):
    qi = pl.program_id(2)
    tq = q_ref.shape[1]
    s_len = ks_ref.shape[1]
    rep = NSA_REP
    lanes = rep * tq
    n_chunk = s_len // tq
    pad_chunks = NSA_WINDOW // tq
    n_cmp = kc_ref.shape[3]

    @pl.when(qi == 0)
    def _():
        ksr_ref[...] = _rope_full(ks_ref[0], cosf_ref[...], sinf_ref[...]).astype(BF16)
        vst_ref[...] = vs_ref[0].T.astype(BF16)
        kw3_ref[0:pad_chunks] = jnp.zeros((pad_chunks, tq, HEAD_DIM), BF16)
        vwt3_ref[0:pad_chunks] = jnp.zeros((pad_chunks, HEAD_DIM, tq), BF16)
        kwr = _rope_full(kw_ref[0], cosf_ref[...], sinf_ref[...]).astype(BF16)
        for c in range(n_chunk):
            kw3_ref[pad_chunks + c] = kwr[c * tq:(c + 1) * tq]
            vwt3_ref[pad_chunks + c] = vw_ref[0, c * tq:(c + 1) * tq, :].T.astype(BF16)
        vct_ref[...] = vc_ref[0, 0, 0].T.astype(BF16)

    cos_q = cosq_ref[...]
    sin_q = sinq_ref[...]
    q_all = jnp.concatenate(
        [_rope_full(q_ref[0, :, r * HEAD_DIM:(r + 1) * HEAD_DIM], cos_q, sin_q) for r in range(rep)],
        axis=0)
    q_bf = q_all.astype(BF16)
    lane = lax.broadcasted_iota(jnp.int32, (1, lanes), 1)
    t = qi * tq + (lane % tq)

    s_c = _dot3_nt(kc_ref[0, 0, 0], q_all)
    cmp_end = (lax.broadcasted_iota(jnp.int32, (n_cmp, lanes), 0) * NSA_CMP_STRIDE + (NSA_CMP_LEN - 1))
    e_c, l_c = _softmax_t(s_c, cmp_end <= t, scale)
    p_c = e_c / l_c
    o_c = _dot_nn(vct_ref[...], p_c.astype(BF16))

    p_hi, p_lo = _split(p_c)
    ovl = ovl_ref[...]
    imp3 = _dot_nn(ovl, p_hi) + _dot_nn(ovl, p_lo)
    imp = imp3[:, 0:tq]
    for r in range(1, rep):
        imp = imp + imp3[:, r * tq:(r + 1) * tq]
    tq_pos = t[:, 0:tq]
    cur = tq_pos // NSA_SEL_BLOCK
    blk = lax.broadcasted_iota(jnp.int32, (n_sel, tq), 0)
    forced = (blk == 0) | (blk == cur) | (blk == cur - 1)
    visible = blk <= cur
    imp = jnp.where(forced, FORCE_SCORE, imp)
    imp = jnp.where(visible, imp, -jnp.inf)
    sel = jnp.where(visible & (_rank_t(imp, n_sel) < NSA_SEL_TOPK), 1.0, 0.0)
    sel3 = jnp.concatenate([sel] * rep, axis=1).astype(BF16)

    s_s = _dot_nt(ksr_ref[...], q_bf)
    tok_sel = _dot_nn(exp_ref[...], sel3) > 0.5
    kpos = lax.broadcasted_iota(jnp.int32, (s_len, lanes), 0)
    e_s, l_s = _softmax_t(s_s, tok_sel & (kpos <= t), scale)
    o_s = _dot_nn(vst_ref[...], e_s.astype(BF16)) / l_s

    n_win = pad_chunks + 1
    s_w = jnp.concatenate([_dot_nt(kw3_ref[qi + c], q_bf) for c in range(n_win)], axis=0)
    w_pos = (qi - pad_chunks) * tq + lax.broadcasted_iota(jnp.int32, (n_win * tq, lanes), 0)
    m_w = (w_pos <= t) & (w_pos > t - NSA_WINDOW) & (w_pos >= 0)
    e_w, l_w = _softmax_t(s_w, m_w, scale)
    e_wb = e_w.astype(BF16)
    o_w = _dot_nn(vwt3_ref[qi], e_wb[0:tq])
    for c in range(1, n_win):
        o_w = o_w + _dot_nn(vwt3_ref[qi + c], e_wb[c * tq:(c + 1) * tq])
    o_w = o_w / l_w

    g_t = jax.nn.sigmoid(gate_ref[0].T)
    def gate_row(branch):
        return jnp.concatenate([g_t[3 * r + branch:3 * r + branch + 1, :] for r in range(rep)], axis=1)
    out_t = gate_row(0) * o_c + gate_row(1) * o_s + gate_row(2) * o_w
    for r in range(rep):
        o_ref[0, :, r * HEAD_DIM:(r + 1) * HEAD_DIM] = out_t[:, r * tq:(r + 1) * tq].T.astype(o_ref.dtype)


def _nsa(z3, cmp_kv, cos, sin_s, ovl_t, expand, tq=128):
    b, s, _ = z3.shape
    g = NSA_KV_GROUPS
    n_sel = s // NSA_SEL_BLOCK
    n_cmp = cmp_kv.shape[3]
    qw = NSA_REP * HEAD_DIM
    n_chunk_pad = (s + NSA_WINDOW) // tq

    def col(off):
        base = off // LANES
        return lambda bi, gi, qi: (bi, 0, base + gi)

    return pl.pallas_call(
        functools.partial(_nsa_kernel, n_sel=n_sel, scale=HEAD_DIM ** -0.5),
        grid=(b, g, s // tq),
        in_specs=[
            pl.BlockSpec((1, tq, qw), lambda bi, gi, qi: (bi, qi, OFF_CQ // qw + gi)),
            pl.BlockSpec((1, s, LANES), col(OFF_CKS)),
            pl.BlockSpec((1, s, LANES), col(OFF_CVS)),
            pl.BlockSpec((1, s, LANES), col(OFF_CKW)),
            pl.BlockSpec((1, s, LANES), col(OFF_CVW)),
            pl.BlockSpec((1, 1, 1, n_cmp, HEAD_DIM), lambda bi, gi, qi: (0, bi, gi, 0, 0)),
            pl.BlockSpec((1, 1, 1, n_cmp, HEAD_DIM), lambda bi, gi, qi: (1, bi, gi, 0, 0)),
            pl.BlockSpec((1, tq, LANES), lambda bi, gi, qi: (bi, qi, OFF_CGATE // LANES + gi)),
            pl.BlockSpec((s, LANES), lambda bi, gi, qi: (0, 0)),
            pl.BlockSpec((s, LANES), lambda bi, gi, qi: (0, 0)),
            pl.BlockSpec((tq, LANES), lambda bi, gi, qi: (qi, 0)),
            pl.BlockSpec((tq, LANES), lambda bi, gi, qi: (qi, 0)),
            pl.BlockSpec((n_sel, n_cmp), lambda bi, gi, qi: (0, 0)),
            pl.BlockSpec((s, n_sel), lambda bi, gi, qi: (0, 0)),
        ],
        out_specs=pl.BlockSpec((1, tq, qw), lambda bi, gi, qi: (bi, qi, gi)),
        out_shape=jax.ShapeDtypeStruct((b, s, NSA_HEADS * HEAD_DIM), BF16),
        scratch_shapes=[
            pltpu.VMEM((s, HEAD_DIM), BF16),
            pltpu.VMEM((HEAD_DIM, s), BF16),
            pltpu.VMEM((n_chunk_pad, tq, HEAD_DIM), BF16),
            pltpu.VMEM((n_chunk_pad, HEAD_DIM, tq), BF16),
            pltpu.VMEM((HEAD_DIM, n_cmp), BF16),
        ],
        compiler_params=_cparams(("parallel", "parallel", "arbitrary")),
        name="nsa",
    )(z3, z3, z3, z3, z3, cmp_kv, cmp_kv, z3, cos, sin_s, cos, sin_s, ovl_t, expand)


def _merge_kernel(ya_ref, yb_ref, yc_ref, ga_ref, gb_ref, gc_ref, wa_ref, wb_ref, wc_ref, o_ref):
    m = jax.nn.sigmoid(ga_ref[...]) * _dot_nn(ya_ref[...], wa_ref[...])
    m = m + jax.nn.sigmoid(gb_ref[...]) * _dot_nn(yb_ref[...], wb_ref[...])
    m = m + jax.nn.sigmoid(gc_ref[...]) * _dot_nn(yc_ref[...], wc_ref[...])
    o_ref[...] = m.astype(o_ref.dtype)


def _merge(ya, yb, yc, z2d, wa, wb, wc, tm=512, tn=512):
    t = ya.shape[0]
    d = wa.shape[1]
    nj = d // tn

    def gate_spec(branch):
        return pl.BlockSpec((tm, tn), lambda i, j: (i, OFF_MERGE // tn + branch * nj + j))

    def y_spec(y):
        return pl.BlockSpec((tm, y.shape[1]), lambda i, j: (i, 0))

    def w_spec(w):
        return pl.BlockSpec((w.shape[0], tn), lambda i, j: (0, j))

    return pl.pallas_call(
        _merge_kernel,
        grid=(t // tm, nj),
        in_specs=[y_spec(ya), y_spec(yb), y_spec(yc), gate_spec(0), gate_spec(1), gate_spec(2),
                  w_spec(wa), w_spec(wb), w_spec(wc)],
        out_specs=pl.BlockSpec((tm, tn), lambda i, j: (i, j)),
        out_shape=jax.ShapeDtypeStruct((t, d), BF16),
        compiler_params=_cparams(("parallel", "parallel")),
        name="merge",
    )(ya, yb, yc, z2d, z2d, z2d, wa, wb, wc)


def _proj_residual_kernel(m_ref, w_ref, x_ref, o_ref):
    o_ref[...] = x_ref[...] + _dot_nn(m_ref[...], w_ref[...])


def _proj_residual(m, w, x2d, tm=512, tn=512):
    t, k = m.shape
    d = w.shape[1]
    return pl.pallas_call(
        _proj_residual_kernel,
        grid=(t // tm, d // tn),
        in_specs=[
            pl.BlockSpec((tm, k), lambda i, j: (i, 0)),
            pl.BlockSpec((k, tn), lambda i, j: (0, j)),
            pl.BlockSpec((tm, tn), lambda i, j: (i, j)),
        ],
        out_specs=pl.BlockSpec((tm, tn), lambda i, j: (i, j)),
        out_shape=jax.ShapeDtypeStruct((t, d), F32),
        compiler_params=_cparams(("parallel", "parallel")),
        name="out_proj",
    )(m, w, x2d)


def _rope_tables(pos, dim, width):
    inv = 1.0 / (ROPE_THETA ** (jnp.arange(0, dim, 2, dtype=F32) / dim))
    ang = pos.astype(F32)[:, None] * inv[None, :]
    cos = jnp.cos(ang)
    sin = jnp.sin(ang)
    cos = jnp.concatenate([cos, cos], axis=-1)
    sin_s = jnp.concatenate([-sin, sin], axis=-1)
    pad = ((0, 0), (0, width - dim))
    return jnp.pad(cos, pad), jnp.pad(sin_s, pad)


def _pad_cols(w, width):
    return jnp.pad(w, ((0, 0), (0, width - w.shape[1])))


def _pack_w_in(w):
    sizes = [MOBA_HEADS * HEAD_DIM] * 3 + [MLA_Q_RANK, MLA_KV_RANK, MLA_ROPE, NSA_HEADS * HEAD_DIM] \
        + [NSA_KV_GROUPS * HEAD_DIM] * 6 + [NSA_HEADS * 3]
    offs = np.concatenate([[0], np.cumsum(sizes)])
    (a_q, a_k, a_v, b_cq, b_ckv, b_kr, c_q, c_kc, c_vc, c_ks, c_vs, c_kw, c_vw, c_gate) = [
        w[:, int(offs[i]):int(offs[i + 1])] for i in range(len(sizes))]
    merge = w[:, int(offs[-1]):]
    per_group = NSA_REP * 3
    gates = [_pad_cols(c_gate[:, gi * per_group:(gi + 1) * per_group], LANES) for gi in range(NSA_KV_GROUPS)]
    cols = [merge, c_q, b_ckv, b_cq, a_q, a_k, a_v, c_kc, c_vc, c_ks, c_vs, c_kw, c_vw,
            _pad_cols(b_kr, LANES)] + gates
    packed = jnp.concatenate(cols, axis=1)
    return _pad_cols(packed, Z_COLS).astype(BF16)


def _pack_w_uq(w):
    w3 = w.reshape(w.shape[0], MLA_HEADS, MLA_NOPE + MLA_ROPE)
    nope = w3[:, :, :MLA_NOPE].reshape(w.shape[0], MLA_HEADS * MLA_NOPE)
    rope = jnp.pad(w3[:, :, MLA_NOPE:], ((0, 0), (0, 0), (0, LANES - MLA_ROPE))).reshape(w.shape[0], MLA_HEADS * LANES)
    return jnp.concatenate([nope, rope], axis=1).astype(BF16)


def _pack_w_ukv(w):
    w3 = w.reshape(w.shape[0], MLA_HEADS, MLA_NOPE + MLA_V)
    k = w3[:, :, :MLA_NOPE].reshape(w.shape[0], MLA_HEADS * MLA_NOPE)
    v = w3[:, :, MLA_NOPE:].reshape(w.shape[0], MLA_HEADS * MLA_V)
    return jnp.concatenate([k, v], axis=1).astype(BF16)


def _to_chunks16(z3, off, s):
    b = z3.shape[0]
    g = NSA_KV_GROUPS
    slab = z3[:, :, off:off + g * HEAD_DIM].reshape(b, s // NSA_CMP_STRIDE, NSA_CMP_STRIDE, g, HEAD_DIM)
    return slab.transpose(0, 3, 1, 2, 4).reshape(b, g, s // NSA_CMP_STRIDE, NSA_CMP_STRIDE * HEAD_DIM)


def kernel(x, ffn1_norm, ffn1_w_gate, ffn1_w_up, ffn1_w_down, mix_norm, w_in, mla_q_norm, mla_w_uq, mla_kv_norm, mla_w_ukv, nsa_cmp_pos_k, nsa_cmp_w1_k, nsa_cmp_w2_k, nsa_cmp_pos_v, nsa_cmp_w1_v, nsa_cmp_w2_v, w_branch_moba, w_branch_mla, w_branch_nsa, w_out, ffn2_norm, ffn2_w_gate, ffn2_w_up, ffn2_w_down, final_norm):
    b, s, d = x.shape
    depth = w_in.shape[0]
    t = b * s
    assert NSA_CMP_LEN == 2 * NSA_CMP_STRIDE and s % MOBA_BLOCK == 0 and s % NSA_SEL_BLOCK == 0

    pos = jnp.arange(s)
    cos_h, sin_h = _rope_tables(pos, HEAD_DIM, LANES)
    cos_r, sin_r = _rope_tables(pos, MLA_ROPE, LANES)
    n16 = s // NSA_CMP_STRIDE
    cmp_end = np.arange(n16) * NSA_CMP_STRIDE + NSA_CMP_LEN - 1
    cos_c, sin_c = _rope_tables(jnp.asarray(cmp_end), HEAD_DIM, LANES)
    n_sel = s // NSA_SEL_BLOCK
    sel_start = np.arange(n_sel) * NSA_SEL_BLOCK
    cmp_start = np.arange(n16) * NSA_CMP_STRIDE
    ovl_t = ((cmp_start[None, :] <= sel_start[:, None] + NSA_SEL_BLOCK - 1)
             & (cmp_end[None, :] >= sel_start[:, None])).astype(np.float32)
    ovl_t = jnp.asarray(ovl_t, BF16)
    expand = jnp.asarray((np.arange(s)[:, None] // NSA_SEL_BLOCK == np.arange(n_sel)[None, :]).astype(np.float32), BF16)

    xf = x.reshape(t, d)
    for i in range(depth):
        xf = _ffn(xf, ffn1_norm[i], ffn1_w_gate[i].astype(BF16), ffn1_w_up[i].astype(BF16),
                  ffn1_w_down[i].astype(BF16))

        z = _norm_matmul(xf, 0, mix_norm[i], _pack_w_in(w_in[i]), tm=1024, tn=512, name="in_proj")
        z3 = z.reshape(b, s, Z_COLS)
        q_mla = _norm_matmul(z, OFF_BCQ // MLA_Q_RANK, mla_q_norm[i], _pack_w_uq(mla_w_uq[i]),
                             tm=1024, tn=512, name="mla_q_proj")
        kv_mla = _norm_matmul(z, OFF_BCKV // MLA_KV_RANK, mla_kv_norm[i], _pack_w_ukv(mla_w_ukv[i]),
                              tm=1024, tn=512, name="mla_kv_proj")

        y_a = _moba(z3, cos_h, sin_h)
        y_b = _mla(q_mla.reshape(b, s, -1), kv_mla.reshape(b, s, -1), z3, cos_r, sin_r)

        x16 = jnp.stack([_to_chunks16(z3, OFF_CKC, s), _to_chunks16(z3, OFF_CVC, s)])
        w1 = jnp.stack([nsa_cmp_w1_k[i], nsa_cmp_w1_v[i]])
        w2 = jnp.stack([nsa_cmp_w2_k[i], nsa_cmp_w2_v[i]])
        posflat = jnp.stack([nsa_cmp_pos_k[i].reshape(1, -1), nsa_cmp_pos_v[i].reshape(1, -1)])
        cmp_kv = _nsa_compress(x16, w1, w2, posflat, cos_c, sin_c)
        y_c = _nsa(z3, cmp_kv, cos_h, sin_h, ovl_t, expand)

        merged = _merge(y_a.reshape(t, -1), y_b.reshape(t, -1), y_c.reshape(t, -1), z,
                        w_branch_moba[i].astype(BF16), w_branch_mla[i].astype(BF16),
                        w_branch_nsa[i].astype(BF16))
        xf = _proj_residual(merged, w_out[i].astype(BF16), xf)

        xf = _ffn(xf, ffn2_norm[i], ffn2_w_gate[i].astype(BF16), ffn2_w_up[i].astype(BF16),
                  ffn2_w_down[i].astype(BF16), final_g=final_norm if i == depth - 1 else None)
    return xf.reshape(b, s, d)
```

```python
import functools

import numpy as np
import jax
import jax.numpy as jnp
from jax import lax
from jax.experimental import pallas as pl
from jax.experimental.pallas import tpu as pltpu

F32 = jnp.float32
BF16 = jnp.bfloat16

HEAD_DIM = 128
ROPE_THETA = 10000.0
NORM_EPS = 1e-6
NEG_BIG = -1e30
TINY = 1e-30
LOG2E = 1.4426950408889634
N_BRANCHES = 3

MOBA_HEADS = 4
MOBA_BLOCK = 256
MOBA_TOPK = 3

MLA_HEADS = 6
MLA_Q_RANK = 512
MLA_KV_RANK = 256
MLA_NOPE = 128
MLA_ROPE = 64
MLA_V = 128

NSA_HEADS = 6
NSA_KV_GROUPS = 2
NSA_REP = NSA_HEADS // NSA_KV_GROUPS
NSA_CMP_LEN = 32
NSA_CMP_STRIDE = 16
NSA_CMP_HIDDEN = 256
NSA_SEL_BLOCK = 64
NSA_SEL_TOPK = 16
NSA_WINDOW = 512
FORCE_SCORE = 1e9

LANES = 128
VMEM_LIMIT = 56 * 2**20

OFF_MERGE = 0
OFF_CQ = 6144
OFF_BCKV = 6912
OFF_BCQ = 7168
OFF_AQ = 7680
OFF_AK = 8192
OFF_AV = 8704
OFF_CKC = 9216
OFF_CVC = 9472
OFF_CKS = 9728
OFF_CVS = 9984
OFF_CKW = 10240
OFF_CVW = 10496
OFF_BKR = 10752
OFF_CGATE = 10880
Z_COLS = 11264


def _cparams(sem):
    return pltpu.CompilerParams(dimension_semantics=sem, vmem_limit_bytes=VMEM_LIMIT)


def _rmsnorm(x, g):
    ms = jnp.mean(x * x, axis=-1, keepdims=True)
    return x * lax.rsqrt(ms + NORM_EPS) * g


def _rope_full(x, cos, sin_signed):
    return x * cos + pltpu.roll(x, HEAD_DIM // 2, axis=1) * sin_signed


def _rope_half(x, cos, sin_signed):
    lane = lax.broadcasted_iota(jnp.int32, x.shape, 1)
    q = MLA_ROPE // 2
    swapped = jnp.where(lane < q, pltpu.roll(x, LANES - q, axis=1), pltpu.roll(x, q, axis=1))
    return x * cos + swapped * sin_signed


def _dot_nt(a, b):
    return lax.dot_general(a, b, (((1,), (1,)), ((), ())), preferred_element_type=F32)


def _dot_nn(a, b):
    return jnp.dot(a, b, preferred_element_type=F32)


def _split(a):
    hi = a.astype(BF16)
    lo = (a - hi.astype(F32)).astype(BF16)
    return hi, lo


def _dot3_nt(a, b):
    ah, al = _split(a)
    bh, bl = _split(b)
    return _dot_nt(ah, bh) + _dot_nt(ah, bl) + _dot_nt(al, bh)


def _dot3_nn(a, b):
    ah, al = _split(a)
    bh, bl = _split(b)
    return _dot_nn(ah, bh) + _dot_nn(ah, bl) + _dot_nn(al, bh)


def _softmax_t(s, mask, scale):
    s = jnp.where(mask, s * scale, NEG_BIG)
    m = jnp.max(s, axis=0, keepdims=True)
    e = jnp.where(mask, jnp.exp(s - m), 0.0)
    l = jnp.maximum(jnp.sum(e, axis=0, keepdims=True), TINY)
    return e, l


def _softmax_parts(parts):
    masked = [s if m is None else jnp.where(m, s, NEG_BIG) for s, m in parts]
    mx = functools.reduce(jnp.maximum, [jnp.max(p, axis=0, keepdims=True) for p in masked])
    es = [jnp.exp2(p - mx) for p in masked]
    l = functools.reduce(jnp.add, [jnp.sum(e, axis=0, keepdims=True) for e in es])
    e = jnp.concatenate([e.astype(BF16) for e in es], axis=0) if len(es) > 1 else es[0].astype(BF16)
    return e, jnp.maximum(l, TINY)


def _rank_t(v, nb):
    blk = lax.broadcasted_iota(jnp.int32, v.shape, 0)
    rank = jnp.zeros(v.shape, F32)
    for j in range(nb):
        vj = v[j:j + 1, :]
        ge = (vj >= v).astype(F32)
        gt = (vj > v).astype(F32)
        rank = rank + jnp.where(blk > j, ge, gt)
    return rank


def _ffn_kernel(x_ref, g_ref, wg_ref, wu_ref, wd_ref, *rest, n_ff, final):
    if final:
        fg_ref, o_ref, h_ref, acc_ref = rest
    else:
        o_ref, h_ref, acc_ref = rest
    j = pl.program_id(1)

    @pl.when(j == 0)
    def _():
        h_ref[...] = _rmsnorm(x_ref[...], g_ref[...]).astype(BF16)
        acc_ref[...] = jnp.zeros_like(acc_ref)

    h = h_ref[...]
    a = _dot_nn(h, wg_ref[...])
    u = _dot_nn(h, wu_ref[...])
    act = (jax.nn.silu(a) * u).astype(BF16)
    acc_ref[...] += _dot_nn(act, wd_ref[...])

    @pl.when(j == n_ff - 1)
    def _():
        y = x_ref[...] + 0.5 * acc_ref[...]
        if final:
            y = _rmsnorm(y, fg_ref[...])
        o_ref[...] = y


def _ffn(x2d, g, wg, wu, wd, final_g=None, tm=512, tf=512):
    t, d = x2d.shape
    f = wg.shape[1]
    n_ff = f // tf
    final = final_g is not None
    in_specs = [
        pl.BlockSpec((tm, d), lambda i, j: (i, 0)),
        pl.BlockSpec((1, d), lambda i, j: (0, 0)),
        pl.BlockSpec((d, tf), lambda i, j: (0, j)),
        pl.BlockSpec((d, tf), lambda i, j: (0, j)),
        pl.BlockSpec((tf, d), lambda i, j: (j, 0)),
    ]
    args = [x2d, g.reshape(1, d), wg, wu, wd]
    if final:
        in_specs.append(pl.BlockSpec((1, d), lambda i, j: (0, 0)))
        args.append(final_g.reshape(1, d))
    return pl.pallas_call(
        functools.partial(_ffn_kernel, n_ff=n_ff, final=final),
        grid=(t // tm, n_ff),
        in_specs=in_specs,
        out_specs=pl.BlockSpec((tm, d), lambda i, j: (i, 0)),
        out_shape=jax.ShapeDtypeStruct((t, d), F32),
        scratch_shapes=[pltpu.VMEM((tm, d), BF16), pltpu.VMEM((tm, d), F32)],
        compiler_params=_cparams(("parallel", "arbitrary")),
        name="ffn",
    )(*args)


def _norm_matmul_kernel(x_ref, g_ref, w_ref, o_ref, h_ref):
    @pl.when(pl.program_id(1) == 0)
    def _():
        h_ref[...] = _rmsnorm(x_ref[...], g_ref[...]).astype(BF16)

    o_ref[...] = _dot_nn(h_ref[...], w_ref[...]).astype(o_ref.dtype)


def _norm_matmul(x2d, col_blk, g, w, tm, tn, name):
    t = x2d.shape[0]
    k, n = w.shape
    return pl.pallas_call(
        _norm_matmul_kernel,
        grid=(t // tm, n // tn),
        in_specs=[
            pl.BlockSpec((tm, k), lambda i, j: (i, col_blk)),
            pl.BlockSpec((1, k), lambda i, j: (0, 0)),
            pl.BlockSpec((k, tn), lambda i, j: (0, j)),
        ],
        out_specs=pl.BlockSpec((tm, tn), lambda i, j: (i, j)),
        out_shape=jax.ShapeDtypeStruct((t, n), F32),
        scratch_shapes=[pltpu.VMEM((tm, k), BF16)],
        compiler_params=_cparams(("parallel", "arbitrary")),
        name=name,
    )(x2d, g.reshape(1, k), w)


def _moba_kernel(q_ref, k_ref, v_ref, cosf_ref, sinf_ref, cosq_ref, sinq_ref, o_ref,
                 kr_ref, km_ref, vt_ref, *, n_blk, scale):
    qi = pl.program_id(2)
    tq = q_ref.shape[1]

    @pl.when(qi == 0)
    def _():
        kr = _rope_full(k_ref[0], cosf_ref[...], sinf_ref[...])
        kr_ref[...] = kr.astype(BF16)
        km_ref[...] = jnp.concatenate(
            [jnp.mean(kr[j * MOBA_BLOCK:(j + 1) * MOBA_BLOCK], axis=0, keepdims=True) for j in range(n_blk)],
            axis=0)
        vt_ref[...] = v_ref[0].T.astype(BF16)

    qr = _rope_full(q_ref[0], cosq_ref[...], sinq_ref[...])
    q_bf = (qr * (scale * LOG2E)).astype(BF16)
    causal = (lax.broadcasted_iota(jnp.int32, (tq, tq), 0)
              <= lax.broadcasted_iota(jnp.int32, (tq, tq), 1))

    for n in range(n_blk):
        @pl.when(qi == n)
        def _(n=n):
            s = _dot_nt(kr_ref[0:(n + 1) * tq], q_bf)
            parts = []
            if n > MOBA_TOPK:
                gate = _dot3_nt(km_ref[...], qr)
                past = lax.broadcasted_iota(jnp.int32, (n_blk, tq), 0) < n
                gate = jnp.where(past, gate, -jnp.inf)
                drop = jnp.where(_rank_t(gate, n) < MOBA_TOPK, 0.0, NEG_BIG)
                for j in range(n):
                    parts.append((s[j * tq:(j + 1) * tq] + drop[j:j + 1, :], None))
            elif n > 0:
                parts.append((s[0:n * tq], None))
            parts.append((s[n * tq:], causal))
            e, l = _softmax_parts(parts)
            o_t = _dot_nn(vt_ref[:, 0:(n + 1) * tq], e) / l
            o_ref[0] = o_t.T.astype(o_ref.dtype)


def _moba(z3, cos, sin_s, tq=MOBA_BLOCK):
    b, s, _ = z3.shape
    n_blk = s // MOBA_BLOCK
    qb, kb, vb = OFF_AQ // LANES, OFF_AK // LANES, OFF_AV // LANES
    return pl.pallas_call(
        functools.partial(_moba_kernel, n_blk=n_blk, scale=HEAD_DIM ** -0.5),
        grid=(b, MOBA_HEADS, s // tq),
        in_specs=[
            pl.BlockSpec((1, tq, LANES), lambda bi, h, qi: (bi, qi, qb + h)),
            pl.BlockSpec((1, s, LANES), lambda bi, h, qi: (bi, 0, kb + h)),
            pl.BlockSpec((1, s, LANES), lambda bi, h, qi: (bi, 0, vb + h)),
            pl.BlockSpec((s, LANES), lambda bi, h, qi: (0, 0)),
            pl.BlockSpec((s, LANES), lambda bi, h, qi: (0, 0)),
            pl.BlockSpec((tq, LANES), lambda bi, h, qi: (qi, 0)),
            pl.BlockSpec((tq, LANES), lambda bi, h, qi: (qi, 0)),
        ],
        out_specs=pl.BlockSpec((1, tq, LANES), lambda bi, h, qi: (bi, qi, h)),
        out_shape=jax.ShapeDtypeStruct((b, s, MOBA_HEADS * HEAD_DIM), BF16),
        scratch_shapes=[pltpu.VMEM((s, LANES), BF16), pltpu.VMEM((n_blk, LANES), F32),
                        pltpu.VMEM((LANES, s), BF16)],
        compiler_params=_cparams(("parallel", "parallel", "arbitrary")),
        name="moba",
    )(z3, z3, z3, cos, sin_s, cos, sin_s)


def _mla_kernel(qn_ref, qr_ref, kn_ref, v_ref, kr_ref, cosf_ref, sinf_ref, cosq_ref, sinq_ref, o_ref,
                kf_ref, vt_ref, *, scale):
    qi = pl.program_id(2)
    tq = qn_ref.shape[1]
    s_len = kn_ref.shape[1]

    @pl.when(qi == 0)
    def _():
        kf_ref[:, :LANES] = kn_ref[0].astype(BF16)
        kf_ref[:, LANES:] = _rope_half(kr_ref[0], cosf_ref[...], sinf_ref[...]).astype(BF16)
        vt_ref[...] = v_ref[0].T.astype(BF16)

    q = jnp.concatenate([qn_ref[0], _rope_half(qr_ref[0], cosq_ref[...], sinq_ref[...])], axis=1)
    q_bf = (q * (scale * LOG2E)).astype(BF16)
    causal = (lax.broadcasted_iota(jnp.int32, (tq, tq), 0)
              <= lax.broadcasted_iota(jnp.int32, (tq, tq), 1))

    for n in range(s_len // tq):
        @pl.when(qi == n)
        def _(n=n):
            s = _dot_nt(kf_ref[0:(n + 1) * tq], q_bf)
            parts = [(s[0:n * tq], None)] if n > 0 else []
            parts.append((s[n * tq:], causal))
            e, l = _softmax_parts(parts)
            o_t = _dot_nn(vt_ref[:, 0:(n + 1) * tq], e) / l
            o_ref[0] = o_t.T.astype(o_ref.dtype)


def _mla(q3, kv3, z3, cos, sin_s, tq=256):
    b, s, _ = q3.shape
    krb = OFF_BKR // LANES
    return pl.pallas_call(
        functools.partial(_mla_kernel, scale=(MLA_NOPE + MLA_ROPE) ** -0.5),
        grid=(b, MLA_HEADS, s // tq),
        in_specs=[
            pl.BlockSpec((1, tq, LANES), lambda bi, h, qi: (bi, qi, h)),
            pl.BlockSpec((1, tq, LANES), lambda bi, h, qi: (bi, qi, MLA_HEADS + h)),
            pl.BlockSpec((1, s, LANES), lambda bi, h, qi: (bi, 0, h)),
            pl.BlockSpec((1, s, LANES), lambda bi, h, qi: (bi, 0, MLA_HEADS + h)),
            pl.BlockSpec((1, s, LANES), lambda bi, h, qi: (bi, 0, krb)),
            pl.BlockSpec((s, LANES), lambda bi, h, qi: (0, 0)),
            pl.BlockSpec((s, LANES), lambda bi, h, qi: (0, 0)),
            pl.BlockSpec((tq, LANES), lambda bi, h, qi: (qi, 0)),
            pl.BlockSpec((tq, LANES), lambda bi, h, qi: (qi, 0)),
        ],
        out_specs=pl.BlockSpec((1, tq, LANES), lambda bi, h, qi: (bi, qi, h)),
        out_shape=jax.ShapeDtypeStruct((b, s, MLA_HEADS * MLA_V), BF16),
        scratch_shapes=[pltpu.VMEM((s, 2 * LANES), BF16), pltpu.VMEM((LANES, s), BF16)],
        compiler_params=_cparams(("parallel", "parallel", "arbitrary")),
        name="mla",
    )(q3, q3, kv3, kv3, z3, cos, sin_s, cos, sin_s)


def _cmp_kernel(x_ref, w1_ref, w2_ref, pos_ref, cos_ref, sin_ref, o_ref):
    kv = pl.program_id(0)
    n16 = o_ref.shape[3]
    x = jnp.concatenate(
        [x_ref[0, pl.ds(l, n16, stride=NSA_CMP_STRIDE), :] for l in range(NSA_CMP_STRIDE)], axis=1)
    half = x.shape[1]
    w1 = w1_ref[0]
    first = _dot3_nn(x, w1[:half])
    second = _dot3_nn(x, w1[half:])
    bias = _dot3_nn(jnp.broadcast_to(pos_ref[0], (8, 2 * half)), w1)[0:1]
    hid = first + pltpu.roll(second, n16 - 1, axis=0) + bias
    y = _dot3_nn(jax.nn.gelu(hid), w2_ref[0])
    roped = _rope_full(y, cos_ref[...], sin_ref[...])
    o_ref[0, 0, 0] = jnp.where(kv == 0, roped, y)


def _nsa_compress(z3, w1, w2, posflat, cos_c, sin_c):
    b, s, _ = z3.shape
    g = NSA_KV_GROUPS
    n16 = s // NSA_CMP_STRIDE
    width = NSA_CMP_STRIDE * HEAD_DIM
    hid = w1.shape[2]
    base = OFF_CKC // LANES
    return pl.pallas_call(
        _cmp_kernel,
        grid=(2, b, g),
        in_specs=[
            pl.BlockSpec((1, s, LANES), lambda kv, bi, gi: (bi, 0, base + g * kv + gi)),
            pl.BlockSpec((1, 2 * width, hid), lambda kv, bi, gi: (kv, 0, 0)),
            pl.BlockSpec((1, hid, HEAD_DIM), lambda kv, bi, gi: (kv, 0, 0)),
            pl.BlockSpec((1, 1, 2 * width), lambda kv, bi, gi: (kv, 0, 0)),
            pl.BlockSpec((n16, HEAD_DIM), lambda kv, bi, gi: (0, 0)),
            pl.BlockSpec((n16, HEAD_DIM), lambda kv, bi, gi: (0, 0)),
        ],
        out_specs=pl.BlockSpec((1, 1, 1, n16, HEAD_DIM), lambda kv, bi, gi: (kv, bi, gi, 0, 0)),
        out_shape=jax.ShapeDtypeStruct((2, b, g, n16, HEAD_DIM), F32),
        compiler_params=_cparams(("parallel", "parallel", "parallel")),
        name="nsa_compress",
    )(z3, w1, w2, posflat, cos_c, sin_c)


def _nsa_kernel(q_ref, ks_ref, vs_ref, kw_ref, vw_ref, kc_ref, vc_ref, gate_ref,
                cosf_ref, sinf_ref, cosq_ref, sinq_ref, ovl_ref, o_ref,
                ksr_ref, vst_ref, kw3_ref, vwt3_ref, vct_ref, os_ref, *, n_sel, scale):
    qi = pl.program_id(2)
    tq = q_ref.shape[1]
    s_len = ks_ref.shape[1]
    rep = NSA_REP
    lanes = rep * tq
    n_chunk = s_len // tq
    pad_chunks = NSA_WINDOW // tq
    n_cmp = kc_ref.shape[3]

    @pl.when(qi == 0)
    def _():
        ksr_ref[...] = _rope_full(ks_ref[0], cosf_ref[...], sinf_ref[...]).astype(BF16)
        vst_ref[...] = vs_ref[0].T.astype(BF16)
        kw3_ref[0:pad_chunks] = jnp.zeros((pad_chunks, tq, HEAD_DIM), BF16)
        vwt3_ref[0:pad_chunks] = jnp.zeros((pad_chunks, HEAD_DIM, tq), BF16)
        kwr = _rope_full(kw_ref[0], cosf_ref[...], sinf_ref[...]).astype(BF16)
        for c in range(n_chunk):
            kw3_ref[pad_chunks + c] = kwr[c * tq:(c + 1) * tq]
            vwt3_ref[pad_chunks + c] = vw_ref[0, c * tq:(c + 1) * tq, :].T.astype(BF16)
        vct_ref[...] = vc_ref[0, 0, 0].T.astype(BF16)

    cos_q = cosq_ref[...]
    sin_q = sinq_ref[...]
    q_all = jnp.concatenate(
        [_rope_full(q_ref[0, :, r * HEAD_DIM:(r + 1) * HEAD_DIM], cos_q, sin_q) for r in range(rep)],
        axis=0)
    q_bf = (q_all * (scale * LOG2E)).astype(BF16)
    lane = lax.broadcasted_iota(jnp.int32, (1, lanes), 1)
    t_loc = lane % tq
    t = qi * tq + t_loc

    s_c = _dot3_nt(kc_ref[0, 0, 0], q_all)
    cmp_end = (lax.broadcasted_iota(jnp.int32, (n_cmp, lanes), 0) * NSA_CMP_STRIDE + (NSA_CMP_LEN - 1))
    e_c, l_c = _softmax_t(s_c, cmp_end <= t, scale)
    p_c = e_c / l_c
    o_c = _dot_nn(vct_ref[...], p_c.astype(BF16))

    p_hi, p_lo = _split(p_c)
    ovl = ovl_ref[...]
    imp3 = _dot_nn(ovl, p_hi) + _dot_nn(ovl, p_lo)
    imp = imp3[:, 0:tq]
    for r in range(1, rep):
        imp = imp + imp3[:, r * tq:(r + 1) * tq]
    tq_pos = t[:, 0:tq]
    cur = tq_pos // NSA_SEL_BLOCK
    blk = lax.broadcasted_iota(jnp.int32, (n_sel, tq), 0)
    forced = (blk == 0) | (blk == cur) | (blk == cur - 1)
    visible = blk <= cur
    imp = jnp.where(forced, FORCE_SCORE, imp)
    imp = jnp.where(visible, imp, -jnp.inf)

    unit = 2 * tq
    sb = NSA_SEL_BLOCK
    for v in range(s_len // unit):
        @pl.when(qi // 2 == v)
        def _(v=v):
            n_keys = (v + 1) * unit
            lo = n_keys - unit
            s = _dot_nt(ksr_ref[0:n_keys], q_bf)

            def causal(start, n):
                return lax.broadcasted_iota(jnp.int32, (n, lanes), 0) <= (t - start)

            n_vis = n_keys // sb
            if n_vis > NSA_SEL_TOPK:
                rank = _rank_t(imp, n_vis)
                rank3 = jnp.concatenate([rank] * rep, axis=1)
                parts = []
                drop = jnp.where(rank3 < NSA_SEL_TOPK, 0.0, NEG_BIG)
                for j in range(n_vis):
                    parts.append((s[j * sb:(j + 1) * sb] + drop[j:j + 1, :],
                                  causal(j * sb, sb) if j * sb >= lo else None))
            else:
                parts = [(s[0:lo], None)] if lo > 0 else []
                parts.append((s[lo:], causal(lo, unit)))
            e_s, l_s = _softmax_parts(parts)
            os_ref[...] = _dot_nn(vst_ref[:, 0:n_keys], e_s) / l_s

    n_win = pad_chunks + 1
    row = lax.broadcasted_iota(jnp.int32, (tq, lanes), 0)
    parts = []
    for c in range(n_win):
        s_w = _dot_nt(kw3_ref[qi + c], q_bf)
        if c < pad_chunks:
            s_w = s_w + jnp.where(qi + c >= pad_chunks, 0.0, NEG_BIG)
        mask = (row > t_loc) if c == 0 else ((row <= t_loc) if c == n_win - 1 else None)
        parts.append((s_w, mask))
    e_w, l_w = _softmax_parts(parts)
    o_w = _dot_nn(vwt3_ref[qi], e_w[0:tq])
    for c in range(1, n_win):
        o_w = o_w + _dot_nn(vwt3_ref[qi + c], e_w[c * tq:(c + 1) * tq])
    o_w = o_w / l_w
    o_s = os_ref[...]

    g_t = jax.nn.sigmoid(gate_ref[0].T)
    def gate_row(branch):
        return jnp.concatenate([g_t[3 * r + branch:3 * r + branch + 1, :] for r in range(rep)], axis=1)
    out_t = gate_row(0) * o_c + gate_row(1) * o_s + gate_row(2) * o_w
    for r in range(rep):
        o_ref[0, :, r * HEAD_DIM:(r + 1) * HEAD_DIM] = out_t[:, r * tq:(r + 1) * tq].T.astype(o_ref.dtype)


def _nsa(z3, cmp_kv, cos, sin_s, ovl_t, tq=128):
    b, s, _ = z3.shape
    g = NSA_KV_GROUPS
    n_sel = s // NSA_SEL_BLOCK
    n_cmp = cmp_kv.shape[3]
    qw = NSA_REP * HEAD_DIM
    n_chunk_pad = (s + NSA_WINDOW) // tq

    def col(off):
        base = off // LANES
        return lambda bi, gi, qi: (bi, 0, base + gi)

    return pl.pallas_call(
        functools.partial(_nsa_kernel, n_sel=n_sel, scale=HEAD_DIM ** -0.5),
        grid=(b, g, s // tq),
        in_specs=[
            pl.BlockSpec((1, tq, qw), lambda bi, gi, qi: (bi, qi, OFF_CQ // qw + gi)),
            pl.BlockSpec((1, s, LANES), col(OFF_CKS)),
            pl.BlockSpec((1, s, LANES), col(OFF_CVS)),
            pl.BlockSpec((1, s, LANES), col(OFF_CKW)),
            pl.BlockSpec((1, s, LANES), col(OFF_CVW)),
            pl.BlockSpec((1, 1, 1, n_cmp, HEAD_DIM), lambda bi, gi, qi: (0, bi, gi, 0, 0)),
            pl.BlockSpec((1, 1, 1, n_cmp, HEAD_DIM), lambda bi, gi, qi: (1, bi, gi, 0, 0)),
            pl.BlockSpec((1, tq, LANES), lambda bi, gi, qi: (bi, qi, OFF_CGATE // LANES + gi)),
            pl.BlockSpec((s, LANES), lambda bi, gi, qi: (0, 0)),
            pl.BlockSpec((s, LANES), lambda bi, gi, qi: (0, 0)),
            pl.BlockSpec((tq, LANES), lambda bi, gi, qi: (qi, 0)),
            pl.BlockSpec((tq, LANES), lambda bi, gi, qi: (qi, 0)),
            pl.BlockSpec((n_sel, n_cmp), lambda bi, gi, qi: (0, 0)),
        ],
        out_specs=pl.BlockSpec((1, tq, qw), lambda bi, gi, qi: (bi, qi, gi)),
        out_shape=jax.ShapeDtypeStruct((b, s, NSA_HEADS * HEAD_DIM), BF16),
        scratch_shapes=[
            pltpu.VMEM((s, HEAD_DIM), BF16),
            pltpu.VMEM((HEAD_DIM, s), BF16),
            pltpu.VMEM((n_chunk_pad, tq, HEAD_DIM), BF16),
            pltpu.VMEM((n_chunk_pad, HEAD_DIM, tq), BF16),
            pltpu.VMEM((HEAD_DIM, n_cmp), BF16),
            pltpu.VMEM((HEAD_DIM, NSA_REP * tq), F32),
        ],
        compiler_params=_cparams(("parallel", "parallel", "arbitrary")),
        name="nsa",
    )(z3, z3, z3, z3, z3, cmp_kv, cmp_kv, z3, cos, sin_s, cos, sin_s, ovl_t)


def _merge_kernel(ya_ref, yb_ref, yc_ref, ga_ref, gb_ref, gc_ref, wa_ref, wb_ref, wc_ref, o_ref):
    m = jax.nn.sigmoid(ga_ref[...]) * _dot_nn(ya_ref[...], wa_ref[...])
    m = m + jax.nn.sigmoid(gb_ref[...]) * _dot_nn(yb_ref[...], wb_ref[...])
    m = m + jax.nn.sigmoid(gc_ref[...]) * _dot_nn(yc_ref[...], wc_ref[...])
    o_ref[...] = m.astype(o_ref.dtype)


def _merge(ya, yb, yc, z2d, wa, wb, wc, tm=512, tn=512):
    t = ya.shape[0]
    d = wa.shape[1]
    nj = d // tn

    def gate_spec(branch):
        return pl.BlockSpec((tm, tn), lambda i, j: (i, OFF_MERGE // tn + branch * nj + j))

    def y_spec(y):
        return pl.BlockSpec((tm, y.shape[1]), lambda i, j: (i, 0))

    def w_spec(w):
        return pl.BlockSpec((w.shape[0], tn), lambda i, j: (0, j))

    return pl.pallas_call(
        _merge_kernel,
        grid=(t // tm, nj),
        in_specs=[y_spec(ya), y_spec(yb), y_spec(yc), gate_spec(0), gate_spec(1), gate_spec(2),
                  w_spec(wa), w_spec(wb), w_spec(wc)],
        out_specs=pl.BlockSpec((tm, tn), lambda i, j: (i, j)),
        out_shape=jax.ShapeDtypeStruct((t, d), BF16),
        compiler_params=_cparams(("parallel", "parallel")),
        name="merge",
    )(ya, yb, yc, z2d, z2d, z2d, wa, wb, wc)


def _proj_residual_kernel(m_ref, w_ref, x_ref, o_ref):
    o_ref[...] = x_ref[...] + _dot_nn(m_ref[...], w_ref[...])


def _proj_residual(m, w, x2d, tm=512, tn=512):
    t, k = m.shape
    d = w.shape[1]
    return pl.pallas_call(
        _proj_residual_kernel,
        grid=(t // tm, d // tn),
        in_specs=[
            pl.BlockSpec((tm, k), lambda i, j: (i, 0)),
            pl.BlockSpec((k, tn), lambda i, j: (0, j)),
            pl.BlockSpec((tm, tn), lambda i, j: (i, j)),
        ],
        out_specs=pl.BlockSpec((tm, tn), lambda i, j: (i, j)),
        out_shape=jax.ShapeDtypeStruct((t, d), F32),
        compiler_params=_cparams(("parallel", "parallel")),
        name="out_proj",
    )(m, w, x2d)


def _rope_tables(pos, dim, width):
    inv = 1.0 / (ROPE_THETA ** (jnp.arange(0, dim, 2, dtype=F32) / dim))
    ang = pos.astype(F32)[:, None] * inv[None, :]
    cos = jnp.cos(ang)
    sin = jnp.sin(ang)
    cos = jnp.concatenate([cos, cos], axis=-1)
    sin_s = jnp.concatenate([-sin, sin], axis=-1)
    pad = ((0, 0), (0, width - dim))
    return jnp.pad(cos, pad), jnp.pad(sin_s, pad)


def _pad_cols(w, width):
    return jnp.pad(w, ((0, 0), (0, width - w.shape[1])))


def _pack_w_in(w):
    sizes = [MOBA_HEADS * HEAD_DIM] * 3 + [MLA_Q_RANK, MLA_KV_RANK, MLA_ROPE, NSA_HEADS * HEAD_DIM] \
        + [NSA_KV_GROUPS * HEAD_DIM] * 6 + [NSA_HEADS * 3]
    offs = np.concatenate([[0], np.cumsum(sizes)])
    (a_q, a_k, a_v, b_cq, b_ckv, b_kr, c_q, c_kc, c_vc, c_ks, c_vs, c_kw, c_vw, c_gate) = [
        w[:, int(offs[i]):int(offs[i + 1])] for i in range(len(sizes))]
    merge = w[:, int(offs[-1]):]
    per_group = NSA_REP * 3
    gates = [_pad_cols(c_gate[:, gi * per_group:(gi + 1) * per_group], LANES) for gi in range(NSA_KV_GROUPS)]
    cols = [merge, c_q, b_ckv, b_cq, a_q, a_k, a_v, c_kc, c_vc, c_ks, c_vs, c_kw, c_vw,
            _pad_cols(b_kr, LANES)] + gates
    packed = jnp.concatenate(cols, axis=1)
    return _pad_cols(packed, Z_COLS).astype(BF16)


def _pack_w_uq(w):
    w3 = w.reshape(w.shape[0], MLA_HEADS, MLA_NOPE + MLA_ROPE)
    nope = w3[:, :, :MLA_NOPE].reshape(w.shape[0], MLA_HEADS * MLA_NOPE)
    rope = jnp.pad(w3[:, :, MLA_NOPE:], ((0, 0), (0, 0), (0, LANES - MLA_ROPE))).reshape(w.shape[0], MLA_HEADS * LANES)
    return jnp.concatenate([nope, rope], axis=1).astype(BF16)


def _pack_w_ukv(w):
    w3 = w.reshape(w.shape[0], MLA_HEADS, MLA_NOPE + MLA_V)
    k = w3[:, :, :MLA_NOPE].reshape(w.shape[0], MLA_HEADS * MLA_NOPE)
    v = w3[:, :, MLA_NOPE:].reshape(w.shape[0], MLA_HEADS * MLA_V)
    return jnp.concatenate([k, v], axis=1).astype(BF16)


def kernel(x, ffn1_norm, ffn1_w_gate, ffn1_w_up, ffn1_w_down, mix_norm, w_in, mla_q_norm, mla_w_uq, mla_kv_norm, mla_w_ukv, nsa_cmp_pos_k, nsa_cmp_w1_k, nsa_cmp_w2_k, nsa_cmp_pos_v, nsa_cmp_w1_v, nsa_cmp_w2_v, w_branch_moba, w_branch_mla, w_branch_nsa, w_out, ffn2_norm, ffn2_w_gate, ffn2_w_up, ffn2_w_down, final_norm):
    b, s, d = x.shape
    depth = w_in.shape[0]
    t = b * s
    assert NSA_CMP_LEN == 2 * NSA_CMP_STRIDE and s % MOBA_BLOCK == 0 and s % NSA_SEL_BLOCK == 0

    pos = jnp.arange(s)
    cos_h, sin_h = _rope_tables(pos, HEAD_DIM, LANES)
    cos_r, sin_r = _rope_tables(pos, MLA_ROPE, LANES)
    n16 = s // NSA_CMP_STRIDE
    cmp_end = np.arange(n16) * NSA_CMP_STRIDE + NSA_CMP_LEN - 1
    cos_c, sin_c = _rope_tables(jnp.asarray(cmp_end), HEAD_DIM, LANES)
    n_sel = s // NSA_SEL_BLOCK
    sel_start = np.arange(n_sel) * NSA_SEL_BLOCK
    cmp_start = np.arange(n16) * NSA_CMP_STRIDE
    ovl_t = ((cmp_start[None, :] <= sel_start[:, None] + NSA_SEL_BLOCK - 1)
             & (cmp_end[None, :] >= sel_start[:, None])).astype(np.float32)
    ovl_t = jnp.asarray(ovl_t, BF16)

    xf = x.reshape(t, d)
    for i in range(depth):
        xf = _ffn(xf, ffn1_norm[i], ffn1_w_gate[i].astype(BF16), ffn1_w_up[i].astype(BF16),
                  ffn1_w_down[i].astype(BF16))

        z = _norm_matmul(xf, 0, mix_norm[i], _pack_w_in(w_in[i]), tm=1024, tn=512, name="in_proj")
        z3 = z.reshape(b, s, Z_COLS)
        q_mla = _norm_matmul(z, OFF_BCQ // MLA_Q_RANK, mla_q_norm[i], _pack_w_uq(mla_w_uq[i]),
                             tm=1024, tn=512, name="mla_q_proj")
        kv_mla = _norm_matmul(z, OFF_BCKV // MLA_KV_RANK, mla_kv_norm[i], _pack_w_ukv(mla_w_ukv[i]),
                              tm=1024, tn=512, name="mla_kv_proj")

        y_a = _moba(z3, cos_h, sin_h)
        y_b = _mla(q_mla.reshape(b, s, -1), kv_mla.reshape(b, s, -1), z3, cos_r, sin_r)

        w1 =jnp.stack([nsa_cmp_w1_k[i], nsa_cmp_w1_v[i]])
        w2 = jnp.stack([nsa_cmp_w2_k[i], nsa_cmp_w2_v[i]])
        posflat = jnp.stack([nsa_cmp_pos_k[i].reshape(1, -1), nsa_cmp_pos_v[i].reshape(1, -1)])
        cmp_kv = _nsa_compress(z3, w1, w2, posflat, cos_c, sin_c)
        y_c = _nsa(z3, cmp_kv, cos_h, sin_h, ovl_t)

        merged = _merge(y_a.reshape(t, -1), y_b.reshape(t, -1), y_c.reshape(t, -1), z,
                        w_branch_moba[i].astype(BF16), w_branch_mla[i].astype(BF16),
                        w_branch_nsa[i].astype(BF16))
        xf = _proj_residual(merged, w_out[i].astype(BF16), xf)

        xf = _ffn(xf, ffn2_norm[i], ffn2_w_gate[i].astype(BF16), ffn2_w_up[i].astype(BF16),
                  ffn2_w_down[i].astype(BF16), final_g=final_norm if i == depth - 1 else None)
    return xf.reshape(b, s, d)
```

```python
import functools

import numpy as np
import jax
import jax.numpy as jnp
from jax import lax
from jax.experimental import pallas as pl
from jax.experimental.pallas import tpu as pltpu

F32 = jnp.float32
BF16 = jnp.bfloat16

HEAD_DIM = 128
ROPE_THETA = 10000.0
NORM_EPS = 1e-6
NEG_BIG = -1e30
TINY = 1e-30
LOG2E = 1.4426950408889634
N_BRANCHES = 3

MOBA_HEADS = 4
MOBA_BLOCK = 256
MOBA_TOPK = 3

MLA_HEADS = 6
MLA_Q_RANK = 512
MLA_KV_RANK = 256
MLA_NOPE = 128
MLA_ROPE = 64
MLA_V = 128

NSA_HEADS = 6
NSA_KV_GROUPS = 2
NSA_REP = NSA_HEADS // NSA_KV_GROUPS
NSA_CMP_LEN = 32
NSA_CMP_STRIDE = 16
NSA_CMP_HIDDEN = 256
NSA_SEL_BLOCK = 64
NSA_SEL_TOPK = 16
NSA_WINDOW = 512
FORCE_SCORE = 1e9

LANES = 128
VMEM_LIMIT = 56 * 2**20

OFF_MERGE = 0
OFF_CQ = 6144
OFF_BCKV = 6912
OFF_BCQ = 7168
OFF_AQ = 7680
OFF_AK = 8192
OFF_AV = 8704
OFF_CKC = 9216
OFF_CVC = 9472
OFF_CKS = 9728
OFF_CVS = 9984
OFF_CKW = 10240
OFF_CVW = 10496
OFF_BKR = 10752
OFF_CGATE = 10880
Z_COLS = 11264


def _cparams(sem):
    return pltpu.CompilerParams(dimension_semantics=sem, vmem_limit_bytes=VMEM_LIMIT)


def _rmsnorm(x, g):
    ms = jnp.mean(x * x, axis=-1, keepdims=True)
    return x * lax.rsqrt(ms + NORM_EPS) * g


def _rope_full(x, cos, sin_signed):
    return x * cos + pltpu.roll(x, HEAD_DIM // 2, axis=1) * sin_signed


def _rope_half(x, cos, sin_signed):
    lane = lax.broadcasted_iota(jnp.int32, x.shape, 1)
    q = MLA_ROPE // 2
    swapped = jnp.where(lane < q, pltpu.roll(x, LANES - q, axis=1), pltpu.roll(x, q, axis=1))
    return x * cos + swapped * sin_signed


def _dot_nt(a, b):
    return lax.dot_general(a, b, (((1,), (1,)), ((), ())), preferred_element_type=F32)


def _dot_nn(a, b):
    return jnp.dot(a, b, preferred_element_type=F32)


def _split(a):
    hi = a.astype(BF16)
    lo = (a - hi.astype(F32)).astype(BF16)
    return hi, lo


def _dot3_nt(a, b):
    ah, al = _split(a)
    bh, bl = _split(b)
    return _dot_nt(ah, bh) + _dot_nt(ah, bl) + _dot_nt(al, bh)


def _dot3_nn(a, b):
    ah, al = _split(a)
    bh, bl = _split(b)
    return _dot_nn(ah, bh) + _dot_nn(ah, bl) + _dot_nn(al, bh)


def _softmax_t(s, mask, scale):
    s = jnp.where(mask, s * scale, NEG_BIG)
    m = jnp.max(s, axis=0, keepdims=True)
    e = jnp.where(mask, jnp.exp(s - m), 0.0)
    l = jnp.maximum(jnp.sum(e, axis=0, keepdims=True), TINY)
    return e, l


def _softmax_parts(parts):
    masked = [s if m is None else jnp.where(m, s, NEG_BIG) for s, m in parts]
    mx = functools.reduce(jnp.maximum, [jnp.max(p, axis=0, keepdims=True) for p in masked])
    es = [jnp.exp2(p - mx) for p in masked]
    l = functools.reduce(jnp.add, [jnp.sum(e, axis=0, keepdims=True) for e in es])
    e = jnp.concatenate([e.astype(BF16) for e in es], axis=0) if len(es) > 1 else es[0].astype(BF16)
    return e, jnp.maximum(l, TINY)


def _rank_t(v, nb):
    blk = lax.broadcasted_iota(jnp.int32, v.shape, 0)
    rank = jnp.zeros(v.shape, F32)
    for j in range(nb):
        vj = v[j:j + 1, :]
        ge = (vj >= v).astype(F32)
        gt = (vj > v).astype(F32)
        rank = rank + jnp.where(blk > j, ge, gt)
    return rank


def _ffn_kernel(x_ref, g_ref, wg_ref, wu_ref, wd_ref, *rest, n_ff, final):
    if final:
        fg_ref, o_ref, h_ref = rest
    else:
        o_ref, h_ref = rest
    j = pl.program_id(1)

    @pl.when(j == 0)
    def _():
        h_ref[...] = _rmsnorm(x_ref[...], g_ref[...]).astype(BF16)

    h = h_ref[...]
    a = _dot_nn(h, wg_ref[...].astype(BF16))
    u = _dot_nn(h, wu_ref[...].astype(BF16))
    act = (jax.nn.silu(a) * u).astype(BF16)
    half_y = 0.5 * _dot_nn(act, wd_ref[...].astype(BF16))

    @pl.when(j == 0)
    def _():
        o_ref[...] = x_ref[...] + half_y

    @pl.when(j > 0)
    def _():
        o_ref[...] += half_y

    if final:
        @pl.when(j == n_ff - 1)
        def _():
            o_ref[...] = _rmsnorm(o_ref[...], fg_ref[...])


def _ffn(x2d, g, wg, wu, wd, layer, final_g=None, tm=1024, tf=256):
    t, d = x2d.shape
    f = wg.shape[2]
    n_ff = f // tf
    final = final_g is not None
    in_specs = [
        pl.BlockSpec((tm, d), lambda i, j: (i, 0), pipeline_mode=pl.Buffered(1)),
        pl.BlockSpec((1, d), lambda i, j: (0, 0)),
        pl.BlockSpec((None, d, tf), lambda i, j: (layer, 0, j)),
        pl.BlockSpec((None, d, tf), lambda i, j: (layer, 0, j)),
        pl.BlockSpec((None, tf, d), lambda i, j: (layer, j, 0)),
    ]
    args = [x2d, g.reshape(1, d), wg, wu, wd]
    if final:
        in_specs.append(pl.BlockSpec((1, d), lambda i, j: (0, 0)))
        args.append(final_g.reshape(1, d))
    return pl.pallas_call(
        functools.partial(_ffn_kernel, n_ff=n_ff, final=final),
        grid=(t // tm, n_ff),
        in_specs=in_specs,
        out_specs=pl.BlockSpec((tm, d), lambda i, j: (i, 0)),
        out_shape=jax.ShapeDtypeStruct((t, d), F32),
        scratch_shapes=[pltpu.VMEM((tm, d), BF16)],
        compiler_params=_cparams(("parallel", "arbitrary")),
        name="ffn",
    )(*args)


def _norm_matmul_kernel(x_ref, g_ref, w_ref, o_ref, h_ref):
    @pl.when(pl.program_id(1) == 0)
    def _():
        h_ref[...] = _rmsnorm(x_ref[...], g_ref[...]).astype(BF16)

    o_ref[...] = _dot_nn(h_ref[...], w_ref[...].astype(BF16)).astype(o_ref.dtype)


def _norm_matmul(x2d, col_blk, g, w, layer, tm, tn, name):
    t = x2d.shape[0]
    _, k, n = w.shape
    return pl.pallas_call(
        _norm_matmul_kernel,
        grid=(t // tm, n // tn),
        in_specs=[
            pl.BlockSpec((tm, k), lambda i, j: (i, col_blk)),
            pl.BlockSpec((1, k), lambda i, j: (0, 0)),
            pl.BlockSpec((None, k, tn), lambda i, j: (layer, 0, j)),
        ],
        out_specs=pl.BlockSpec((tm, tn), lambda i, j: (i, j)),
        out_shape=jax.ShapeDtypeStruct((t, n), F32),
        scratch_shapes=[pltpu.VMEM((tm, k), BF16)],
        compiler_params=_cparams(("parallel", "arbitrary")),
        name=name,
    )(x2d, g.reshape(1, k), w)


def _moba_kernel(q_ref, k_ref, v_ref, cosf_ref, sinf_ref, cosq_ref, sinq_ref, o_ref,
                 kr_ref, km_ref, vt_ref, *, n_blk, scale):
    qi = pl.program_id(2)
    tq = q_ref.shape[1]

    @pl.when(qi == 0)
    def _():
        kr = _rope_full(k_ref[0], cosf_ref[...], sinf_ref[...])
        kr_ref[...] = kr.astype(BF16)
        km_ref[...] = jnp.concatenate(
            [jnp.mean(kr[j * MOBA_BLOCK:(j + 1) * MOBA_BLOCK], axis=0, keepdims=True) for j in range(n_blk)],
            axis=0)
        vt_ref[...] = v_ref[0].T.astype(BF16)

    qr = _rope_full(q_ref[0], cosq_ref[...], sinq_ref[...])
    q_bf = (qr * (scale * LOG2E)).astype(BF16)
    causal = (lax.broadcasted_iota(jnp.int32, (tq, tq), 0)
              <= lax.broadcasted_iota(jnp.int32, (tq, tq), 1))

    for n in range(n_blk):
        @pl.when(qi == n)
        def _(n=n):
            s = _dot_nt(kr_ref[0:(n + 1) * tq], q_bf)
            parts = []
            if n > MOBA_TOPK:
                gate = _dot3_nt(km_ref[...], qr)
                past = lax.broadcasted_iota(jnp.int32, (n_blk, tq), 0) < n
                gate = jnp.where(past, gate, -jnp.inf)
                drop = jnp.where(_rank_t(gate, n) < MOBA_TOPK, 0.0, NEG_BIG)
                for j in range(n):
                    parts.append((s[j * tq:(j + 1) * tq] + drop[j:j + 1, :], None))
            elif n > 0:
                parts.append((s[0:n * tq], None))
            parts.append((s[n * tq:], causal))
            e, l = _softmax_parts(parts)
            o_t = _dot_nn(vt_ref[:, 0:(n + 1) * tq], e) / l
            o_ref[0] = o_t.T.astype(o_ref.dtype)


def _moba(z3, cos, sin_s, tq=MOBA_BLOCK):
    b, s, _ = z3.shape
    n_blk = s // MOBA_BLOCK
    qb, kb, vb = OFF_AQ // LANES, OFF_AK // LANES, OFF_AV // LANES
    return pl.pallas_call(
        functools.partial(_moba_kernel, n_blk=n_blk, scale=HEAD_DIM ** -0.5),
        grid=(b, MOBA_HEADS, s // tq),
        in_specs=[
            pl.BlockSpec((1, tq, LANES), lambda bi, h, qi: (bi, qi, qb + h)),
            pl.BlockSpec((1, s, LANES), lambda bi, h, qi: (bi, 0, kb + h)),
            pl.BlockSpec((1, s, LANES), lambda bi, h, qi: (bi, 0, vb + h)),
            pl.BlockSpec((s, LANES), lambda bi, h, qi: (0, 0)),
            pl.BlockSpec((s, LANES), lambda bi, h, qi: (0, 0)),
            pl.BlockSpec((tq, LANES), lambda bi, h, qi: (qi, 0)),
            pl.BlockSpec((tq, LANES), lambda bi, h, qi: (qi, 0)),
        ],
        out_specs=pl.BlockSpec((1, tq, LANES), lambda bi, h, qi: (bi, qi, h)),
        out_shape=jax.ShapeDtypeStruct((b, s, MOBA_HEADS * HEAD_DIM), BF16),
        scratch_shapes=[pltpu.VMEM((s, LANES), BF16), pltpu.VMEM((n_blk, LANES), F32),
                        pltpu.VMEM((LANES, s), BF16)],
        compiler_params=_cparams(("parallel", "parallel", "arbitrary")),
        name="moba",
    )(z3, z3, z3, cos, sin_s, cos, sin_s)


def _mla_kernel(qn_ref, qr_ref, kn_ref, v_ref, kr_ref, cosf_ref, sinf_ref, cosq_ref, sinq_ref, o_ref,
                kf_ref, vt_ref, *, scale):
    qi = pl.program_id(2)
    tq = qn_ref.shape[1]
    s_len = kn_ref.shape[1]

    @pl.when(qi == 0)
    def _():
        kf_ref[:, :LANES] = kn_ref[0].astype(BF16)
        kf_ref[:, LANES:] = _rope_half(kr_ref[0], cosf_ref[...], sinf_ref[...]).astype(BF16)
        vt_ref[...] = v_ref[0].T.astype(BF16)

    q = jnp.concatenate([qn_ref[0], _rope_half(qr_ref[0], cosq_ref[...], sinq_ref[...])], axis=1)
    q_bf = (q * (scale * LOG2E)).astype(BF16)
    causal = (lax.broadcasted_iota(jnp.int32, (tq, tq), 0)
              <= lax.broadcasted_iota(jnp.int32, (tq, tq), 1))

    for n in range(s_len // tq):
        @pl.when(qi == n)
        def _(n=n):
            s = _dot_nt(kf_ref[0:(n + 1) * tq], q_bf)
            parts = [(s[0:n * tq], None)] if n > 0 else []
            parts.append((s[n * tq:], causal))
            e, l = _softmax_parts(parts)
            o_t = _dot_nn(vt_ref[:, 0:(n + 1) * tq], e) / l
            o_ref[0] = o_t.T.astype(o_ref.dtype)


def _mla(q3, kv3, z3, cos, sin_s, tq=256):
    b, s, _ = q3.shape
    krb = OFF_BKR // LANES
    return pl.pallas_call(
        functools.partial(_mla_kernel, scale=(MLA_NOPE + MLA_ROPE) ** -0.5),
        grid=(b, MLA_HEADS, s // tq),
        in_specs=[
            pl.BlockSpec((1, tq, LANES), lambda bi, h, qi: (bi, qi, h)),
            pl.BlockSpec((1, tq, LANES), lambda bi, h, qi: (bi, qi, MLA_HEADS + h)),
            pl.BlockSpec((1, s, LANES), lambda bi, h, qi: (bi, 0, h)),
            pl.BlockSpec((1, s, LANES), lambda bi, h, qi: (bi, 0, MLA_HEADS + h)),
            pl.BlockSpec((1, s, LANES), lambda bi, h, qi: (bi, 0, krb)),
            pl.BlockSpec((s, LANES), lambda bi, h, qi: (0, 0)),
            pl.BlockSpec((s, LANES), lambda bi, h, qi: (0, 0)),
            pl.BlockSpec((tq, LANES), lambda bi, h, qi: (qi, 0)),
            pl.BlockSpec((tq, LANES), lambda bi, h, qi: (qi, 0)),
        ],
        out_specs=pl.BlockSpec((1, tq, LANES), lambda bi, h, qi: (bi, qi, h)),
        out_shape=jax.ShapeDtypeStruct((b, s, MLA_HEADS * MLA_V), BF16),
        scratch_shapes=[pltpu.VMEM((s, 2 * LANES), BF16), pltpu.VMEM((LANES, s), BF16)],
        compiler_params=_cparams(("parallel", "parallel", "arbitrary")),
        name="mla",
    )(q3, q3, kv3, kv3, z3, cos, sin_s, cos, sin_s)


def _cmp_kernel(x_ref, w1_ref, w2_ref, pos_ref, cos_ref, sin_ref, o_ref):
    kv = pl.program_id(0)
    n16 = o_ref.shape[3]
    x = jnp.concatenate(
        [x_ref[0, pl.ds(l, n16, stride=NSA_CMP_STRIDE), :] for l in range(NSA_CMP_STRIDE)], axis=1)
    half = x.shape[1]
    w1 = w1_ref[0]
    first = _dot3_nn(x, w1[:half])
    second = _dot3_nn(x, w1[half:])
    bias = _dot3_nn(jnp.broadcast_to(pos_ref[0], (8, 2 * half)), w1)[0:1]
    hid = first + pltpu.roll(second, n16 - 1, axis=0) + bias
    y = _dot3_nn(jax.nn.gelu(hid), w2_ref[0])
    roped = _rope_full(y, cos_ref[...], sin_ref[...])
    o_ref[0, 0, 0] = jnp.where(kv == 0, roped, y)


def _nsa_compress(z3, w1, w2, posflat, cos_c, sin_c):
    b, s, _ = z3.shape
    g = NSA_KV_GROUPS
    n16 = s // NSA_CMP_STRIDE
    width = NSA_CMP_STRIDE * HEAD_DIM
    hid = w1.shape[2]
    base = OFF_CKC // LANES
    return pl.pallas_call(
        _cmp_kernel,
        grid=(2, b, g),
        in_specs=[
            pl.BlockSpec((1, s, LANES), lambda kv, bi, gi: (bi, 0, base + g * kv + gi)),
            pl.BlockSpec((1, 2 * width, hid), lambda kv, bi, gi: (kv, 0, 0)),
            pl.BlockSpec((1, hid, HEAD_DIM), lambda kv, bi, gi: (kv, 0, 0)),
            pl.BlockSpec((1, 1, 2 * width), lambda kv, bi, gi: (kv, 0, 0)),
            pl.BlockSpec((n16, HEAD_DIM), lambda kv, bi, gi: (0, 0)),
            pl.BlockSpec((n16, HEAD_DIM), lambda kv, bi, gi: (0, 0)),
        ],
        out_specs=pl.BlockSpec((1, 1, 1, n16, HEAD_DIM), lambda kv, bi, gi: (kv, bi, gi, 0, 0)),
        out_shape=jax.ShapeDtypeStruct((2, b, g, n16, HEAD_DIM), F32),
        compiler_params=_cparams(("parallel", "parallel", "parallel")),
        name="nsa_compress",
    )(z3, w1, w2, posflat, cos_c, sin_c)


def _nsa_kernel(q_ref, ks_ref, vs_ref, kw_ref, vw_ref, kc_ref, vc_ref, gate_ref,
                cosf_ref, sinf_ref, cosq_ref, sinq_ref, ovl_ref, o_ref,
                ksr_ref, vst_ref, kw3_ref, vwt3_ref, vct_ref, os_ref, *, n_sel, scale):
    qi = pl.program_id(2)
    tq = q_ref.shape[1]
    s_len = ks_ref.shape[1]
    rep = NSA_REP
    lanes = rep * tq
    n_chunk = s_len // tq
    pad_chunks = NSA_WINDOW // tq
    n_cmp = kc_ref.shape[3]

    @pl.when(qi == 0)
    def _():
        ksr_ref[...] = _rope_full(ks_ref[0], cosf_ref[...], sinf_ref[...]).astype(BF16)
        vst_ref[...] = vs_ref[0].T.astype(BF16)
        kw3_ref[0:pad_chunks] = jnp.zeros((pad_chunks, tq, HEAD_DIM), BF16)
        vwt3_ref[0:pad_chunks] = jnp.zeros((pad_chunks, HEAD_DIM, tq), BF16)
        kwr = _rope_full(kw_ref[0], cosf_ref[...], sinf_ref[...]).astype(BF16)
        for c in range(n_chunk):
            kw3_ref[pad_chunks + c] = kwr[c * tq:(c + 1) * tq]
            vwt3_ref[pad_chunks + c] = vw_ref[0, c * tq:(c + 1) * tq, :].T.astype(BF16)
        vct_ref[...] = vc_ref[0, 0, 0].T.astype(BF16)

    cos_q = cosq_ref[...]
    sin_q = sinq_ref[...]
    q_all = jnp.concatenate(
        [_rope_full(q_ref[0, :, r * HEAD_DIM:(r + 1) * HEAD_DIM], cos_q, sin_q) for r in range(rep)],
        axis=0)
    q_bf = (q_all * (scale * LOG2E)).astype(BF16)
    lane = lax.broadcasted_iota(jnp.int32, (1, lanes), 1)
    t_loc = lane % tq
    t = qi * tq + t_loc

    s_c = _dot3_nt(kc_ref[0, 0, 0], q_all)
    cmp_end = (lax.broadcasted_iota(jnp.int32, (n_cmp, lanes), 0) * NSA_CMP_STRIDE + (NSA_CMP_LEN - 1))
    e_c, l_c = _softmax_t(s_c, cmp_end <= t, scale)
    p_c = e_c / l_c
    o_c = _dot_nn(vct_ref[...], p_c.astype(BF16))

    p_hi, p_lo = _split(p_c)
    ovl = ovl_ref[...]
    imp3 = _dot_nn(ovl, p_hi) + _dot_nn(ovl, p_lo)
    imp = imp3[:, 0:tq]
    for r in range(1, rep):
        imp = imp + imp3[:, r * tq:(r + 1) * tq]
    tq_pos = t[:, 0:tq]
    cur = tq_pos // NSA_SEL_BLOCK
    blk = lax.broadcasted_iota(jnp.int32, (n_sel, tq), 0)
    forced = (blk == 0) | (blk == cur) | (blk == cur - 1)
    visible = blk <= cur
    imp = jnp.where(forced, FORCE_SCORE, imp)
    imp = jnp.where(visible, imp, -jnp.inf)

    unit = 2 * tq
    sb = NSA_SEL_BLOCK
    for v in range(s_len // unit):
        @pl.when(qi // 2 == v)
        def _(v=v):
            n_keys = (v + 1) * unit
            lo = n_keys - unit
            s = _dot_nt(ksr_ref[0:n_keys], q_bf)

            def causal(start, n):
                return lax.broadcasted_iota(jnp.int32, (n, lanes), 0) <= (t - start)

            n_vis = n_keys // sb
            if n_vis > NSA_SEL_TOPK:
                rank = _rank_t(imp, n_vis)
                rank3 = jnp.concatenate([rank] * rep, axis=1)
                parts = []
                drop = jnp.where(rank3 < NSA_SEL_TOPK, 0.0, NEG_BIG)
                for j in range(n_vis):
                    parts.append((s[j * sb:(j + 1) * sb] + drop[j:j + 1, :],
                                  causal(j * sb, sb) if j * sb >= lo else None))
            else:
                parts = [(s[0:lo], None)] if lo > 0 else []
                parts.append((s[lo:], causal(lo, unit)))
            e_s, l_s = _softmax_parts(parts)
            os_ref[...] = _dot_nn(vst_ref[:, 0:n_keys], e_s) / l_s

    n_win = pad_chunks + 1
    row = lax.broadcasted_iota(jnp.int32, (tq, lanes), 0)
    parts = []
    for c in range(n_win):
        s_w = _dot_nt(kw3_ref[qi + c], q_bf)
        if c < pad_chunks:
            s_w = s_w + jnp.where(qi + c >= pad_chunks, 0.0, NEG_BIG)
        mask = (row > t_loc) if c == 0 else ((row <= t_loc) if c == n_win - 1 else None)
        parts.append((s_w, mask))
    e_w, l_w = _softmax_parts(parts)
    o_w = _dot_nn(vwt3_ref[qi], e_w[0:tq])
    for c in range(1, n_win):
        o_w = o_w + _dot_nn(vwt3_ref[qi + c], e_w[c * tq:(c + 1) * tq])
    o_w = o_w / l_w
    o_s = os_ref[...]

    g_t = jax.nn.sigmoid(gate_ref[0].T)
    def gate_row(branch):
        return jnp.concatenate([g_t[3 * r + branch:3 * r + branch + 1, :] for r in range(rep)], axis=1)
    out_t = gate_row(0) * o_c + gate_row(1) * o_s + gate_row(2) * o_w
    for r in range(rep):
        o_ref[0, :, r * HEAD_DIM:(r + 1) * HEAD_DIM] = out_t[:, r * tq:(r + 1) * tq].T.astype(o_ref.dtype)


def _nsa(z3, cmp_kv, cos, sin_s, ovl_t, tq=128):
    b, s, _ = z3.shape
    g = NSA_KV_GROUPS
    n_sel = s // NSA_SEL_BLOCK
    n_cmp = cmp_kv.shape[3]
    qw = NSA_REP * HEAD_DIM
    n_chunk_pad = (s + NSA_WINDOW) // tq

    def col(off):
        base = off // LANES
        return lambda bi, gi, qi: (bi, 0, base + gi)

    return pl.pallas_call(
        functools.partial(_nsa_kernel, n_sel=n_sel, scale=HEAD_DIM ** -0.5),
        grid=(b, g, s // tq),
        in_specs=[
            pl.BlockSpec((1, tq, qw), lambda bi, gi, qi: (bi, qi, OFF_CQ // qw + gi)),
            pl.BlockSpec((1, s, LANES), col(OFF_CKS)),
            pl.BlockSpec((1, s, LANES), col(OFF_CVS)),
            pl.BlockSpec((1, s, LANES), col(OFF_CKW)),
            pl.BlockSpec((1, s, LANES), col(OFF_CVW)),
            pl.BlockSpec((1, 1, 1, n_cmp, HEAD_DIM), lambda bi, gi, qi: (0, bi, gi, 0, 0)),
            pl.BlockSpec((1, 1, 1, n_cmp, HEAD_DIM), lambda bi, gi, qi: (1, bi, gi, 0, 0)),
            pl.BlockSpec((1, tq, LANES), lambda bi, gi, qi: (bi, qi, OFF_CGATE // LANES + gi)),
            pl.BlockSpec((s, LANES), lambda bi, gi, qi: (0, 0)),
            pl.BlockSpec((s, LANES), lambda bi, gi, qi: (0, 0)),
            pl.BlockSpec((tq, LANES), lambda bi, gi, qi: (qi, 0)),
            pl.BlockSpec((tq, LANES), lambda bi, gi, qi: (qi, 0)),
            pl.BlockSpec((n_sel, n_cmp), lambda bi, gi, qi: (0, 0)),
        ],
        out_specs=pl.BlockSpec((1, tq, qw), lambda bi, gi, qi: (bi, qi, gi)),
        out_shape=jax.ShapeDtypeStruct((b, s, NSA_HEADS * HEAD_DIM), BF16),
        scratch_shapes=[
            pltpu.VMEM((s, HEAD_DIM), BF16),
            pltpu.VMEM((HEAD_DIM, s), BF16),
            pltpu.VMEM((n_chunk_pad, tq, HEAD_DIM), BF16),
            pltpu.VMEM((n_chunk_pad, HEAD_DIM, tq), BF16),
            pltpu.VMEM((HEAD_DIM, n_cmp), BF16),
            pltpu.VMEM((HEAD_DIM, NSA_REP * tq), F32),
        ],
        compiler_params=_cparams(("parallel", "parallel", "arbitrary")),
        name="nsa",
    )(z3, z3, z3, z3, z3, cmp_kv, cmp_kv, z3, cos, sin_s, cos, sin_s, ovl_t)


def _merge_kernel(ya_ref, yb_ref, yc_ref, ga_ref, gb_ref, gc_ref, wa_ref, wb_ref, wc_ref, o_ref):
    m = jax.nn.sigmoid(ga_ref[...]) * _dot_nn(ya_ref[...], wa_ref[...].astype(BF16))
    m = m + jax.nn.sigmoid(gb_ref[...]) * _dot_nn(yb_ref[...], wb_ref[...].astype(BF16))
    m = m + jax.nn.sigmoid(gc_ref[...]) * _dot_nn(yc_ref[...], wc_ref[...].astype(BF16))
    o_ref[...] = m.astype(o_ref.dtype)


def _merge(ya, yb, yc, z2d, wa, wb, wc, layer, tm=512, tn=512):
    t = ya.shape[0]
    d = wa.shape[2]
    nj = d // tn

    def gate_spec(branch):
        return pl.BlockSpec((tm, tn), lambda i, j: (i, OFF_MERGE // tn + branch * nj + j))

    def y_spec(y):
        return pl.BlockSpec((tm, y.shape[1]), lambda i, j: (i, 0))

    def w_spec(w):
        return pl.BlockSpec((None, w.shape[1], tn), lambda i, j: (layer, 0, j))

    return pl.pallas_call(
        _merge_kernel,
        grid=(t // tm, nj),
        in_specs=[y_spec(ya), y_spec(yb), y_spec(yc), gate_spec(0), gate_spec(1), gate_spec(2),
                  w_spec(wa), w_spec(wb), w_spec(wc)],
        out_specs=pl.BlockSpec((tm, tn), lambda i, j: (i, j)),
        out_shape=jax.ShapeDtypeStruct((t, d), BF16),
        compiler_params=_cparams(("parallel", "parallel")),
        name="merge",
    )(ya, yb, yc, z2d, z2d, z2d, wa, wb, wc)


def _proj_residual_kernel(m_ref, w_ref, x_ref, o_ref):
    o_ref[...] = x_ref[...] + _dot_nn(m_ref[...], w_ref[...].astype(BF16))


def _proj_residual(m, w, x2d, layer, tm=512, tn=512):
    t, k = m.shape
    d = w.shape[2]
    return pl.pallas_call(
        _proj_residual_kernel,
        grid=(t // tm, d // tn),
        in_specs=[
            pl.BlockSpec((tm, k), lambda i, j: (i, 0)),
            pl.BlockSpec((None, k, tn), lambda i, j: (layer, 0, j)),
            pl.BlockSpec((tm, tn), lambda i, j: (i, j)),
        ],
        out_specs=pl.BlockSpec((tm, tn), lambda i, j: (i, j)),
        out_shape=jax.ShapeDtypeStruct((t, d), F32),
        compiler_params=_cparams(("parallel", "parallel")),
        name="out_proj",
    )(m, w, x2d)


def _rope_tables(pos, dim, width):
    inv = 1.0 / (ROPE_THETA ** (jnp.arange(0, dim, 2, dtype=F32) / dim))
    ang = pos.astype(F32)[:, None] * inv[None, :]
    cos = jnp.cos(ang)
    sin = jnp.sin(ang)
    cos = jnp.concatenate([cos, cos], axis=-1)
    sin_s = jnp.concatenate([-sin, sin], axis=-1)
    pad = ((0, 0), (0, width - dim))
    return jnp.pad(cos, pad), jnp.pad(sin_s, pad)


def _repack_kernel(srca_ref, srcb_ref, shift_ref, valid_ref, a_ref, b_ref, o_ref, *, n_src_cols):
    j = pl.program_id(1)
    shift = shift_ref[j]
    valid = valid_ref[j]
    row = lax.broadcasted_iota(jnp.int32, (LANES, LANES), 0)
    col = lax.broadcasted_iota(jnp.int32, (LANES, LANES), 1)
    take_a = jnp.where((row == col + shift) & (col < valid), 1.0, 0.0).astype(BF16)
    take_b = jnp.where((row + LANES == col + shift) & (col < valid), 1.0, 0.0).astype(BF16)
    lane = lax.broadcasted_iota(jnp.int32, (1, LANES), 1)

    def in_bounds(blk_ref, src_blk):
        return jnp.where(src_blk * LANES + lane < n_src_cols, blk_ref[...], 0.0).astype(BF16)

    o_ref[...] = (_dot_nn(in_bounds(a_ref, srca_ref[j]), take_a)
                  + _dot_nn(in_bounds(b_ref, srcb_ref[j]), take_b)).astype(o_ref.dtype)


def _repack(w, pieces, n_out_blocks):
    depth, k, n = w.shape
    n_src_blocks = -(-n // LANES)
    starts, valids = [], []
    for start, width in pieces:
        for off in range(0, width, LANES):
            starts.append(start + off)
            valids.append(min(LANES, width - off))
    starts += [0] * (n_out_blocks - len(starts))
    valids += [0] * (n_out_blocks - len(valids))
    srca = np.asarray(starts, np.int32) // LANES
    srcb = np.minimum(srca + 1, n_src_blocks - 1).astype(np.int32)
    shift = np.asarray(starts, np.int32) % LANES
    grid_spec = pltpu.PrefetchScalarGridSpec(
        num_scalar_prefetch=4,
        grid=(depth, n_out_blocks),
        in_specs=[
            pl.BlockSpec((None, k, LANES), lambda d, j, sa, sb, sh, va: (d, 0, sa[j])),
            pl.BlockSpec((None, k, LANES), lambda d, j, sa, sb, sh, va: (d, 0, sb[j])),
        ],
        out_specs=pl.BlockSpec((None, k, LANES), lambda d, j, sa, sb, sh, va: (d, 0, j)),
    )
    return pl.pallas_call(
        functools.partial(_repack_kernel, n_src_cols=n),
        grid_spec=grid_spec,
        out_shape=jax.ShapeDtypeStruct((depth, k, n_out_blocks * LANES), BF16),
        compiler_params=_cparams(("parallel", "parallel")),
        name="repack",
    )(jnp.asarray(srca), jnp.asarray(srcb), jnp.asarray(shift), jnp.asarray(np.asarray(valids, np.int32)), w, w)


def _pack_w_in(w):
    sizes = [MOBA_HEADS * HEAD_DIM] * 3 + [MLA_Q_RANK, MLA_KV_RANK, MLA_ROPE, NSA_HEADS * HEAD_DIM] \
        + [NSA_KV_GROUPS * HEAD_DIM] * 6
    offs = [int(v) for v in np.concatenate([[0], np.cumsum(sizes)])]
    (a_q, a_k, a_v, b_cq, b_ckv, b_kr, c_q, c_kc, c_vc, c_ks, c_vs, c_kw, c_vw) = [
        (offs[i], sizes[i]) for i in range(len(sizes))]
    gate0 = offs[-1]
    per_group = NSA_REP * 3
    gates = [(gate0 + gi * per_group, per_group) for gi in range(NSA_KV_GROUPS)]
    merge = (gate0 + NSA_HEADS * 3, w.shape[2] - gate0 - NSA_HEADS * 3)
    pieces = [merge, c_q, b_ckv, b_cq, a_q, a_k, a_v, c_kc, c_vc, c_ks, c_vs, c_kw, c_vw, b_kr] + gates
    return _repack(w, pieces, Z_COLS // LANES)


def _pack_w_uq(w):
    per_head = MLA_NOPE + MLA_ROPE
    pieces = [(h * per_head, MLA_NOPE) for h in range(MLA_HEADS)] \
        + [(h * per_head + MLA_NOPE, MLA_ROPE) for h in range(MLA_HEADS)]
    return _repack(w, pieces, 2 * MLA_HEADS)


def _pack_w_ukv(w):
    per_head = MLA_NOPE + MLA_V
    pieces = [(h * per_head, MLA_NOPE) for h in range(MLA_HEADS)] \
        + [(h * per_head + MLA_NOPE, MLA_V) for h in range(MLA_HEADS)]
    return _repack(w, pieces, 2 * MLA_HEADS)


def kernel(x, ffn1_norm, ffn1_w_gate, ffn1_w_up, ffn1_w_down, mix_norm, w_in, mla_q_norm, mla_w_uq, mla_kv_norm, mla_w_ukv, nsa_cmp_pos_k, nsa_cmp_w1_k, nsa_cmp_w2_k, nsa_cmp_pos_v, nsa_cmp_w1_v, nsa_cmp_w2_v, w_branch_moba, w_branch_mla, w_branch_nsa, w_out, ffn2_norm, ffn2_w_gate, ffn2_w_up, ffn2_w_down, final_norm):
    b, s, d = x.shape
    depth = w_in.shape[0]
    t = b * s
    assert NSA_CMP_LEN == 2 * NSA_CMP_STRIDE and s % MOBA_BLOCK == 0 and s % NSA_SEL_BLOCK == 0

    pos = jnp.arange(s)
    cos_h, sin_h = _rope_tables(pos, HEAD_DIM, LANES)
    cos_r, sin_r = _rope_tables(pos, MLA_ROPE, LANES)
    n16 = s // NSA_CMP_STRIDE
    cmp_end = np.arange(n16) * NSA_CMP_STRIDE + NSA_CMP_LEN - 1
    cos_c, sin_c = _rope_tables(jnp.asarray(cmp_end), HEAD_DIM, LANES)
    n_sel = s // NSA_SEL_BLOCK
    sel_start = np.arange(n_sel) * NSA_SEL_BLOCK
    cmp_start = np.arange(n16) * NSA_CMP_STRIDE
    ovl_t = ((cmp_start[None, :] <= sel_start[:, None] + NSA_SEL_BLOCK - 1)
             & (cmp_end[None, :] >= sel_start[:, None])).astype(np.float32)
    ovl_t = jnp.asarray(ovl_t, BF16)

    w_in_p = _pack_w_in(w_in)
    w_uq_p = _pack_w_uq(mla_w_uq)
    w_ukv_p = _pack_w_ukv(mla_w_ukv)

    xf = x.reshape(t, d)
    for i in range(depth):
        xf = _ffn(xf, ffn1_norm[i], ffn1_w_gate, ffn1_w_up, ffn1_w_down, i)

        z = _norm_matmul(xf, 0, mix_norm[i], w_in_p, i, tm=1024, tn=512, name="in_proj")
        z3 = z.reshape(b, s, Z_COLS)
        q_mla = _norm_matmul(z, OFF_BCQ // MLA_Q_RANK, mla_q_norm[i], w_uq_p, i,
                             tm=1024, tn=512, name="mla_q_proj")
        kv_mla = _norm_matmul(z, OFF_BCKV // MLA_KV_RANK, mla_kv_norm[i], w_ukv_p, i,
                              tm=1024, tn=512, name="mla_kv_proj")

        y_a = _moba(z3, cos_h, sin_h)
        y_b = _mla(q_mla.reshape(b, s, -1), kv_mla.reshape(b, s, -1), z3, cos_r, sin_r)

        w1 =jnp.stack([nsa_cmp_w1_k[i], nsa_cmp_w1_v[i]])
        w2 = jnp.stack([nsa_cmp_w2_k[i], nsa_cmp_w2_v[i]])
        posflat = jnp.stack([nsa_cmp_pos_k[i].reshape(1, -1), nsa_cmp_pos_v[i].reshape(1, -1)])
        cmp_kv = _nsa_compress(z3, w1, w2, posflat, cos_c, sin_c)
        y_c = _nsa(z3, cmp_kv, cos_h, sin_h, ovl_t)

        merged = _merge(y_a.reshape(t, -1), y_b.reshape(t, -1), y_c.reshape(t, -1), z,
                        w_branch_moba, w_branch_mla, w_branch_nsa, i)
        xf = _proj_residual(merged, w_out, xf, i)

        xf = _ffn(xf, ffn2_norm[i], ffn2_w_gate, ffn2_w_up, ffn2_w_down, i,
                  final_g=final_norm if i == depth - 1 else None)
    return xf.reshape(b, s, d)
```

```python
import functools

import numpy as np
import jax
import jax.numpy as jnp
from jax import lax
from jax.experimental import pallas as pl
from jax.experimental.pallas import tpu as pltpu

F32 = jnp.float32
BF16 = jnp.bfloat16

HEAD_DIM = 128
ROPE_THETA = 10000.0
NORM_EPS = 1e-6
NEG_BIG = -1e30
TINY = 1e-30
LOG2E = 1.4426950408889634
N_BRANCHES = 3

MOBA_HEADS = 4
MOBA_BLOCK = 256
MOBA_TOPK = 3

MLA_HEADS = 6
MLA_Q_RANK = 512
MLA_KV_RANK = 256
MLA_NOPE = 128
MLA_ROPE = 64
MLA_V = 128

NSA_HEADS = 6
NSA_KV_GROUPS = 2
NSA_REP = NSA_HEADS // NSA_KV_GROUPS
NSA_CMP_LEN = 32
NSA_CMP_STRIDE = 16
NSA_CMP_HIDDEN = 256
NSA_SEL_BLOCK = 64
NSA_SEL_TOPK = 16
NSA_WINDOW = 512
FORCE_SCORE = 1e9

LANES = 128
VMEM_LIMIT = 56 * 2**20

OFF_CQ = 0
OFF_BCKV = 768
OFF_BCQ = 1024
OFF_AQ = 1536
OFF_AK = 2048
OFF_AV = 2560
OFF_CKC = 3072
OFF_CVC = 3328
OFF_CKS = 3584
OFF_CVS = 3840
OFF_CKW = 4096
OFF_CVW = 4352
OFF_BKR = 4608
OFF_CGATE = 4736
Z_COLS = 5120


def _cparams(sem):
    return pltpu.CompilerParams(dimension_semantics=sem, vmem_limit_bytes=VMEM_LIMIT)


def _rmsnorm(x, g):
    ms = jnp.mean(x * x, axis=-1, keepdims=True)
    return x * lax.rsqrt(ms + NORM_EPS) * g


def _rope_full(x, cos, sin_signed):
    return x * cos + pltpu.roll(x, HEAD_DIM // 2, axis=1) * sin_signed


def _rope_half(x, cos, sin_signed):
    lane = lax.broadcasted_iota(jnp.int32, x.shape, 1)
    q = MLA_ROPE // 2
    swapped = jnp.where(lane < q, pltpu.roll(x, LANES - q, axis=1), pltpu.roll(x, q, axis=1))
    return x * cos + swapped * sin_signed


def _dot_nt(a, b):
    return lax.dot_general(a, b, (((1,), (1,)), ((), ())), preferred_element_type=F32)


def _dot_nn(a, b):
    return jnp.dot(a, b, preferred_element_type=F32)


def _split(a):
    hi = a.astype(BF16)
    lo = (a - hi.astype(F32)).astype(BF16)
    return hi, lo


def _dot3_nt(a, b):
    ah, al = _split(a)
    bh, bl = _split(b)
    return _dot_nt(ah, bh) + _dot_nt(ah, bl) + _dot_nt(al, bh)


def _dot3_nn(a, b):
    ah, al = _split(a)
    bh, bl = _split(b)
    return _dot_nn(ah, bh) + _dot_nn(ah, bl) + _dot_nn(al, bh)


def _softmax_t(s, mask, scale):
    s = jnp.where(mask, s * scale, NEG_BIG)
    m = jnp.max(s, axis=0, keepdims=True)
    e = jnp.where(mask, jnp.exp(s - m), 0.0)
    l = jnp.maximum(jnp.sum(e, axis=0, keepdims=True), TINY)
    return e, l


def _softmax_parts(parts):
    masked = [s if m is None else jnp.where(m, s, NEG_BIG) for s, m in parts]
    mx = functools.reduce(jnp.maximum, [jnp.max(p, axis=0, keepdims=True) for p in masked])
    es = [jnp.exp2(p - mx) for p in masked]
    l = functools.reduce(jnp.add, [jnp.sum(e, axis=0, keepdims=True) for e in es])
    e = jnp.concatenate([e.astype(BF16) for e in es], axis=0) if len(es) > 1 else es[0].astype(BF16)
    return e, jnp.maximum(l, TINY)


def _rank_t(v, nb):
    blk = lax.broadcasted_iota(jnp.int32, v.shape, 0)
    rank = jnp.zeros(v.shape, F32)
    for j in range(nb):
        vj = v[j:j + 1, :]
        ge = (vj >= v).astype(F32)
        gt = (vj > v).astype(F32)
        rank = rank + jnp.where(blk > j, ge, gt)
    return rank


def _rms_kernel(x_ref, g_ref, o_ref):
    o_ref[...] = _rmsnorm(x_ref[...], g_ref[...]).astype(o_ref.dtype)


def _rms(x2d, g, out_dtype, tm=512):
    t, d = x2d.shape
    return pl.pallas_call(
        _rms_kernel,
        grid=(t // tm,),
        in_specs=[pl.BlockSpec((tm, d), lambda i: (i, 0)), pl.BlockSpec((1, d), lambda i: (0, 0))],
        out_specs=pl.BlockSpec((tm, d), lambda i: (i, 0)),
        out_shape=jax.ShapeDtypeStruct((t, d), out_dtype),
        compiler_params=_cparams(("parallel",)),
        name="rmsnorm",
    )(x2d, g.reshape(1, d))


def _ffn_up_kernel(h_ref, wg_ref, wu_ref, o_ref, wg_bf_ref, wu_bf_ref):
    @pl.when(pl.program_id(1) == 0)
    def _():
        wg_bf_ref[...] = wg_ref[...].astype(BF16)
        wu_bf_ref[...] = wu_ref[...].astype(BF16)

    h = h_ref[...]
    a = _dot_nn(h, wg_bf_ref[...])
    u = _dot_nn(h, wu_bf_ref[...])
    o_ref[...] = (jax.nn.silu(a) * u).astype(o_ref.dtype)


def _ffn_down_kernel(a_ref, wd_ref, x_ref, o_ref, wd_bf_ref):
    @pl.when(pl.program_id(1) == 0)
    def _():
        wd_bf_ref[...] = wd_ref[...].astype(BF16)

    o_ref[...] = x_ref[...] + 0.5 * _dot_nn(a_ref[...], wd_bf_ref[...])


def _ffn(x2d, g, wg, wu, wd, layer, tm=1024, tf=512, tn=512):
    t, d = x2d.shape
    f = wg.shape[2]
    h = _rms(x2d, g, BF16)
    act = pl.pallas_call(
        _ffn_up_kernel,
        grid=(f // tf, t // tm),
        in_specs=[
            pl.BlockSpec((tm, d), lambda j, i: (i, 0)),
            pl.BlockSpec((None, d, tf), lambda j, i: (layer, 0, j)),
            pl.BlockSpec((None, d, tf), lambda j, i: (layer, 0, j)),
        ],
        out_specs=pl.BlockSpec((tm, tf), lambda j, i: (i, j)),
        out_shape=jax.ShapeDtypeStruct((t, f), BF16),
        scratch_shapes=[pltpu.VMEM((d, tf), BF16), pltpu.VMEM((d, tf), BF16)],
        compiler_params=_cparams(("parallel", "arbitrary")),
        name="ffn_up",
    )(h, wg, wu)
    return pl.pallas_call(
        _ffn_down_kernel,
        grid=(d // tn, t // tm),
        in_specs=[
            pl.BlockSpec((tm, f), lambda n, i: (i, 0)),
            pl.BlockSpec((None, f, tn), lambda n, i: (layer, 0, n), pipeline_mode=pl.Buffered(1)),
            pl.BlockSpec((tm, tn), lambda n, i: (i, n)),
        ],
        out_specs=pl.BlockSpec((tm, tn), lambda n, i: (i, n)),
        out_shape=jax.ShapeDtypeStruct((t, d), F32),
        scratch_shapes=[pltpu.VMEM((f, tn), BF16)],
        compiler_params=_cparams(("parallel", "arbitrary")),
        name="ffn_down",
    )(act, wd, x2d)


def _in_proj_kernel(h_ref, w_ref, z_ref, g_ref, *, n_z):
    j = pl.program_id(1)
    y = _dot_nn(h_ref[...], w_ref[...])

    @pl.when(j < n_z)
    def _():
        z_ref[...] = y

    @pl.when(j >= n_z)
    def _():
        g_ref[...] = jax.nn.sigmoid(y).astype(g_ref.dtype)


def _in_proj(h, w, layer, tm=1024, tn=512):
    t, d = h.shape
    n = w.shape[2]
    n_z = Z_COLS // tn
    n_g = (n - Z_COLS) // tn
    return pl.pallas_call(
        functools.partial(_in_proj_kernel, n_z=n_z),
        grid=(t // tm, n_z + n_g),
        in_specs=[
            pl.BlockSpec((tm, d), lambda i, j: (i, 0)),
            pl.BlockSpec((None, d, tn), lambda i, j: (layer, 0, j)),
        ],
        out_specs=[
            pl.BlockSpec((tm, tn), lambda i, j: (i, jnp.minimum(j, n_z - 1))),
            pl.BlockSpec((tm, tn), lambda i, j: (i, jnp.maximum(j - n_z, 0))),
        ],
        out_shape=[jax.ShapeDtypeStruct((t, Z_COLS), F32), jax.ShapeDtypeStruct((t, n - Z_COLS), BF16)],
        compiler_params=_cparams(("parallel", "arbitrary")),
        name="in_proj",
    )(h, w)


def _norm_matmul_kernel(x_ref, g_ref, w_ref, o_ref, h_ref):
    @pl.when(pl.program_id(1) == 0)
    def _():
        h_ref[...] = _rmsnorm(x_ref[...], g_ref[...]).astype(BF16)

    o_ref[...] = _dot_nn(h_ref[...], w_ref[...].astype(BF16)).astype(o_ref.dtype)


def _norm_matmul(x2d, col_blk, g, w, layer, tm, tn, name):
    t = x2d.shape[0]
    _, k, n = w.shape
    return pl.pallas_call(
        _norm_matmul_kernel,
        grid=(t // tm, n // tn),
        in_specs=[
            pl.BlockSpec((tm, k), lambda i, j: (i, col_blk)),
            pl.BlockSpec((1, k), lambda i, j: (0, 0)),
            pl.BlockSpec((None, k, tn), lambda i, j: (layer, 0, j)),
        ],
        out_specs=pl.BlockSpec((tm, tn), lambda i, j: (i, j)),
        out_shape=jax.ShapeDtypeStruct((t, n), F32),
        scratch_shapes=[pltpu.VMEM((tm, k), BF16)],
        compiler_params=_cparams(("parallel", "arbitrary")),
        name=name,
    )(x2d, g.reshape(1, k), w)


def _moba_kernel(q_ref, k_ref, v_ref, cosf_ref, sinf_ref, cosq_ref, sinq_ref, o_ref,
                 kr_ref, km_ref, vt_ref, *, n_blk, scale):
    qi = pl.program_id(2)
    tq = q_ref.shape[1]

    @pl.when(qi == 0)
    def _():
        kr = _rope_full(k_ref[0], cosf_ref[...], sinf_ref[...])
        kr_ref[...] = kr.astype(BF16)
        km_ref[...] = jnp.concatenate(
            [jnp.mean(kr[j * MOBA_BLOCK:(j + 1) * MOBA_BLOCK], axis=0, keepdims=True) for j in range(n_blk)],
            axis=0)
        vt_ref[...] = v_ref[0].T.astype(BF16)

    qr = _rope_full(q_ref[0], cosq_ref[...], sinq_ref[...])
    q_bf = (qr * (scale * LOG2E)).astype(BF16)
    causal = (lax.broadcasted_iota(jnp.int32, (tq, tq), 0)
              <= lax.broadcasted_iota(jnp.int32, (tq, tq), 1))

    for n in range(n_blk):
        @pl.when(qi == n)
        def _(n=n):
            s = _dot_nt(kr_ref[0:(n + 1) * tq], q_bf)
            parts = []
            if n > MOBA_TOPK:
                gate = _dot3_nt(km_ref[...], qr)
                past = lax.broadcasted_iota(jnp.int32, (n_blk, tq), 0) < n
                gate = jnp.where(past, gate, -jnp.inf)
                drop = jnp.where(_rank_t(gate, n) < MOBA_TOPK, 0.0, NEG_BIG)
                for j in range(n):
                    parts.append((s[j * tq:(j + 1) * tq] + drop[j:j + 1, :], None))
            elif n > 0:
                parts.append((s[0:n * tq], None))
            parts.append((s[n * tq:], causal))
            e, l = _softmax_parts(parts)
            o_t = _dot_nn(vt_ref[:, 0:(n + 1) * tq], e) / l
            o_ref[0] = o_t.T.astype(o_ref.dtype)


def _moba(z3, cos, sin_s, tq=MOBA_BLOCK):
    b, s, _ = z3.shape
    n_blk = s // MOBA_BLOCK
    qb, kb, vb = OFF_AQ // LANES, OFF_AK // LANES, OFF_AV // LANES
    return pl.pallas_call(
        functools.partial(_moba_kernel, n_blk=n_blk, scale=HEAD_DIM ** -0.5),
        grid=(b, MOBA_HEADS, s // tq),
        in_specs=[
            pl.BlockSpec((1, tq, LANES), lambda bi, h, qi: (bi, qi, qb + h)),
            pl.BlockSpec((1, s, LANES), lambda bi, h, qi: (bi, 0, kb + h)),
            pl.BlockSpec((1, s, LANES), lambda bi, h, qi: (bi, 0, vb + h)),
            pl.BlockSpec((s, LANES), lambda bi, h, qi: (0, 0)),
            pl.BlockSpec((s, LANES), lambda bi, h, qi: (0, 0)),
            pl.BlockSpec((tq, LANES), lambda bi, h, qi: (qi, 0)),
            pl.BlockSpec((tq, LANES), lambda bi, h, qi: (qi, 0)),
        ],
        out_specs=pl.BlockSpec((1, tq, LANES), lambda bi, h, qi: (bi, qi, h)),
        out_shape=jax.ShapeDtypeStruct((b, s, MOBA_HEADS * HEAD_DIM), BF16),
        scratch_shapes=[pltpu.VMEM((s, LANES), BF16), pltpu.VMEM((n_blk, LANES), F32),
                        pltpu.VMEM((LANES, s), BF16)],
        compiler_params=_cparams(("parallel", "parallel", "arbitrary")),
        name="moba",
    )(z3, z3, z3, cos, sin_s, cos, sin_s)


def _mla_kernel(qn_ref, qr_ref, kn_ref, v_ref, kr_ref, cosf_ref, sinf_ref, cosq_ref, sinq_ref, o_ref,
                kf_ref, vt_ref, *, scale):
    qi = pl.program_id(2)
    tq = qn_ref.shape[1]
    s_len = kn_ref.shape[1]

    @pl.when(qi == 0)
    def _():
        kf_ref[:, :LANES] = kn_ref[0].astype(BF16)
        kf_ref[:, LANES:] = _rope_half(kr_ref[0], cosf_ref[...], sinf_ref[...]).astype(BF16)
        vt_ref[...] = v_ref[0].T.astype(BF16)

    q = jnp.concatenate([qn_ref[0], _rope_half(qr_ref[0], cosq_ref[...], sinq_ref[...])], axis=1)
    q_bf = (q * (scale * LOG2E)).astype(BF16)
    causal = (lax.broadcasted_iota(jnp.int32, (tq, tq), 0)
              <= lax.broadcasted_iota(jnp.int32, (tq, tq), 1))

    for n in range(s_len // tq):
        @pl.when(qi == n)
        def _(n=n):
            s = _dot_nt(kf_ref[0:(n + 1) * tq], q_bf)
            parts = [(s[0:n * tq], None)] if n > 0 else []
            parts.append((s[n * tq:], causal))
            e, l = _softmax_parts(parts)
            o_t = _dot_nn(vt_ref[:, 0:(n + 1) * tq], e) / l
            o_ref[0] = o_t.T.astype(o_ref.dtype)


def _mla(q3, kv3, z3, cos, sin_s, tq=256):
    b, s, _ = q3.shape
    krb = OFF_BKR // LANES
    return pl.pallas_call(
        functools.partial(_mla_kernel, scale=(MLA_NOPE + MLA_ROPE) ** -0.5),
        grid=(b, MLA_HEADS, s // tq),
        in_specs=[
            pl.BlockSpec((1, tq, LANES), lambda bi, h, qi: (bi, qi, h)),
            pl.BlockSpec((1, tq, LANES), lambda bi, h, qi: (bi, qi, MLA_HEADS + h)),
            pl.BlockSpec((1, s, LANES), lambda bi, h, qi: (bi, 0, h)),
            pl.BlockSpec((1, s, LANES), lambda bi, h, qi: (bi, 0, MLA_HEADS + h)),
            pl.BlockSpec((1, s, LANES), lambda bi, h, qi: (bi, 0, krb)),
            pl.BlockSpec((s, LANES), lambda bi, h, qi: (0, 0)),
            pl.BlockSpec((s, LANES), lambda bi, h, qi: (0, 0)),
            pl.BlockSpec((tq, LANES), lambda bi, h, qi: (qi, 0)),
            pl.BlockSpec((tq, LANES), lambda bi, h, qi: (qi, 0)),
        ],
        out_specs=pl.BlockSpec((1, tq, LANES), lambda bi, h, qi: (bi, qi, h)),
        out_shape=jax.ShapeDtypeStruct((b, s, MLA_HEADS * MLA_V), BF16),
        scratch_shapes=[pltpu.VMEM((s, 2 * LANES), BF16), pltpu.VMEM((LANES, s), BF16)],
        compiler_params=_cparams(("parallel", "parallel", "arbitrary")),
        name="mla",
    )(q3, q3, kv3, kv3, z3, cos, sin_s, cos, sin_s)


def _cmp_kernel(x_ref, w1_ref, w2_ref, pos_ref, cos_ref, sin_ref, o_ref):
    kv = pl.program_id(0)
    n16 = o_ref.shape[3]
    x = jnp.concatenate(
        [x_ref[0, pl.ds(l, n16, stride=NSA_CMP_STRIDE), :] for l in range(NSA_CMP_STRIDE)], axis=1)
    half = x.shape[1]
    w1 = w1_ref[0]
    first = _dot3_nn(x, w1[:half])
    second = _dot3_nn(x, w1[half:])
    bias = _dot3_nn(jnp.broadcast_to(pos_ref[0], (8, 2 * half)), w1)[0:1]
    hid = first + pltpu.roll(second, n16 - 1, axis=0) + bias
    y = _dot3_nn(jax.nn.gelu(hid), w2_ref[0])
    roped = _rope_full(y, cos_ref[...], sin_ref[...])
    o_ref[0, 0, 0] = jnp.where(kv == 0, roped, y)


def _nsa_compress(z3, w1, w2, posflat, cos_c, sin_c):
    b, s, _ = z3.shape
    g = NSA_KV_GROUPS
    n16 = s // NSA_CMP_STRIDE
    width = NSA_CMP_STRIDE * HEAD_DIM
    hid = w1.shape[2]
    base = OFF_CKC // LANES
    return pl.pallas_call(
        _cmp_kernel,
        grid=(2, b, g),
        in_specs=[
            pl.BlockSpec((1, s, LANES), lambda kv, bi, gi: (bi, 0, base + g * kv + gi)),
            pl.BlockSpec((1, 2 * width, hid), lambda kv, bi, gi: (kv, 0, 0)),
            pl.BlockSpec((1, hid, HEAD_DIM), lambda kv, bi, gi: (kv, 0, 0)),
            pl.BlockSpec((1, 1, 2 * width), lambda kv, bi, gi: (kv, 0, 0)),
            pl.BlockSpec((n16, HEAD_DIM), lambda kv, bi, gi: (0, 0)),
            pl.BlockSpec((n16, HEAD_DIM), lambda kv, bi, gi: (0, 0)),
        ],
        out_specs=pl.BlockSpec((1, 1, 1, n16, HEAD_DIM), lambda kv, bi, gi: (kv, bi, gi, 0, 0)),
        out_shape=jax.ShapeDtypeStruct((2, b, g, n16, HEAD_DIM), F32),
        compiler_params=_cparams(("parallel", "parallel", "parallel")),
        name="nsa_compress",
    )(z3, w1, w2, posflat, cos_c, sin_c)


def _nsa_kernel(q_ref, ks_ref, vs_ref, kw_ref, vw_ref, kc_ref, vc_ref, gate_ref,
                cosf_ref, sinf_ref, cosq_ref, sinq_ref, ovl_ref, o_ref,
                ksr_ref, vst_ref, kw3_ref, vwt3_ref, vct_ref, os_ref, *, n_sel, scale):
    qi = pl.program_id(2)
    tq = q_ref.shape[1]
    s_len = ks_ref.shape[1]
    rep = NSA_REP
    lanes = rep * tq
    n_chunk = s_len // tq
    pad_chunks = NSA_WINDOW // tq
    n_cmp = kc_ref.shape[3]

    @pl.when(qi == 0)
    def _():
        ksr_ref[...] = _rope_full(ks_ref[0], cosf_ref[...], sinf_ref[...]).astype(BF16)
        vst_ref[...] = vs_ref[0].T.astype(BF16)
        kw3_ref[0:pad_chunks] = jnp.zeros((pad_chunks, tq, HEAD_DIM), BF16)
        vwt3_ref[0:pad_chunks] = jnp.zeros((pad_chunks, HEAD_DIM, tq), BF16)
        kwr = _rope_full(kw_ref[0], cosf_ref[...], sinf_ref[...]).astype(BF16)
        for c in range(n_chunk):
            kw3_ref[pad_chunks + c] = kwr[c * tq:(c + 1) * tq]
            vwt3_ref[pad_chunks + c] = vw_ref[0, c * tq:(c + 1) * tq, :].T.astype(BF16)
        vct_ref[...] = vc_ref[0, 0, 0].T.astype(BF16)

    cos_q = cosq_ref[...]
    sin_q = sinq_ref[...]
    q_all = jnp.concatenate(
        [_rope_full(q_ref[0, :, r * HEAD_DIM:(r + 1) * HEAD_DIM], cos_q, sin_q) for r in range(rep)],
        axis=0)
    q_bf = (q_all * (scale * LOG2E)).astype(BF16)
    lane = lax.broadcasted_iota(jnp.int32, (1, lanes), 1)
    t_loc = lane % tq
    t = qi * tq + t_loc

    s_c = _dot3_nt(kc_ref[0, 0, 0], q_all)
    cmp_end = (lax.broadcasted_iota(jnp.int32, (n_cmp, lanes), 0) * NSA_CMP_STRIDE + (NSA_CMP_LEN - 1))
    e_c, l_c = _softmax_t(s_c, cmp_end <= t, scale)
    p_c = e_c / l_c
    o_c = _dot_nn(vct_ref[...], p_c.astype(BF16))

    p_hi, p_lo = _split(p_c)
    ovl = ovl_ref[...]
    imp3 = _dot_nn(ovl, p_hi) + _dot_nn(ovl, p_lo)
    imp = imp3[:, 0:tq]
    for r in range(1, rep):
        imp = imp + imp3[:, r * tq:(r + 1) * tq]
    tq_pos = t[:, 0:tq]
    cur = tq_pos // NSA_SEL_BLOCK
    blk = lax.broadcasted_iota(jnp.int32, (n_sel, tq), 0)
    forced = (blk == 0) | (blk == cur) | (blk == cur - 1)
    visible = blk <= cur
    imp = jnp.where(forced, FORCE_SCORE, imp)
    imp = jnp.where(visible, imp, -jnp.inf)

    unit = 2 * tq
    sb = NSA_SEL_BLOCK
    for v in range(s_len // unit):
        @pl.when(qi // 2 == v)
        def _(v=v):
            n_keys = (v + 1) * unit
            lo = n_keys - unit
            s = _dot_nt(ksr_ref[0:n_keys], q_bf)

            def causal(start, n):
                return lax.broadcasted_iota(jnp.int32, (n, lanes), 0) <= (t - start)

            n_vis = n_keys // sb
            if n_vis > NSA_SEL_TOPK:
                rank = _rank_t(imp, n_vis)
                rank3 = jnp.concatenate([rank] * rep, axis=1)
                parts = []
                drop = jnp.where(rank3 < NSA_SEL_TOPK, 0.0, NEG_BIG)
                for j in range(n_vis):
                    parts.append((s[j * sb:(j + 1) * sb] + drop[j:j + 1, :],
                                  causal(j * sb, sb) if j * sb >= lo else None))
            else:
                parts = [(s[0:lo], None)] if lo > 0 else []
                parts.append((s[lo:], causal(lo, unit)))
            e_s, l_s = _softmax_parts(parts)
            os_ref[...] = _dot_nn(vst_ref[:, 0:n_keys], e_s) / l_s

    n_win = pad_chunks + 1
    row = lax.broadcasted_iota(jnp.int32, (tq, lanes), 0)
    parts = []
    for c in range(n_win):
        s_w = _dot_nt(kw3_ref[qi + c], q_bf)
        if c < pad_chunks:
            s_w = s_w + jnp.where(qi + c >= pad_chunks, 0.0, NEG_BIG)
        mask = (row > t_loc) if c == 0 else ((row <= t_loc) if c == n_win - 1 else None)
        parts.append((s_w, mask))
    e_w, l_w = _softmax_parts(parts)
    o_w = _dot_nn(vwt3_ref[qi], e_w[0:tq])
    for c in range(1, n_win):
        o_w = o_w + _dot_nn(vwt3_ref[qi + c], e_w[c * tq:(c + 1) * tq])
    o_w = o_w / l_w
    o_s = os_ref[...]

    g_t = jax.nn.sigmoid(gate_ref[0].T)
    def gate_row(branch):
        return jnp.concatenate([g_t[3 * r + branch:3 * r + branch + 1, :] for r in range(rep)], axis=1)
    out_t = gate_row(0) * o_c + gate_row(1) * o_s + gate_row(2) * o_w
    for r in range(rep):
        o_ref[0, :, r * HEAD_DIM:(r + 1) * HEAD_DIM] = out_t[:, r * tq:(r + 1) * tq].T.astype(o_ref.dtype)


def _nsa(z3, cmp_kv, cos, sin_s, ovl_t, tq=128):
    b, s, _ = z3.shape
    g = NSA_KV_GROUPS
    n_sel = s // NSA_SEL_BLOCK
    n_cmp = cmp_kv.shape[3]
    qw = NSA_REP * HEAD_DIM
    n_chunk_pad = (s + NSA_WINDOW) // tq

    def col(off):
        base = off // LANES
        return lambda bi, gi, qi: (bi, 0, base + gi)

    return pl.pallas_call(
        functools.partial(_nsa_kernel, n_sel=n_sel, scale=HEAD_DIM ** -0.5),
        grid=(b, g, s // tq),
        in_specs=[
            pl.BlockSpec((1, tq, qw), lambda bi, gi, qi: (bi, qi, OFF_CQ // qw + gi)),
            pl.BlockSpec((1, s, LANES), col(OFF_CKS)),
            pl.BlockSpec((1, s, LANES), col(OFF_CVS)),
            pl.BlockSpec((1, s, LANES), col(OFF_CKW)),
            pl.BlockSpec((1, s, LANES), col(OFF_CVW)),
            pl.BlockSpec((1, 1, 1, n_cmp, HEAD_DIM), lambda bi, gi, qi: (0, bi, gi, 0, 0)),
            pl.BlockSpec((1, 1, 1, n_cmp, HEAD_DIM), lambda bi, gi, qi: (1, bi, gi, 0, 0)),
            pl.BlockSpec((1, tq, LANES), lambda bi, gi, qi: (bi, qi, OFF_CGATE // LANES + gi)),
            pl.BlockSpec((s, LANES), lambda bi, gi, qi: (0, 0)),
            pl.BlockSpec((s, LANES), lambda bi, gi, qi: (0, 0)),
            pl.BlockSpec((tq, LANES), lambda bi, gi, qi: (qi, 0)),
            pl.BlockSpec((tq, LANES), lambda bi, gi, qi: (qi, 0)),
            pl.BlockSpec((n_sel, n_cmp), lambda bi, gi, qi: (0, 0)),
        ],
        out_specs=pl.BlockSpec((1, tq, qw), lambda bi, gi, qi: (bi, qi, gi)),
        out_shape=jax.ShapeDtypeStruct((b, s, NSA_HEADS * HEAD_DIM), BF16),
        scratch_shapes=[
            pltpu.VMEM((s, HEAD_DIM), BF16),
            pltpu.VMEM((HEAD_DIM, s), BF16),
            pltpu.VMEM((n_chunk_pad, tq, HEAD_DIM), BF16),
            pltpu.VMEM((n_chunk_pad, HEAD_DIM, tq), BF16),
            pltpu.VMEM((HEAD_DIM, n_cmp), BF16),
            pltpu.VMEM((HEAD_DIM, NSA_REP * tq), F32),
        ],
        compiler_params=_cparams(("parallel", "parallel", "arbitrary")),
        name="nsa",
    )(z3, z3, z3, z3, z3, cmp_kv, cmp_kv, z3, cos, sin_s, cos, sin_s, ovl_t)


def _merge_kernel(ya_ref, yb_ref, yc_ref, ga_ref, gb_ref, gc_ref, wa_ref, wb_ref, wc_ref, o_ref):
    m = ga_ref[...].astype(F32) * _dot_nn(ya_ref[...], wa_ref[...].astype(BF16))
    m = m + gb_ref[...].astype(F32) * _dot_nn(yb_ref[...], wb_ref[...].astype(BF16))
    m = m + gc_ref[...].astype(F32) * _dot_nn(yc_ref[...], wc_ref[...].astype(BF16))
    o_ref[...] = m.astype(o_ref.dtype)


def _merge(ya, yb, yc, gates, wa, wb, wc, layer, tm=512, tn=512):
    t = ya.shape[0]
    d = wa.shape[2]
    nj = d // tn

    def gate_spec(branch):
        return pl.BlockSpec((tm, tn), lambda i, j: (i, branch * nj + j))

    def y_spec(y):
        return pl.BlockSpec((tm, y.shape[1]), lambda i, j: (i, 0))

    def w_spec(w):
        return pl.BlockSpec((None, w.shape[1], tn), lambda i, j: (layer, 0, j))

    return pl.pallas_call(
        _merge_kernel,
        grid=(t // tm, nj),
        in_specs=[y_spec(ya), y_spec(yb), y_spec(yc), gate_spec(0), gate_spec(1), gate_spec(2),
                  w_spec(wa), w_spec(wb), w_spec(wc)],
        out_specs=pl.BlockSpec((tm, tn), lambda i, j: (i, j)),
        out_shape=jax.ShapeDtypeStruct((t, d), BF16),
        compiler_params=_cparams(("parallel", "parallel")),
        name="merge",
    )(ya, yb, yc, gates, gates, gates, wa, wb, wc)


def _proj_residual_kernel(m_ref, w_ref, x_ref, o_ref):
    o_ref[...] = x_ref[...] + _dot_nn(m_ref[...], w_ref[...].astype(BF16))


def _proj_residual(m, w, x2d, layer, tm=512, tn=512):
    t, k = m.shape
    d = w.shape[2]
    return pl.pallas_call(
        _proj_residual_kernel,
        grid=(t // tm, d // tn),
        in_specs=[
            pl.BlockSpec((tm, k), lambda i, j: (i, 0)),
            pl.BlockSpec((None, k, tn), lambda i, j: (layer, 0, j)),
            pl.BlockSpec((tm, tn), lambda i, j: (i, j)),
        ],
        out_specs=pl.BlockSpec((tm, tn), lambda i, j: (i, j)),
        out_shape=jax.ShapeDtypeStruct((t, d), F32),
        compiler_params=_cparams(("parallel", "parallel")),
        name="out_proj",
    )(m, w, x2d)


def _rope_tables(pos, dim, width):
    inv = 1.0 / (ROPE_THETA ** (jnp.arange(0, dim, 2, dtype=F32) / dim))
    ang = pos.astype(F32)[:, None] * inv[None, :]
    cos = jnp.cos(ang)
    sin = jnp.sin(ang)
    cos = jnp.concatenate([cos, cos], axis=-1)
    sin_s = jnp.concatenate([-sin, sin], axis=-1)
    pad = ((0, 0), (0, width - dim))
    return jnp.pad(cos, pad), jnp.pad(sin_s, pad)


def _repack_kernel(srca_ref, srcb_ref, shift_ref, valid_ref, a_ref, b_ref, o_ref, *, n_src_cols):
    j = pl.program_id(1)
    shift = shift_ref[j]
    valid = valid_ref[j]
    row = lax.broadcasted_iota(jnp.int32, (LANES, LANES), 0)
    col = lax.broadcasted_iota(jnp.int32, (LANES, LANES), 1)
    take_a = jnp.where((row == col + shift) & (col < valid), 1.0, 0.0).astype(BF16)
    take_b = jnp.where((row + LANES == col + shift) & (col < valid), 1.0, 0.0).astype(BF16)
    lane = lax.broadcasted_iota(jnp.int32, (1, LANES), 1)

    def in_bounds(blk_ref, src_blk):
        return jnp.where(src_blk * LANES + lane < n_src_cols, blk_ref[...], 0.0).astype(BF16)

    o_ref[...] = (_dot_nn(in_bounds(a_ref, srca_ref[j]), take_a)
                  + _dot_nn(in_bounds(b_ref, srcb_ref[j]), take_b)).astype(o_ref.dtype)


def _repack(w, pieces, n_out_blocks):
    depth, k, n = w.shape
    n_src_blocks = -(-n // LANES)
    starts, valids = [], []
    for start, width in pieces:
        if width == 0:
            starts.append(0)
            valids.append(0)
        for off in range(0, width, LANES):
            starts.append(start + off)
            valids.append(min(LANES, width - off))
    starts += [0] * (n_out_blocks - len(starts))
    valids += [0] * (n_out_blocks - len(valids))
    srca = np.asarray(starts, np.int32) // LANES
    srcb = np.minimum(srca + 1, n_src_blocks - 1).astype(np.int32)
    shift = np.asarray(starts, np.int32) % LANES
    grid_spec = pltpu.PrefetchScalarGridSpec(
        num_scalar_prefetch=4,
        grid=(depth, n_out_blocks),
        in_specs=[
            pl.BlockSpec((None, k, LANES), lambda d, j, sa, sb, sh, va: (d, 0, sa[j])),
            pl.BlockSpec((None, k, LANES), lambda d, j, sa, sb, sh, va: (d, 0, sb[j])),
        ],
        out_specs=pl.BlockSpec((None, k, LANES), lambda d, j, sa, sb, sh, va: (d, 0, j)),
    )
    return pl.pallas_call(
        functools.partial(_repack_kernel, n_src_cols=n),
        grid_spec=grid_spec,
        out_shape=jax.ShapeDtypeStruct((depth, k, n_out_blocks * LANES), BF16),
        compiler_params=_cparams(("parallel", "parallel")),
        name="repack",
    )(jnp.asarray(srca), jnp.asarray(srcb), jnp.asarray(shift), jnp.asarray(np.asarray(valids, np.int32)), w, w)


def _pack_w_in(w):
    sizes = [MOBA_HEADS * HEAD_DIM] * 3 + [MLA_Q_RANK, MLA_KV_RANK, MLA_ROPE, NSA_HEADS * HEAD_DIM] \
        + [NSA_KV_GROUPS * HEAD_DIM] * 6
    offs = [int(v) for v in np.concatenate([[0], np.cumsum(sizes)])]
    (a_q, a_k, a_v, b_cq, b_ckv, b_kr, c_q, c_kc, c_vc, c_ks, c_vs, c_kw, c_vw) = [
        (offs[i], sizes[i]) for i in range(len(sizes))]
    gate0 = offs[-1]
    per_group = NSA_REP * 3
    gates = [(gate0 + gi * per_group, per_group) for gi in range(NSA_KV_GROUPS)]
    merge = (gate0 + NSA_HEADS * 3, w.shape[2] - gate0 - NSA_HEADS * 3)
    pieces = [c_q, b_ckv, b_cq, a_q, a_k, a_v, c_kc, c_vc, c_ks, c_vs, c_kw, c_vw, b_kr] + gates
    used = sum(-(-width // LANES) for _, width in pieces)
    pieces += [(0, 0)] * (Z_COLS // LANES - used) + [merge]
    return _repack(w, pieces, (Z_COLS + merge[1]) // LANES)


def _pack_w_uq(w):
    per_head = MLA_NOPE + MLA_ROPE
    pieces = [(h * per_head, MLA_NOPE) for h in range(MLA_HEADS)] \
        + [(h * per_head + MLA_NOPE, MLA_ROPE) for h in range(MLA_HEADS)]
    return _repack(w, pieces, 2 * MLA_HEADS)


def _pack_w_ukv(w):
    per_head = MLA_NOPE + MLA_V
    pieces = [(h * per_head, MLA_NOPE) for h in range(MLA_HEADS)] \
        + [(h * per_head + MLA_NOPE, MLA_V) for h in range(MLA_HEADS)]
    return _repack(w, pieces, 2 * MLA_HEADS)


def kernel(x, ffn1_norm, ffn1_w_gate, ffn1_w_up, ffn1_w_down, mix_norm, w_in, mla_q_norm, mla_w_uq, mla_kv_norm, mla_w_ukv, nsa_cmp_pos_k, nsa_cmp_w1_k, nsa_cmp_w2_k, nsa_cmp_pos_v, nsa_cmp_w1_v, nsa_cmp_w2_v, w_branch_moba, w_branch_mla, w_branch_nsa, w_out, ffn2_norm, ffn2_w_gate, ffn2_w_up, ffn2_w_down, final_norm):
    b, s, d = x.shape
    depth = w_in.shape[0]
    t = b * s
    assert NSA_CMP_LEN == 2 * NSA_CMP_STRIDE and s % MOBA_BLOCK == 0 and s % NSA_SEL_BLOCK == 0

    pos = jnp.arange(s)
    cos_h, sin_h = _rope_tables(pos, HEAD_DIM, LANES)
    cos_r, sin_r = _rope_tables(pos, MLA_ROPE, LANES)
    n16 = s // NSA_CMP_STRIDE
    cmp_end = np.arange(n16) * NSA_CMP_STRIDE + NSA_CMP_LEN - 1
    cos_c, sin_c = _rope_tables(jnp.asarray(cmp_end), HEAD_DIM, LANES)
    n_sel = s // NSA_SEL_BLOCK
    sel_start = np.arange(n_sel) * NSA_SEL_BLOCK
    cmp_start = np.arange(n16) * NSA_CMP_STRIDE
    ovl_t = ((cmp_start[None, :] <= sel_start[:, None] + NSA_SEL_BLOCK - 1)
             & (cmp_end[None, :] >= sel_start[:, None])).astype(np.float32)
    ovl_t = jnp.asarray(ovl_t, BF16)

    w_in_p = _pack_w_in(w_in)
    w_uq_p = _pack_w_uq(mla_w_uq)
    w_ukv_p = _pack_w_ukv(mla_w_ukv)

    xf = x.reshape(t, d)
    for i in range(depth):
        xf = _ffn(xf, ffn1_norm[i], ffn1_w_gate, ffn1_w_up, ffn1_w_down, i)

        z, gates = _in_proj(_rms(xf, mix_norm[i], BF16), w_in_p, i)
        z3 = z.reshape(b, s, Z_COLS)
        q_mla = _norm_matmul(z, OFF_BCQ // MLA_Q_RANK, mla_q_norm[i], w_uq_p, i,
                             tm=1024, tn=512, name="mla_q_proj")
        kv_mla = _norm_matmul(z, OFF_BCKV // MLA_KV_RANK, mla_kv_norm[i], w_ukv_p, i,
                              tm=1024, tn=512, name="mla_kv_proj")

        y_a = _moba(z3, cos_h, sin_h)
        y_b = _mla(q_mla.reshape(b, s, -1), kv_mla.reshape(b, s, -1), z3, cos_r, sin_r)

        w1 =jnp.stack([nsa_cmp_w1_k[i], nsa_cmp_w1_v[i]])
        w2 = jnp.stack([nsa_cmp_w2_k[i], nsa_cmp_w2_v[i]])
        posflat = jnp.stack([nsa_cmp_pos_k[i].reshape(1, -1), nsa_cmp_pos_v[i].reshape(1, -1)])
        cmp_kv = _nsa_compress(z3, w1, w2, posflat, cos_c, sin_c)
        y_c = _nsa(z3, cmp_kv, cos_h, sin_h, ovl_t)

        merged = _merge(y_a.reshape(t, -1), y_b.reshape(t, -1), y_c.reshape(t, -1), gates,
                        w_branch_moba, w_branch_mla, w_branch_nsa, i)
        xf = _proj_residual(merged, w_out, xf, i)

        xf = _ffn(xf, ffn2_norm[i], ffn2_w_gate, ffn2_w_up, ffn2_w_down, i)
    return _rms(xf, final_norm, F32).reshape(b, s, d)
```

```python
import functools

import numpy as np
import jax
import jax.numpy as jnp
from jax import lax
from jax.experimental import pallas as pl
from jax.experimental.pallas import tpu as pltpu

F32 = jnp.float32
BF16 = jnp.bfloat16

HEAD_DIM = 128
ROPE_THETA = 10000.0
NORM_EPS = 1e-6
NEG_BIG = -1e30
TINY = 1e-30
LOG2E = 1.4426950408889634
N_BRANCHES = 3

MOBA_HEADS = 4
MOBA_BLOCK = 256
MOBA_TOPK = 3

MLA_HEADS = 6
MLA_Q_RANK = 512
MLA_KV_RANK = 256
MLA_NOPE = 128
MLA_ROPE = 64
MLA_V = 128

NSA_HEADS = 6
NSA_KV_GROUPS = 2
NSA_REP = NSA_HEADS // NSA_KV_GROUPS
NSA_CMP_LEN = 32
NSA_CMP_STRIDE = 16
NSA_CMP_HIDDEN = 256
NSA_SEL_BLOCK = 64
NSA_SEL_TOPK = 16
NSA_WINDOW = 512
FORCE_SCORE = 1e9

LANES = 128
VMEM_LIMIT = 56 * 2**20

OFF_CQ = 0
OFF_BCKV = 768
OFF_BCQ = 1024
OFF_AQ = 1536
OFF_AK = 2048
OFF_AV = 2560
OFF_CKC = 3072
OFF_CVC = 3328
OFF_CKS = 3584
OFF_CVS = 3840
OFF_CKW = 4096
OFF_CVW = 4352
OFF_CGATE = 4608
OFF_BKR = 4864
Z_COLS = 5120


def _cparams(sem):
    return pltpu.CompilerParams(dimension_semantics=sem, vmem_limit_bytes=VMEM_LIMIT)


def _rmsnorm(x, g):
    ms = jnp.mean(x * x, axis=-1, keepdims=True)
    return x * lax.rsqrt(ms + NORM_EPS) * g


def _rope_full(x, cos, sin_signed):
    return x * cos + pltpu.roll(x, HEAD_DIM // 2, axis=1) * sin_signed


def _rope_half(x, cos, sin_signed):
    lane = lax.broadcasted_iota(jnp.int32, x.shape, 1)
    q = MLA_ROPE // 2
    swapped = jnp.where(lane < q, pltpu.roll(x, LANES - q, axis=1), pltpu.roll(x, q, axis=1))
    return x * cos + swapped * sin_signed


def _dot_nt(a, b):
    return lax.dot_general(a, b, (((1,), (1,)), ((), ())), preferred_element_type=F32)


def _dot_nn(a, b):
    return jnp.dot(a, b, preferred_element_type=F32)


def _split(a):
    hi = a.astype(BF16)
    lo = (a - hi.astype(F32)).astype(BF16)
    return hi, lo


def _dot3_nt(a, b):
    ah, al = _split(a)
    bh, bl = _split(b)
    return _dot_nt(ah, bh) + _dot_nt(ah, bl) + _dot_nt(al, bh)


def _dot3_nn(a, b):
    ah, al = _split(a)
    bh, bl = _split(b)
    return _dot_nn(ah, bh) + _dot_nn(ah, bl) + _dot_nn(al, bh)


def _softmax_t(s, mask, scale):
    s = jnp.where(mask, s * scale, NEG_BIG)
    m = jnp.max(s, axis=0, keepdims=True)
    e = jnp.where(mask, jnp.exp(s - m), 0.0)
    l = jnp.maximum(jnp.sum(e, axis=0, keepdims=True), TINY)
    return e, l


def _softmax_parts(parts):
    masked = [s if m is None else jnp.where(m, s, NEG_BIG) for s, m in parts]
    mx = functools.reduce(jnp.maximum, [jnp.max(p, axis=0, keepdims=True) for p in masked])
    es = [jnp.exp2(p - mx) for p in masked]
    l = functools.reduce(jnp.add, [jnp.sum(e, axis=0, keepdims=True) for e in es])
    e = jnp.concatenate([e.astype(BF16) for e in es], axis=0) if len(es) > 1 else es[0].astype(BF16)
    return e, jnp.maximum(l, TINY)


def _rank_t(v, nb):
    blk = lax.broadcasted_iota(jnp.int32, v.shape, 0)
    rank = jnp.zeros(v.shape, F32)
    for j in range(nb):
        vj = v[j:j + 1, :]
        ge = (vj >= v).astype(F32)
        gt = (vj > v).astype(F32)
        rank = rank + jnp.where(blk > j, ge, gt)
    return rank


def _rms_kernel(x_ref, g_ref, o_ref):
    o_ref[...] = _rmsnorm(x_ref[...], g_ref[...]).astype(o_ref.dtype)


def _rms(x2d, g, out_dtype, tm=512):
    t, d = x2d.shape
    return pl.pallas_call(
        _rms_kernel,
        grid=(t // tm,),
        in_specs=[pl.BlockSpec((tm, d), lambda i: (i, 0)), pl.BlockSpec((1, d), lambda i: (0, 0))],
        out_specs=pl.BlockSpec((tm, d), lambda i: (i, 0)),
        out_shape=jax.ShapeDtypeStruct((t, d), out_dtype),
        compiler_params=_cparams(("parallel",)),
        name="rmsnorm",
    )(x2d, g.reshape(1, d))


def _ffn_up_kernel(h_ref, wg_ref, wu_ref, o_ref, wg_bf_ref, wu_bf_ref):
    @pl.when(pl.program_id(1) == 0)
    def _():
        wg_bf_ref[...] = wg_ref[...].astype(BF16)
        wu_bf_ref[...] = wu_ref[...].astype(BF16)

    h = h_ref[...]
    a = _dot_nn(h, wg_bf_ref[...])
    u = _dot_nn(h, wu_bf_ref[...])
    o_ref[...] = (jax.nn.silu(a) * u).astype(o_ref.dtype)


def _ffn_down_kernel(a_ref, wd_ref, x_ref, o_ref, wd_bf_ref):
    @pl.when(pl.program_id(1) == 0)
    def _():
        wd_bf_ref[...] = wd_ref[...].astype(BF16)

    o_ref[...] = x_ref[...] + 0.5 * _dot_nn(a_ref[...], wd_bf_ref[...])


def _ffn(x2d, g, wg, wu, wd, layer, tm=1024, tf=512, tn=512):
    t, d = x2d.shape
    f = wg.shape[2]
    h = _rms(x2d, g, BF16)
    act = pl.pallas_call(
        _ffn_up_kernel,
        grid=(f // tf, t // tm),
        in_specs=[
            pl.BlockSpec((tm, d), lambda j, i: (i, 0)),
            pl.BlockSpec((None, d, tf), lambda j, i: (layer, 0, j)),
            pl.BlockSpec((None, d, tf), lambda j, i: (layer, 0, j)),
        ],
        out_specs=pl.BlockSpec((tm, tf), lambda j, i: (i, j)),
        out_shape=jax.ShapeDtypeStruct((t, f), BF16),
        scratch_shapes=[pltpu.VMEM((d, tf), BF16), pltpu.VMEM((d, tf), BF16)],
        compiler_params=_cparams(("parallel", "arbitrary")),
        name="ffn_up",
    )(h, wg, wu)
    return pl.pallas_call(
        _ffn_down_kernel,
        grid=(d // tn, t // tm),
        in_specs=[
            pl.BlockSpec((tm, f), lambda n, i: (i, 0)),
            pl.BlockSpec((None, f, tn), lambda n, i: (layer, 0, n), pipeline_mode=pl.Buffered(1)),
            pl.BlockSpec((tm, tn), lambda n, i: (i, n)),
        ],
        out_specs=pl.BlockSpec((tm, tn), lambda n, i: (i, n)),
        out_shape=jax.ShapeDtypeStruct((t, d), F32),
        scratch_shapes=[pltpu.VMEM((f, tn), BF16)],
        compiler_params=_cparams(("parallel", "arbitrary")),
        name="ffn_down",
    )(act, wd, x2d)


def _in_proj_kernel(x_ref, g_ref, w_ref, z_ref, gate_ref, h_ref, *, n_z):
    j = pl.program_id(1)

    @pl.when(j == 0)
    def _():
        h_ref[...] = _rmsnorm(x_ref[...], g_ref[...]).astype(BF16)

    y = _dot_nn(h_ref[...], w_ref[...])

    @pl.when(j < n_z)
    def _():
        z_ref[...] = y

    @pl.when(j >= n_z)
    def _():
        gate_ref[...] = jax.nn.sigmoid(y).astype(gate_ref.dtype)


def _in_proj(x2d, g, w, layer, tm=1024, tn=1024):
    t, d = x2d.shape
    n = w.shape[2]
    n_z = Z_COLS // tn
    n_g = (n - Z_COLS) // tn
    return pl.pallas_call(
        functools.partial(_in_proj_kernel, n_z=n_z),
        grid=(t // tm, n_z + n_g),
        in_specs=[
            pl.BlockSpec((tm, d), lambda i, j: (i, 0)),
            pl.BlockSpec((1, d), lambda i, j: (0, 0)),
            pl.BlockSpec((None, d, tn), lambda i, j: (layer, 0, j)),
        ],
        out_specs=[
            pl.BlockSpec((tm, tn), lambda i, j: (i, jnp.minimum(j, n_z - 1))),
            pl.BlockSpec((tm, tn), lambda i, j: (i, jnp.maximum(j - n_z, 0))),
        ],
        out_shape=[jax.ShapeDtypeStruct((t, Z_COLS), F32), jax.ShapeDtypeStruct((t, n - Z_COLS), BF16)],
        scratch_shapes=[pltpu.VMEM((tm, d), BF16)],
        compiler_params=_cparams(("parallel", "arbitrary")),
        name="in_proj",
    )(x2d, g.reshape(1, d), w)


def _norm_matmul_kernel(x_ref, g_ref, w_ref, o_ref, h_ref):
    @pl.when(pl.program_id(1) == 0)
    def _():
        h_ref[...] = _rmsnorm(x_ref[...], g_ref[...]).astype(BF16)

    o_ref[...] = _dot_nn(h_ref[...], w_ref[...].astype(BF16)).astype(o_ref.dtype)


def _norm_matmul(x2d, col_blk, g, w, layer, tm, tn, name):
    t = x2d.shape[0]
    _, k, n = w.shape
    return pl.pallas_call(
        _norm_matmul_kernel,
        grid=(t // tm, n // tn),
        in_specs=[
            pl.BlockSpec((tm, k), lambda i, j: (i, col_blk)),
            pl.BlockSpec((1, k), lambda i, j: (0, 0)),
            pl.BlockSpec((None, k, tn), lambda i, j: (layer, 0, j)),
        ],
        out_specs=pl.BlockSpec((tm, tn), lambda i, j: (i, j)),
        out_shape=jax.ShapeDtypeStruct((t, n), F32),
        scratch_shapes=[pltpu.VMEM((tm, k), BF16)],
        compiler_params=_cparams(("parallel", "arbitrary")),
        name=name,
    )(x2d, g.reshape(1, k), w)


def _moba_kernel(q_ref, k_ref, v_ref, cosf_ref, sinf_ref, cosq_ref, sinq_ref, o_ref,
                 kr_ref, km_ref, vt_ref, *, n_blk, scale):
    qi = pl.program_id(2)
    tq = q_ref.shape[1]
    hp = q_ref.shape[2] // LANES
    head = lambda hh: slice(hh * LANES, (hh + 1) * LANES)

    @pl.when(qi == 0)
    def _():
        for hh in range(hp):
            kr = _rope_full(k_ref[0, :, head(hh)], cosf_ref[...], sinf_ref[...])
            kr_ref[hh] = kr.astype(BF16)
            km_ref[hh] = jnp.concatenate(
                [jnp.mean(kr[j * MOBA_BLOCK:(j + 1) * MOBA_BLOCK], axis=0, keepdims=True) for j in range(n_blk)],
                axis=0)
            vt_ref[hh] = v_ref[0, :, head(hh)].T.astype(BF16)

    causal = (lax.broadcasted_iota(jnp.int32, (tq, tq), 0)
              <= lax.broadcasted_iota(jnp.int32, (tq, tq), 1))

    for n in range(n_blk):
        @pl.when(qi == n)
        def _(n=n):
            for hh in range(hp):
                qr = _rope_full(q_ref[0, :, head(hh)], cosq_ref[...], sinq_ref[...])
                q_bf = (qr * (scale * LOG2E)).astype(BF16)
                s = _dot_nt(kr_ref[hh, 0:(n + 1) * tq], q_bf)
                parts = []
                if n > MOBA_TOPK:
                    gate = _dot3_nt(km_ref[hh], qr)
                    past = lax.broadcasted_iota(jnp.int32, (n_blk, tq), 0) < n
                    gate = jnp.where(past, gate, -jnp.inf)
                    drop = jnp.where(_rank_t(gate, n) < MOBA_TOPK, 0.0, NEG_BIG)
                    for j in range(n):
                        parts.append((s[j * tq:(j + 1) * tq] + drop[j:j + 1, :], None))
                elif n > 0:
                    parts.append((s[0:n * tq], None))
                parts.append((s[n * tq:], causal))
                e, l = _softmax_parts(parts)
                o_t = _dot_nn(vt_ref[hh, :, 0:(n + 1) * tq], e) / l
                o_ref[0, :, head(hh)] = o_t.T.astype(o_ref.dtype)


def _moba(z3, cos, sin_s, tq=MOBA_BLOCK, hp=2):
    b, s, _ = z3.shape
    n_blk = s // MOBA_BLOCK
    w = hp * LANES
    qb, kb, vb = OFF_AQ // w, OFF_AK // w, OFF_AV // w
    return pl.pallas_call(
        functools.partial(_moba_kernel, n_blk=n_blk, scale=HEAD_DIM ** -0.5),
        grid=(b, MOBA_HEADS // hp, s // tq),
        in_specs=[
            pl.BlockSpec((1, tq, w), lambda bi, h, qi: (bi, qi, qb + h)),
            pl.BlockSpec((1, s, w), lambda bi, h, qi: (bi, 0, kb + h)),
            pl.BlockSpec((1, s, w), lambda bi, h, qi: (bi, 0, vb + h)),
            pl.BlockSpec((s, LANES), lambda bi, h, qi: (0, 0)),
            pl.BlockSpec((s, LANES), lambda bi, h, qi: (0, 0)),
            pl.BlockSpec((tq, LANES), lambda bi, h, qi: (qi, 0)),
            pl.BlockSpec((tq, LANES), lambda bi, h, qi: (qi, 0)),
        ],
        out_specs=pl.BlockSpec((1, tq, w), lambda bi, h, qi: (bi, qi, h)),
        out_shape=jax.ShapeDtypeStruct((b, s, MOBA_HEADS * HEAD_DIM), BF16),
        scratch_shapes=[pltpu.VMEM((hp, s, LANES), BF16), pltpu.VMEM((hp, n_blk, LANES), F32),
                        pltpu.VMEM((hp, LANES, s), BF16)],
        compiler_params=_cparams(("parallel", "parallel", "arbitrary")),
        name="moba",
    )(z3, z3, z3, cos, sin_s, cos, sin_s)


def _mla_kernel(qn_ref, qr_ref, kn_ref, v_ref, kr_ref, cosf_ref, sinf_ref, cosq_ref, sinq_ref, o_ref,
                kf_ref, vt_ref, *, scale):
    qi = pl.program_id(2)
    tq = qn_ref.shape[1]
    s_len = kn_ref.shape[1]
    hp = qn_ref.shape[2] // LANES
    head = lambda hh: slice(hh * LANES, (hh + 1) * LANES)

    @pl.when(qi == 0)
    def _():
        k_rope = _rope_half(kr_ref[0], cosf_ref[...], sinf_ref[...]).astype(BF16)
        for hh in range(hp):
            kf_ref[hh, :, :LANES] = kn_ref[0, :, head(hh)].astype(BF16)
            kf_ref[hh, :, LANES:] = k_rope
            vt_ref[hh] = v_ref[0, :, head(hh)].T.astype(BF16)

    causal = (lax.broadcasted_iota(jnp.int32, (tq, tq), 0)
              <= lax.broadcasted_iota(jnp.int32, (tq, tq), 1))

    for n in range(s_len // tq):
        @pl.when(qi == n)
        def _(n=n):
            for hh in range(hp):
                q = jnp.concatenate([qn_ref[0, :, head(hh)],
                                     _rope_half(qr_ref[0, :, head(hh)], cosq_ref[...], sinq_ref[...])], axis=1)
                q_bf = (q * (scale * LOG2E)).astype(BF16)
                s = _dot_nt(kf_ref[hh, 0:(n + 1) * tq], q_bf)
                parts = [(s[0:n * tq], None)] if n > 0 else []
                parts.append((s[n * tq:], causal))
                e, l = _softmax_parts(parts)
                o_t = _dot_nn(vt_ref[hh, :, 0:(n + 1) * tq], e) / l
                o_ref[0, :, head(hh)] = o_t.T.astype(o_ref.dtype)


def _mla(q3, kv3, z3, cos, sin_s, tq=256, hp=2):
    b, s, _ = q3.shape
    krb = OFF_BKR // LANES
    w = hp * LANES
    ng = MLA_HEADS // hp
    return pl.pallas_call(
        functools.partial(_mla_kernel, scale=(MLA_NOPE + MLA_ROPE) ** -0.5),
        grid=(b, ng, s // tq),
        in_specs=[
            pl.BlockSpec((1, tq, w), lambda bi, h, qi: (bi, qi, h)),
            pl.BlockSpec((1, tq, w), lambda bi, h, qi: (bi, qi, ng + h)),
            pl.BlockSpec((1, s, w), lambda bi, h, qi: (bi, 0, h)),
            pl.BlockSpec((1, s, w), lambda bi, h, qi: (bi, 0, ng + h)),
            pl.BlockSpec((1, s, LANES), lambda bi, h, qi: (bi, 0, krb)),
            pl.BlockSpec((s, LANES), lambda bi, h, qi: (0, 0)),
            pl.BlockSpec((s, LANES), lambda bi, h, qi: (0, 0)),
            pl.BlockSpec((tq, LANES), lambda bi, h, qi: (qi, 0)),
            pl.BlockSpec((tq, LANES), lambda bi, h, qi: (qi, 0)),
        ],
        out_specs=pl.BlockSpec((1, tq, w), lambda bi, h, qi: (bi, qi, h)),
        out_shape=jax.ShapeDtypeStruct((b, s, MLA_HEADS * MLA_V), BF16),
        scratch_shapes=[pltpu.VMEM((hp, s, 2 * LANES), BF16), pltpu.VMEM((hp, LANES, s), BF16)],
        compiler_params=_cparams(("parallel", "parallel", "arbitrary")),
        name="mla",
    )(q3, q3, kv3, kv3, z3, cos, sin_s, cos, sin_s)


def _cmp_kernel(x_ref, w1_ref, w2_ref, pos_ref, cos_ref, sin_ref, o_ref):
    kv = pl.program_id(0)
    n16 = o_ref.shape[3]
    x = jnp.concatenate(
        [x_ref[0, pl.ds(l, n16, stride=NSA_CMP_STRIDE), :] for l in range(NSA_CMP_STRIDE)], axis=1)
    half = x.shape[1]
    w1 = w1_ref[0]
    first = _dot3_nn(x, w1[:half])
    second = _dot3_nn(x, w1[half:])
    bias = _dot3_nn(jnp.broadcast_to(pos_ref[0], (8, 2 * half)), w1)[0:1]
    hid = first + pltpu.roll(second, n16 - 1, axis=0) + bias
    y = _dot3_nn(jax.nn.gelu(hid), w2_ref[0])
    roped = _rope_full(y, cos_ref[...], sin_ref[...])
    o_ref[0, 0, 0] = jnp.where(kv == 0, roped, y)


def _nsa_compress(z3, w1, w2, posflat, cos_c, sin_c):
    b, s, _ = z3.shape
    g = NSA_KV_GROUPS
    n16 = s // NSA_CMP_STRIDE
    width = NSA_CMP_STRIDE * HEAD_DIM
    hid = w1.shape[2]
    base = OFF_CKC // LANES
    return pl.pallas_call(
        _cmp_kernel,
        grid=(2, b, g),
        in_specs=[
            pl.BlockSpec((1, s, LANES), lambda kv, bi, gi: (bi, 0, base + g * kv + gi)),
            pl.BlockSpec((1, 2 * width, hid), lambda kv, bi, gi: (kv, 0, 0)),
            pl.BlockSpec((1, hid, HEAD_DIM), lambda kv, bi, gi: (kv, 0, 0)),
            pl.BlockSpec((1, 1, 2 * width), lambda kv, bi, gi: (kv, 0, 0)),
            pl.BlockSpec((n16, HEAD_DIM), lambda kv, bi, gi: (0, 0)),
            pl.BlockSpec((n16, HEAD_DIM), lambda kv, bi, gi: (0, 0)),
        ],
        out_specs=pl.BlockSpec((1, 1, 1, n16, HEAD_DIM), lambda kv, bi, gi: (kv, bi, gi, 0, 0)),
        out_shape=jax.ShapeDtypeStruct((2, b, g, n16, HEAD_DIM), F32),
        compiler_params=_cparams(("parallel", "parallel", "parallel")),
        name="nsa_compress",
    )(z3, w1, w2, posflat, cos_c, sin_c)


def _nsa_kernel(q_ref, ks_ref, vs_ref, kw_ref, vw_ref, kc_ref, vc_ref, gate_ref,
                cosf_ref, sinf_ref, cosq_ref, sinq_ref, ovl_ref, o_ref,
                ksr_ref, vst_ref, kw3_ref, vwt3_ref, vct_ref, os_ref, *, n_sel, scale):
    qi = pl.program_id(1)
    tq = q_ref.shape[1]
    s_len = ks_ref.shape[1]
    n_grp = ks_ref.shape[2] // HEAD_DIM
    rep = NSA_REP
    lanes = rep * tq
    n_chunk = s_len // tq
    pad_chunks = NSA_WINDOW // tq
    n_cmp = kc_ref.shape[3]
    grp = lambda gg: slice(gg * HEAD_DIM, (gg + 1) * HEAD_DIM)
    groups = range(n_grp)

    @pl.when(qi == 0)
    def _():
        for gg in groups:
            ksr_ref[gg] = _rope_full(ks_ref[0, :, grp(gg)], cosf_ref[...], sinf_ref[...]).astype(BF16)
            vst_ref[gg] = vs_ref[0, :, grp(gg)].T.astype(BF16)
            kw3_ref[gg, 0:pad_chunks] = jnp.zeros((pad_chunks, tq, HEAD_DIM), BF16)
            vwt3_ref[gg, 0:pad_chunks] = jnp.zeros((pad_chunks, HEAD_DIM, tq), BF16)
            kwr = _rope_full(kw_ref[0, :, grp(gg)], cosf_ref[...], sinf_ref[...]).astype(BF16)
            for c in range(n_chunk):
                kw3_ref[gg, pad_chunks + c] = kwr[c * tq:(c + 1) * tq]
                vwt3_ref[gg, pad_chunks + c] = vw_ref[0, c * tq:(c + 1) * tq, grp(gg)].T.astype(BF16)
            vct_ref[gg] = vc_ref[0, 0, gg].T.astype(BF16)

    cos_q = cosq_ref[...]
    sin_q = sinq_ref[...]
    lane = lax.broadcasted_iota(jnp.int32, (1, lanes), 1)
    t_loc = lane % tq
    t = qi * tq + t_loc
    cur = t[:, 0:tq] // NSA_SEL_BLOCK
    blk = lax.broadcasted_iota(jnp.int32, (n_sel, tq), 0)
    forced = (blk == 0) | (blk == cur) | (blk == cur - 1)
    visible = blk <= cur
    cmp_end = (lax.broadcasted_iota(jnp.int32, (n_cmp, lanes), 0) * NSA_CMP_STRIDE + (NSA_CMP_LEN - 1))
    ovl = ovl_ref[...]

    q_bf, o_c, imp = [], [], []
    for gg in groups:
        q_all = jnp.concatenate(
            [_rope_full(q_ref[0, :, grp(gg * rep + r)], cos_q, sin_q) for r in range(rep)],
            axis=0)
        q_bf.append((q_all * (scale * LOG2E)).astype(BF16))

        s_c = _dot3_nt(kc_ref[0, 0, gg], q_all)
        e_c, l_c = _softmax_t(s_c, cmp_end <= t, scale)
        p_c = e_c / l_c
        o_c.append(_dot_nn(vct_ref[gg], p_c.astype(BF16)))

        p_hi, p_lo = _split(p_c)
        imp3 = _dot_nn(ovl, p_hi) + _dot_nn(ovl, p_lo)
        imp_g = imp3[:, 0:tq]
        for r in range(1, rep):
            imp_g = imp_g + imp3[:, r * tq:(r + 1) * tq]
        imp_g = jnp.where(forced, FORCE_SCORE, imp_g)
        imp.append(jnp.where(visible, imp_g, -jnp.inf))

    unit = 2 * tq
    sb = NSA_SEL_BLOCK
    for v in range(s_len // unit):
        @pl.when(qi // 2 == v)
        def _(v=v):
            n_keys = (v + 1) * unit
            lo = n_keys - unit

            def causal(start, n):
                return lax.broadcasted_iota(jnp.int32, (n, lanes), 0) <= (t - start)

            n_vis = n_keys // sb
            for gg in groups:
                s = _dot_nt(ksr_ref[gg, 0:n_keys], q_bf[gg])
                if n_vis > NSA_SEL_TOPK:
                    rank = _rank_t(imp[gg], n_vis)
                    rank3 = jnp.concatenate([rank] * rep, axis=1)
                    drop = jnp.where(rank3 < NSA_SEL_TOPK, 0.0, NEG_BIG)
                    parts = []
                    for j in range(n_vis):
                        parts.append((s[j * sb:(j + 1) * sb] + drop[j:j + 1, :],
                                      causal(j * sb, sb) if j * sb >= lo else None))
                else:
                    parts = [(s[0:lo], None)] if lo > 0 else []
                    parts.append((s[lo:], causal(lo, unit)))
                e_s, l_s = _softmax_parts(parts)
                os_ref[gg] = _dot_nn(vst_ref[gg, :, 0:n_keys], e_s) / l_s

    n_win = pad_chunks + 1
    row = lax.broadcasted_iota(jnp.int32, (tq, lanes), 0)
    for gg in groups:
        parts = []
        for c in range(n_win):
            s_w = _dot_nt(kw3_ref[gg, qi + c], q_bf[gg])
            if c < pad_chunks:
                s_w = s_w + jnp.where(qi + c >= pad_chunks, 0.0, NEG_BIG)
            mask = (row > t_loc) if c == 0 else ((row <= t_loc) if c == n_win - 1 else None)
            parts.append((s_w, mask))
        e_w, l_w = _softmax_parts(parts)
        o_w = _dot_nn(vwt3_ref[gg, qi], e_w[0:tq])
        for c in range(1, n_win):
            o_w = o_w + _dot_nn(vwt3_ref[gg, qi + c], e_w[c * tq:(c + 1) * tq])
        o_w = o_w / l_w

        g_t = jax.nn.sigmoid(gate_ref[0, :, grp(gg)].T)

        def gate_row(branch, g_t=g_t):
            return jnp.concatenate([g_t[3 * r + branch:3 * r + branch + 1, :] for r in range(rep)], axis=1)

        out_t = gate_row(0) * o_c[gg] + gate_row(1) * os_ref[gg] + gate_row(2) * o_w
        for r in range(rep):
            o_ref[0, :, grp(gg * rep + r)] = out_t[:, r * tq:(r + 1) * tq].T.astype(o_ref.dtype)


def _nsa(z3, cmp_kv, cos, sin_s, ovl_t, tq=128):
    b, s, _ = z3.shape
    g = NSA_KV_GROUPS
    n_sel = s // NSA_SEL_BLOCK
    n_cmp = cmp_kv.shape[3]
    qw = NSA_HEADS * HEAD_DIM
    kvw = g * HEAD_DIM
    n_chunk_pad = (s + NSA_WINDOW) // tq

    def col(off):
        return lambda bi, qi: (bi, 0, off // kvw)

    return pl.pallas_call(
        functools.partial(_nsa_kernel, n_sel=n_sel, scale=HEAD_DIM ** -0.5),
        grid=(b, s // tq),
        in_specs=[
            pl.BlockSpec((1, tq, qw), lambda bi, qi: (bi, qi, OFF_CQ // qw)),
            pl.BlockSpec((1, s, kvw), col(OFF_CKS)),
            pl.BlockSpec((1, s, kvw), col(OFF_CVS)),
            pl.BlockSpec((1, s, kvw), col(OFF_CKW)),
            pl.BlockSpec((1, s, kvw), col(OFF_CVW)),
            pl.BlockSpec((1, 1, g, n_cmp, HEAD_DIM), lambda bi, qi: (0, bi, 0, 0, 0)),
            pl.BlockSpec((1, 1, g, n_cmp, HEAD_DIM), lambda bi, qi: (1, bi, 0, 0, 0)),
            pl.BlockSpec((1, tq, g * LANES), lambda bi, qi: (bi, qi, OFF_CGATE // (g * LANES))),
            pl.BlockSpec((s, LANES), lambda bi, qi: (0, 0)),
            pl.BlockSpec((s, LANES), lambda bi, qi: (0, 0)),
            pl.BlockSpec((tq, LANES), lambda bi, qi: (qi, 0)),
            pl.BlockSpec((tq, LANES), lambda bi, qi: (qi, 0)),
            pl.BlockSpec((n_sel, n_cmp), lambda bi, qi: (0, 0)),
        ],
        out_specs=pl.BlockSpec((1, tq, qw), lambda bi, qi: (bi, qi, 0)),
        out_shape=jax.ShapeDtypeStruct((b, s, qw), BF16),
        scratch_shapes=[
            pltpu.VMEM((g, s, HEAD_DIM), BF16),
            pltpu.VMEM((g, HEAD_DIM, s), BF16),
            pltpu.VMEM((g, n_chunk_pad, tq, HEAD_DIM), BF16),
            pltpu.VMEM((g, n_chunk_pad, HEAD_DIM, tq), BF16),
            pltpu.VMEM((g, HEAD_DIM, n_cmp), BF16),
            pltpu.VMEM((g, HEAD_DIM, NSA_REP * tq), F32),
        ],
        compiler_params=_cparams(("parallel", "arbitrary")),
        name="nsa",
    )(z3, z3, z3, z3, z3, cmp_kv, cmp_kv, z3, cos, sin_s, cos, sin_s, ovl_t)


def _merge_kernel(ya_ref, yb_ref, yc_ref, ga_ref, gb_ref, gc_ref, wa_ref, wb_ref, wc_ref, o_ref,
                  wa_bf_ref, wb_bf_ref, wc_bf_ref):
    @pl.when(pl.program_id(1) == 0)
    def _():
        wa_bf_ref[...] = wa_ref[...].astype(BF16)
        wb_bf_ref[...] = wb_ref[...].astype(BF16)
        wc_bf_ref[...] = wc_ref[...].astype(BF16)

    m = ga_ref[...].astype(F32) * _dot_nn(ya_ref[...], wa_bf_ref[...])
    m = m + gb_ref[...].astype(F32) * _dot_nn(yb_ref[...], wb_bf_ref[...])
    m = m + gc_ref[...].astype(F32) * _dot_nn(yc_ref[...], wc_bf_ref[...])
    o_ref[...] = m.astype(o_ref.dtype)


def _merge(ya, yb, yc, gates, wa, wb, wc, layer, tm=1024, tn=1024):
    t = ya.shape[0]
    d = wa.shape[2]
    nj = d // tn

    def gate_spec(branch):
        return pl.BlockSpec((tm, tn), lambda j, i: (i, branch * nj + j))

    def y_spec(y):
        return pl.BlockSpec((tm, y.shape[1]), lambda j, i: (i, 0))

    def w_spec(w):
        return pl.BlockSpec((None, w.shape[1], tn), lambda j, i: (layer, 0, j), pipeline_mode=pl.Buffered(1))

    return pl.pallas_call(
        _merge_kernel,
        grid=(nj, t // tm),
        in_specs=[y_spec(ya), y_spec(yb), y_spec(yc), gate_spec(0), gate_spec(1), gate_spec(2),
                  w_spec(wa), w_spec(wb), w_spec(wc)],
        out_specs=pl.BlockSpec((tm, tn), lambda j, i: (i, j)),
        out_shape=jax.ShapeDtypeStruct((t, d), BF16),
        scratch_shapes=[pltpu.VMEM((w.shape[1], tn), BF16) for w in (wa, wb, wc)],
        compiler_params=_cparams(("parallel", "arbitrary")),
        name="merge",
    )(ya, yb, yc, gates, gates, gates, wa, wb, wc)


def _proj_residual_kernel(m_ref, w_ref, x_ref, o_ref, w_bf_ref):
    @pl.when(pl.program_id(1) == 0)
    def _():
        w_bf_ref[...] = w_ref[...].astype(BF16)

    o_ref[...] = x_ref[...] + _dot_nn(m_ref[...], w_bf_ref[...])


def _proj_residual(m, w, x2d, layer, tm=1024, tn=1024):
    t, k = m.shape
    d = w.shape[2]
    return pl.pallas_call(
        _proj_residual_kernel,
        grid=(d // tn, t // tm),
        in_specs=[
            pl.BlockSpec((tm, k), lambda j, i: (i, 0)),
            pl.BlockSpec((None, k, tn), lambda j, i: (layer, 0, j), pipeline_mode=pl.Buffered(1)),
            pl.BlockSpec((tm, tn), lambda j, i: (i, j)),
        ],
        out_specs=pl.BlockSpec((tm, tn), lambda j, i: (i, j)),
        out_shape=jax.ShapeDtypeStruct((t, d), F32),
        scratch_shapes=[pltpu.VMEM((k, tn), BF16)],
        compiler_params=_cparams(("parallel", "arbitrary")),
        name="out_proj",
    )(m, w, x2d)


def _rope_tables(pos, dim, width):
    inv = 1.0 / (ROPE_THETA ** (jnp.arange(0, dim, 2, dtype=F32) / dim))
    ang = pos.astype(F32)[:, None] * inv[None, :]
    cos = jnp.cos(ang)
    sin = jnp.sin(ang)
    cos = jnp.concatenate([cos, cos], axis=-1)
    sin_s = jnp.concatenate([-sin, sin], axis=-1)
    pad = ((0, 0), (0, width - dim))
    return jnp.pad(cos, pad), jnp.pad(sin_s, pad)


def _repack_kernel(srca_ref, srcb_ref, shift_ref, valid_ref, a_ref, b_ref, o_ref, *, n_src_cols):
    j = pl.program_id(1)
    shift = shift_ref[j]
    valid = valid_ref[j]
    row = lax.broadcasted_iota(jnp.int32, (LANES, LANES), 0)
    col = lax.broadcasted_iota(jnp.int32, (LANES, LANES), 1)
    take_a = jnp.where((row == col + shift) & (col < valid), 1.0, 0.0).astype(BF16)
    take_b = jnp.where((row + LANES == col + shift) & (col < valid), 1.0, 0.0).astype(BF16)
    lane = lax.broadcasted_iota(jnp.int32, (1, LANES), 1)

    def in_bounds(blk_ref, src_blk):
        return jnp.where(src_blk * LANES + lane < n_src_cols, blk_ref[...], 0.0).astype(BF16)

    o_ref[...] = (_dot_nn(in_bounds(a_ref, srca_ref[j]), take_a)
                  + _dot_nn(in_bounds(b_ref, srcb_ref[j]), take_b)).astype(o_ref.dtype)


def _repack(w, pieces, n_out_blocks):
    depth, k, n = w.shape
    n_src_blocks = -(-n // LANES)
    starts, valids = [], []
    for start, width in pieces:
        if width == 0:
            starts.append(0)
            valids.append(0)
        for off in range(0, width, LANES):
            starts.append(start + off)
            valids.append(min(LANES, width - off))
    starts += [0] * (n_out_blocks - len(starts))
    valids += [0] * (n_out_blocks - len(valids))
    srca = np.asarray(starts, np.int32) // LANES
    srcb = np.minimum(srca + 1, n_src_blocks - 1).astype(np.int32)
    shift = np.asarray(starts, np.int32) % LANES
    grid_spec = pltpu.PrefetchScalarGridSpec(
        num_scalar_prefetch=4,
        grid=(depth, n_out_blocks),
        in_specs=[
            pl.BlockSpec((None, k, LANES), lambda d, j, sa, sb, sh, va: (d, 0, sa[j])),
            pl.BlockSpec((None, k, LANES), lambda d, j, sa, sb, sh, va: (d, 0, sb[j])),
        ],
        out_specs=pl.BlockSpec((None, k, LANES), lambda d, j, sa, sb, sh, va: (d, 0, j)),
    )
    return pl.pallas_call(
        functools.partial(_repack_kernel, n_src_cols=n),
        grid_spec=grid_spec,
        out_shape=jax.ShapeDtypeStruct((depth, k, n_out_blocks * LANES), BF16),
        compiler_params=_cparams(("parallel", "parallel")),
        name="repack",
    )(jnp.asarray(srca), jnp.asarray(srcb), jnp.asarray(shift), jnp.asarray(np.asarray(valids, np.int32)), w, w)


def _pack_w_in(w):
    sizes = [MOBA_HEADS * HEAD_DIM] * 3 + [MLA_Q_RANK, MLA_KV_RANK, MLA_ROPE, NSA_HEADS * HEAD_DIM] \
        + [NSA_KV_GROUPS * HEAD_DIM] * 6
    offs = [int(v) for v in np.concatenate([[0], np.cumsum(sizes)])]
    (a_q, a_k, a_v, b_cq, b_ckv, b_kr, c_q, c_kc, c_vc, c_ks, c_vs, c_kw, c_vw) = [
        (offs[i], sizes[i]) for i in range(len(sizes))]
    gate0 = offs[-1]
    per_group = NSA_REP * 3
    gates = [(gate0 + gi * per_group, per_group) for gi in range(NSA_KV_GROUPS)]
    merge = (gate0 + NSA_HEADS * 3, w.shape[2] - gate0 - NSA_HEADS * 3)
    pieces = [c_q, b_ckv, b_cq, a_q, a_k, a_v, c_kc, c_vc, c_ks, c_vs, c_kw, c_vw] + gates + [b_kr]
    used = sum(-(-width // LANES) for _, width in pieces)
    pieces += [(0, 0)] * (Z_COLS // LANES - used) + [merge]
    return _repack(w, pieces, (Z_COLS + merge[1]) // LANES)


def _pack_w_uq(w):
    per_head = MLA_NOPE + MLA_ROPE
    pieces = [(h * per_head, MLA_NOPE) for h in range(MLA_HEADS)] \
        + [(h * per_head + MLA_NOPE, MLA_ROPE) for h in range(MLA_HEADS)]
    return _repack(w, pieces, 2 * MLA_HEADS)


def _pack_w_ukv(w):
    per_head = MLA_NOPE + MLA_V
    pieces = [(h * per_head, MLA_NOPE) for h in range(MLA_HEADS)] \
        + [(h * per_head + MLA_NOPE, MLA_V) for h in range(MLA_HEADS)]
    return _repack(w, pieces, 2 * MLA_HEADS)


def kernel(x, ffn1_norm, ffn1_w_gate, ffn1_w_up, ffn1_w_down, mix_norm, w_in, mla_q_norm, mla_w_uq, mla_kv_norm, mla_w_ukv, nsa_cmp_pos_k, nsa_cmp_w1_k, nsa_cmp_w2_k, nsa_cmp_pos_v, nsa_cmp_w1_v, nsa_cmp_w2_v, w_branch_moba, w_branch_mla, w_branch_nsa, w_out, ffn2_norm, ffn2_w_gate, ffn2_w_up, ffn2_w_down, final_norm):
    b, s, d = x.shape
    depth = w_in.shape[0]
    t = b * s
    assert NSA_CMP_LEN == 2 * NSA_CMP_STRIDE and s % MOBA_BLOCK == 0 and s % NSA_SEL_BLOCK == 0

    pos = jnp.arange(s)
    cos_h, sin_h = _rope_tables(pos, HEAD_DIM, LANES)
    cos_r, sin_r = _rope_tables(pos, MLA_ROPE, LANES)
    n16 = s // NSA_CMP_STRIDE
    cmp_end = np.arange(n16) * NSA_CMP_STRIDE + NSA_CMP_LEN - 1
    cos_c, sin_c = _rope_tables(jnp.asarray(cmp_end), HEAD_DIM, LANES)
    n_sel = s // NSA_SEL_BLOCK
    sel_start = np.arange(n_sel) * NSA_SEL_BLOCK
    cmp_start = np.arange(n16) * NSA_CMP_STRIDE
    ovl_t = ((cmp_start[None, :] <= sel_start[:, None] + NSA_SEL_BLOCK - 1)
             & (cmp_end[None, :] >= sel_start[:, None])).astype(np.float32)
    ovl_t = jnp.asarray(ovl_t, BF16)

    w_in_p = _pack_w_in(w_in)
    w_uq_p = _pack_w_uq(mla_w_uq)
    w_ukv_p = _pack_w_ukv(mla_w_ukv)

    xf = x.reshape(t, d)
    for i in range(depth):
        xf = _ffn(xf, ffn1_norm[i], ffn1_w_gate, ffn1_w_up, ffn1_w_down, i)

        z, gates = _in_proj(xf, mix_norm[i], w_in_p, i)
        z3 = z.reshape(b, s, Z_COLS)
        q_mla = _norm_matmul(z, OFF_BCQ // MLA_Q_RANK, mla_q_norm[i], w_uq_p, i,
                             tm=1024, tn=512, name="mla_q_proj")
        kv_mla = _norm_matmul(z, OFF_BCKV // MLA_KV_RANK, mla_kv_norm[i], w_ukv_p, i,
                              tm=1024, tn=512, name="mla_kv_proj")

        y_a = _moba(z3, cos_h, sin_h)
        y_b = _mla(q_mla.reshape(b, s, -1), kv_mla.reshape(b, s, -1), z3, cos_r, sin_r)

        w1 =jnp.stack([nsa_cmp_w1_k[i], nsa_cmp_w1_v[i]])
        w2 = jnp.stack([nsa_cmp_w2_k[i], nsa_cmp_w2_v[i]])
        posflat = jnp.stack([nsa_cmp_pos_k[i].reshape(1, -1), nsa_cmp_pos_v[i].reshape(1, -1)])
        cmp_kv = _nsa_compress(z3, w1, w2, posflat, cos_c, sin_c)
        y_c = _nsa(z3, cmp_kv, cos_h, sin_h, ovl_t)

        merged = _merge(y_a.reshape(t, -1), y_b.reshape(t, -1), y_c.reshape(t, -1), gates,
                        w_branch_moba, w_branch_mla, w_branch_nsa, i)
        xf = _proj_residual(merged, w_out, xf, i)

        xf = _ffn(xf, ffn2_norm[i], ffn2_w_gate, ffn2_w_up, ffn2_w_down, i)
    return _rms(xf, final_norm, F32).reshape(b, s, d)
```

```python
import functools

import numpy as np
import jax
import jax.numpy as jnp
from jax import lax
from jax.experimental import pallas as pl
from jax.experimental.pallas import tpu as pltpu

F32 = jnp.float32
BF16 = jnp.bfloat16

HEAD_DIM = 128
ROPE_THETA = 10000.0
NORM_EPS = 1e-6
NEG_BIG = -1e30
TINY = 1e-30
LOG2E = 1.4426950408889634
N_BRANCHES = 3

MOBA_HEADS = 4
MOBA_BLOCK = 256
MOBA_TOPK = 3

MLA_HEADS = 6
MLA_Q_RANK = 512
MLA_KV_RANK = 256
MLA_NOPE = 128
MLA_ROPE = 64
MLA_V = 128

NSA_HEADS = 6
NSA_KV_GROUPS = 2
NSA_REP = NSA_HEADS // NSA_KV_GROUPS
NSA_CMP_LEN = 32
NSA_CMP_STRIDE = 16
NSA_CMP_HIDDEN = 256
NSA_SEL_BLOCK = 64
NSA_SEL_TOPK = 16
NSA_WINDOW = 512
FORCE_SCORE = 1e9

LANES = 128
VMEM_LIMIT = 56 * 2**20

OFF_CQ = 0
OFF_BCKV = 768
OFF_BCQ = 1024
OFF_AQ = 1536
OFF_AK = 2048
OFF_AV = 2560
OFF_CKC = 3072
OFF_CVC = 3328
OFF_CKS = 3584
OFF_CVS = 3840
OFF_CKW = 4096
OFF_CVW = 4352
OFF_CGATE = 4608
OFF_BKR = 4864
Z_COLS = 5120


def _cparams(sem):
    return pltpu.CompilerParams(dimension_semantics=sem, vmem_limit_bytes=VMEM_LIMIT)


def _rmsnorm(x, g):
    ms = jnp.mean(x * x, axis=-1, keepdims=True)
    return x * lax.rsqrt(ms + NORM_EPS) * g


def _sigmoid(x):
    return 0.5 * jnp.tanh(0.5 * x) + 0.5


def _rope_full(x, cos, sin_signed):
    return x * cos + pltpu.roll(x, HEAD_DIM // 2, axis=1) * sin_signed


def _rope_half(x, cos, sin_signed):
    lane = lax.broadcasted_iota(jnp.int32, x.shape, 1)
    q = MLA_ROPE // 2
    swapped = jnp.where(lane < q, pltpu.roll(x, LANES - q, axis=1), pltpu.roll(x, q, axis=1))
    return x * cos + swapped * sin_signed


def _dot_nt(a, b):
    return lax.dot_general(a, b, (((1,), (1,)), ((), ())), preferred_element_type=F32)


def _dot_nn(a, b):
    return jnp.dot(a, b, preferred_element_type=F32)


def _split(a):
    hi = a.astype(BF16)
    lo = (a - hi.astype(F32)).astype(BF16)
    return hi, lo


def _dot3_nt(a, b):
    ah, al = _split(a)
    bh, bl = _split(b)
    return _dot_nt(ah, bh) + _dot_nt(ah, bl) + _dot_nt(al, bh)


def _dot3_nn(a, b):
    ah, al = _split(a)
    bh, bl = _split(b)
    return _dot_nn(ah, bh) + _dot_nn(ah, bl) + _dot_nn(al, bh)


def _softmax_t(s, mask, scale):
    s = jnp.where(mask, s * scale, NEG_BIG)
    m = jnp.max(s, axis=0, keepdims=True)
    e = jnp.where(mask, jnp.exp(s - m), 0.0)
    l = jnp.maximum(jnp.sum(e, axis=0, keepdims=True), TINY)
    return e, l


def _softmax_parts(parts):
    masked = [s if m is None else jnp.where(m, s, NEG_BIG) for s, m in parts]
    mx = functools.reduce(jnp.maximum, [jnp.max(p, axis=0, keepdims=True) for p in masked])
    es = [jnp.exp2(p - mx) for p in masked]
    l = functools.reduce(jnp.add, [jnp.sum(e, axis=0, keepdims=True) for e in es])
    e = jnp.concatenate([e.astype(BF16) for e in es], axis=0) if len(es) > 1 else es[0].astype(BF16)
    return e, jnp.maximum(l, TINY)


def _rank_t(v, nb):
    blk = lax.broadcasted_iota(jnp.int32, v.shape, 0)
    rank = jnp.zeros(v.shape, F32)
    for j in range(nb):
        vj = v[j:j + 1, :]
        ge = (vj >= v).astype(F32)
        gt = (vj > v).astype(F32)
        rank = rank + jnp.where(blk > j, ge, gt)
    return rank


def _rms_kernel(x_ref, g_ref, o_ref):
    o_ref[...] = _rmsnorm(x_ref[...], g_ref[...]).astype(o_ref.dtype)


def _rms(x2d, g, out_dtype, tm=512):
    t, d = x2d.shape
    return pl.pallas_call(
        _rms_kernel,
        grid=(t // tm,),
        in_specs=[pl.BlockSpec((tm, d), lambda i: (i, 0)), pl.BlockSpec((1, d), lambda i: (0, 0))],
        out_specs=pl.BlockSpec((tm, d), lambda i: (i, 0)),
        out_shape=jax.ShapeDtypeStruct((t, d), out_dtype),
        compiler_params=_cparams(("parallel",)),
        name="rmsnorm",
    )(x2d, g.reshape(1, d))


def _ffn_up_kernel(h_ref, wg_ref, wu_ref, o_ref, wg_bf_ref, wu_bf_ref):
    @pl.when(pl.program_id(1) == 0)
    def _():
        wg_bf_ref[...] = wg_ref[...].astype(BF16)
        wu_bf_ref[...] = wu_ref[...].astype(BF16)

    h = h_ref[...]
    a = _dot_nn(h, wg_bf_ref[...])
    u = _dot_nn(h, wu_bf_ref[...])
    o_ref[...] = (a * _sigmoid(a) * u).astype(o_ref.dtype)


def _ffn_down_kernel(a_ref, wd_ref, x_ref, o_ref, wd_bf_ref):
    @pl.when(pl.program_id(1) == 0)
    def _():
        wd_bf_ref[...] = wd_ref[...].astype(BF16)

    o_ref[...] = x_ref[...] + 0.5 * _dot_nn(a_ref[...], wd_bf_ref[...])


def _ffn(x2d, g, wg, wu, wd, layer, tm=1024, tf=512, tn=512):
    t, d = x2d.shape
    f = wg.shape[2]
    h = _rms(x2d, g, BF16)
    act = pl.pallas_call(
        _ffn_up_kernel,
        grid=(f // tf, t // tm),
        in_specs=[
            pl.BlockSpec((tm, d), lambda j, i: (i, 0)),
            pl.BlockSpec((None, d, tf), lambda j, i: (layer, 0, j)),
            pl.BlockSpec((None, d, tf), lambda j, i: (layer, 0, j)),
        ],
        out_specs=pl.BlockSpec((tm, tf), lambda j, i: (i, j)),
        out_shape=jax.ShapeDtypeStruct((t, f), BF16),
        scratch_shapes=[pltpu.VMEM((d, tf), BF16), pltpu.VMEM((d, tf), BF16)],
        compiler_params=_cparams(("parallel", "arbitrary")),
        name="ffn_up",
    )(h, wg, wu)
    return pl.pallas_call(
        _ffn_down_kernel,
        grid=(d // tn, t // tm),
        in_specs=[
            pl.BlockSpec((tm, f), lambda n, i: (i, 0)),
            pl.BlockSpec((None, f, tn), lambda n, i: (layer, 0, n), pipeline_mode=pl.Buffered(1)),
            pl.BlockSpec((tm, tn), lambda n, i: (i, n)),
        ],
        out_specs=pl.BlockSpec((tm, tn), lambda n, i: (i, n)),
        out_shape=jax.ShapeDtypeStruct((t, d), F32),
        scratch_shapes=[pltpu.VMEM((f, tn), BF16)],
        compiler_params=_cparams(("parallel", "arbitrary")),
        name="ffn_down",
    )(act, wd, x2d)


def _in_proj_kernel(x_ref, g_ref, w_ref, z_ref, gate_ref, h_ref, *, n_z):
    j = pl.program_id(1)

    @pl.when(j == 0)
    def _():
        h_ref[...] = _rmsnorm(x_ref[...], g_ref[...]).astype(BF16)

    y = _dot_nn(h_ref[...], w_ref[...])

    @pl.when(j < n_z)
    def _():
        z_ref[...] = y

    @pl.when(j >= n_z)
    def _():
        gate_ref[...] = _sigmoid(y).astype(gate_ref.dtype)


def _in_proj(x2d, g, w, layer, tm=1024, tn=1024):
    t, d = x2d.shape
    n = w.shape[2]
    n_z = Z_COLS // tn
    n_g = (n - Z_COLS) // tn
    return pl.pallas_call(
        functools.partial(_in_proj_kernel, n_z=n_z),
        grid=(t // tm, n_z + n_g),
        in_specs=[
            pl.BlockSpec((tm, d), lambda i, j: (i, 0)),
            pl.BlockSpec((1, d), lambda i, j: (0, 0)),
            pl.BlockSpec((None, d, tn), lambda i, j: (layer, 0, j)),
        ],
        out_specs=[
            pl.BlockSpec((tm, tn), lambda i, j: (i, jnp.minimum(j, n_z - 1))),
            pl.BlockSpec((tm, tn), lambda i, j: (i, jnp.maximum(j - n_z, 0))),
        ],
        out_shape=[jax.ShapeDtypeStruct((t, Z_COLS), F32), jax.ShapeDtypeStruct((t, n - Z_COLS), BF16)],
        scratch_shapes=[pltpu.VMEM((tm, d), BF16)],
        compiler_params=_cparams(("parallel", "arbitrary")),
        name="in_proj",
    )(x2d, g.reshape(1, d), w)


def _moba_kernel(q_ref, k_ref, v_ref, cosf_ref, sinf_ref, cosq_ref, sinq_ref, o_ref,
                 kr_ref, km_ref, vt_ref, *, n_blk, scale):
    qi = pl.program_id(2)
    tq = q_ref.shape[1]
    hp = q_ref.shape[2] // LANES
    head = lambda hh: slice(hh * LANES, (hh + 1) * LANES)

    @pl.when(qi == 0)
    def _():
        for hh in range(hp):
            kr = _rope_full(k_ref[0, :, head(hh)], cosf_ref[...], sinf_ref[...])
            kr_ref[hh] = kr.astype(BF16)
            km_ref[hh] = jnp.concatenate(
                [jnp.mean(kr[j * MOBA_BLOCK:(j + 1) * MOBA_BLOCK], axis=0, keepdims=True) for j in range(n_blk)],
                axis=0)
            vt_ref[hh] = v_ref[0, :, head(hh)].T.astype(BF16)

    causal = (lax.broadcasted_iota(jnp.int32, (tq, tq), 0)
              <= lax.broadcasted_iota(jnp.int32, (tq, tq), 1))

    for n in range(n_blk):
        @pl.when(qi == n)
        def _(n=n):
            for hh in range(hp):
                qr = _rope_full(q_ref[0, :, head(hh)], cosq_ref[...], sinq_ref[...])
                q_bf = (qr * (scale * LOG2E)).astype(BF16)
                s = _dot_nt(kr_ref[hh, 0:(n + 1) * tq], q_bf)
                parts = []
                if n > MOBA_TOPK:
                    gate = _dot3_nt(km_ref[hh], qr)
                    past = lax.broadcasted_iota(jnp.int32, (n_blk, tq), 0) < n
                    gate = jnp.where(past, gate, -jnp.inf)
                    drop = jnp.where(_rank_t(gate, n) < MOBA_TOPK, 0.0, NEG_BIG)
                    for j in range(n):
                        parts.append((s[j * tq:(j + 1) * tq] + drop[j:j + 1, :], None))
                elif n > 0:
                    parts.append((s[0:n * tq], None))
                parts.append((s[n * tq:], causal))
                e, l = _softmax_parts(parts)
                o_t = _dot_nn(vt_ref[hh, :, 0:(n + 1) * tq], e) / l
                o_ref[0, :, head(hh)] = o_t.T.astype(o_ref.dtype)


def _moba(z3, cos, sin_s, tq=MOBA_BLOCK, hp=2):
    b, s, _ = z3.shape
    n_blk = s // MOBA_BLOCK
    w = hp * LANES
    qb, kb, vb = OFF_AQ // w, OFF_AK // w, OFF_AV // w
    return pl.pallas_call(
        functools.partial(_moba_kernel, n_blk=n_blk, scale=HEAD_DIM ** -0.5),
        grid=(b, MOBA_HEADS // hp, s // tq),
        in_specs=[
            pl.BlockSpec((1, tq, w), lambda bi, h, qi: (bi, qi, qb + h)),
            pl.BlockSpec((1, s, w), lambda bi, h, qi: (bi, 0, kb + h)),
            pl.BlockSpec((1, s, w), lambda bi, h, qi: (bi, 0, vb + h)),
            pl.BlockSpec((s, LANES), lambda bi, h, qi: (0, 0)),
            pl.BlockSpec((s, LANES), lambda bi, h, qi: (0, 0)),
            pl.BlockSpec((tq, LANES), lambda bi, h, qi: (qi, 0)),
            pl.BlockSpec((tq, LANES), lambda bi, h, qi: (qi, 0)),
        ],
        out_specs=pl.BlockSpec((1, tq, w), lambda bi, h, qi: (bi, qi, h)),
        out_shape=jax.ShapeDtypeStruct((b, s, MOBA_HEADS * HEAD_DIM), BF16),
        scratch_shapes=[pltpu.VMEM((hp, s, LANES), BF16), pltpu.VMEM((hp, n_blk, LANES), F32),
                        pltpu.VMEM((hp, LANES, s), BF16)],
        compiler_params=_cparams(("parallel", "parallel", "arbitrary")),
        name="moba",
    )(z3, z3, z3, cos, sin_s, cos, sin_s)


def _mla_kernel(cq_ref, ckv_ref, kr_ref, qg_ref, kvg_ref, wqn_ref, wqr_ref, wkn_ref, wv_ref,
                cosf_ref, sinf_ref, cosq_ref, sinq_ref, o_ref, kf_ref, vt_ref, *, scale):
    qi = pl.program_id(2)
    tq = cq_ref.shape[1]
    s_len = ckv_ref.shape[1]
    hp = wqn_ref.shape[1] // LANES
    head = lambda hh: slice(hh * LANES, (hh + 1) * LANES)

    @pl.when(qi == 0)
    def _():
        ckv = _rmsnorm(ckv_ref[0], kvg_ref[...]).astype(BF16)
        k_nope = _dot_nn(ckv, wkn_ref[...])
        v = _dot_nn(ckv, wv_ref[...])
        k_rope = _rope_half(kr_ref[0], cosf_ref[...], sinf_ref[...]).astype(BF16)
        for hh in range(hp):
            kf_ref[hh, :, :LANES] = k_nope[:, head(hh)].astype(BF16)
            kf_ref[hh, :, LANES:] = k_rope
            vt_ref[hh] = v[:, head(hh)].T.astype(BF16)

    cq = _rmsnorm(cq_ref[0], qg_ref[...]).astype(BF16)
    q_nope = _dot_nn(cq, wqn_ref[...])
    q_rope = _dot_nn(cq, wqr_ref[...])
    causal = (lax.broadcasted_iota(jnp.int32, (tq, tq), 0)
              <= lax.broadcasted_iota(jnp.int32, (tq, tq), 1))

    for n in range(s_len // tq):
        @pl.when(qi == n)
        def _(n=n):
            for hh in range(hp):
                q = jnp.concatenate([q_nope[:, head(hh)],
                                     _rope_half(q_rope[:, head(hh)], cosq_ref[...], sinq_ref[...])], axis=1)
                q_bf = (q * (scale * LOG2E)).astype(BF16)
                s = _dot_nt(kf_ref[hh, 0:(n + 1) * tq], q_bf)
                parts = [(s[0:n * tq], None)] if n > 0 else []
                parts.append((s[n * tq:], causal))
                e, l = _softmax_parts(parts)
                o_t = _dot_nn(vt_ref[hh, :, 0:(n + 1) * tq], e) / l
                o_ref[0, :, head(hh)] = o_t.T.astype(o_ref.dtype)


def _mla(z3, q_gain, kv_gain, w_uq, w_ukv, layer, cos, sin_s, tq=256, hp=2):
    b, s, _ = z3.shape
    w = hp * LANES
    ng = MLA_HEADS // hp
    return pl.pallas_call(
        functools.partial(_mla_kernel, scale=(MLA_NOPE + MLA_ROPE) ** -0.5),
        grid=(b, ng, s // tq),
        in_specs=[
            pl.BlockSpec((1, tq, MLA_Q_RANK), lambda bi, h, qi: (bi, qi, OFF_BCQ // MLA_Q_RANK)),
            pl.BlockSpec((1, s, MLA_KV_RANK), lambda bi, h, qi: (bi, 0, OFF_BCKV // MLA_KV_RANK)),
            pl.BlockSpec((1, s, LANES), lambda bi, h, qi: (bi, 0, OFF_BKR // LANES)),
            pl.BlockSpec((1, MLA_Q_RANK), lambda bi, h, qi: (0, 0)),
            pl.BlockSpec((1, MLA_KV_RANK), lambda bi, h, qi: (0, 0)),
            pl.BlockSpec((None, MLA_Q_RANK, w), lambda bi, h, qi: (layer, 0, h)),
            pl.BlockSpec((None, MLA_Q_RANK, w), lambda bi, h, qi: (layer, 0, ng + h)),
            pl.BlockSpec((None, MLA_KV_RANK, w), lambda bi, h, qi: (layer, 0, h)),
            pl.BlockSpec((None, MLA_KV_RANK, w), lambda bi, h, qi: (layer, 0, ng + h)),
            pl.BlockSpec((s, LANES), lambda bi, h, qi: (0, 0)),
            pl.BlockSpec((s, LANES), lambda bi, h, qi: (0, 0)),
            pl.BlockSpec((tq, LANES), lambda bi, h, qi: (qi, 0)),
            pl.BlockSpec((tq, LANES), lambda bi, h, qi: (qi, 0)),
        ],
        out_specs=pl.BlockSpec((1, tq, w), lambda bi, h, qi: (bi, qi, h)),
        out_shape=jax.ShapeDtypeStruct((b, s, MLA_HEADS * MLA_V), BF16),
        scratch_shapes=[pltpu.VMEM((hp, s, 2 * LANES), BF16), pltpu.VMEM((hp, LANES, s), BF16)],
        compiler_params=_cparams(("parallel", "parallel", "arbitrary")),
        name="mla",
    )(z3, z3, z3, q_gain.reshape(1, -1), kv_gain.reshape(1, -1), w_uq, w_uq, w_ukv, w_ukv,
      cos, sin_s, cos, sin_s)


def _cmp_kernel(x_ref, w1_ref, w2_ref, pos_ref, cos_ref, sin_ref, o_ref):
    kv = pl.program_id(0)
    n16 = o_ref.shape[3]
    x = jnp.concatenate(
        [x_ref[0, pl.ds(l, n16, stride=NSA_CMP_STRIDE), :] for l in range(NSA_CMP_STRIDE)], axis=1)
    half = x.shape[1]
    w1 = w1_ref[0]
    first = _dot3_nn(x, w1[:half])
    second = _dot3_nn(x, w1[half:])
    bias = _dot3_nn(jnp.broadcast_to(pos_ref[0], (8, 2 * half)), w1)[0:1]
    hid = first + pltpu.roll(second, n16 - 1, axis=0) + bias
    y = _dot3_nn(jax.nn.gelu(hid), w2_ref[0])
    roped = _rope_full(y, cos_ref[...], sin_ref[...])
    o_ref[0, 0, 0] = jnp.where(kv == 0, roped, y)


def _nsa_compress(z3, w1, w2, posflat, cos_c, sin_c):
    b, s, _ = z3.shape
    g = NSA_KV_GROUPS
    n16 = s // NSA_CMP_STRIDE
    width = NSA_CMP_STRIDE * HEAD_DIM
    hid = w1.shape[2]
    base = OFF_CKC // LANES
    return pl.pallas_call(
        _cmp_kernel,
        grid=(2, b, g),
        in_specs=[
            pl.BlockSpec((1, s, LANES), lambda kv, bi, gi: (bi, 0, base + g * kv + gi)),
            pl.BlockSpec((1, 2 * width, hid), lambda kv, bi, gi: (kv, 0, 0)),
            pl.BlockSpec((1, hid, HEAD_DIM), lambda kv, bi, gi: (kv, 0, 0)),
            pl.BlockSpec((1, 1, 2 * width), lambda kv, bi, gi: (kv, 0, 0)),
            pl.BlockSpec((n16, HEAD_DIM), lambda kv, bi, gi: (0, 0)),
            pl.BlockSpec((n16, HEAD_DIM), lambda kv, bi, gi: (0, 0)),
        ],
        out_specs=pl.BlockSpec((1, 1, 1, n16, HEAD_DIM), lambda kv, bi, gi: (kv, bi, gi, 0, 0)),
        out_shape=jax.ShapeDtypeStruct((2, b, g, n16, HEAD_DIM), F32),
        compiler_params=_cparams(("parallel", "parallel", "parallel")),
        name="nsa_compress",
    )(z3, w1, w2, posflat, cos_c, sin_c)


def _nsa_kernel(q_ref, ks_ref, vs_ref, kw_ref, vw_ref, kc_ref, vc_ref, gate_ref,
                cosf_ref, sinf_ref, cosq_ref, sinq_ref, ovl_ref, o_ref,
                ksr_ref, vst_ref, kw3_ref, vwt3_ref, vct_ref, os_ref, *, n_sel, scale):
    qi = pl.program_id(1)
    tq = q_ref.shape[1]
    s_len = ks_ref.shape[1]
    n_grp = ks_ref.shape[2] // HEAD_DIM
    rep = NSA_REP
    lanes = rep * tq
    n_chunk = s_len // tq
    pad_chunks = NSA_WINDOW // tq
    n_cmp = kc_ref.shape[3]
    grp = lambda gg: slice(gg * HEAD_DIM, (gg + 1) * HEAD_DIM)
    groups = range(n_grp)

    @pl.when(qi == 0)
    def _():
        for gg in groups:
            ksr_ref[gg] = _rope_full(ks_ref[0, :, grp(gg)], cosf_ref[...], sinf_ref[...]).astype(BF16)
            vst_ref[gg] = vs_ref[0, :, grp(gg)].T.astype(BF16)
            kw3_ref[gg, 0:pad_chunks] = jnp.zeros((pad_chunks, tq, HEAD_DIM), BF16)
            vwt3_ref[gg, 0:pad_chunks] = jnp.zeros((pad_chunks, HEAD_DIM, tq), BF16)
            kwr = _rope_full(kw_ref[0, :, grp(gg)], cosf_ref[...], sinf_ref[...]).astype(BF16)
            for c in range(n_chunk):
                kw3_ref[gg, pad_chunks + c] = kwr[c * tq:(c + 1) * tq]
                vwt3_ref[gg, pad_chunks + c] = vw_ref[0, c * tq:(c + 1) * tq, grp(gg)].T.astype(BF16)
            vct_ref[gg] = vc_ref[0, 0, gg].T.astype(BF16)

    cos_q = cosq_ref[...]
    sin_q = sinq_ref[...]
    lane = lax.broadcasted_iota(jnp.int32, (1, lanes), 1)
    t_loc = lane % tq
    t = qi * tq + t_loc
    cur = t[:, 0:tq] // NSA_SEL_BLOCK
    blk = lax.broadcasted_iota(jnp.int32, (n_sel, tq), 0)
    forced = (blk == 0) | (blk == cur) | (blk == cur - 1)
    visible = blk <= cur
    cmp_end = (lax.broadcasted_iota(jnp.int32, (n_cmp, lanes), 0) * NSA_CMP_STRIDE + (NSA_CMP_LEN - 1))
    ovl = ovl_ref[...]

    q_bf, o_c, imp = [], [], []
    for gg in groups:
        q_all = jnp.concatenate(
            [_rope_full(q_ref[0, :, grp(gg * rep + r)], cos_q, sin_q) for r in range(rep)],
            axis=0)
        q_bf.append((q_all * (scale * LOG2E)).astype(BF16))

        s_c = _dot3_nt(kc_ref[0, 0, gg], q_all)
        e_c, l_c = _softmax_t(s_c, cmp_end <= t, scale)
        p_c = e_c / l_c
        o_c.append(_dot_nn(vct_ref[gg], p_c.astype(BF16)))

        p_hi, p_lo = _split(p_c)
        imp3 = _dot_nn(ovl, p_hi) + _dot_nn(ovl, p_lo)
        imp_g = imp3[:, 0:tq]
        for r in range(1, rep):
            imp_g = imp_g + imp3[:, r * tq:(r + 1) * tq]
        imp_g = jnp.where(forced, FORCE_SCORE, imp_g)
        imp.append(jnp.where(visible, imp_g, -jnp.inf))

    unit = 2 * tq
    sb = NSA_SEL_BLOCK
    for v in range(s_len // unit):
        @pl.when(qi // 2 == v)
        def _(v=v):
            n_keys = (v + 1) * unit
            lo = n_keys - unit

            def causal(start, n):
                return lax.broadcasted_iota(jnp.int32, (n, lanes), 0) <= (t - start)

            n_vis = n_keys // sb
            for gg in groups:
                s = _dot_nt(ksr_ref[gg, 0:n_keys], q_bf[gg])
                if n_vis > NSA_SEL_TOPK:
                    rank = _rank_t(imp[gg], n_vis)
                    rank3 = jnp.concatenate([rank] * rep, axis=1)
                    drop = jnp.where(rank3 < NSA_SEL_TOPK, 0.0, NEG_BIG)
                    parts = []
                    for j in range(n_vis):
                        parts.append((s[j * sb:(j + 1) * sb] + drop[j:j + 1, :],
                                      causal(j * sb, sb) if j * sb >= lo else None))
                else:
                    parts = [(s[0:lo], None)] if lo > 0 else []
                    parts.append((s[lo:], causal(lo, unit)))
                e_s, l_s = _softmax_parts(parts)
                os_ref[gg] = _dot_nn(vst_ref[gg, :, 0:n_keys], e_s) / l_s

    n_win = pad_chunks + 1
    row = lax.broadcasted_iota(jnp.int32, (tq, lanes), 0)
    for gg in groups:
        parts = []
        for c in range(n_win):
            s_w = _dot_nt(kw3_ref[gg, qi + c], q_bf[gg])
            if c < pad_chunks:
                s_w = s_w + jnp.where(qi + c >= pad_chunks, 0.0, NEG_BIG)
            mask = (row > t_loc) if c == 0 else ((row <= t_loc) if c == n_win - 1 else None)
            parts.append((s_w, mask))
        e_w, l_w = _softmax_parts(parts)
        o_w = _dot_nn(vwt3_ref[gg, qi], e_w[0:tq])
        for c in range(1, n_win):
            o_w = o_w + _dot_nn(vwt3_ref[gg, qi + c], e_w[c * tq:(c + 1) * tq])
        o_w = o_w / l_w

        g_t = _sigmoid(gate_ref[0, :, grp(gg)].T)

        def gate_row(branch, g_t=g_t):
            return jnp.concatenate([g_t[3 * r + branch:3 * r + branch + 1, :] for r in range(rep)], axis=1)

        out_t = gate_row(0) * o_c[gg] + gate_row(1) * os_ref[gg] + gate_row(2) * o_w
        for r in range(rep):
            o_ref[0, :, grp(gg * rep + r)] = out_t[:, r * tq:(r + 1) * tq].T.astype(o_ref.dtype)


def _nsa(z3, cmp_kv, cos, sin_s, ovl_t, tq=128):
    b, s, _ = z3.shape
    g = NSA_KV_GROUPS
    n_sel = s // NSA_SEL_BLOCK
    n_cmp = cmp_kv.shape[3]
    qw = NSA_HEADS * HEAD_DIM
    kvw = g * HEAD_DIM
    n_chunk_pad = (s + NSA_WINDOW) // tq

    def col(off):
        return lambda bi, qi: (bi, 0, off // kvw)

    return pl.pallas_call(
        functools.partial(_nsa_kernel, n_sel=n_sel, scale=HEAD_DIM ** -0.5),
        grid=(b, s // tq),
        in_specs=[
            pl.BlockSpec((1, tq, qw), lambda bi, qi: (bi, qi, OFF_CQ // qw)),
            pl.BlockSpec((1, s, kvw), col(OFF_CKS)),
            pl.BlockSpec((1, s, kvw), col(OFF_CVS)),
            pl.BlockSpec((1, s, kvw), col(OFF_CKW)),
            pl.BlockSpec((1, s, kvw), col(OFF_CVW)),
            pl.BlockSpec((1, 1, g, n_cmp, HEAD_DIM), lambda bi, qi: (0, bi, 0, 0, 0)),
            pl.BlockSpec((1, 1, g, n_cmp, HEAD_DIM), lambda bi, qi: (1, bi, 0, 0, 0)),
            pl.BlockSpec((1, tq, g * LANES), lambda bi, qi: (bi, qi, OFF_CGATE // (g * LANES))),
            pl.BlockSpec((s, LANES), lambda bi, qi: (0, 0)),
            pl.BlockSpec((s, LANES), lambda bi, qi: (0, 0)),
            pl.BlockSpec((tq, LANES), lambda bi, qi: (qi, 0)),
            pl.BlockSpec((tq, LANES), lambda bi, qi: (qi, 0)),
            pl.BlockSpec((n_sel, n_cmp), lambda bi, qi: (0, 0)),
        ],
        out_specs=pl.BlockSpec((1, tq, qw), lambda bi, qi: (bi, qi, 0)),
        out_shape=jax.ShapeDtypeStruct((b, s, qw), BF16),
        scratch_shapes=[
            pltpu.VMEM((g, s, HEAD_DIM), BF16),
            pltpu.VMEM((g, HEAD_DIM, s), BF16),
            pltpu.VMEM((g, n_chunk_pad, tq, HEAD_DIM), BF16),
            pltpu.VMEM((g, n_chunk_pad, HEAD_DIM, tq), BF16),
            pltpu.VMEM((g, HEAD_DIM, n_cmp), BF16),
            pltpu.VMEM((g, HEAD_DIM, NSA_REP * tq), F32),
        ],
        compiler_params=_cparams(("parallel", "arbitrary")),
        name="nsa",
    )(z3, z3, z3, z3, z3, cmp_kv, cmp_kv, z3, cos, sin_s, cos, sin_s, ovl_t)


def _merge_kernel(ya_ref, yb_ref, yc_ref, ga_ref, gb_ref, gc_ref, wa_ref, wb_ref, wc_ref, o_ref,
                  wa_bf_ref, wb_bf_ref, wc_bf_ref):
    @pl.when(pl.program_id(1) == 0)
    def _():
        wa_bf_ref[...] = wa_ref[...].astype(BF16)
        wb_bf_ref[...] = wb_ref[...].astype(BF16)
        wc_bf_ref[...] = wc_ref[...].astype(BF16)

    m = ga_ref[...].astype(F32) * _dot_nn(ya_ref[...], wa_bf_ref[...])
    m = m + gb_ref[...].astype(F32) * _dot_nn(yb_ref[...], wb_bf_ref[...])
    m = m + gc_ref[...].astype(F32) * _dot_nn(yc_ref[...], wc_bf_ref[...])
    o_ref[...] = m.astype(o_ref.dtype)


def _merge(ya, yb, yc, gates, wa, wb, wc, layer, tm=1024, tn=1024):
    t = ya.shape[0]
    d = wa.shape[2]
    nj = d // tn

    def gate_spec(branch):
        return pl.BlockSpec((tm, tn), lambda j, i: (i, branch * nj + j))

    def y_spec(y):
        return pl.BlockSpec((tm, y.shape[1]), lambda j, i: (i, 0))

    def w_spec(w):
        return pl.BlockSpec((None, w.shape[1], tn), lambda j, i: (layer, 0, j), pipeline_mode=pl.Buffered(1))

    return pl.pallas_call(
        _merge_kernel,
        grid=(nj, t // tm),
        in_specs=[y_spec(ya), y_spec(yb), y_spec(yc), gate_spec(0), gate_spec(1), gate_spec(2),
                  w_spec(wa), w_spec(wb), w_spec(wc)],
        out_specs=pl.BlockSpec((tm, tn), lambda j, i: (i, j)),
        out_shape=jax.ShapeDtypeStruct((t, d), BF16),
        scratch_shapes=[pltpu.VMEM((w.shape[1], tn), BF16) for w in (wa, wb, wc)],
        compiler_params=_cparams(("parallel", "arbitrary")),
        name="merge",
    )(ya, yb, yc, gates, gates, gates, wa, wb, wc)


def _proj_residual_kernel(m_ref, w_ref, x_ref, o_ref, w_bf_ref):
    @pl.when(pl.program_id(1) == 0)
    def _():
        w_bf_ref[...] = w_ref[...].astype(BF16)

    o_ref[...] = x_ref[...] + _dot_nn(m_ref[...], w_bf_ref[...])


def _proj_residual(m, w, x2d, layer, tm=1024, tn=1024):
    t, k = m.shape
    d = w.shape[2]
    return pl.pallas_call(
        _proj_residual_kernel,
        grid=(d // tn, t // tm),
        in_specs=[
            pl.BlockSpec((tm, k), lambda j, i: (i, 0)),
            pl.BlockSpec((None, k, tn), lambda j, i: (layer, 0, j), pipeline_mode=pl.Buffered(1)),
            pl.BlockSpec((tm, tn), lambda j, i: (i, j)),
        ],
        out_specs=pl.BlockSpec((tm, tn), lambda j, i: (i, j)),
        out_shape=jax.ShapeDtypeStruct((t, d), F32),
        scratch_shapes=[pltpu.VMEM((k, tn), BF16)],
        compiler_params=_cparams(("parallel", "arbitrary")),
        name="out_proj",
    )(m, w, x2d)


def _rope_tables(pos, dim, width):
    inv = 1.0 / (ROPE_THETA ** (jnp.arange(0, dim, 2, dtype=F32) / dim))
    ang = pos.astype(F32)[:, None] * inv[None, :]
    cos = jnp.cos(ang)
    sin = jnp.sin(ang)
    cos = jnp.concatenate([cos, cos], axis=-1)
    sin_s = jnp.concatenate([-sin, sin], axis=-1)
    pad = ((0, 0), (0, width - dim))
    return jnp.pad(cos, pad), jnp.pad(sin_s, pad)


def _repack_kernel(srcp_ref, srcq_ref, start_ref, valid_ref, p_ref, q_ref, o_ref, *, n_src_cols):
    j = pl.program_id(1)
    start = start_ref[j]
    valid = valid_ref[j]
    row = lax.broadcasted_iota(jnp.int32, (LANES, LANES), 0)
    col = lax.broadcasted_iota(jnp.int32, (LANES, LANES), 1)
    lane = lax.broadcasted_iota(jnp.int32, (1, LANES), 1)

    def contribution(blk_ref, src_blk):
        take = jnp.where((src_blk * LANES + row == start + col) & (col < valid), 1.0, 0.0).astype(BF16)
        blk = jnp.where(src_blk * LANES + lane < n_src_cols, blk_ref[...], 0.0).astype(BF16)
        return _dot_nn(blk, take)

    o_ref[...] = (contribution(p_ref, srcp_ref[j]) + contribution(q_ref, srcq_ref[j])).astype(o_ref.dtype)


def _repack(w, pieces, n_out_blocks):
    depth, k, n = w.shape
    n_src_blocks = -(-n // LANES)
    starts, valids = [], []
    for start, width in pieces:
        if width == 0:
            starts.append(0)
            valids.append(0)
        for off in range(0, width, LANES):
            starts.append(start + off)
            valids.append(min(LANES, width - off))
    starts += [0] * (n_out_blocks - len(starts))
    valids += [0] * (n_out_blocks - len(valids))
    assert n_src_blocks >= 2
    slots = [0, 1]
    srcp, srcq = [], []
    for start, valid in zip(starts, valids):
        need = []
        if valid > 0:
            need = list(range(start // LANES, (start + valid - 1) // LANES + 1))
        for blk in need:
            if blk not in slots:
                slots[0 if slots[0] not in need else 1] = blk
        srcp.append(slots[0])
        srcq.append(slots[1])
    as_i32 = lambda v: jnp.asarray(np.asarray(v, np.int32))
    grid_spec = pltpu.PrefetchScalarGridSpec(
        num_scalar_prefetch=4,
        grid=(depth, n_out_blocks),
        in_specs=[
            pl.BlockSpec((None, k, LANES), lambda d, j, sp, sq, st, va: (d, 0, sp[j])),
            pl.BlockSpec((None, k, LANES), lambda d, j, sp, sq, st, va: (d, 0, sq[j])),
        ],
        out_specs=pl.BlockSpec((None, k, LANES), lambda d, j, sp, sq, st, va: (d, 0, j)),
    )
    return pl.pallas_call(
        functools.partial(_repack_kernel, n_src_cols=n),
        grid_spec=grid_spec,
        out_shape=jax.ShapeDtypeStruct((depth, k, n_out_blocks * LANES), BF16),
        compiler_params=_cparams(("parallel", "arbitrary")),
        name="repack",
    )(as_i32(srcp), as_i32(srcq), as_i32(starts), as_i32(valids), w, w)


def _pack_w_in(w):
    sizes = [MOBA_HEADS * HEAD_DIM] * 3 + [MLA_Q_RANK, MLA_KV_RANK, MLA_ROPE, NSA_HEADS * HEAD_DIM] \
        + [NSA_KV_GROUPS * HEAD_DIM] * 6
    offs = [int(v) for v in np.concatenate([[0], np.cumsum(sizes)])]
    (a_q, a_k, a_v, b_cq, b_ckv, b_kr, c_q, c_kc, c_vc, c_ks, c_vs, c_kw, c_vw) = [
        (offs[i], sizes[i]) for i in range(len(sizes))]
    gate0 = offs[-1]
    per_group = NSA_REP * 3
    gates = [(gate0 + gi * per_group, per_group) for gi in range(NSA_KV_GROUPS)]
    merge = (gate0 + NSA_HEADS * 3, w.shape[2] - gate0 - NSA_HEADS * 3)
    pieces = [c_q, b_ckv, b_cq, a_q, a_k, a_v, c_kc, c_vc, c_ks, c_vs, c_kw, c_vw] + gates + [b_kr]
    used = sum(-(-width // LANES) for _, width in pieces)
    pieces += [(0, 0)] * (Z_COLS // LANES - used) + [merge]
    return _repack(w, pieces, (Z_COLS + merge[1]) // LANES)


def _pack_w_uq(w):
    per_head = MLA_NOPE + MLA_ROPE
    pieces = [(h * per_head, MLA_NOPE) for h in range(MLA_HEADS)] \
        + [(h * per_head + MLA_NOPE, MLA_ROPE) for h in range(MLA_HEADS)]
    return _repack(w, pieces, 2 * MLA_HEADS)


def _pack_w_ukv(w):
    per_head = MLA_NOPE + MLA_V
    pieces = [(h * per_head, MLA_NOPE) for h in range(MLA_HEADS)] \
        + [(h * per_head + MLA_NOPE, MLA_V) for h in range(MLA_HEADS)]
    return _repack(w, pieces, 2 * MLA_HEADS)


def kernel(x, ffn1_norm, ffn1_w_gate, ffn1_w_up, ffn1_w_down, mix_norm, w_in, mla_q_norm, mla_w_uq, mla_kv_norm, mla_w_ukv, nsa_cmp_pos_k, nsa_cmp_w1_k, nsa_cmp_w2_k, nsa_cmp_pos_v, nsa_cmp_w1_v, nsa_cmp_w2_v, w_branch_moba, w_branch_mla, w_branch_nsa, w_out, ffn2_norm, ffn2_w_gate, ffn2_w_up, ffn2_w_down, final_norm):
    b, s, d = x.shape
    depth = w_in.shape[0]
    t = b * s
    assert NSA_CMP_LEN == 2 * NSA_CMP_STRIDE and s % MOBA_BLOCK == 0 and s % NSA_SEL_BLOCK == 0

    pos = jnp.arange(s)
    cos_h, sin_h = _rope_tables(pos, HEAD_DIM, LANES)
    cos_r, sin_r = _rope_tables(pos, MLA_ROPE, LANES)
    n16 = s // NSA_CMP_STRIDE
    cmp_end = np.arange(n16) * NSA_CMP_STRIDE + NSA_CMP_LEN - 1
    cos_c, sin_c = _rope_tables(jnp.asarray(cmp_end), HEAD_DIM, LANES)
    n_sel = s // NSA_SEL_BLOCK
    sel_start = np.arange(n_sel) * NSA_SEL_BLOCK
    cmp_start = np.arange(n16) * NSA_CMP_STRIDE
    ovl_t = ((cmp_start[None, :] <= sel_start[:, None] + NSA_SEL_BLOCK - 1)
             & (cmp_end[None, :] >= sel_start[:, None])).astype(np.float32)
    ovl_t = jnp.asarray(ovl_t, BF16)

    w_in_p = _pack_w_in(w_in)
    w_uq_p = _pack_w_uq(mla_w_uq)
    w_ukv_p = _pack_w_ukv(mla_w_ukv)

    xf = x.reshape(t, d)
    for i in range(depth):
        xf = _ffn(xf, ffn1_norm[i], ffn1_w_gate, ffn1_w_up, ffn1_w_down, i)

        z, gates = _in_proj(xf, mix_norm[i], w_in_p, i)
        z3 = z.reshape(b, s, Z_COLS)
        y_a = _moba(z3, cos_h, sin_h)
        y_b = _mla(z3, mla_q_norm[i], mla_kv_norm[i], w_uq_p, w_ukv_p, i, cos_r, sin_r)

        w1 =jnp.stack([nsa_cmp_w1_k[i], nsa_cmp_w1_v[i]])
        w2 = jnp.stack([nsa_cmp_w2_k[i], nsa_cmp_w2_v[i]])
        posflat = jnp.stack([nsa_cmp_pos_k[i].reshape(1, -1), nsa_cmp_pos_v[i].reshape(1, -1)])
        cmp_kv = _nsa_compress(z3, w1, w2, posflat, cos_c, sin_c)
        y_c = _nsa(z3, cmp_kv, cos_h, sin_h, ovl_t)

        merged = _merge(y_a.reshape(t, -1), y_b.reshape(t, -1), y_c.reshape(t, -1), gates,
                        w_branch_moba, w_branch_mla, w_branch_nsa, i)
        xf = _proj_residual(merged, w_out, xf, i)

        xf = _ffn(xf, ffn2_norm[i], ffn2_w_gate, ffn2_w_up, ffn2_w_down, i)
    return _rms(xf, final_norm, F32).reshape(b, s, d)
```

```python
import functools

import numpy as np
import jax
import jax.numpy as jnp
from jax import lax
from jax.experimental import pallas as pl
from jax.experimental.pallas import tpu as pltpu

F32 = jnp.float32
BF16 = jnp.bfloat16

HEAD_DIM = 128
ROPE_THETA = 10000.0
NORM_EPS = 1e-6
NEG_BIG = -1e30
TINY = 1e-30
LOG2E = 1.4426950408889634
N_BRANCHES = 3

MOBA_HEADS = 4
MOBA_BLOCK = 256
MOBA_TOPK = 3

MLA_HEADS = 6
MLA_Q_RANK = 512
MLA_KV_RANK = 256
MLA_NOPE = 128
MLA_ROPE = 64
MLA_V = 128

NSA_HEADS = 6
NSA_KV_GROUPS = 2
NSA_REP = NSA_HEADS // NSA_KV_GROUPS
NSA_CMP_LEN = 32
NSA_CMP_STRIDE = 16
NSA_CMP_HIDDEN = 256
NSA_SEL_BLOCK = 64
NSA_SEL_TOPK = 16
NSA_WINDOW = 512
FORCE_SCORE = 1e9

LANES = 128
VMEM_LIMIT = 56 * 2**20

OFF_CQ = 0
OFF_BCKV = 768
OFF_BCQ = 1024
OFF_AQ = 1536
OFF_AK = 2048
OFF_AV = 2560
OFF_CKC = 3072
OFF_CVC = 3328
OFF_CKS = 3584
OFF_CVS = 3840
OFF_CKW = 4096
OFF_CVW = 4352
OFF_CGATE = 4608
OFF_BKR = 4864
Z_COLS = 5120


def _cparams(sem):
    return pltpu.CompilerParams(dimension_semantics=sem, vmem_limit_bytes=VMEM_LIMIT)


def _rmsnorm(x, g):
    ms = jnp.mean(x * x, axis=-1, keepdims=True)
    return x * lax.rsqrt(ms + NORM_EPS) * g


def _sigmoid(x):
    return 0.5 * jnp.tanh(0.5 * x) + 0.5


def _rope_full(x, cos, sin_signed):
    return x * cos + pltpu.roll(x, HEAD_DIM // 2, axis=1) * sin_signed


def _rope_half(x, cos, sin_signed):
    lane = lax.broadcasted_iota(jnp.int32, x.shape, 1)
    q = MLA_ROPE // 2
    swapped = jnp.where(lane < q, pltpu.roll(x, LANES - q, axis=1), pltpu.roll(x, q, axis=1))
    return x * cos + swapped * sin_signed


def _dot_nt(a, b):
    return lax.dot_general(a, b, (((1,), (1,)), ((), ())), preferred_element_type=F32)


def _dot_nn(a, b):
    return jnp.dot(a, b, preferred_element_type=F32)


def _split(a):
    hi = a.astype(BF16)
    lo = (a - hi.astype(F32)).astype(BF16)
    return hi, lo


def _dot3_nt(a, b):
    ah, al = _split(a)
    bh, bl = _split(b)
    return _dot_nt(ah, bh) + _dot_nt(ah, bl) + _dot_nt(al, bh)


def _dot3_nn(a, b):
    ah, al = _split(a)
    bh, bl = _split(b)
    return _dot_nn(ah, bh) + _dot_nn(ah, bl) + _dot_nn(al, bh)


def _softmax_t(s, mask, scale):
    s = jnp.where(mask, s * scale, NEG_BIG)
    m = jnp.max(s, axis=0, keepdims=True)
    e = jnp.where(mask, jnp.exp(s - m), 0.0)
    l = jnp.maximum(jnp.sum(e, axis=0, keepdims=True), TINY)
    return e, l


def _softmax_parts(parts):
    masked = [s if m is None else jnp.where(m, s, NEG_BIG) for s, m in parts]
    mx = functools.reduce(jnp.maximum, [jnp.max(p, axis=0, keepdims=True) for p in masked])
    es = [jnp.exp2(p - mx) for p in masked]
    l = functools.reduce(jnp.add, [jnp.sum(e, axis=0, keepdims=True) for e in es])
    e = jnp.concatenate([e.astype(BF16) for e in es], axis=0) if len(es) > 1 else es[0].astype(BF16)
    return e, jnp.maximum(l, TINY)


def _rank_t(v, nb):
    blk = lax.broadcasted_iota(jnp.int32, v.shape, 0)
    rank = jnp.zeros(v.shape, F32)
    for j in range(nb):
        vj = v[j:j + 1, :]
        ge = (vj >= v).astype(F32)
        gt = (vj > v).astype(F32)
        rank = rank + jnp.where(blk > j, ge, gt)
    return rank


def _rms_kernel(x_ref, g_ref, o_ref):
    o_ref[...] = _rmsnorm(x_ref[...], g_ref[...]).astype(o_ref.dtype)


def _rms(x2d, g, out_dtype, tm=512):
    t, d = x2d.shape
    return pl.pallas_call(
        _rms_kernel,
        grid=(t // tm,),
        in_specs=[pl.BlockSpec((tm, d), lambda i: (i, 0)), pl.BlockSpec((1, d), lambda i: (0, 0))],
        out_specs=pl.BlockSpec((tm, d), lambda i: (i, 0)),
        out_shape=jax.ShapeDtypeStruct((t, d), out_dtype),
        compiler_params=_cparams(("parallel",)),
        name="rmsnorm",
    )(x2d, g.reshape(1, d))


def _ffn_up_kernel(h_ref, wg_ref, wu_ref, wd_ref, o_ref, wd_bf_out_ref, wg_bf_ref, wu_bf_ref):
    @pl.when(pl.program_id(1) == 0)
    def _():
        wg_bf_ref[...] = wg_ref[...].astype(BF16)
        wu_bf_ref[...] = wu_ref[...].astype(BF16)
        wd_bf_out_ref[...] = wd_ref[...].astype(BF16)

    h = h_ref[...]
    a = _dot_nn(h, wg_bf_ref[...])
    u = _dot_nn(h, wu_bf_ref[...])
    o_ref[...] = (a * _sigmoid(a) * u).astype(o_ref.dtype)


def _ffn_down_kernel(a_ref, wd_ref, x_ref, o_ref):
    o_ref[...] = x_ref[...] + 0.5 * _dot_nn(a_ref[...], wd_ref[...])


def _ffn(x2d, g, wg, wu, wd, layer, tm=1024, tf=512, tm_down=512, tn=1024):
    t, d = x2d.shape
    f = wg.shape[2]
    h = _rms(x2d, g, BF16)
    act, wd_bf = pl.pallas_call(
        _ffn_up_kernel,
        grid=(f // tf, t // tm),
        in_specs=[
            pl.BlockSpec((tm, d), lambda j, i: (i, 0)),
            pl.BlockSpec((None, d, tf), lambda j, i: (layer, 0, j)),
            pl.BlockSpec((None, d, tf), lambda j, i: (layer, 0, j)),
            pl.BlockSpec((None, tf, d), lambda j, i: (layer, j, 0)),
        ],
        out_specs=[
            pl.BlockSpec((tm, tf), lambda j, i: (i, j)),
            pl.BlockSpec((tf, d), lambda j, i: (j, 0)),
        ],
        out_shape=[jax.ShapeDtypeStruct((t, f), BF16), jax.ShapeDtypeStruct((f, d), BF16)],
        scratch_shapes=[pltpu.VMEM((d, tf), BF16), pltpu.VMEM((d, tf), BF16)],
        compiler_params=_cparams(("parallel", "arbitrary")),
        name="ffn_up",
    )(h, wg, wu, wd)
    return pl.pallas_call(
        _ffn_down_kernel,
        grid=(d // tn, t // tm_down),
        in_specs=[
            pl.BlockSpec((tm_down, f), lambda n, i: (i, 0)),
            pl.BlockSpec((f, tn), lambda n, i: (0, n)),
            pl.BlockSpec((tm_down, tn), lambda n, i: (i, n)),
        ],
        out_specs=pl.BlockSpec((tm_down, tn), lambda n, i: (i, n)),
        out_shape=jax.ShapeDtypeStruct((t, d), F32),
        compiler_params=_cparams(("parallel", "parallel")),
        name="ffn_down",
    )(act, wd_bf, x2d)


def _in_proj_kernel(x_ref, g_ref, w_ref, z_ref, gate_ref, h_ref, *, n_z):
    j = pl.program_id(1)

    @pl.when(j == 0)
    def _():
        h_ref[...] = _rmsnorm(x_ref[...], g_ref[...]).astype(BF16)

    @pl.when(j < n_z)
    def _():
        z_ref[...] = _dot_nn(h_ref[...], w_ref[...])

    @pl.when(j >= n_z)
    def _():
        gate_ref[...] = _sigmoid(_dot_nn(h_ref[...], w_ref[...])).astype(gate_ref.dtype)


def _in_proj(x2d, g, w, layer, tm=1024, tn=1024):
    t, d = x2d.shape
    n = w.shape[2]
    n_z = Z_COLS // tn
    n_g = (n - Z_COLS) // tn
    return pl.pallas_call(
        functools.partial(_in_proj_kernel, n_z=n_z),
        grid=(t // tm, n_z + n_g),
        in_specs=[
            pl.BlockSpec((tm, d), lambda i, j: (i, 0)),
            pl.BlockSpec((1, d), lambda i, j: (0, 0)),
            pl.BlockSpec((None, d, tn), lambda i, j: (layer, 0, j)),
        ],
        out_specs=[
            pl.BlockSpec((tm, tn), lambda i, j: (i, jnp.minimum(j, n_z - 1))),
            pl.BlockSpec((tm, tn), lambda i, j: (i, jnp.maximum(j - n_z, 0))),
        ],
        out_shape=[jax.ShapeDtypeStruct((t, Z_COLS), F32), jax.ShapeDtypeStruct((t, n - Z_COLS), BF16)],
        scratch_shapes=[pltpu.VMEM((tm, d), BF16)],
        compiler_params=_cparams(("parallel", "arbitrary")),
        name="in_proj",
    )(x2d, g.reshape(1, d), w)


def _moba_kernel(q_ref, k_ref, v_ref, cosf_ref, sinf_ref, cosq_ref, sinq_ref, o_ref,
                 kr_ref, km_ref, vt_ref, *, n_blk, scale):
    qi = pl.program_id(2)
    tq = q_ref.shape[1]
    hp = q_ref.shape[2] // LANES
    head = lambda hh: slice(hh * LANES, (hh + 1) * LANES)

    @pl.when(qi == 0)
    def _():
        for hh in range(hp):
            kr = _rope_full(k_ref[0, :, head(hh)], cosf_ref[...], sinf_ref[...])
            kr_ref[hh] = kr.astype(BF16)
            km_ref[hh] = jnp.concatenate(
                [jnp.mean(kr[j * MOBA_BLOCK:(j + 1) * MOBA_BLOCK], axis=0, keepdims=True) for j in range(n_blk)],
                axis=0)
            vt_ref[hh] = v_ref[0, :, head(hh)].T.astype(BF16)

    causal = (lax.broadcasted_iota(jnp.int32, (tq, tq), 0)
              <= lax.broadcasted_iota(jnp.int32, (tq, tq), 1))

    for n in range(n_blk):
        @pl.when(qi == n)
        def _(n=n):
            for hh in range(hp):
                qr = _rope_full(q_ref[0, :, head(hh)], cosq_ref[...], sinq_ref[...])
                q_bf = (qr * (scale * LOG2E)).astype(BF16)
                s = _dot_nt(kr_ref[hh, 0:(n + 1) * tq], q_bf)
                parts = []
                if n > MOBA_TOPK:
                    gate = _dot3_nt(km_ref[hh], qr)
                    past = lax.broadcasted_iota(jnp.int32, (n_blk, tq), 0) < n
                    gate = jnp.where(past, gate, -jnp.inf)
                    drop = jnp.where(_rank_t(gate, n) < MOBA_TOPK, 0.0, NEG_BIG)
                    for j in range(n):
                        parts.append((s[j * tq:(j + 1) * tq] + drop[j:j + 1, :], None))
                elif n > 0:
                    parts.append((s[0:n * tq], None))
                parts.append((s[n * tq:], causal))
                e, l = _softmax_parts(parts)
                o_t = _dot_nn(vt_ref[hh, :, 0:(n + 1) * tq], e) / l
                o_ref[0, :, head(hh)] = o_t.T.astype(o_ref.dtype)


def _moba(z3, cos, sin_s, tq=MOBA_BLOCK, hp=MOBA_HEADS):
    b, s, _ = z3.shape
    n_blk = s // MOBA_BLOCK
    w = hp * LANES
    qb, kb, vb = OFF_AQ // w, OFF_AK // w, OFF_AV // w
    return pl.pallas_call(
        functools.partial(_moba_kernel, n_blk=n_blk, scale=HEAD_DIM ** -0.5),
        grid=(b, MOBA_HEADS // hp, s // tq),
        in_specs=[
            pl.BlockSpec((1, tq, w), lambda bi, h, qi: (bi, qi, qb + h)),
            pl.BlockSpec((1, s, w), lambda bi, h, qi: (bi, 0, kb + h)),
            pl.BlockSpec((1, s, w), lambda bi, h, qi: (bi, 0, vb + h)),
            pl.BlockSpec((s, LANES), lambda bi, h, qi: (0, 0)),
            pl.BlockSpec((s, LANES), lambda bi, h, qi: (0, 0)),
            pl.BlockSpec((tq, LANES), lambda bi, h, qi: (qi, 0)),
            pl.BlockSpec((tq, LANES), lambda bi, h, qi: (qi, 0)),
        ],
        out_specs=pl.BlockSpec((1, tq, w), lambda bi, h, qi: (bi, qi, h)),
        out_shape=jax.ShapeDtypeStruct((b, s, MOBA_HEADS * HEAD_DIM), BF16),
        scratch_shapes=[pltpu.VMEM((hp, s, LANES), BF16), pltpu.VMEM((hp, n_blk, LANES), F32),
                        pltpu.VMEM((hp, LANES, s), BF16)],
        compiler_params=_cparams(("parallel", "parallel", "arbitrary")),
        name="moba",
    )(z3, z3, z3, cos, sin_s, cos, sin_s)


def _mla_kernel(cq_ref, ckv_ref, kr_ref, qg_ref, kvg_ref, wqn_ref, wqr_ref, wkn_ref, wv_ref,
                cosf_ref, sinf_ref, cosq_ref, sinq_ref, o_ref, kf_ref, vt_ref, *, scale):
    qi = pl.program_id(2)
    tq = cq_ref.shape[1]
    s_len = ckv_ref.shape[1]
    hp = wqn_ref.shape[1] // LANES
    head = lambda hh: slice(hh * LANES, (hh + 1) * LANES)

    @pl.when(qi == 0)
    def _():
        ckv = _rmsnorm(ckv_ref[0], kvg_ref[...]).astype(BF16)
        k_nope = _dot_nn(ckv, wkn_ref[...])
        v = _dot_nn(ckv, wv_ref[...])
        k_rope = _rope_half(kr_ref[0], cosf_ref[...], sinf_ref[...]).astype(BF16)
        for hh in range(hp):
            kf_ref[hh, :, :LANES] = k_nope[:, head(hh)].astype(BF16)
            kf_ref[hh, :, LANES:] = k_rope
            vt_ref[hh] = v[:, head(hh)].T.astype(BF16)

    cq = _rmsnorm(cq_ref[0], qg_ref[...]).astype(BF16)
    q_nope = _dot_nn(cq, wqn_ref[...])
    q_rope = _dot_nn(cq, wqr_ref[...])
    causal = (lax.broadcasted_iota(jnp.int32, (tq, tq), 0)
              <= lax.broadcasted_iota(jnp.int32, (tq, tq), 1))

    for n in range(s_len // tq):
        @pl.when(qi == n)
        def _(n=n):
            for hh in range(hp):
                q = jnp.concatenate([q_nope[:, head(hh)],
                                     _rope_half(q_rope[:, head(hh)], cosq_ref[...], sinq_ref[...])], axis=1)
                q_bf = (q * (scale * LOG2E)).astype(BF16)
                s = _dot_nt(kf_ref[hh, 0:(n + 1) * tq], q_bf)
                parts = [(s[0:n * tq], None)] if n > 0 else []
                parts.append((s[n * tq:], causal))
                e, l = _softmax_parts(parts)
                o_t = _dot_nn(vt_ref[hh, :, 0:(n + 1) * tq], e) / l
                o_ref[0, :, head(hh)] = o_t.T.astype(o_ref.dtype)


def _mla(z3, q_gain, kv_gain, w_uq, w_ukv, layer, cos, sin_s, tq=256, hp=3):
    b, s, _ = z3.shape
    w = hp * LANES
    ng = MLA_HEADS // hp
    return pl.pallas_call(
        functools.partial(_mla_kernel, scale=(MLA_NOPE + MLA_ROPE) ** -0.5),
        grid=(b, ng, s // tq),
        in_specs=[
            pl.BlockSpec((1, tq, MLA_Q_RANK), lambda bi, h, qi: (bi, qi, OFF_BCQ // MLA_Q_RANK)),
            pl.BlockSpec((1, s, MLA_KV_RANK), lambda bi, h, qi: (bi, 0, OFF_BCKV // MLA_KV_RANK)),
            pl.BlockSpec((1, s, LANES), lambda bi, h, qi: (bi, 0, OFF_BKR // LANES)),
            pl.BlockSpec((1, MLA_Q_RANK), lambda bi, h, qi: (0, 0)),
            pl.BlockSpec((1, MLA_KV_RANK), lambda bi, h, qi: (0, 0)),
            pl.BlockSpec((None, MLA_Q_RANK, w), lambda bi, h, qi: (layer, 0, h)),
            pl.BlockSpec((None, MLA_Q_RANK, w), lambda bi, h, qi: (layer, 0, ng + h)),
            pl.BlockSpec((None, MLA_KV_RANK, w), lambda bi, h, qi: (layer, 0, h)),
            pl.BlockSpec((None, MLA_KV_RANK, w), lambda bi, h, qi: (layer, 0, ng + h)),
            pl.BlockSpec((s, LANES), lambda bi, h, qi: (0, 0)),
            pl.BlockSpec((s, LANES), lambda bi, h, qi: (0, 0)),
            pl.BlockSpec((tq, LANES), lambda bi, h, qi: (qi, 0)),
            pl.BlockSpec((tq, LANES), lambda bi, h, qi: (qi, 0)),
        ],
        out_specs=pl.BlockSpec((1, tq, w), lambda bi, h, qi: (bi, qi, h)),
        out_shape=jax.ShapeDtypeStruct((b, s, MLA_HEADS * MLA_V), BF16),
        scratch_shapes=[pltpu.VMEM((hp, s, 2 * LANES), BF16), pltpu.VMEM((hp, LANES, s), BF16)],
        compiler_params=_cparams(("parallel", "parallel", "arbitrary")),
        name="mla",
    )(z3, z3, z3, q_gain.reshape(1, -1), kv_gain.reshape(1, -1), w_uq, w_uq, w_ukv, w_ukv,
      cos, sin_s, cos, sin_s)


def _cmp_kernel(x_ref, w1_ref, w2_ref, pos_ref, cos_ref, sin_ref, o_ref):
    kv = pl.program_id(0)
    n16 = o_ref.shape[3]
    x = jnp.concatenate(
        [x_ref[0, pl.ds(l, n16, stride=NSA_CMP_STRIDE), :] for l in range(NSA_CMP_STRIDE)], axis=1)
    half = x.shape[1]
    w1 = w1_ref[0]
    first = _dot3_nn(x, w1[:half])
    second = _dot3_nn(x, w1[half:])
    bias = _dot3_nn(jnp.broadcast_to(pos_ref[0], (8, 2 * half)), w1)[0:1]
    hid = first + pltpu.roll(second, n16 - 1, axis=0) + bias
    y = _dot3_nn(jax.nn.gelu(hid), w2_ref[0])
    roped = _rope_full(y, cos_ref[...], sin_ref[...])
    o_ref[0, 0, 0] = jnp.where(kv == 0, roped, y)


def _nsa_compress(z3, w1, w2, posflat, cos_c, sin_c):
    b, s, _ = z3.shape
    g = NSA_KV_GROUPS
    n16 = s // NSA_CMP_STRIDE
    width = NSA_CMP_STRIDE * HEAD_DIM
    hid = w1.shape[2]
    base = OFF_CKC // LANES
    return pl.pallas_call(
        _cmp_kernel,
        grid=(2, b, g),
        in_specs=[
            pl.BlockSpec((1, s, LANES), lambda kv, bi, gi: (bi, 0, base + g * kv + gi)),
            pl.BlockSpec((1, 2 * width, hid), lambda kv, bi, gi: (kv, 0, 0)),
            pl.BlockSpec((1, hid, HEAD_DIM), lambda kv, bi, gi: (kv, 0, 0)),
            pl.BlockSpec((1, 1, 2 * width), lambda kv, bi, gi: (kv, 0, 0)),
            pl.BlockSpec((n16, HEAD_DIM), lambda kv, bi, gi: (0, 0)),
            pl.BlockSpec((n16, HEAD_DIM), lambda kv, bi, gi: (0, 0)),
        ],
        out_specs=pl.BlockSpec((1, 1, 1, n16, HEAD_DIM), lambda kv, bi, gi: (kv, bi, gi, 0, 0)),
        out_shape=jax.ShapeDtypeStruct((2, b, g, n16, HEAD_DIM), F32),
        compiler_params=_cparams(("parallel", "parallel", "parallel")),
        name="nsa_compress",
    )(z3, w1, w2, posflat, cos_c, sin_c)


def _nsa_kernel(q_ref, ks_ref, vs_ref, kw_ref, vw_ref, kc_ref, vc_ref, gate_ref,
                cosf_ref, sinf_ref, cosq_ref, sinq_ref, ovl_ref, o_ref,
                ksr_ref, vst_ref, kw3_ref, vwt3_ref, vct_ref, os_ref, *, n_sel, scale):
    qi = pl.program_id(1)
    tq = q_ref.shape[1]
    s_len = ks_ref.shape[1]
    n_grp = ks_ref.shape[2] // HEAD_DIM
    rep = NSA_REP
    lanes = rep * tq
    n_chunk = s_len // tq
    pad_chunks = NSA_WINDOW // tq
    n_cmp = kc_ref.shape[3]
    grp = lambda gg: slice(gg * HEAD_DIM, (gg + 1) * HEAD_DIM)
    groups = range(n_grp)

    @pl.when(qi == 0)
    def _():
        for gg in groups:
            ksr_ref[gg] = _rope_full(ks_ref[0, :, grp(gg)], cosf_ref[...], sinf_ref[...]).astype(BF16)
            vst_ref[gg] = vs_ref[0, :, grp(gg)].T.astype(BF16)
            kw3_ref[gg, 0:pad_chunks] = jnp.zeros((pad_chunks, tq, HEAD_DIM), BF16)
            vwt3_ref[gg, 0:pad_chunks] = jnp.zeros((pad_chunks, HEAD_DIM, tq), BF16)
            kwr = _rope_full(kw_ref[0, :, grp(gg)], cosf_ref[...], sinf_ref[...]).astype(BF16)
            for c in range(n_chunk):
                kw3_ref[gg, pad_chunks + c] = kwr[c * tq:(c + 1) * tq]
                vwt3_ref[gg, pad_chunks + c] = vw_ref[0, c * tq:(c + 1) * tq, grp(gg)].T.astype(BF16)
            vct_ref[gg] = vc_ref[0, 0, gg].T.astype(BF16)

    cos_q = cosq_ref[...]
    sin_q = sinq_ref[...]
    lane = lax.broadcasted_iota(jnp.int32, (1, lanes), 1)
    t_loc = lane % tq
    t = qi * tq + t_loc
    cur = t[:, 0:tq] // NSA_SEL_BLOCK
    blk = lax.broadcasted_iota(jnp.int32, (n_sel, tq), 0)
    forced = (blk == 0) | (blk == cur) | (blk == cur - 1)
    visible = blk <= cur
    cmp_end = (lax.broadcasted_iota(jnp.int32, (n_cmp, lanes), 0) * NSA_CMP_STRIDE + (NSA_CMP_LEN - 1))
    ovl = ovl_ref[...]

    q_bf, o_c, imp = [], [], []
    for gg in groups:
        q_all = jnp.concatenate(
            [_rope_full(q_ref[0, :, grp(gg * rep + r)], cos_q, sin_q) for r in range(rep)],
            axis=0)
        q_bf.append((q_all * (scale * LOG2E)).astype(BF16))

        s_c = _dot3_nt(kc_ref[0, 0, gg], q_all)
        e_c, l_c = _softmax_t(s_c, cmp_end <= t, scale)
        p_c = e_c / l_c
        o_c.append(_dot_nn(vct_ref[gg], p_c.astype(BF16)))

        p_hi, p_lo = _split(p_c)
        imp3 = _dot_nn(ovl, p_hi) + _dot_nn(ovl, p_lo)
        imp_g = imp3[:, 0:tq]
        for r in range(1, rep):
            imp_g = imp_g + imp3[:, r * tq:(r + 1) * tq]
        imp_g = jnp.where(forced, FORCE_SCORE, imp_g)
        imp.append(jnp.where(visible, imp_g, -jnp.inf))

    unit = 2 * tq
    sb = NSA_SEL_BLOCK
    for v in range(s_len // unit):
        @pl.when(qi // 2 == v)
        def _(v=v):
            n_keys = (v + 1) * unit
            lo = n_keys - unit

            def causal(start, n):
                return lax.broadcasted_iota(jnp.int32, (n, lanes), 0) <= (t - start)

            n_vis = n_keys // sb
            for gg in groups:
                s = _dot_nt(ksr_ref[gg, 0:n_keys], q_bf[gg])
                if n_vis > NSA_SEL_TOPK:
                    rank = _rank_t(imp[gg], n_vis)
                    rank3 = jnp.concatenate([rank] * rep, axis=1)
                    drop = jnp.where(rank3 < NSA_SEL_TOPK, 0.0, NEG_BIG)
                    parts = []
                    for j in range(n_vis):
                        parts.append((s[j * sb:(j + 1) * sb] + drop[j:j + 1, :],
                                      causal(j * sb, sb) if j * sb >= lo else None))
                else:
                    parts = [(s[0:lo], None)] if lo > 0 else []
                    parts.append((s[lo:], causal(lo, unit)))
                e_s, l_s = _softmax_parts(parts)
                os_ref[gg] = _dot_nn(vst_ref[gg, :, 0:n_keys], e_s) / l_s

    n_win = pad_chunks + 1
    row = lax.broadcasted_iota(jnp.int32, (tq, lanes), 0)
    for gg in groups:
        parts = []
        for c in range(n_win):
            s_w = _dot_nt(kw3_ref[gg, qi + c], q_bf[gg])
            if c < pad_chunks:
                s_w = s_w + jnp.where(qi + c >= pad_chunks, 0.0, NEG_BIG)
            mask = (row > t_loc) if c == 0 else ((row <= t_loc) if c == n_win - 1 else None)
            parts.append((s_w, mask))
        e_w, l_w = _softmax_parts(parts)
        o_w = _dot_nn(vwt3_ref[gg, qi], e_w[0:tq])
        for c in range(1, n_win):
            o_w = o_w + _dot_nn(vwt3_ref[gg, qi + c], e_w[c * tq:(c + 1) * tq])
        o_w = o_w / l_w

        g_t = _sigmoid(gate_ref[0, :, grp(gg)].T)

        def gate_row(branch, g_t=g_t):
            return jnp.concatenate([g_t[3 * r + branch:3 * r + branch + 1, :] for r in range(rep)], axis=1)

        out_t = gate_row(0) * o_c[gg] + gate_row(1) * os_ref[gg] + gate_row(2) * o_w
        for r in range(rep):
            o_ref[0, :, grp(gg * rep + r)] = out_t[:, r * tq:(r + 1) * tq].T.astype(o_ref.dtype)


def _nsa(z3, cmp_kv, cos, sin_s, ovl_t, tq=128):
    b, s, _ = z3.shape
    g = NSA_KV_GROUPS
    n_sel = s // NSA_SEL_BLOCK
    n_cmp = cmp_kv.shape[3]
    qw = NSA_HEADS * HEAD_DIM
    kvw = g * HEAD_DIM
    n_chunk_pad = (s + NSA_WINDOW) // tq

    def col(off):
        return lambda bi, qi: (bi, 0, off // kvw)

    return pl.pallas_call(
        functools.partial(_nsa_kernel, n_sel=n_sel, scale=HEAD_DIM ** -0.5),
        grid=(b, s // tq),
        in_specs=[
            pl.BlockSpec((1, tq, qw), lambda bi, qi: (bi, qi, OFF_CQ // qw)),
            pl.BlockSpec((1, s, kvw), col(OFF_CKS)),
            pl.BlockSpec((1, s, kvw), col(OFF_CVS)),
            pl.BlockSpec((1, s, kvw), col(OFF_CKW)),
            pl.BlockSpec((1, s, kvw), col(OFF_CVW)),
            pl.BlockSpec((1, 1, g, n_cmp, HEAD_DIM), lambda bi, qi: (0, bi, 0, 0, 0)),
            pl.BlockSpec((1, 1, g, n_cmp, HEAD_DIM), lambda bi, qi: (1, bi, 0, 0, 0)),
            pl.BlockSpec((1, tq, g * LANES), lambda bi, qi: (bi, qi, OFF_CGATE // (g * LANES))),
            pl.BlockSpec((s, LANES), lambda bi, qi: (0, 0)),
            pl.BlockSpec((s, LANES), lambda bi, qi: (0, 0)),
            pl.BlockSpec((tq, LANES), lambda bi, qi: (qi, 0)),
            pl.BlockSpec((tq, LANES), lambda bi, qi: (qi, 0)),
            pl.BlockSpec((n_sel, n_cmp), lambda bi, qi: (0, 0)),
        ],
        out_specs=pl.BlockSpec((1, tq, qw), lambda bi, qi: (bi, qi, 0)),
        out_shape=jax.ShapeDtypeStruct((b, s, qw), BF16),
        scratch_shapes=[
            pltpu.VMEM((g, s, HEAD_DIM), BF16),
            pltpu.VMEM((g, HEAD_DIM, s), BF16),
            pltpu.VMEM((g, n_chunk_pad, tq, HEAD_DIM), BF16),
            pltpu.VMEM((g, n_chunk_pad, HEAD_DIM, tq), BF16),
            pltpu.VMEM((g, HEAD_DIM, n_cmp), BF16),
            pltpu.VMEM((g, HEAD_DIM, NSA_REP * tq), F32),
        ],
        compiler_params=_cparams(("parallel", "arbitrary")),
        name="nsa",
    )(z3, z3, z3, z3, z3, cmp_kv, cmp_kv, z3, cos, sin_s, cos, sin_s, ovl_t)


def _merge_kernel(ya_ref, yb_ref, yc_ref, ga_ref, gb_ref, gc_ref, wa_ref, wb_ref, wc_ref, o_ref,
                  wa_bf_ref, wb_bf_ref, wc_bf_ref):
    @pl.when(pl.program_id(1) == 0)
    def _():
        wa_bf_ref[...] = wa_ref[...].astype(BF16)
        wb_bf_ref[...] = wb_ref[...].astype(BF16)
        wc_bf_ref[...] = wc_ref[...].astype(BF16)

    m = ga_ref[...].astype(F32) * _dot_nn(ya_ref[...], wa_bf_ref[...])
    m = m + gb_ref[...].astype(F32) * _dot_nn(yb_ref[...], wb_bf_ref[...])
    m = m + gc_ref[...].astype(F32) * _dot_nn(yc_ref[...], wc_bf_ref[...])
    o_ref[...] = m.astype(o_ref.dtype)


def _merge(ya, yb, yc, gates, wa, wb, wc, layer, tm=1024, tn=1024):
    t = ya.shape[0]
    d = wa.shape[2]
    nj = d // tn

    def gate_spec(branch):
        return pl.BlockSpec((tm, tn), lambda j, i: (i, branch * nj + j))

    def y_spec(y):
        return pl.BlockSpec((tm, y.shape[1]), lambda j, i: (i, 0))

    def w_spec(w):
        return pl.BlockSpec((None, w.shape[1], tn), lambda j, i: (layer, 0, j), pipeline_mode=pl.Buffered(1))

    return pl.pallas_call(
        _merge_kernel,
        grid=(nj, t // tm),
        in_specs=[y_spec(ya), y_spec(yb), y_spec(yc), gate_spec(0), gate_spec(1), gate_spec(2),
                  w_spec(wa), w_spec(wb), w_spec(wc)],
        out_specs=pl.BlockSpec((tm, tn), lambda j, i: (i, j)),
        out_shape=jax.ShapeDtypeStruct((t, d), BF16),
        scratch_shapes=[pltpu.VMEM((w.shape[1], tn), BF16) for w in (wa, wb, wc)],
        compiler_params=_cparams(("parallel", "arbitrary")),
        name="merge",
    )(ya, yb, yc, gates, gates, gates, wa, wb, wc)


def _proj_residual_kernel(m_ref, w_ref, x_ref, o_ref, w_bf_ref):
    @pl.when(pl.program_id(1) == 0)
    def _():
        w_bf_ref[...] = w_ref[...].astype(BF16)

    o_ref[...] = x_ref[...] + _dot_nn(m_ref[...], w_bf_ref[...])


def _proj_residual(m, w, x2d, layer, tm=1024, tn=1024):
    t, k = m.shape
    d = w.shape[2]
    return pl.pallas_call(
        _proj_residual_kernel,
        grid=(d // tn, t // tm),
        in_specs=[
            pl.BlockSpec((tm, k), lambda j, i: (i, 0)),
            pl.BlockSpec((None, k, tn), lambda j, i: (layer, 0, j), pipeline_mode=pl.Buffered(1)),
            pl.BlockSpec((tm, tn), lambda j, i: (i, j)),
        ],
        out_specs=pl.BlockSpec((tm, tn), lambda j, i: (i, j)),
        out_shape=jax.ShapeDtypeStruct((t, d), F32),
        scratch_shapes=[pltpu.VMEM((k, tn), BF16)],
        compiler_params=_cparams(("parallel", "arbitrary")),
        name="out_proj",
    )(m, w, x2d)


def _rope_tables(pos, dim, width):
    inv = 1.0 / (ROPE_THETA ** (jnp.arange(0, dim, 2, dtype=F32) / dim))
    ang = pos.astype(F32)[:, None] * inv[None, :]
    cos = jnp.cos(ang)
    sin = jnp.sin(ang)
    cos = jnp.concatenate([cos, cos], axis=-1)
    sin_s = jnp.concatenate([-sin, sin], axis=-1)
    pad = ((0, 0), (0, width - dim))
    return jnp.pad(cos, pad), jnp.pad(sin_s, pad)


def _repack_kernel(srcp_ref, srcq_ref, start_ref, valid_ref, p_ref, q_ref, o_ref, *, n_src_cols):
    j = pl.program_id(1)
    start = start_ref[j]
    valid = valid_ref[j]
    row = lax.broadcasted_iota(jnp.int32, (LANES, LANES), 0)
    col = lax.broadcasted_iota(jnp.int32, (LANES, LANES), 1)
    lane = lax.broadcasted_iota(jnp.int32, (1, LANES), 1)

    def contribution(blk_ref, src_blk):
        take = jnp.where((src_blk * LANES + row == start + col) & (col < valid), 1.0, 0.0).astype(BF16)
        blk = jnp.where(src_blk * LANES + lane < n_src_cols, blk_ref[...], 0.0).astype(BF16)
        return _dot_nn(blk, take)

    o_ref[...] = (contribution(p_ref, srcp_ref[j]) + contribution(q_ref, srcq_ref[j])).astype(o_ref.dtype)


def _repack(w, pieces, n_out_blocks):
    depth, k, n = w.shape
    n_src_blocks = -(-n // LANES)
    starts, valids = [], []
    for start, width in pieces:
        if width == 0:
            starts.append(0)
            valids.append(0)
        for off in range(0, width, LANES):
            starts.append(start + off)
            valids.append(min(LANES, width - off))
    starts += [0] * (n_out_blocks - len(starts))
    valids += [0] * (n_out_blocks - len(valids))
    assert n_src_blocks >= 2
    slots = [0, 1]
    srcp, srcq = [], []
    for start, valid in zip(starts, valids):
        need = []
        if valid > 0:
            need = list(range(start // LANES, (start + valid - 1) // LANES + 1))
        for blk in need:
            if blk not in slots:
                slots[0 if slots[0] not in need else 1] = blk
        srcp.append(slots[0])
        srcq.append(slots[1])
    as_i32 = lambda v: jnp.asarray(np.asarray(v, np.int32))
    grid_spec = pltpu.PrefetchScalarGridSpec(
        num_scalar_prefetch=4,
        grid=(depth, n_out_blocks),
        in_specs=[
            pl.BlockSpec((None, k, LANES), lambda d, j, sp, sq, st, va: (d, 0, sp[j])),
            pl.BlockSpec((None, k, LANES), lambda d, j, sp, sq, st, va: (d, 0, sq[j])),
        ],
        out_specs=pl.BlockSpec((None, k, LANES), lambda d, j, sp, sq, st, va: (d, 0, j)),
    )
    return pl.pallas_call(
        functools.partial(_repack_kernel, n_src_cols=n),
        grid_spec=grid_spec,
        out_shape=jax.ShapeDtypeStruct((depth, k, n_out_blocks * LANES), BF16),
        compiler_params=_cparams(("parallel", "arbitrary")),
        name="repack",
    )(as_i32(srcp), as_i32(srcq), as_i32(starts), as_i32(valids), w, w)


def _pack_w_in(w):
    sizes = [MOBA_HEADS * HEAD_DIM] * 3 + [MLA_Q_RANK, MLA_KV_RANK, MLA_ROPE, NSA_HEADS * HEAD_DIM] \
        + [NSA_KV_GROUPS * HEAD_DIM] * 6
    offs = [int(v) for v in np.concatenate([[0], np.cumsum(sizes)])]
    (a_q, a_k, a_v, b_cq, b_ckv, b_kr, c_q, c_kc, c_vc, c_ks, c_vs, c_kw, c_vw) = [
        (offs[i], sizes[i]) for i in range(len(sizes))]
    gate0 = offs[-1]
    per_group = NSA_REP * 3
    gates = [(gate0 + gi * per_group, per_group) for gi in range(NSA_KV_GROUPS)]
    merge = (gate0 + NSA_HEADS * 3, w.shape[2] - gate0 - NSA_HEADS * 3)
    pieces = [c_q, b_ckv, b_cq, a_q, a_k, a_v, c_kc, c_vc, c_ks, c_vs, c_kw, c_vw] + gates + [b_kr]
    used = sum(-(-width // LANES) for _, width in pieces)
    pieces += [(0, 0)] * (Z_COLS // LANES - used) + [merge]
    return _repack(w, pieces, (Z_COLS + merge[1]) // LANES)


def _pack_w_uq(w):
    per_head = MLA_NOPE + MLA_ROPE
    pieces = [(h * per_head, MLA_NOPE) for h in range(MLA_HEADS)] \
        + [(h * per_head + MLA_NOPE, MLA_ROPE) for h in range(MLA_HEADS)]
    return _repack(w, pieces, 2 * MLA_HEADS)


def _pack_w_ukv(w):
    per_head = MLA_NOPE + MLA_V
    pieces = [(h * per_head, MLA_NOPE) for h in range(MLA_HEADS)] \
        + [(h * per_head + MLA_NOPE, MLA_V) for h in range(MLA_HEADS)]
    return _repack(w, pieces, 2 * MLA_HEADS)


def kernel(x, ffn1_norm, ffn1_w_gate, ffn1_w_up, ffn1_w_down, mix_norm, w_in, mla_q_norm, mla_w_uq, mla_kv_norm, mla_w_ukv, nsa_cmp_pos_k, nsa_cmp_w1_k, nsa_cmp_w2_k, nsa_cmp_pos_v, nsa_cmp_w1_v, nsa_cmp_w2_v, w_branch_moba, w_branch_mla, w_branch_nsa, w_out, ffn2_norm, ffn2_w_gate, ffn2_w_up, ffn2_w_down, final_norm):
    b, s, d = x.shape
    depth = w_in.shape[0]
    t = b * s
    assert NSA_CMP_LEN == 2 * NSA_CMP_STRIDE and s % MOBA_BLOCK == 0 and s % NSA_SEL_BLOCK == 0

    pos = jnp.arange(s)
    cos_h, sin_h = _rope_tables(pos, HEAD_DIM, LANES)
    cos_r, sin_r = _rope_tables(pos, MLA_ROPE, LANES)
    n16 = s // NSA_CMP_STRIDE
    cmp_end = np.arange(n16) * NSA_CMP_STRIDE + NSA_CMP_LEN - 1
    cos_c, sin_c = _rope_tables(jnp.asarray(cmp_end), HEAD_DIM, LANES)
    n_sel = s // NSA_SEL_BLOCK
    sel_start = np.arange(n_sel) * NSA_SEL_BLOCK
    cmp_start = np.arange(n16) * NSA_CMP_STRIDE
    ovl_t = ((cmp_start[None, :] <= sel_start[:, None] + NSA_SEL_BLOCK - 1)
             & (cmp_end[None, :] >= sel_start[:, None])).astype(np.float32)
    ovl_t = jnp.asarray(ovl_t, BF16)

    w_in_p = _pack_w_in(w_in)
    w_uq_p = _pack_w_uq(mla_w_uq)
    w_ukv_p = _pack_w_ukv(mla_w_ukv)

    xf = x.reshape(t, d)
    for i in range(depth):
        xf = _ffn(xf, ffn1_norm[i], ffn1_w_gate, ffn1_w_up, ffn1_w_down, i)

        z, gates = _in_proj(xf, mix_norm[i], w_in_p, i)
        z3 = z.reshape(b, s, Z_COLS)
        y_a = _moba(z3, cos_h, sin_h)
        y_b = _mla(z3, mla_q_norm[i], mla_kv_norm[i], w_uq_p, w_ukv_p, i, cos_r, sin_r)

        w1 =jnp.stack([nsa_cmp_w1_k[i], nsa_cmp_w1_v[i]])
        w2 = jnp.stack([nsa_cmp_w2_k[i], nsa_cmp_w2_v[i]])
        posflat = jnp.stack([nsa_cmp_pos_k[i].reshape(1, -1), nsa_cmp_pos_v[i].reshape(1, -1)])
        cmp_kv = _nsa_compress(z3, w1, w2, posflat, cos_c, sin_c)
        y_c = _nsa(z3, cmp_kv, cos_h, sin_h, ovl_t)

        merged = _merge(y_a.reshape(t, -1), y_b.reshape(t, -1), y_c.reshape(t, -1), gates,
                        w_branch_moba, w_branch_mla, w_branch_nsa, i)
        xf = _proj_residual(merged, w_out, xf, i)

        xf = _ffn(xf, ffn2_norm[i], ffn2_w_gate, ffn2_w_up, ffn2_w_down, i)
    return _rms(xf, final_norm, F32).reshape(b, s, d)
```

```python
import functools

import numpy as np
import jax
import jax.numpy as jnp
from jax import lax
from jax.experimental import pallas as pl
from jax.experimental.pallas import tpu as pltpu

F32 = jnp.float32
BF16 = jnp.bfloat16

HEAD_DIM = 128
ROPE_THETA = 10000.0
NORM_EPS = 1e-6
NEG_BIG = -1e30
TINY = 1e-30
LOG2E = 1.4426950408889634
N_BRANCHES = 3

MOBA_HEADS = 4
MOBA_BLOCK = 256
MOBA_TOPK = 3

MLA_HEADS = 6
MLA_Q_RANK = 512
MLA_KV_RANK = 256
MLA_NOPE = 128
MLA_ROPE = 64
MLA_V = 128

NSA_HEADS = 6
NSA_KV_GROUPS = 2
NSA_REP = NSA_HEADS // NSA_KV_GROUPS
NSA_CMP_LEN = 32
NSA_CMP_STRIDE = 16
NSA_CMP_HIDDEN = 256
NSA_SEL_BLOCK = 64
NSA_SEL_TOPK = 16
NSA_WINDOW = 512
FORCE_SCORE = 1e9

LANES = 128
VMEM_LIMIT = 56 * 2**20

OFF_CQ = 0
OFF_BCKV = 768
OFF_BCQ = 1024
OFF_AQ = 1536
OFF_AK = 2048
OFF_AV = 2560
OFF_CKC = 3072
OFF_CVC = 3328
OFF_CKS = 3584
OFF_CVS = 3840
OFF_CKW = 4096
OFF_CVW = 4352
OFF_CGATE = 4608
OFF_BKR = 4864
Z_COLS = 5120


def _cparams(sem):
    return pltpu.CompilerParams(dimension_semantics=sem, vmem_limit_bytes=VMEM_LIMIT)


def _rmsnorm(x, g):
    ms = jnp.mean(x * x, axis=-1, keepdims=True)
    return x * lax.rsqrt(ms + NORM_EPS) * g


def _sigmoid(x):
    return 0.5 * jnp.tanh(0.5 * x) + 0.5


def _rope_full(x, cos, sin_signed):
    return x * cos + pltpu.roll(x, HEAD_DIM // 2, axis=1) * sin_signed


def _rope_half(x, cos, sin_signed):
    lane = lax.broadcasted_iota(jnp.int32, x.shape, 1)
    q = MLA_ROPE // 2
    swapped = jnp.where(lane < q, pltpu.roll(x, LANES - q, axis=1), pltpu.roll(x, q, axis=1))
    return x * cos + swapped * sin_signed


def _dot_nt(a, b):
    return lax.dot_general(a, b, (((1,), (1,)), ((), ())), preferred_element_type=F32)


def _dot_nn(a, b):
    return jnp.dot(a, b, preferred_element_type=F32)


def _split(a):
    hi = a.astype(BF16)
    lo = (a - hi.astype(F32)).astype(BF16)
    return hi, lo


def _dot3_nt(a, b):
    ah, al = _split(a)
    bh, bl = _split(b)
    return _dot_nt(ah, bh) + _dot_nt(ah, bl) + _dot_nt(al, bh)


def _dot3_nn(a, b):
    ah, al = _split(a)
    bh, bl = _split(b)
    return _dot_nn(ah, bh) + _dot_nn(ah, bl) + _dot_nn(al, bh)


def _softmax_t(s, mask, scale):
    s = jnp.where(mask, s * scale, NEG_BIG)
    m = jnp.max(s, axis=0, keepdims=True)
    e = jnp.where(mask, jnp.exp(s - m), 0.0)
    l = jnp.maximum(jnp.sum(e, axis=0, keepdims=True), TINY)
    return e, l


def _softmax_parts(parts):
    masked = [s if m is None else jnp.where(m, s, NEG_BIG) for s, m in parts]
    mx = functools.reduce(jnp.maximum, [jnp.max(p, axis=0, keepdims=True) for p in masked])
    es = [jnp.exp2(p - mx) for p in masked]
    l = functools.reduce(jnp.add, [jnp.sum(e, axis=0, keepdims=True) for e in es])
    e = jnp.concatenate([e.astype(BF16) for e in es], axis=0) if len(es) > 1 else es[0].astype(BF16)
    return e, jnp.maximum(l, TINY)


def _rank_t(v, nb):
    blk = lax.broadcasted_iota(jnp.int32, v.shape, 0)
    rank = jnp.zeros(v.shape, F32)
    for j in range(nb):
        vj = v[j:j + 1, :]
        ge = (vj >= v).astype(F32)
        gt = (vj > v).astype(F32)
        rank = rank + jnp.where(blk > j, ge, gt)
    return rank


def _rms_kernel(x_ref, g_ref, o_ref):
    o_ref[...] = _rmsnorm(x_ref[...], g_ref[...]).astype(o_ref.dtype)


def _rms(x2d, g, out_dtype, tm=512):
    t, d = x2d.shape
    return pl.pallas_call(
        _rms_kernel,
        grid=(t // tm,),
        in_specs=[pl.BlockSpec((tm, d), lambda i: (i, 0)), pl.BlockSpec((1, d), lambda i: (0, 0))],
        out_specs=pl.BlockSpec((tm, d), lambda i: (i, 0)),
        out_shape=jax.ShapeDtypeStruct((t, d), out_dtype),
        compiler_params=_cparams(("parallel",)),
        name="rmsnorm",
    )(x2d, g.reshape(1, d))


def _ffn_up_kernel(h_ref, wg_ref, wu_ref, wd_ref, o_ref, wd_bf_out_ref, wg_bf_ref, wu_bf_ref):
    @pl.when(pl.program_id(1) == 0)
    def _():
        wg_bf_ref[...] = wg_ref[...].astype(BF16)
        wu_bf_ref[...] = wu_ref[...].astype(BF16)
        wd_bf_out_ref[...] = wd_ref[...].astype(BF16)

    h = h_ref[...]
    a = _dot_nn(h, wg_bf_ref[...])
    u = _dot_nn(h, wu_bf_ref[...])
    o_ref[...] = (a * _sigmoid(a) * u).astype(o_ref.dtype)


def _ffn_down_kernel(a_ref, wd_ref, x_ref, o_ref):
    o_ref[...] = x_ref[...] + 0.5 * _dot_nn(a_ref[...], wd_ref[...])


def _ffn(x2d, g, wg, wu, wd, layer, tm=1024, tf=512, tm_down=512, tn=1024):
    t, d = x2d.shape
    f = wg.shape[2]
    h = _rms(x2d, g, BF16)
    act, wd_bf = pl.pallas_call(
        _ffn_up_kernel,
        grid=(f // tf, t // tm),
        in_specs=[
            pl.BlockSpec((tm, d), lambda j, i: (i, 0)),
            pl.BlockSpec((None, d, tf), lambda j, i: (layer, 0, j)),
            pl.BlockSpec((None, d, tf), lambda j, i: (layer, 0, j)),
            pl.BlockSpec((None, tf, d), lambda j, i: (layer, j, 0)),
        ],
        out_specs=[
            pl.BlockSpec((tm, tf), lambda j, i: (i, j)),
            pl.BlockSpec((tf, d), lambda j, i: (j, 0)),
        ],
        out_shape=[jax.ShapeDtypeStruct((t, f), BF16), jax.ShapeDtypeStruct((f, d), BF16)],
        scratch_shapes=[pltpu.VMEM((d, tf), BF16), pltpu.VMEM((d, tf), BF16)],
        compiler_params=_cparams(("parallel", "arbitrary")),
        name="ffn_up",
    )(h, wg, wu, wd)
    return pl.pallas_call(
        _ffn_down_kernel,
        grid=(d // tn, t // tm_down),
        in_specs=[
            pl.BlockSpec((tm_down, f), lambda n, i: (i, 0)),
            pl.BlockSpec((f, tn), lambda n, i: (0, n)),
            pl.BlockSpec((tm_down, tn), lambda n, i: (i, n)),
        ],
        out_specs=pl.BlockSpec((tm_down, tn), lambda n, i: (i, n)),
        out_shape=jax.ShapeDtypeStruct((t, d), F32),
        compiler_params=_cparams(("parallel", "parallel")),
        name="ffn_down",
    )(act, wd_bf, x2d)


def _in_proj_kernel(x_ref, g_ref, w_ref, z_ref, gate_ref, h_ref, *, n_z):
    j = pl.program_id(1)

    @pl.when(j == 0)
    def _():
        h_ref[...] = _rmsnorm(x_ref[...], g_ref[...]).astype(BF16)

    @pl.when(j < n_z)
    def _():
        z_ref[...] = _dot_nt(h_ref[...], w_ref[...])

    @pl.when(j >= n_z)
    def _():
        gate_ref[...] = _sigmoid(_dot_nt(h_ref[...], w_ref[...])).astype(gate_ref.dtype)


def _in_proj(x2d, g, w, layer, tm=1024, tn=1024):
    t, d = x2d.shape
    n = w.shape[1]
    n_z = Z_COLS // tn
    n_g = (n - Z_COLS) // tn
    return pl.pallas_call(
        functools.partial(_in_proj_kernel, n_z=n_z),
        grid=(t // tm, n_z + n_g),
        in_specs=[
            pl.BlockSpec((tm, d), lambda i, j: (i, 0)),
            pl.BlockSpec((1, d), lambda i, j: (0, 0)),
            pl.BlockSpec((None, tn, d), lambda i, j: (layer, j, 0)),
        ],
        out_specs=[
            pl.BlockSpec((tm, tn), lambda i, j: (i, jnp.minimum(j, n_z - 1))),
            pl.BlockSpec((tm, tn), lambda i, j: (i, jnp.maximum(j - n_z, 0))),
        ],
        out_shape=[jax.ShapeDtypeStruct((t, Z_COLS), F32), jax.ShapeDtypeStruct((t, n - Z_COLS), BF16)],
        scratch_shapes=[pltpu.VMEM((tm, d), BF16)],
        compiler_params=_cparams(("parallel", "arbitrary")),
        name="in_proj",
    )(x2d, g.reshape(1, d), w)


def _moba_kernel(q_ref, k_ref, v_ref, cosf_ref, sinf_ref, cosq_ref, sinq_ref, o_ref,
                 kr_ref, km_ref, vt_ref, *, n_blk, scale):
    qi = pl.program_id(2)
    tq = q_ref.shape[1]
    hp = q_ref.shape[2] // LANES
    head = lambda hh: slice(hh * LANES, (hh + 1) * LANES)

    @pl.when(qi == 0)
    def _():
        for hh in range(hp):
            kr = _rope_full(k_ref[0, :, head(hh)], cosf_ref[...], sinf_ref[...])
            kr_ref[hh] = kr.astype(BF16)
            km_ref[hh] = jnp.concatenate(
                [jnp.mean(kr[j * MOBA_BLOCK:(j + 1) * MOBA_BLOCK], axis=0, keepdims=True) for j in range(n_blk)],
                axis=0)
            vt_ref[hh] = v_ref[0, :, head(hh)].T.astype(BF16)

    causal = (lax.broadcasted_iota(jnp.int32, (tq, tq), 0)
              <= lax.broadcasted_iota(jnp.int32, (tq, tq), 1))

    for n in range(n_blk):
        @pl.when(qi == n)
        def _(n=n):
            for hh in range(hp):
                qr = _rope_full(q_ref[0, :, head(hh)], cosq_ref[...], sinq_ref[...])
                q_bf = (qr * (scale * LOG2E)).astype(BF16)
                s = _dot_nt(kr_ref[hh, 0:(n + 1) * tq], q_bf)
                parts = []
                if n > MOBA_TOPK:
                    gate = _dot3_nt(km_ref[hh], qr)
                    past = lax.broadcasted_iota(jnp.int32, (n_blk, tq), 0) < n
                    gate = jnp.where(past, gate, -jnp.inf)
                    drop = jnp.where(_rank_t(gate, n) < MOBA_TOPK, 0.0, NEG_BIG)
                    for j in range(n):
                        parts.append((s[j * tq:(j + 1) * tq] + drop[j:j + 1, :], None))
                elif n > 0:
                    parts.append((s[0:n * tq], None))
                parts.append((s[n * tq:], causal))
                e, l = _softmax_parts(parts)
                o_t = _dot_nn(vt_ref[hh, :, 0:(n + 1) * tq], e) / l
                o_ref[0, :, head(hh)] = o_t.T.astype(o_ref.dtype)


def _moba(z3, cos, sin_s, tq=MOBA_BLOCK, hp=MOBA_HEADS):
    b, s, _ = z3.shape
    n_blk = s // MOBA_BLOCK
    w = hp * LANES
    qb, kb, vb = OFF_AQ // w, OFF_AK // w, OFF_AV // w
    return pl.pallas_call(
        functools.partial(_moba_kernel, n_blk=n_blk, scale=HEAD_DIM ** -0.5),
        grid=(b, MOBA_HEADS // hp, s // tq),
        in_specs=[
            pl.BlockSpec((1, tq, w), lambda bi, h, qi: (bi, qi, qb + h)),
            pl.BlockSpec((1, s, w), lambda bi, h, qi: (bi, 0, kb + h)),
            pl.BlockSpec((1, s, w), lambda bi, h, qi: (bi, 0, vb + h)),
            pl.BlockSpec((s, LANES), lambda bi, h, qi: (0, 0)),
            pl.BlockSpec((s, LANES), lambda bi, h, qi: (0, 0)),
            pl.BlockSpec((tq, LANES), lambda bi, h, qi: (qi, 0)),
            pl.BlockSpec((tq, LANES), lambda bi, h, qi: (qi, 0)),
        ],
        out_specs=pl.BlockSpec((1, tq, w), lambda bi, h, qi: (bi, qi, h)),
        out_shape=jax.ShapeDtypeStruct((b, s, MOBA_HEADS * HEAD_DIM), BF16),
        scratch_shapes=[pltpu.VMEM((hp, s, LANES), BF16), pltpu.VMEM((hp, n_blk, LANES), F32),
                        pltpu.VMEM((hp, LANES, s), BF16)],
        compiler_params=_cparams(("parallel", "parallel", "arbitrary")),
        name="moba",
    )(z3, z3, z3, cos, sin_s, cos, sin_s)


def _mla_kernel(cq_ref, ckv_ref, kr_ref, qg_ref, kvg_ref, wqn_ref, wqr_ref, wkn_ref, wv_ref,
                cosf_ref, sinf_ref, cosq_ref, sinq_ref, o_ref, kf_ref, vt_ref, *, scale):
    qi = pl.program_id(2)
    tq = cq_ref.shape[1]
    s_len = ckv_ref.shape[1]
    hp = wqn_ref.shape[1] // LANES
    head = lambda hh: slice(hh * LANES, (hh + 1) * LANES)

    @pl.when(qi == 0)
    def _():
        ckv = _rmsnorm(ckv_ref[0], kvg_ref[...]).astype(BF16)
        k_nope = _dot_nn(ckv, wkn_ref[...])
        v = _dot_nn(ckv, wv_ref[...])
        k_rope = _rope_half(kr_ref[0], cosf_ref[...], sinf_ref[...]).astype(BF16)
        for hh in range(hp):
            kf_ref[hh, :, :LANES] = k_nope[:, head(hh)].astype(BF16)
            kf_ref[hh, :, LANES:] = k_rope
            vt_ref[hh] = v[:, head(hh)].T.astype(BF16)

    cq = _rmsnorm(cq_ref[0], qg_ref[...]).astype(BF16)
    q_nope = _dot_nn(cq, wqn_ref[...])
    q_rope = _dot_nn(cq, wqr_ref[...])
    causal = (lax.broadcasted_iota(jnp.int32, (tq, tq), 0)
              <= lax.broadcasted_iota(jnp.int32, (tq, tq), 1))

    for n in range(s_len // tq):
        @pl.when(qi == n)
        def _(n=n):
            for hh in range(hp):
                q = jnp.concatenate([q_nope[:, head(hh)],
                                     _rope_half(q_rope[:, head(hh)], cosq_ref[...], sinq_ref[...])], axis=1)
                q_bf = (q * (scale * LOG2E)).astype(BF16)
                s = _dot_nt(kf_ref[hh, 0:(n + 1) * tq], q_bf)
                parts = [(s[0:n * tq], None)] if n > 0 else []
                parts.append((s[n * tq:], causal))
                e, l = _softmax_parts(parts)
                o_t = _dot_nn(vt_ref[hh, :, 0:(n + 1) * tq], e) / l
                o_ref[0, :, head(hh)] = o_t.T.astype(o_ref.dtype)


def _mla(z3, q_gain, kv_gain, w_uq, w_ukv, layer, cos, sin_s, tq=256, hp=3):
    b, s, _ = z3.shape
    w = hp * LANES
    ng = MLA_HEADS // hp
    return pl.pallas_call(
        functools.partial(_mla_kernel, scale=(MLA_NOPE + MLA_ROPE) ** -0.5),
        grid=(b, ng, s // tq),
        in_specs=[
            pl.BlockSpec((1, tq, MLA_Q_RANK), lambda bi, h, qi: (bi, qi, OFF_BCQ // MLA_Q_RANK)),
            pl.BlockSpec((1, s, MLA_KV_RANK), lambda bi, h, qi: (bi, 0, OFF_BCKV // MLA_KV_RANK)),
            pl.BlockSpec((1, s, LANES), lambda bi, h, qi: (bi, 0, OFF_BKR // LANES)),
            pl.BlockSpec((1, MLA_Q_RANK), lambda bi, h, qi: (0, 0)),
            pl.BlockSpec((1, MLA_KV_RANK), lambda bi, h, qi: (0, 0)),
            pl.BlockSpec((None, MLA_Q_RANK, w), lambda bi, h, qi: (layer, 0, h)),
            pl.BlockSpec((None, MLA_Q_RANK, w), lambda bi, h, qi: (layer, 0, ng + h)),
            pl.BlockSpec((None, MLA_KV_RANK, w), lambda bi, h, qi: (layer, 0, h)),
            pl.BlockSpec((None, MLA_KV_RANK, w), lambda bi, h, qi: (layer, 0, ng + h)),
            pl.BlockSpec((s, LANES), lambda bi, h, qi: (0, 0)),
            pl.BlockSpec((s, LANES), lambda bi, h, qi: (0, 0)),
            pl.BlockSpec((tq, LANES), lambda bi, h, qi: (qi, 0)),
            pl.BlockSpec((tq, LANES), lambda bi, h, qi: (qi, 0)),
        ],
        out_specs=pl.BlockSpec((1, tq, w), lambda bi, h, qi: (bi, qi, h)),
        out_shape=jax.ShapeDtypeStruct((b, s, MLA_HEADS * MLA_V), BF16),
        scratch_shapes=[pltpu.VMEM((hp, s, 2 * LANES), BF16), pltpu.VMEM((hp, LANES, s), BF16)],
        compiler_params=_cparams(("parallel", "parallel", "arbitrary")),
        name="mla",
    )(z3, z3, z3, q_gain.reshape(1, -1), kv_gain.reshape(1, -1), w_uq, w_uq, w_ukv, w_ukv,
      cos, sin_s, cos, sin_s)


def _cmp_kernel(x_ref, w1_ref, w2_ref, pos_ref, cos_ref, sin_ref, o_ref):
    kv = pl.program_id(0)
    n16 = o_ref.shape[3]
    x = jnp.concatenate(
        [x_ref[0, pl.ds(l, n16, stride=NSA_CMP_STRIDE), :] for l in range(NSA_CMP_STRIDE)], axis=1)
    half = x.shape[1]
    w1 = w1_ref[0]
    first = _dot3_nn(x, w1[:half])
    second = _dot3_nn(x, w1[half:])
    bias = _dot3_nn(jnp.broadcast_to(pos_ref[0], (8, 2 * half)), w1)[0:1]
    hid = first + pltpu.roll(second, n16 - 1, axis=0) + bias
    y = _dot3_nn(jax.nn.gelu(hid), w2_ref[0])
    roped = _rope_full(y, cos_ref[...], sin_ref[...])
    o_ref[0, 0, 0] = jnp.where(kv == 0, roped, y)


def _nsa_compress(z3, w1, w2, posflat, cos_c, sin_c):
    b, s, _ = z3.shape
    g = NSA_KV_GROUPS
    n16 = s // NSA_CMP_STRIDE
    width = NSA_CMP_STRIDE * HEAD_DIM
    hid = w1.shape[2]
    base = OFF_CKC // LANES
    return pl.pallas_call(
        _cmp_kernel,
        grid=(2, b, g),
        in_specs=[
            pl.BlockSpec((1, s, LANES), lambda kv, bi, gi: (bi, 0, base + g * kv + gi)),
            pl.BlockSpec((1, 2 * width, hid), lambda kv, bi, gi: (kv, 0, 0)),
            pl.BlockSpec((1, hid, HEAD_DIM), lambda kv, bi, gi: (kv, 0, 0)),
            pl.BlockSpec((1, 1, 2 * width), lambda kv, bi, gi: (kv, 0, 0)),
            pl.BlockSpec((n16, HEAD_DIM), lambda kv, bi, gi: (0, 0)),
            pl.BlockSpec((n16, HEAD_DIM), lambda kv, bi, gi: (0, 0)),
        ],
        out_specs=pl.BlockSpec((1, 1, 1, n16, HEAD_DIM), lambda kv, bi, gi: (kv, bi, gi, 0, 0)),
        out_shape=jax.ShapeDtypeStruct((2, b, g, n16, HEAD_DIM), F32),
        compiler_params=_cparams(("parallel", "parallel", "parallel")),
        name="nsa_compress",
    )(z3, w1, w2, posflat, cos_c, sin_c)


def _nsa_kernel(q_ref, ks_ref, vs_ref, kw_ref, vw_ref, kc_ref, vc_ref, gate_ref,
                cosf_ref, sinf_ref, cosq_ref, sinq_ref, ovl_ref, o_ref,
                ksr_ref, vst_ref, kw3_ref, vwt3_ref, vct_ref, os_ref, *, n_sel, scale):
    qi = pl.program_id(1)
    tq = q_ref.shape[1]
    s_len = ks_ref.shape[1]
    n_grp = ks_ref.shape[2] // HEAD_DIM
    rep = NSA_REP
    lanes = rep * tq
    n_chunk = s_len // tq
    pad_chunks = NSA_WINDOW // tq
    n_cmp = kc_ref.shape[3]
    grp = lambda gg: slice(gg * HEAD_DIM, (gg + 1) * HEAD_DIM)
    groups = range(n_grp)

    @pl.when(qi == 0)
    def _():
        for gg in groups:
            ksr_ref[gg] = _rope_full(ks_ref[0, :, grp(gg)], cosf_ref[...], sinf_ref[...]).astype(BF16)
            vst_ref[gg] = vs_ref[0, :, grp(gg)].T.astype(BF16)
            kw3_ref[gg, 0:pad_chunks] = jnp.zeros((pad_chunks, tq, HEAD_DIM), BF16)
            vwt3_ref[gg, 0:pad_chunks] = jnp.zeros((pad_chunks, HEAD_DIM, tq), BF16)
            kwr = _rope_full(kw_ref[0, :, grp(gg)], cosf_ref[...], sinf_ref[...]).astype(BF16)
            for c in range(n_chunk):
                kw3_ref[gg, pad_chunks + c] = kwr[c * tq:(c + 1) * tq]
                vwt3_ref[gg, pad_chunks + c] = vw_ref[0, c * tq:(c + 1) * tq, grp(gg)].T.astype(BF16)
            vct_ref[gg] = vc_ref[0, 0, gg].T.astype(BF16)

    cos_q = cosq_ref[...]
    sin_q = sinq_ref[...]
    lane = lax.broadcasted_iota(jnp.int32, (1, lanes), 1)
    t_loc = lane % tq
    t = qi * tq + t_loc
    cur = t[:, 0:tq] // NSA_SEL_BLOCK
    blk = lax.broadcasted_iota(jnp.int32, (n_sel, tq), 0)
    forced = (blk == 0) | (blk == cur) | (blk == cur - 1)
    visible = blk <= cur
    cmp_end = (lax.broadcasted_iota(jnp.int32, (n_cmp, lanes), 0) * NSA_CMP_STRIDE + (NSA_CMP_LEN - 1))
    ovl = ovl_ref[...]

    q_bf, o_c, imp = [], [], []
    for gg in groups:
        q_all = jnp.concatenate(
            [_rope_full(q_ref[0, :, grp(gg * rep + r)], cos_q, sin_q) for r in range(rep)],
            axis=0)
        q_bf.append((q_all * (scale * LOG2E)).astype(BF16))

        s_c = _dot3_nt(kc_ref[0, 0, gg], q_all)
        e_c, l_c = _softmax_t(s_c, cmp_end <= t, scale)
        p_c = e_c / l_c
        o_c.append(_dot_nn(vct_ref[gg], p_c.astype(BF16)))

        p_hi, p_lo = _split(p_c)
        imp3 = _dot_nn(ovl, p_hi) + _dot_nn(ovl, p_lo)
        imp_g = imp3[:, 0:tq]
        for r in range(1, rep):
            imp_g = imp_g + imp3[:, r * tq:(r + 1) * tq]
        imp_g = jnp.where(forced, FORCE_SCORE, imp_g)
        imp.append(jnp.where(visible, imp_g, -jnp.inf))

    unit = 2 * tq
    sb = NSA_SEL_BLOCK
    for v in range(s_len // unit):
        @pl.when(qi // 2 == v)
        def _(v=v):
            n_keys = (v + 1) * unit
            lo = n_keys - unit

            def causal(start, n):
                return lax.broadcasted_iota(jnp.int32, (n, lanes), 0) <= (t - start)

            n_vis = n_keys // sb
            for gg in groups:
                s = _dot_nt(ksr_ref[gg, 0:n_keys], q_bf[gg])
                if n_vis > NSA_SEL_TOPK:
                    rank = _rank_t(imp[gg], n_vis)
                    rank3 = jnp.concatenate([rank] * rep, axis=1)
                    drop = jnp.where(rank3 < NSA_SEL_TOPK, 0.0, NEG_BIG)
                    parts = []
                    for j in range(n_vis):
                        parts.append((s[j * sb:(j + 1) * sb] + drop[j:j + 1, :],
                                      causal(j * sb, sb) if j * sb >= lo else None))
                else:
                    parts = [(s[0:lo], None)] if lo > 0 else []
                    parts.append((s[lo:], causal(lo, unit)))
                e_s, l_s = _softmax_parts(parts)
                os_ref[gg] = _dot_nn(vst_ref[gg, :, 0:n_keys], e_s) / l_s

    n_win = pad_chunks + 1
    row = lax.broadcasted_iota(jnp.int32, (tq, lanes), 0)
    for gg in groups:
        parts = []
        for c in range(n_win):
            s_w = _dot_nt(kw3_ref[gg, qi + c], q_bf[gg])
            if c < pad_chunks:
                s_w = s_w + jnp.where(qi + c >= pad_chunks, 0.0, NEG_BIG)
            mask = (row > t_loc) if c == 0 else ((row <= t_loc) if c == n_win - 1 else None)
            parts.append((s_w, mask))
        e_w, l_w = _softmax_parts(parts)
        o_w = _dot_nn(vwt3_ref[gg, qi], e_w[0:tq])
        for c in range(1, n_win):
            o_w = o_w + _dot_nn(vwt3_ref[gg, qi + c], e_w[c * tq:(c + 1) * tq])
        o_w = o_w / l_w

        g_t = _sigmoid(gate_ref[0, :, grp(gg)].T)

        def gate_row(branch, g_t=g_t):
            return jnp.concatenate([g_t[3 * r + branch:3 * r + branch + 1, :] for r in range(rep)], axis=1)

        out_t = gate_row(0) * o_c[gg] + gate_row(1) * os_ref[gg] + gate_row(2) * o_w
        for r in range(rep):
            o_ref[0, :, grp(gg * rep + r)] = out_t[:, r * tq:(r + 1) * tq].T.astype(o_ref.dtype)


def _nsa(z3, cmp_kv, cos, sin_s, ovl_t, tq=128):
    b, s, _ = z3.shape
    g = NSA_KV_GROUPS
    n_sel = s // NSA_SEL_BLOCK
    n_cmp = cmp_kv.shape[3]
    qw = NSA_HEADS * HEAD_DIM
    kvw = g * HEAD_DIM
    n_chunk_pad = (s + NSA_WINDOW) // tq

    def col(off):
        return lambda bi, qi: (bi, 0, off // kvw)

    return pl.pallas_call(
        functools.partial(_nsa_kernel, n_sel=n_sel, scale=HEAD_DIM ** -0.5),
        grid=(b, s // tq),
        in_specs=[
            pl.BlockSpec((1, tq, qw), lambda bi, qi: (bi, qi, OFF_CQ // qw)),
            pl.BlockSpec((1, s, kvw), col(OFF_CKS)),
            pl.BlockSpec((1, s, kvw), col(OFF_CVS)),
            pl.BlockSpec((1, s, kvw), col(OFF_CKW)),
            pl.BlockSpec((1, s, kvw), col(OFF_CVW)),
            pl.BlockSpec((1, 1, g, n_cmp, HEAD_DIM), lambda bi, qi: (0, bi, 0, 0, 0)),
            pl.BlockSpec((1, 1, g, n_cmp, HEAD_DIM), lambda bi, qi: (1, bi, 0, 0, 0)),
            pl.BlockSpec((1, tq, g * LANES), lambda bi, qi: (bi, qi, OFF_CGATE // (g * LANES))),
            pl.BlockSpec((s, LANES), lambda bi, qi: (0, 0)),
            pl.BlockSpec((s, LANES), lambda bi, qi: (0, 0)),
            pl.BlockSpec((tq, LANES), lambda bi, qi: (qi, 0)),
            pl.BlockSpec((tq, LANES), lambda bi, qi: (qi, 0)),
            pl.BlockSpec((n_sel, n_cmp), lambda bi, qi: (0, 0)),
        ],
        out_specs=pl.BlockSpec((1, tq, qw), lambda bi, qi: (bi, qi, 0)),
        out_shape=jax.ShapeDtypeStruct((b, s, qw), BF16),
        scratch_shapes=[
            pltpu.VMEM((g, s, HEAD_DIM), BF16),
            pltpu.VMEM((g, HEAD_DIM, s), BF16),
            pltpu.VMEM((g, n_chunk_pad, tq, HEAD_DIM), BF16),
            pltpu.VMEM((g, n_chunk_pad, HEAD_DIM, tq), BF16),
            pltpu.VMEM((g, HEAD_DIM, n_cmp), BF16),
            pltpu.VMEM((g, HEAD_DIM, NSA_REP * tq), F32),
        ],
        compiler_params=_cparams(("parallel", "arbitrary")),
        name="nsa",
    )(z3, z3, z3, z3, z3, cmp_kv, cmp_kv, z3, cos, sin_s, cos, sin_s, ovl_t)


def _merge_kernel(ya_ref, yb_ref, yc_ref, ga_ref, gb_ref, gc_ref, wa_ref, wb_ref, wc_ref, o_ref,
                  wa_bf_ref, wb_bf_ref, wc_bf_ref):
    @pl.when(pl.program_id(1) == 0)
    def _():
        wa_bf_ref[...] = wa_ref[...].astype(BF16)
        wb_bf_ref[...] = wb_ref[...].astype(BF16)
        wc_bf_ref[...] = wc_ref[...].astype(BF16)

    m = ga_ref[...].astype(F32) * _dot_nn(ya_ref[...], wa_bf_ref[...])
    m = m + gb_ref[...].astype(F32) * _dot_nn(yb_ref[...], wb_bf_ref[...])
    m = m + gc_ref[...].astype(F32) * _dot_nn(yc_ref[...], wc_bf_ref[...])
    o_ref[...] = m.astype(o_ref.dtype)


def _merge(ya, yb, yc, gates, wa, wb, wc, layer, tm=1024, tn=1024):
    t = ya.shape[0]
    d = wa.shape[2]
    nj = d // tn

    def gate_spec(branch):
        return pl.BlockSpec((tm, tn), lambda j, i: (i, branch * nj + j))

    def y_spec(y):
        return pl.BlockSpec((tm, y.shape[1]), lambda j, i: (i, 0))

    def w_spec(w):
        return pl.BlockSpec((None, w.shape[1], tn), lambda j, i: (layer, 0, j), pipeline_mode=pl.Buffered(1))

    return pl.pallas_call(
        _merge_kernel,
        grid=(nj, t // tm),
        in_specs=[y_spec(ya), y_spec(yb), y_spec(yc), gate_spec(0), gate_spec(1), gate_spec(2),
                  w_spec(wa), w_spec(wb), w_spec(wc)],
        out_specs=pl.BlockSpec((tm, tn), lambda j, i: (i, j)),
        out_shape=jax.ShapeDtypeStruct((t, d), BF16),
        scratch_shapes=[pltpu.VMEM((w.shape[1], tn), BF16) for w in (wa, wb, wc)],
        compiler_params=_cparams(("parallel", "arbitrary")),
        name="merge",
    )(ya, yb, yc, gates, gates, gates, wa, wb, wc)


def _proj_residual_kernel(m_ref, w_ref, x_ref, o_ref, w_bf_ref):
    @pl.when(pl.program_id(1) == 0)
    def _():
        w_bf_ref[...] = w_ref[...].astype(BF16)

    o_ref[...] = x_ref[...] + _dot_nn(m_ref[...], w_bf_ref[...])


def _proj_residual(m, w, x2d, layer, tm=1024, tn=1024):
    t, k = m.shape
    d = w.shape[2]
    return pl.pallas_call(
        _proj_residual_kernel,
        grid=(d // tn, t // tm),
        in_specs=[
            pl.BlockSpec((tm, k), lambda j, i: (i, 0)),
            pl.BlockSpec((None, k, tn), lambda j, i: (layer, 0, j), pipeline_mode=pl.Buffered(1)),
            pl.BlockSpec((tm, tn), lambda j, i: (i, j)),
        ],
        out_specs=pl.BlockSpec((tm, tn), lambda j, i: (i, j)),
        out_shape=jax.ShapeDtypeStruct((t, d), F32),
        scratch_shapes=[pltpu.VMEM((k, tn), BF16)],
        compiler_params=_cparams(("parallel", "arbitrary")),
        name="out_proj",
    )(m, w, x2d)


def _rope_tables(pos, dim, width):
    inv = 1.0 / (ROPE_THETA ** (jnp.arange(0, dim, 2, dtype=F32) / dim))
    ang = pos.astype(F32)[:, None] * inv[None, :]
    cos = jnp.cos(ang)
    sin = jnp.sin(ang)
    cos = jnp.concatenate([cos, cos], axis=-1)
    sin_s = jnp.concatenate([-sin, sin], axis=-1)
    pad = ((0, 0), (0, width - dim))
    return jnp.pad(cos, pad), jnp.pad(sin_s, pad)


def _repack_kernel(srcp_ref, srcq_ref, start_ref, valid_ref, p_ref, q_ref, o_ref, *, n_src_cols):
    j = pl.program_id(1)
    start = start_ref[j]
    valid = valid_ref[j]
    row = lax.broadcasted_iota(jnp.int32, (LANES, LANES), 0)
    col = lax.broadcasted_iota(jnp.int32, (LANES, LANES), 1)
    lane = lax.broadcasted_iota(jnp.int32, (1, LANES), 1)

    def contribution(blk_ref, src_blk):
        take = jnp.where((src_blk * LANES + row == start + col) & (col < valid), 1.0, 0.0).astype(BF16)
        blk = jnp.where(src_blk * LANES + lane < n_src_cols, blk_ref[...], 0.0).astype(BF16)
        return _dot_nn(blk, take)

    o_ref[...] = (contribution(p_ref, srcp_ref[j]) + contribution(q_ref, srcq_ref[j])).astype(o_ref.dtype)


def _repack_tables(pieces, n_out_blocks, n_src_blocks):
    starts, valids = [], []
    for start, width in pieces:
        if width == 0:
            starts.append(0)
            valids.append(0)
        for off in range(0, width, LANES):
            starts.append(start + off)
            valids.append(min(LANES, width - off))
    starts += [0] * (n_out_blocks - len(starts))
    valids += [0] * (n_out_blocks - len(valids))
    assert n_src_blocks >= 2
    slots = [0, 1]
    srcp, srcq = [], []
    for start, valid in zip(starts, valids):
        need = []
        if valid > 0:
            need = list(range(start // LANES, (start + valid - 1) // LANES + 1))
        for blk in need:
            if blk not in slots:
                slots[0 if slots[0] not in need else 1] = blk
        srcp.append(slots[0])
        srcq.append(slots[1])
    return tuple(jnp.asarray(np.asarray(v, np.int32)) for v in (srcp, srcq, starts, valids))


def _repack(w, pieces, n_out_blocks):
    depth, k, n = w.shape
    tables = _repack_tables(pieces, n_out_blocks, -(-n // LANES))
    grid_spec = pltpu.PrefetchScalarGridSpec(
        num_scalar_prefetch=4,
        grid=(depth, n_out_blocks),
        in_specs=[
            pl.BlockSpec((None, k, LANES), lambda d, j, sp, sq, st, va: (d, 0, sp[j])),
            pl.BlockSpec((None, k, LANES), lambda d, j, sp, sq, st, va: (d, 0, sq[j])),
        ],
        out_specs=pl.BlockSpec((None, k, LANES), lambda d, j, sp, sq, st, va: (d, 0, j)),
    )
    return pl.pallas_call(
        functools.partial(_repack_kernel, n_src_cols=n),
        grid_spec=grid_spec,
        out_shape=jax.ShapeDtypeStruct((depth, k, n_out_blocks * LANES), BF16),
        compiler_params=_cparams(("parallel", "arbitrary")),
        name="repack",
    )(*tables, w, w)


def _repack_t_kernel(srcp_ref, srcq_ref, start_ref, valid_ref, p_ref, q_ref, o_ref, *, n_src_cols):
    j = pl.program_id(0)
    start = start_ref[j]
    valid = valid_ref[j]
    out_row = lax.broadcasted_iota(jnp.int32, (LANES, LANES), 0)
    src_row = lax.broadcasted_iota(jnp.int32, (LANES, LANES), 1)
    sub = lax.broadcasted_iota(jnp.int32, (LANES, 1), 0)

    def contribution(blk_ref, src_blk, layer):
        take = jnp.where((src_blk * LANES + src_row == start + out_row) & (out_row < valid), 1.0, 0.0).astype(BF16)
        blk = jnp.where(src_blk * LANES + sub < n_src_cols, blk_ref[:, layer, :], 0.0).astype(BF16)
        return _dot_nn(take, blk)

    for layer in range(o_ref.shape[0]):
        o_ref[layer] = (contribution(p_ref, srcp_ref[j], layer)
                        + contribution(q_ref, srcq_ref[j], layer)).astype(o_ref.dtype)


def _repack_t(wt, pieces, n_out_blocks):
    n, depth, k = wt.shape
    tables = _repack_tables(pieces, n_out_blocks, -(-n // LANES))
    grid_spec = pltpu.PrefetchScalarGridSpec(
        num_scalar_prefetch=4,
        grid=(n_out_blocks,),
        in_specs=[
            pl.BlockSpec((LANES, depth, k), lambda j, sp, sq, st, va: (sp[j], 0, 0)),
            pl.BlockSpec((LANES, depth, k), lambda j, sp, sq, st, va: (sq[j], 0, 0)),
        ],
        out_specs=pl.BlockSpec((depth, LANES, k), lambda j, sp, sq, st, va: (0, j, 0)),
    )
    return pl.pallas_call(
        functools.partial(_repack_t_kernel, n_src_cols=n),
        grid_spec=grid_spec,
        out_shape=jax.ShapeDtypeStruct((depth, n_out_blocks * LANES, k), BF16),
        compiler_params=_cparams(("arbitrary",)),
        name="repack_t",
    )(*tables, wt, wt)


def _pack_w_in(w):
    sizes = [MOBA_HEADS * HEAD_DIM] * 3 + [MLA_Q_RANK, MLA_KV_RANK, MLA_ROPE, NSA_HEADS * HEAD_DIM] \
        + [NSA_KV_GROUPS * HEAD_DIM] * 6
    offs = [int(v) for v in np.concatenate([[0], np.cumsum(sizes)])]
    (a_q, a_k, a_v, b_cq, b_ckv, b_kr, c_q, c_kc, c_vc, c_ks, c_vs, c_kw, c_vw) = [
        (offs[i], sizes[i]) for i in range(len(sizes))]
    gate0 = offs[-1]
    per_group = NSA_REP * 3
    gates = [(gate0 + gi * per_group, per_group) for gi in range(NSA_KV_GROUPS)]
    merge = (gate0 + NSA_HEADS * 3, w.shape[2] - gate0 - NSA_HEADS * 3)
    pieces = [c_q, b_ckv, b_cq, a_q, a_k, a_v, c_kc, c_vc, c_ks, c_vs, c_kw, c_vw] + gates + [b_kr]
    used = sum(-(-width // LANES) for _, width in pieces)
    pieces += [(0, 0)] * (Z_COLS // LANES - used) + [merge]
    return _repack_t(jnp.transpose(w, (2, 0, 1)), pieces, (Z_COLS + merge[1]) // LANES)


def _pack_w_uq(w):
    per_head = MLA_NOPE + MLA_ROPE
    pieces = [(h * per_head, MLA_NOPE) for h in range(MLA_HEADS)] \
        + [(h * per_head + MLA_NOPE, MLA_ROPE) for h in range(MLA_HEADS)]
    return _repack(w, pieces, 2 * MLA_HEADS)


def _pack_w_ukv(w):
    per_head = MLA_NOPE + MLA_V
    pieces = [(h * per_head, MLA_NOPE) for h in range(MLA_HEADS)] \
        + [(h * per_head + MLA_NOPE, MLA_V) for h in range(MLA_HEADS)]
    return _repack(w, pieces, 2 * MLA_HEADS)


def kernel(x, ffn1_norm, ffn1_w_gate, ffn1_w_up, ffn1_w_down, mix_norm, w_in, mla_q_norm, mla_w_uq, mla_kv_norm, mla_w_ukv, nsa_cmp_pos_k, nsa_cmp_w1_k, nsa_cmp_w2_k, nsa_cmp_pos_v, nsa_cmp_w1_v, nsa_cmp_w2_v, w_branch_moba, w_branch_mla, w_branch_nsa, w_out, ffn2_norm, ffn2_w_gate, ffn2_w_up, ffn2_w_down, final_norm):
    b, s, d = x.shape
    depth = w_in.shape[0]
    t = b * s
    assert NSA_CMP_LEN == 2 * NSA_CMP_STRIDE and s % MOBA_BLOCK == 0 and s % NSA_SEL_BLOCK == 0

    pos = jnp.arange(s)
    cos_h, sin_h = _rope_tables(pos, HEAD_DIM, LANES)
    cos_r, sin_r = _rope_tables(pos, MLA_ROPE, LANES)
    n16 = s // NSA_CMP_STRIDE
    cmp_end = np.arange(n16) * NSA_CMP_STRIDE + NSA_CMP_LEN - 1
    cos_c, sin_c = _rope_tables(jnp.asarray(cmp_end), HEAD_DIM, LANES)
    n_sel = s // NSA_SEL_BLOCK
    sel_start = np.arange(n_sel) * NSA_SEL_BLOCK
    cmp_start = np.arange(n16) * NSA_CMP_STRIDE
    ovl_t = ((cmp_start[None, :] <= sel_start[:, None] + NSA_SEL_BLOCK - 1)
             & (cmp_end[None, :] >= sel_start[:, None])).astype(np.float32)
    ovl_t = jnp.asarray(ovl_t, BF16)

    w_in_p = _pack_w_in(w_in)
    w_uq_p = _pack_w_uq(mla_w_uq)
    w_ukv_p = _pack_w_ukv(mla_w_ukv)

    xf = x.reshape(t, d)
    for i in range(depth):
        xf = _ffn(xf, ffn1_norm[i], ffn1_w_gate, ffn1_w_up, ffn1_w_down, i)

        z, gates = _in_proj(xf, mix_norm[i], w_in_p, i)
        z3 = z.reshape(b, s, Z_COLS)
        y_a = _moba(z3, cos_h, sin_h)
        y_b = _mla(z3, mla_q_norm[i], mla_kv_norm[i], w_uq_p, w_ukv_p, i, cos_r, sin_r)

        w1 =jnp.stack([nsa_cmp_w1_k[i], nsa_cmp_w1_v[i]])
        w2 = jnp.stack([nsa_cmp_w2_k[i], nsa_cmp_w2_v[i]])
        posflat = jnp.stack([nsa_cmp_pos_k[i].reshape(1, -1), nsa_cmp_pos_v[i].reshape(1, -1)])
        cmp_kv = _nsa_compress(z3, w1, w2, posflat, cos_c, sin_c)
        y_c = _nsa(z3, cmp_kv, cos_h, sin_h, ovl_t)

        merged = _merge(y_a.reshape(t, -1), y_b.reshape(t, -1), y_c.reshape(t, -1), gates,
                        w_branch_moba, w_branch_mla, w_branch_nsa, i)
        xf = _proj_residual(merged, w_out, xf, i)

        xf = _ffn(xf, ffn2_norm[i], ffn2_w_gate, ffn2_w_up, ffn2_w_down, i)
    return _rms(xf, final_norm, F32).reshape(b, s, d)
```

```python
import functools

import numpy as np
import jax
import jax.numpy as jnp
from jax import lax
from jax.experimental import pallas as pl
from jax.experimental.pallas import tpu as pltpu

F32 = jnp.float32
BF16 = jnp.bfloat16

HEAD_DIM = 128
ROPE_THETA = 10000.0
NORM_EPS = 1e-6
NEG_BIG = -1e30
TINY = 1e-30
LOG2E = 1.4426950408889634
N_BRANCHES = 3

MOBA_HEADS = 4
MOBA_BLOCK = 256
MOBA_TOPK = 3

MLA_HEADS = 6
MLA_Q_RANK = 512
MLA_KV_RANK = 256
MLA_NOPE = 128
MLA_ROPE = 64
MLA_V = 128

NSA_HEADS = 6
NSA_KV_GROUPS = 2
NSA_REP = NSA_HEADS // NSA_KV_GROUPS
NSA_CMP_LEN = 32
NSA_CMP_STRIDE = 16
NSA_CMP_HIDDEN = 256
NSA_SEL_BLOCK = 64
NSA_SEL_TOPK = 16
NSA_WINDOW = 512
FORCE_SCORE = 1e9

LANES = 128
SUBLANES = 8
SOFTMAX_CHUNK = 64
VMEM_LIMIT = 56 * 2**20

OFF_CQ = 0
OFF_BCKV = 768
OFF_BCQ = 1024
OFF_AQ = 1536
OFF_AK = 2048
OFF_AV = 2560
OFF_CKC = 3072
OFF_CVC = 3328
OFF_CKS = 3584
OFF_CVS = 3840
OFF_CKW = 4096
OFF_CVW = 4352
OFF_CGATE = 4608
OFF_BKR = 4864
Z_COLS = 5120


def _cparams(sem):
    return pltpu.CompilerParams(dimension_semantics=sem, vmem_limit_bytes=VMEM_LIMIT)


def _rmsnorm(x, g):
    ms = jnp.mean(x * x, axis=-1, keepdims=True)
    return x * lax.rsqrt(ms + NORM_EPS) * g


def _sigmoid(x):
    return 0.5 * jnp.tanh(0.5 * x) + 0.5


def _rope_full(x, cos, sin_signed):
    return x * cos + pltpu.roll(x, HEAD_DIM // 2, axis=1) * sin_signed


def _rope_half(x, cos, sin_signed):
    lane = lax.broadcasted_iota(jnp.int32, x.shape, 1)
    q = MLA_ROPE // 2
    swapped = jnp.where(lane < q, pltpu.roll(x, LANES - q, axis=1), pltpu.roll(x, q, axis=1))
    return x * cos + swapped * sin_signed


def _dot_nt(a, b):
    return lax.dot_general(a, b, (((1,), (1,)), ((), ())), preferred_element_type=F32)


def _dot_nn(a, b):
    return jnp.dot(a, b, preferred_element_type=F32)


def _split(a):
    hi = a.astype(BF16)
    lo = (a - hi.astype(F32)).astype(BF16)
    return hi, lo


def _dot3_nt(a, b):
    ah, al = _split(a)
    bh, bl = _split(b)
    return _dot_nt(ah, bh) + _dot_nt(ah, bl) + _dot_nt(al, bh)


def _dot3_nn(a, b):
    ah, al = _split(a)
    bh, bl = _split(b)
    return _dot_nn(ah, bh) + _dot_nn(ah, bl) + _dot_nn(al, bh)


def _softmax_t(s, mask, scale):
    s = jnp.where(mask, s * scale, NEG_BIG)
    m = jnp.max(s, axis=0, keepdims=True)
    e = jnp.where(mask, jnp.exp(s - m), 0.0)
    l = jnp.maximum(jnp.sum(e, axis=0, keepdims=True), TINY)
    return e, l


def _softmax_parts(parts, chunk=SOFTMAX_CHUNK):
    lanes = parts[0][0].shape[1]
    pieces = []
    for s, mask_fn in parts:
        for a in range(0, s.shape[0], chunk):
            n = min(chunk, s.shape[0] - a)
            p = s[a:a + n]
            pieces.append(p if mask_fn is None else jnp.where(mask_fn(a, n), p, NEG_BIG))

    def fold(x, op):
        return op(x.reshape(x.shape[0] // SUBLANES, SUBLANES, lanes), axis=0)

    m8 = functools.reduce(jnp.maximum, [fold(p, jnp.max) for p in pieces])
    mx = jnp.max(m8, axis=0, keepdims=True)
    es, l8 = [], None
    for p in pieces:
        e = jnp.exp2(p - mx)
        l8 = fold(e, jnp.sum) if l8 is None else l8 + fold(e, jnp.sum)
        es.append(e.astype(BF16))
    l = jnp.sum(l8, axis=0, keepdims=True)
    e = jnp.concatenate(es, axis=0) if len(es) > 1 else es[0]
    return e, jnp.maximum(l, TINY)


def _rank_t(v, nb):
    blk = lax.broadcasted_iota(jnp.int32, v.shape, 0)
    rank = jnp.zeros(v.shape, F32)
    for j in range(nb):
        vj = v[j:j + 1, :]
        ge = (vj >= v).astype(F32)
        gt = (vj > v).astype(F32)
        rank = rank + jnp.where(blk > j, ge, gt)
    return rank


def _rms_kernel(x_ref, g_ref, o_ref):
    o_ref[...] = _rmsnorm(x_ref[...], g_ref[...]).astype(o_ref.dtype)


def _rms(x2d, g, out_dtype, tm=512):
    t, d = x2d.shape
    return pl.pallas_call(
        _rms_kernel,
        grid=(t // tm,),
        in_specs=[pl.BlockSpec((tm, d), lambda i: (i, 0)), pl.BlockSpec((1, d), lambda i: (0, 0))],
        out_specs=pl.BlockSpec((tm, d), lambda i: (i, 0)),
        out_shape=jax.ShapeDtypeStruct((t, d), out_dtype),
        compiler_params=_cparams(("parallel",)),
        name="rmsnorm",
    )(x2d, g.reshape(1, d))


def _ffn_up_kernel(h_ref, wg_ref, wu_ref, wd_ref, o_ref, wd_bf_out_ref, wg_bf_ref, wu_bf_ref):
    @pl.when(pl.program_id(1) == 0)
    def _():
        wg_bf_ref[...] = wg_ref[...].astype(BF16)
        wu_bf_ref[...] = wu_ref[...].astype(BF16)
        wd_bf_out_ref[...] = wd_ref[...].astype(BF16)

    h = h_ref[...]
    a = _dot_nn(h, wg_bf_ref[...])
    u = _dot_nn(h, wu_bf_ref[...])
    o_ref[...] = (a * _sigmoid(a) * u).astype(o_ref.dtype)


def _ffn_down_kernel(a_ref, wd_ref, x_ref, o_ref):
    o_ref[...] = x_ref[...] + 0.5 * _dot_nn(a_ref[...], wd_ref[...])


def _ffn(x2d, g, wg, wu, wd, layer, tm=2048, tf=256, tm_down=512, tn=1024):
    t, d = x2d.shape
    f = wg.shape[2]
    h = _rms(x2d, g, BF16)
    act, wd_bf = pl.pallas_call(
        _ffn_up_kernel,
        grid=(f // tf, t // tm),
        in_specs=[
            pl.BlockSpec((tm, d), lambda j, i: (i, 0)),
            pl.BlockSpec((None, d, tf), lambda j, i: (layer, 0, j)),
            pl.BlockSpec((None, d, tf), lambda j, i: (layer, 0, j)),
            pl.BlockSpec((None, tf, d), lambda j, i: (layer, j, 0)),
        ],
        out_specs=[
            pl.BlockSpec((tm, tf), lambda j, i: (i, j)),
            pl.BlockSpec((tf, d), lambda j, i: (j, 0)),
        ],
        out_shape=[jax.ShapeDtypeStruct((t, f), BF16), jax.ShapeDtypeStruct((f, d), BF16)],
        scratch_shapes=[pltpu.VMEM((d, tf), BF16), pltpu.VMEM((d, tf), BF16)],
        compiler_params=_cparams(("parallel", "arbitrary")),
        name="ffn_up",
    )(h, wg, wu, wd)
    return pl.pallas_call(
        _ffn_down_kernel,
        grid=(d // tn, t // tm_down),
        in_specs=[
            pl.BlockSpec((tm_down, f), lambda n, i: (i, 0)),
            pl.BlockSpec((f, tn), lambda n, i: (0, n)),
            pl.BlockSpec((tm_down, tn), lambda n, i: (i, n)),
        ],
        out_specs=pl.BlockSpec((tm_down, tn), lambda n, i: (i, n)),
        out_shape=jax.ShapeDtypeStruct((t, d), F32),
        compiler_params=_cparams(("parallel", "parallel")),
        name="ffn_down",
    )(act, wd_bf, x2d)


def _in_proj_kernel(x_ref, g_ref, w_ref, z_ref, gate_ref, h_ref, *, n_z):
    j = pl.program_id(1)

    @pl.when(j == 0)
    def _():
        h_ref[...] = _rmsnorm(x_ref[...], g_ref[...]).astype(BF16)

    @pl.when(j < n_z)
    def _():
        z_ref[...] = _dot_nt(h_ref[...], w_ref[...])

    @pl.when(j >= n_z)
    def _():
        gate_ref[...] = _sigmoid(_dot_nt(h_ref[...], w_ref[...])).astype(gate_ref.dtype)


def _in_proj(x2d, g, w, layer, tm=1024, tn=1024):
    t, d = x2d.shape
    n = w.shape[1]
    n_z = Z_COLS // tn
    n_g = (n - Z_COLS) // tn
    return pl.pallas_call(
        functools.partial(_in_proj_kernel, n_z=n_z),
        grid=(t // tm, n_z + n_g),
        in_specs=[
            pl.BlockSpec((tm, d), lambda i, j: (i, 0)),
            pl.BlockSpec((1, d), lambda i, j: (0, 0)),
            pl.BlockSpec((None, tn, d), lambda i, j: (layer, j, 0)),
        ],
        out_specs=[
            pl.BlockSpec((tm, tn), lambda i, j: (i, jnp.minimum(j, n_z - 1))),
            pl.BlockSpec((tm, tn), lambda i, j: (i, jnp.maximum(j - n_z, 0))),
        ],
        out_shape=[jax.ShapeDtypeStruct((t, Z_COLS), F32), jax.ShapeDtypeStruct((t, n - Z_COLS), BF16)],
        scratch_shapes=[pltpu.VMEM((tm, d), BF16)],
        compiler_params=_cparams(("parallel", "arbitrary")),
        name="in_proj",
    )(x2d, g.reshape(1, d), w)


def _moba_kernel(q_ref, k_ref, v_ref, cosf_ref, sinf_ref, cosq_ref, sinq_ref, o_ref,
                 kr_ref, km_ref, vt_ref, *, n_blk, scale):
    qi = pl.program_id(2)
    tq = q_ref.shape[1]
    hp = q_ref.shape[2] // LANES
    head = lambda hh: slice(hh * LANES, (hh + 1) * LANES)

    @pl.when(qi == 0)
    def _():
        for hh in range(hp):
            kr = _rope_full(k_ref[0, :, head(hh)], cosf_ref[...], sinf_ref[...])
            kr_ref[hh] = kr.astype(BF16)
            km_ref[hh] = jnp.concatenate(
                [jnp.mean(kr[j * MOBA_BLOCK:(j + 1) * MOBA_BLOCK], axis=0, keepdims=True) for j in range(n_blk)],
                axis=0)
            vt_ref[hh] = v_ref[0, :, head(hh)].T.astype(BF16)

    def causal(a, n):
        return (lax.broadcasted_iota(jnp.int32, (n, tq), 0) + a
                <= lax.broadcasted_iota(jnp.int32, (n, tq), 1))

    for n in range(n_blk):
        @pl.when(qi == n)
        def _(n=n):
            for hh in range(hp):
                qr = _rope_full(q_ref[0, :, head(hh)], cosq_ref[...], sinq_ref[...])
                q_bf = (qr * (scale * LOG2E)).astype(BF16)
                s = _dot_nt(kr_ref[hh, 0:(n + 1) * tq], q_bf)
                parts = []
                if n > MOBA_TOPK:
                    gate = _dot3_nt(km_ref[hh], qr)
                    past = lax.broadcasted_iota(jnp.int32, (n_blk, tq), 0) < n
                    gate = jnp.where(past, gate, -jnp.inf)
                    drop = jnp.where(_rank_t(gate, n) < MOBA_TOPK, 0.0, NEG_BIG)
                    for j in range(n):
                        parts.append((s[j * tq:(j + 1) * tq] + drop[j:j + 1, :], None))
                elif n > 0:
                    parts.append((s[0:n * tq], None))
                parts.append((s[n * tq:], causal))
                e, l = _softmax_parts(parts)
                o_t = _dot_nn(vt_ref[hh, :, 0:(n + 1) * tq], e) / l
                o_ref[0, :, head(hh)] = o_t.T.astype(o_ref.dtype)


def _moba(z3, cos, sin_s, tq=MOBA_BLOCK, hp=MOBA_HEADS):
    b, s, _ = z3.shape
    n_blk = s // MOBA_BLOCK
    w = hp * LANES
    qb, kb, vb = OFF_AQ // w, OFF_AK // w, OFF_AV // w
    return pl.pallas_call(
        functools.partial(_moba_kernel, n_blk=n_blk, scale=HEAD_DIM ** -0.5),
        grid=(b, MOBA_HEADS // hp, s // tq),
        in_specs=[
            pl.BlockSpec((1, tq, w), lambda bi, h, qi: (bi, qi, qb + h)),
            pl.BlockSpec((1, s, w), lambda bi, h, qi: (bi, 0, kb + h)),
            pl.BlockSpec((1, s, w), lambda bi, h, qi: (bi, 0, vb + h)),
            pl.BlockSpec((s, LANES), lambda bi, h, qi: (0, 0)),
            pl.BlockSpec((s, LANES), lambda bi, h, qi: (0, 0)),
            pl.BlockSpec((tq, LANES), lambda bi, h, qi: (qi, 0)),
            pl.BlockSpec((tq, LANES), lambda bi, h, qi: (qi, 0)),
        ],
        out_specs=pl.BlockSpec((1, tq, w), lambda bi, h, qi: (bi, qi, h)),
        out_shape=jax.ShapeDtypeStruct((b, s, MOBA_HEADS * HEAD_DIM), BF16),
        scratch_shapes=[pltpu.VMEM((hp, s, LANES), BF16), pltpu.VMEM((hp, n_blk, LANES), F32),
                        pltpu.VMEM((hp, LANES, s), BF16)],
        compiler_params=_cparams(("parallel", "parallel", "arbitrary")),
        name="moba",
    )(z3, z3, z3, cos, sin_s, cos, sin_s)


def _mla_kernel(cq_ref, ckv_ref, kr_ref, qg_ref, kvg_ref, wqn_ref, wqr_ref, wkn_ref, wv_ref,
                cosf_ref, sinf_ref, cosq_ref, sinq_ref, o_ref, kf_ref, vt_ref, *, scale):
    qi = pl.program_id(2)
    tq = cq_ref.shape[1]
    s_len = ckv_ref.shape[1]
    hp = wqn_ref.shape[1] // LANES
    head = lambda hh: slice(hh * LANES, (hh + 1) * LANES)

    @pl.when(qi == 0)
    def _():
        ckv = _rmsnorm(ckv_ref[0], kvg_ref[...]).astype(BF16)
        k_nope = _dot_nn(ckv, wkn_ref[...])
        v = _dot_nn(ckv, wv_ref[...])
        k_rope = _rope_half(kr_ref[0], cosf_ref[...], sinf_ref[...]).astype(BF16)
        for hh in range(hp):
            kf_ref[hh, :, :LANES] = k_nope[:, head(hh)].astype(BF16)
            kf_ref[hh, :, LANES:] = k_rope
            vt_ref[hh] = v[:, head(hh)].T.astype(BF16)

    cq = _rmsnorm(cq_ref[0], qg_ref[...]).astype(BF16)
    q_nope = _dot_nn(cq, wqn_ref[...])
    q_rope = _dot_nn(cq, wqr_ref[...])
    def causal(a, n):
        return (lax.broadcasted_iota(jnp.int32, (n, tq), 0) + a
                <= lax.broadcasted_iota(jnp.int32, (n, tq), 1))

    for n in range(s_len // tq):
        @pl.when(qi == n)
        def _(n=n):
            for hh in range(hp):
                q = jnp.concatenate([q_nope[:, head(hh)],
                                     _rope_half(q_rope[:, head(hh)], cosq_ref[...], sinq_ref[...])], axis=1)
                q_bf = (q * (scale * LOG2E)).astype(BF16)
                s = _dot_nt(kf_ref[hh, 0:(n + 1) * tq], q_bf)
                parts = [(s[0:n * tq], None)] if n > 0 else []
                parts.append((s[n * tq:], causal))
                e, l = _softmax_parts(parts)
                o_t = _dot_nn(vt_ref[hh, :, 0:(n + 1) * tq], e) / l
                o_ref[0, :, head(hh)] = o_t.T.astype(o_ref.dtype)


def _mla(z3, q_gain, kv_gain, w_uq, w_ukv, layer, cos, sin_s, tq=256, hp=3):
    b, s, _ = z3.shape
    w = hp * LANES
    ng = MLA_HEADS // hp
    return pl.pallas_call(
        functools.partial(_mla_kernel, scale=(MLA_NOPE + MLA_ROPE) ** -0.5),
        grid=(b, ng, s // tq),
        in_specs=[
            pl.BlockSpec((1, tq, MLA_Q_RANK), lambda bi, h, qi: (bi, qi, OFF_BCQ // MLA_Q_RANK)),
            pl.BlockSpec((1, s, MLA_KV_RANK), lambda bi, h, qi: (bi, 0, OFF_BCKV // MLA_KV_RANK)),
            pl.BlockSpec((1, s, LANES), lambda bi, h, qi: (bi, 0, OFF_BKR // LANES)),
            pl.BlockSpec((1, MLA_Q_RANK), lambda bi, h, qi: (0, 0)),
            pl.BlockSpec((1, MLA_KV_RANK), lambda bi, h, qi: (0, 0)),
            pl.BlockSpec((None, MLA_Q_RANK, w), lambda bi, h, qi: (layer, 0, h)),
            pl.BlockSpec((None, MLA_Q_RANK, w), lambda bi, h, qi: (layer, 0, ng + h)),
            pl.BlockSpec((None, MLA_KV_RANK, w), lambda bi, h, qi: (layer, 0, h)),
            pl.BlockSpec((None, MLA_KV_RANK, w), lambda bi, h, qi: (layer, 0, ng + h)),
            pl.BlockSpec((s, LANES), lambda bi, h, qi: (0, 0)),
            pl.BlockSpec((s, LANES), lambda bi, h, qi: (0, 0)),
            pl.BlockSpec((tq, LANES), lambda bi, h, qi: (qi, 0)),
            pl.BlockSpec((tq, LANES), lambda bi, h, qi: (qi, 0)),
        ],
        out_specs=pl.BlockSpec((1, tq, w), lambda bi, h, qi: (bi, qi, h)),
        out_shape=jax.ShapeDtypeStruct((b, s, MLA_HEADS * MLA_V), BF16),
        scratch_shapes=[pltpu.VMEM((hp, s, 2 * LANES), BF16), pltpu.VMEM((hp, LANES, s), BF16)],
        compiler_params=_cparams(("parallel", "parallel", "arbitrary")),
        name="mla",
    )(z3, z3, z3, q_gain.reshape(1, -1), kv_gain.reshape(1, -1), w_uq, w_uq, w_ukv, w_ukv,
      cos, sin_s, cos, sin_s)


def _cmp_kernel(x_ref, w1_ref, w2_ref, pos_ref, cos_ref, sin_ref, o_ref):
    kv = pl.program_id(0)
    n16 = o_ref.shape[3]
    x = jnp.concatenate(
        [x_ref[0, pl.ds(l, n16, stride=NSA_CMP_STRIDE), :] for l in range(NSA_CMP_STRIDE)], axis=1)
    half = x.shape[1]
    w1 = w1_ref[0]
    first = _dot3_nn(x, w1[:half])
    second = _dot3_nn(x, w1[half:])
    bias = _dot3_nn(jnp.broadcast_to(pos_ref[0], (8, 2 * half)), w1)[0:1]
    hid = first + pltpu.roll(second, n16 - 1, axis=0) + bias
    y = _dot3_nn(jax.nn.gelu(hid), w2_ref[0])
    roped = _rope_full(y, cos_ref[...], sin_ref[...])
    o_ref[0, 0, 0] = jnp.where(kv == 0, roped, y)


def _nsa_compress(z3, w1, w2, posflat, cos_c, sin_c):
    b, s, _ = z3.shape
    g = NSA_KV_GROUPS
    n16 = s // NSA_CMP_STRIDE
    width = NSA_CMP_STRIDE * HEAD_DIM
    hid = w1.shape[2]
    base = OFF_CKC // LANES
    return pl.pallas_call(
        _cmp_kernel,
        grid=(2, b, g),
        in_specs=[
            pl.BlockSpec((1, s, LANES), lambda kv, bi, gi: (bi, 0, base + g * kv + gi)),
            pl.BlockSpec((1, 2 * width, hid), lambda kv, bi, gi: (kv, 0, 0)),
            pl.BlockSpec((1, hid, HEAD_DIM), lambda kv, bi, gi: (kv, 0, 0)),
            pl.BlockSpec((1, 1, 2 * width), lambda kv, bi, gi: (kv, 0, 0)),
            pl.BlockSpec((n16, HEAD_DIM), lambda kv, bi, gi: (0, 0)),
            pl.BlockSpec((n16, HEAD_DIM), lambda kv, bi, gi: (0, 0)),
        ],
        out_specs=pl.BlockSpec((1, 1, 1, n16, HEAD_DIM), lambda kv, bi, gi: (kv, bi, gi, 0, 0)),
        out_shape=jax.ShapeDtypeStruct((2, b, g, n16, HEAD_DIM), F32),
        compiler_params=_cparams(("parallel", "parallel", "parallel")),
        name="nsa_compress",
    )(z3, w1, w2, posflat, cos_c, sin_c)


def _nsa_kernel(q_ref, ks_ref, vs_ref, kw_ref, vw_ref, kc_ref, vc_ref, gate_ref,
                cosf_ref, sinf_ref, cosq_ref, sinq_ref, ovl_ref, o_ref,
                ksr_ref, vst_ref, kw3_ref, vwt3_ref, vct_ref, os_ref, *, n_sel, scale):
    qi = pl.program_id(1)
    tq = q_ref.shape[1]
    s_len = ks_ref.shape[1]
    n_grp = ks_ref.shape[2] // HEAD_DIM
    rep = NSA_REP
    lanes = rep * tq
    n_chunk = s_len // tq
    pad_chunks = NSA_WINDOW // tq
    n_cmp = kc_ref.shape[3]
    grp = lambda gg: slice(gg * HEAD_DIM, (gg + 1) * HEAD_DIM)
    groups = range(n_grp)

    @pl.when(qi == 0)
    def _():
        for gg in groups:
            ksr_ref[gg] = _rope_full(ks_ref[0, :, grp(gg)], cosf_ref[...], sinf_ref[...]).astype(BF16)
            vst_ref[gg] = vs_ref[0, :, grp(gg)].T.astype(BF16)
            kw3_ref[gg, 0:pad_chunks] = jnp.zeros((pad_chunks, tq, HEAD_DIM), BF16)
            vwt3_ref[gg, 0:pad_chunks] = jnp.zeros((pad_chunks, HEAD_DIM, tq), BF16)
            kwr = _rope_full(kw_ref[0, :, grp(gg)], cosf_ref[...], sinf_ref[...]).astype(BF16)
            for c in range(n_chunk):
                kw3_ref[gg, pad_chunks + c] = kwr[c * tq:(c + 1) * tq]
                vwt3_ref[gg, pad_chunks + c] = vw_ref[0, c * tq:(c + 1) * tq, grp(gg)].T.astype(BF16)
            vct_ref[gg] = vc_ref[0, 0, gg].T.astype(BF16)

    cos_q = cosq_ref[...]
    sin_q = sinq_ref[...]
    lane = lax.broadcasted_iota(jnp.int32, (1, lanes), 1)
    t_loc = lane % tq
    t = qi * tq + t_loc
    cur = t[:, 0:tq] // NSA_SEL_BLOCK
    blk = lax.broadcasted_iota(jnp.int32, (n_sel, tq), 0)
    forced = (blk == 0) | (blk == cur) | (blk == cur - 1)
    visible = blk <= cur
    cmp_end = (lax.broadcasted_iota(jnp.int32, (n_cmp, lanes), 0) * NSA_CMP_STRIDE + (NSA_CMP_LEN - 1))
    ovl = ovl_ref[...]

    q_bf, o_c, imp = [], [], []
    for gg in groups:
        q_all = jnp.concatenate(
            [_rope_full(q_ref[0, :, grp(gg * rep + r)], cos_q, sin_q) for r in range(rep)],
            axis=0)
        q_bf.append((q_all * (scale * LOG2E)).astype(BF16))

        s_c = _dot3_nt(kc_ref[0, 0, gg], q_all)
        e_c, l_c = _softmax_t(s_c, cmp_end <= t, scale)
        p_c = e_c / l_c
        o_c.append(_dot_nn(vct_ref[gg], p_c.astype(BF16)))

        p_hi, p_lo = _split(p_c)
        imp3 = _dot_nn(ovl, p_hi) + _dot_nn(ovl, p_lo)
        imp_g = imp3[:, 0:tq]
        for r in range(1, rep):
            imp_g = imp_g + imp3[:, r * tq:(r + 1) * tq]
        imp_g = jnp.where(forced, FORCE_SCORE, imp_g)
        imp.append(jnp.where(visible, imp_g, -jnp.inf))

    unit = 2 * tq
    sb = NSA_SEL_BLOCK
    for v in range(s_len // unit):
        @pl.when(qi // 2 == v)
        def _(v=v):
            n_keys = (v + 1) * unit
            lo = n_keys - unit

            def causal(base, a, n):
                return lax.broadcasted_iota(jnp.int32, (n, lanes), 0) <= (t - (base + a))

            n_vis = n_keys // sb
            for gg in groups:
                s = _dot_nt(ksr_ref[gg, 0:n_keys], q_bf[gg])
                if n_vis > NSA_SEL_TOPK:
                    rank = _rank_t(imp[gg], n_vis)
                    rank3 = jnp.concatenate([rank] * rep, axis=1)
                    drop = jnp.where(rank3 < NSA_SEL_TOPK, 0.0, NEG_BIG)
                    parts = []
                    for j in range(n_vis):
                        parts.append((s[j * sb:(j + 1) * sb] + drop[j:j + 1, :],
                                      functools.partial(causal, j * sb) if j * sb >= lo else None))
                else:
                    parts = [(s[0:lo], None)] if lo > 0 else []
                    parts.append((s[lo:], functools.partial(causal, lo)))
                e_s, l_s = _softmax_parts(parts)
                os_ref[gg] = _dot_nn(vst_ref[gg, :, 0:n_keys], e_s) / l_s

    n_win = pad_chunks + 1
    def oldest(a, n):
        return lax.broadcasted_iota(jnp.int32, (n, lanes), 0) + a > t_loc

    def newest(a, n):
        return lax.broadcasted_iota(jnp.int32, (n, lanes), 0) + a <= t_loc

    for gg in groups:
        parts = []
        for c in range(n_win):
            s_w = _dot_nt(kw3_ref[gg, qi + c], q_bf[gg])
            if c < pad_chunks:
                s_w = s_w + jnp.where(qi + c >= pad_chunks, 0.0, NEG_BIG)
            parts.append((s_w, oldest if c == 0 else (newest if c == n_win - 1 else None)))
        e_w, l_w = _softmax_parts(parts)
        o_w = _dot_nn(vwt3_ref[gg, qi], e_w[0:tq])
        for c in range(1, n_win):
            o_w = o_w + _dot_nn(vwt3_ref[gg, qi + c], e_w[c * tq:(c + 1) * tq])
        o_w = o_w / l_w

        g_t = _sigmoid(gate_ref[0, :, grp(gg)].T)

        def gate_row(branch, g_t=g_t):
            return jnp.concatenate([g_t[3 * r + branch:3 * r + branch + 1, :] for r in range(rep)], axis=1)

        out_t = gate_row(0) * o_c[gg] + gate_row(1) * os_ref[gg] + gate_row(2) * o_w
        for r in range(rep):
            o_ref[0, :, grp(gg * rep + r)] = out_t[:, r * tq:(r + 1) * tq].T.astype(o_ref.dtype)


def _nsa(z3, cmp_kv, cos, sin_s, ovl_t, tq=128):
    b, s, _ = z3.shape
    g = NSA_KV_GROUPS
    n_sel = s // NSA_SEL_BLOCK
    n_cmp = cmp_kv.shape[3]
    qw = NSA_HEADS * HEAD_DIM
    kvw = g * HEAD_DIM
    n_chunk_pad = (s + NSA_WINDOW) // tq

    def col(off):
        return lambda bi, qi: (bi, 0, off // kvw)

    return pl.pallas_call(
        functools.partial(_nsa_kernel, n_sel=n_sel, scale=HEAD_DIM ** -0.5),
        grid=(b, s // tq),
        in_specs=[
            pl.BlockSpec((1, tq, qw), lambda bi, qi: (bi, qi, OFF_CQ // qw)),
            pl.BlockSpec((1, s, kvw), col(OFF_CKS)),
            pl.BlockSpec((1, s, kvw), col(OFF_CVS)),
            pl.BlockSpec((1, s, kvw), col(OFF_CKW)),
            pl.BlockSpec((1, s, kvw), col(OFF_CVW)),
            pl.BlockSpec((1, 1, g, n_cmp, HEAD_DIM), lambda bi, qi: (0, bi, 0, 0, 0)),
            pl.BlockSpec((1, 1, g, n_cmp, HEAD_DIM), lambda bi, qi: (1, bi, 0, 0, 0)),
            pl.BlockSpec((1, tq, g * LANES), lambda bi, qi: (bi, qi, OFF_CGATE // (g * LANES))),
            pl.BlockSpec((s, LANES), lambda bi, qi: (0, 0)),
            pl.BlockSpec((s, LANES), lambda bi, qi: (0, 0)),
            pl.BlockSpec((tq, LANES), lambda bi, qi: (qi, 0)),
            pl.BlockSpec((tq, LANES), lambda bi, qi: (qi, 0)),
            pl.BlockSpec((n_sel, n_cmp), lambda bi, qi: (0, 0)),
        ],
        out_specs=pl.BlockSpec((1, tq, qw), lambda bi, qi: (bi, qi, 0)),
        out_shape=jax.ShapeDtypeStruct((b, s, qw), BF16),
        scratch_shapes=[
            pltpu.VMEM((g, s, HEAD_DIM), BF16),
            pltpu.VMEM((g, HEAD_DIM, s), BF16),
            pltpu.VMEM((g, n_chunk_pad, tq, HEAD_DIM), BF16),
            pltpu.VMEM((g, n_chunk_pad, HEAD_DIM, tq), BF16),
            pltpu.VMEM((g, HEAD_DIM, n_cmp), BF16),
            pltpu.VMEM((g, HEAD_DIM, NSA_REP * tq), F32),
        ],
        compiler_params=_cparams(("parallel", "arbitrary")),
        name="nsa",
    )(z3, z3, z3, z3, z3, cmp_kv, cmp_kv, z3, cos, sin_s, cos, sin_s, ovl_t)


def _merge_kernel(ya_ref, yb_ref, yc_ref, ga_ref, gb_ref, gc_ref, wa_ref, wb_ref, wc_ref, o_ref,
                  wa_bf_ref, wb_bf_ref, wc_bf_ref):
    @pl.when(pl.program_id(1) == 0)
    def _():
        wa_bf_ref[...] = wa_ref[...].astype(BF16)
        wb_bf_ref[...] = wb_ref[...].astype(BF16)
        wc_bf_ref[...] = wc_ref[...].astype(BF16)

    m = ga_ref[...].astype(F32) * _dot_nn(ya_ref[...], wa_bf_ref[...])
    m = m + gb_ref[...].astype(F32) * _dot_nn(yb_ref[...], wb_bf_ref[...])
    m = m + gc_ref[...].astype(F32) * _dot_nn(yc_ref[...], wc_bf_ref[...])
    o_ref[...] = m.astype(o_ref.dtype)


def _merge(ya, yb, yc, gates, wa, wb, wc, layer, tm=1024, tn=1024):
    t = ya.shape[0]
    d = wa.shape[2]
    nj = d // tn

    def gate_spec(branch):
        return pl.BlockSpec((tm, tn), lambda j, i: (i, branch * nj + j))

    def y_spec(y):
        return pl.BlockSpec((tm, y.shape[1]), lambda j, i: (i, 0))

    def w_spec(w):
        return pl.BlockSpec((None, w.shape[1], tn), lambda j, i: (layer, 0, j), pipeline_mode=pl.Buffered(1))

    return pl.pallas_call(
        _merge_kernel,
        grid=(nj, t // tm),
        in_specs=[y_spec(ya), y_spec(yb), y_spec(yc), gate_spec(0), gate_spec(1), gate_spec(2),
                  w_spec(wa), w_spec(wb), w_spec(wc)],
        out_specs=pl.BlockSpec((tm, tn), lambda j, i: (i, j)),
        out_shape=jax.ShapeDtypeStruct((t, d), BF16),
        scratch_shapes=[pltpu.VMEM((w.shape[1], tn), BF16) for w in (wa, wb, wc)],
        compiler_params=_cparams(("parallel", "arbitrary")),
        name="merge",
    )(ya, yb, yc, gates, gates, gates, wa, wb, wc)


def _proj_residual_kernel(m_ref, w_ref, x_ref, o_ref, w_bf_ref):
    @pl.when(pl.program_id(1) == 0)
    def _():
        w_bf_ref[...] = w_ref[...].astype(BF16)

    o_ref[...] = x_ref[...] + _dot_nn(m_ref[...], w_bf_ref[...])


def _proj_residual(m, w, x2d, layer, tm=1024, tn=1024):
    t, k = m.shape
    d = w.shape[2]
    return pl.pallas_call(
        _proj_residual_kernel,
        grid=(d // tn, t // tm),
        in_specs=[
            pl.BlockSpec((tm, k), lambda j, i: (i, 0)),
            pl.BlockSpec((None, k, tn), lambda j, i: (layer, 0, j), pipeline_mode=pl.Buffered(1)),
            pl.BlockSpec((tm, tn), lambda j, i: (i, j)),
        ],
        out_specs=pl.BlockSpec((tm, tn), lambda j, i: (i, j)),
        out_shape=jax.ShapeDtypeStruct((t, d), F32),
        scratch_shapes=[pltpu.VMEM((k, tn), BF16)],
        compiler_params=_cparams(("parallel", "arbitrary")),
        name="out_proj",
    )(m, w, x2d)


def _rope_tables(pos, dim, width):
    inv = 1.0 / (ROPE_THETA ** (jnp.arange(0, dim, 2, dtype=F32) / dim))
    ang = pos.astype(F32)[:, None] * inv[None, :]
    cos = jnp.cos(ang)
    sin = jnp.sin(ang)
    cos = jnp.concatenate([cos, cos], axis=-1)
    sin_s = jnp.concatenate([-sin, sin], axis=-1)
    pad = ((0, 0), (0, width - dim))
    return jnp.pad(cos, pad), jnp.pad(sin_s, pad)


def _repack_kernel(srcp_ref, srcq_ref, start_ref, valid_ref, p_ref, q_ref, o_ref, *, n_src_cols):
    j = pl.program_id(1)
    start = start_ref[j]
    valid = valid_ref[j]
    row = lax.broadcasted_iota(jnp.int32, (LANES, LANES), 0)
    col = lax.broadcasted_iota(jnp.int32, (LANES, LANES), 1)
    lane = lax.broadcasted_iota(jnp.int32, (1, LANES), 1)

    def contribution(blk_ref, src_blk):
        take = jnp.where((src_blk * LANES + row == start + col) & (col < valid), 1.0, 0.0).astype(BF16)
        blk = jnp.where(src_blk * LANES + lane < n_src_cols, blk_ref[...], 0.0).astype(BF16)
        return _dot_nn(blk, take)

    o_ref[...] = (contribution(p_ref, srcp_ref[j]) + contribution(q_ref, srcq_ref[j])).astype(o_ref.dtype)


def _repack_tables(pieces, n_out_blocks, n_src_blocks):
    starts, valids = [], []
    for start, width in pieces:
        if width == 0:
            starts.append(0)
            valids.append(0)
        for off in range(0, width, LANES):
            starts.append(start + off)
            valids.append(min(LANES, width - off))
    starts += [0] * (n_out_blocks - len(starts))
    valids += [0] * (n_out_blocks - len(valids))
    assert n_src_blocks >= 2
    slots = [0, 1]
    srcp, srcq = [], []
    for start, valid in zip(starts, valids):
        need = []
        if valid > 0:
            need = list(range(start // LANES, (start + valid - 1) // LANES + 1))
        for blk in need:
            if blk not in slots:
                slots[0 if slots[0] not in need else 1] = blk
        srcp.append(slots[0])
        srcq.append(slots[1])
    return tuple(jnp.asarray(np.asarray(v, np.int32)) for v in (srcp, srcq, starts, valids))


def _repack(w, pieces, n_out_blocks):
    depth, k, n = w.shape
    tables = _repack_tables(pieces, n_out_blocks, -(-n // LANES))
    grid_spec = pltpu.PrefetchScalarGridSpec(
        num_scalar_prefetch=4,
        grid=(depth, n_out_blocks),
        in_specs=[
            pl.BlockSpec((None, k, LANES), lambda d, j, sp, sq, st, va: (d, 0, sp[j])),
            pl.BlockSpec((None, k, LANES), lambda d, j, sp, sq, st, va: (d, 0, sq[j])),
        ],
        out_specs=pl.BlockSpec((None, k, LANES), lambda d, j, sp, sq, st, va: (d, 0, j)),
    )
    return pl.pallas_call(
        functools.partial(_repack_kernel, n_src_cols=n),
        grid_spec=grid_spec,
        out_shape=jax.ShapeDtypeStruct((depth, k, n_out_blocks * LANES), BF16),
        compiler_params=_cparams(("parallel", "arbitrary")),
        name="repack",
    )(*tables, w, w)


def _repack_t_kernel(srcp_ref, srcq_ref, start_ref, valid_ref, p_ref, q_ref, o_ref, *, n_src_cols):
    j = pl.program_id(0)
    start = start_ref[j]
    valid = valid_ref[j]
    out_row = lax.broadcasted_iota(jnp.int32, (LANES, LANES), 0)
    src_row = lax.broadcasted_iota(jnp.int32, (LANES, LANES), 1)
    sub = lax.broadcasted_iota(jnp.int32, (LANES, 1), 0)

    def contribution(blk_ref, src_blk, layer):
        take = jnp.where((src_blk * LANES + src_row == start + out_row) & (out_row < valid), 1.0, 0.0).astype(BF16)
        blk = jnp.where(src_blk * LANES + sub < n_src_cols, blk_ref[:, layer, :], 0.0).astype(BF16)
        return _dot_nn(take, blk)

    for layer in range(o_ref.shape[0]):
        o_ref[layer] = (contribution(p_ref, srcp_ref[j], layer)
                        + contribution(q_ref, srcq_ref[j], layer)).astype(o_ref.dtype)


def _repack_t(wt, pieces, n_out_blocks):
    n, depth, k = wt.shape
    tables = _repack_tables(pieces, n_out_blocks, -(-n // LANES))
    grid_spec = pltpu.PrefetchScalarGridSpec(
        num_scalar_prefetch=4,
        grid=(n_out_blocks,),
        in_specs=[
            pl.BlockSpec((LANES, depth, k), lambda j, sp, sq, st, va: (sp[j], 0, 0)),
            pl.BlockSpec((LANES, depth, k), lambda j, sp, sq, st, va: (sq[j], 0, 0)),
        ],
        out_specs=pl.BlockSpec((depth, LANES, k), lambda j, sp, sq, st, va: (0, j, 0)),
    )
    return pl.pallas_call(
        functools.partial(_repack_t_kernel, n_src_cols=n),
        grid_spec=grid_spec,
        out_shape=jax.ShapeDtypeStruct((depth, n_out_blocks * LANES, k), BF16),
        compiler_params=_cparams(("arbitrary",)),
        name="repack_t",
    )(*tables, wt, wt)


def _pack_w_in(w):
    sizes = [MOBA_HEADS * HEAD_DIM] * 3 + [MLA_Q_RANK, MLA_KV_RANK, MLA_ROPE, NSA_HEADS * HEAD_DIM] \
        + [NSA_KV_GROUPS * HEAD_DIM] * 6
    offs = [int(v) for v in np.concatenate([[0], np.cumsum(sizes)])]
    (a_q, a_k, a_v, b_cq, b_ckv, b_kr, c_q, c_kc, c_vc, c_ks, c_vs, c_kw, c_vw) = [
        (offs[i], sizes[i]) for i in range(len(sizes))]
    gate0 = offs[-1]
    per_group = NSA_REP * 3
    gates = [(gate0 + gi * per_group, per_group) for gi in range(NSA_KV_GROUPS)]
    merge = (gate0 + NSA_HEADS * 3, w.shape[2] - gate0 - NSA_HEADS * 3)
    pieces = [c_q, b_ckv, b_cq, a_q, a_k, a_v, c_kc, c_vc, c_ks, c_vs, c_kw, c_vw] + gates + [b_kr]
    used = sum(-(-width // LANES) for _, width in pieces)
    pieces += [(0, 0)] * (Z_COLS // LANES - used) + [merge]
    return _repack_t(jnp.transpose(w, (2, 0, 1)), pieces, (Z_COLS + merge[1]) // LANES)


def _pack_w_uq(w):
    per_head = MLA_NOPE + MLA_ROPE
    pieces = [(h * per_head, MLA_NOPE) for h in range(MLA_HEADS)] \
        + [(h * per_head + MLA_NOPE, MLA_ROPE) for h in range(MLA_HEADS)]
    return _repack(w, pieces, 2 * MLA_HEADS)


def _pack_w_ukv(w):
    per_head = MLA_NOPE + MLA_V
    pieces = [(h * per_head, MLA_NOPE) for h in range(MLA_HEADS)] \
        + [(h * per_head + MLA_NOPE, MLA_V) for h in range(MLA_HEADS)]
    return _repack(w, pieces, 2 * MLA_HEADS)


def kernel(x, ffn1_norm, ffn1_w_gate, ffn1_w_up, ffn1_w_down, mix_norm, w_in, mla_q_norm, mla_w_uq, mla_kv_norm, mla_w_ukv, nsa_cmp_pos_k, nsa_cmp_w1_k, nsa_cmp_w2_k, nsa_cmp_pos_v, nsa_cmp_w1_v, nsa_cmp_w2_v, w_branch_moba, w_branch_mla, w_branch_nsa, w_out, ffn2_norm, ffn2_w_gate, ffn2_w_up, ffn2_w_down, final_norm):
    b, s, d = x.shape
    depth = w_in.shape[0]
    t = b * s
    assert NSA_CMP_LEN == 2 * NSA_CMP_STRIDE and s % MOBA_BLOCK == 0 and s % NSA_SEL_BLOCK == 0

    pos = jnp.arange(s)
    cos_h, sin_h = _rope_tables(pos, HEAD_DIM, LANES)
    cos_r, sin_r = _rope_tables(pos, MLA_ROPE, LANES)
    n16 = s // NSA_CMP_STRIDE
    cmp_end = np.arange(n16) * NSA_CMP_STRIDE + NSA_CMP_LEN - 1
    cos_c, sin_c = _rope_tables(jnp.asarray(cmp_end), HEAD_DIM, LANES)
    n_sel = s // NSA_SEL_BLOCK
    sel_start = np.arange(n_sel) * NSA_SEL_BLOCK
    cmp_start = np.arange(n16) * NSA_CMP_STRIDE
    ovl_t = ((cmp_start[None, :] <= sel_start[:, None] + NSA_SEL_BLOCK - 1)
             & (cmp_end[None, :] >= sel_start[:, None])).astype(np.float32)
    ovl_t = jnp.asarray(ovl_t, BF16)

    w_in_p = _pack_w_in(w_in)
    w_uq_p = _pack_w_uq(mla_w_uq)
    w_ukv_p = _pack_w_ukv(mla_w_ukv)

    xf = x.reshape(t, d)
    for i in range(depth):
        xf = _ffn(xf, ffn1_norm[i], ffn1_w_gate, ffn1_w_up, ffn1_w_down, i)

        z, gates = _in_proj(xf, mix_norm[i], w_in_p, i)
        z3 = z.reshape(b, s, Z_COLS)
        y_a = _moba(z3, cos_h, sin_h)
        y_b = _mla(z3, mla_q_norm[i], mla_kv_norm[i], w_uq_p, w_ukv_p, i, cos_r, sin_r)

        w1 =jnp.stack([nsa_cmp_w1_k[i], nsa_cmp_w1_v[i]])
        w2 = jnp.stack([nsa_cmp_w2_k[i], nsa_cmp_w2_v[i]])
        posflat = jnp.stack([nsa_cmp_pos_k[i].reshape(1, -1), nsa_cmp_pos_v[i].reshape(1, -1)])
        cmp_kv = _nsa_compress(z3, w1, w2, posflat, cos_c, sin_c)
        y_c = _nsa(z3, cmp_kv, cos_h, sin_h, ovl_t)

        merged = _merge(y_a.reshape(t, -1), y_b.reshape(t, -1), y_c.reshape(t, -1), gates,
                        w_branch_moba, w_branch_mla, w_branch_nsa, i)
        xf = _proj_residual(merged, w_out, xf, i)

        xf = _ffn(xf, ffn2_norm[i], ffn2_w_gate, ffn2_w_up, ffn2_w_down, i)
    return _rms(xf, final_norm, F32).reshape(b, s, d)
```

```python
import functools

import numpy as np
import jax
import jax.numpy as jnp
from jax import lax
from jax.experimental import pallas as pl
from jax.experimental.pallas import tpu as pltpu

F32 = jnp.float32
BF16 = jnp.bfloat16

HEAD_DIM = 128
ROPE_THETA = 10000.0
NORM_EPS = 1e-6
NEG_BIG = -1e30
TINY = 1e-30
LOG2E = 1.4426950408889634
N_BRANCHES = 3

MOBA_HEADS = 4
MOBA_BLOCK = 256
MOBA_TOPK = 3

MLA_HEADS = 6
MLA_Q_RANK = 512
MLA_KV_RANK = 256
MLA_NOPE = 128
MLA_ROPE = 64
MLA_V = 128

NSA_HEADS = 6
NSA_KV_GROUPS = 2
NSA_REP = NSA_HEADS // NSA_KV_GROUPS
NSA_CMP_LEN = 32
NSA_CMP_STRIDE = 16
NSA_CMP_HIDDEN = 256
NSA_SEL_BLOCK = 64
NSA_SEL_TOPK = 16
NSA_WINDOW = 512
FORCE_SCORE = 1e9

LANES = 128
SUBLANES = 8
SOFTMAX_CHUNK = 64
VMEM_LIMIT = 56 * 2**20

OFF_CQ = 0
OFF_BCKV = 768
OFF_BCQ = 1024
OFF_AQ = 1536
OFF_AK = 2048
OFF_AV = 2560
OFF_CKC = 3072
OFF_CVC = 3328
OFF_CKS = 3584
OFF_CVS = 3840
OFF_CKW = 4096
OFF_CVW = 4352
OFF_CGATE = 4608
OFF_BKR = 4864
Z_COLS = 5120


def _cparams(sem):
    return pltpu.CompilerParams(dimension_semantics=sem, vmem_limit_bytes=VMEM_LIMIT)


def _rmsnorm(x, g):
    ms = jnp.mean(x * x, axis=-1, keepdims=True)
    return x * lax.rsqrt(ms + NORM_EPS) * g


def _sigmoid(x):
    return 0.5 * jnp.tanh(0.5 * x) + 0.5


def _rope_full(x, cos, sin_signed):
    return x * cos + pltpu.roll(x, HEAD_DIM // 2, axis=1) * sin_signed


def _rope_half(x, cos, sin_signed):
    lane = lax.broadcasted_iota(jnp.int32, x.shape, 1)
    q = MLA_ROPE // 2
    swapped = jnp.where(lane < q, pltpu.roll(x, LANES - q, axis=1), pltpu.roll(x, q, axis=1))
    return x * cos + swapped * sin_signed


def _dot_nt(a, b):
    return lax.dot_general(a, b, (((1,), (1,)), ((), ())), preferred_element_type=F32)


def _dot_nn(a, b):
    return jnp.dot(a, b, preferred_element_type=F32)


def _split(a):
    hi = a.astype(BF16)
    lo = (a - hi.astype(F32)).astype(BF16)
    return hi, lo


def _dot3_nt(a, b):
    ah, al = _split(a)
    bh, bl = _split(b)
    return _dot_nt(ah, bh) + _dot_nt(ah, bl) + _dot_nt(al, bh)


def _dot3_nn(a, b):
    ah, al = _split(a)
    bh, bl = _split(b)
    return _dot_nn(ah, bh) + _dot_nn(ah, bl) + _dot_nn(al, bh)


def _softmax_t(s, mask, scale):
    s = jnp.where(mask, s * scale, NEG_BIG)
    m = jnp.max(s, axis=0, keepdims=True)
    e = jnp.where(mask, jnp.exp(s - m), 0.0)
    l = jnp.maximum(jnp.sum(e, axis=0, keepdims=True), TINY)
    return e, l


def _softmax_parts(parts, chunk=SOFTMAX_CHUNK):
    lanes = parts[0][0].shape[1]
    chunk = max(SUBLANES, chunk * 384 // max(lanes, 384))
    pieces = []
    for s, mask_fn in parts:
        for a in range(0, s.shape[0], chunk):
            n = min(chunk, s.shape[0] - a)
            p = s[a:a + n]
            pieces.append(p if mask_fn is None else jnp.where(mask_fn(a, n), p, NEG_BIG))

    def fold(x, op):
        return op(x.reshape(x.shape[0] // SUBLANES, SUBLANES, lanes), axis=0)

    m8 = functools.reduce(jnp.maximum, [fold(p, jnp.max) for p in pieces])
    mx = jnp.max(m8, axis=0, keepdims=True)
    es, l8 = [], None
    for p in pieces:
        e = jnp.exp2(p - mx)
        l8 = fold(e, jnp.sum) if l8 is None else l8 + fold(e, jnp.sum)
        es.append(e.astype(BF16))
    l = jnp.sum(l8, axis=0, keepdims=True)
    e = jnp.concatenate(es, axis=0) if len(es) > 1 else es[0]
    return e, jnp.maximum(l, TINY)


def _rank_t(v, nb):
    blk = lax.broadcasted_iota(jnp.int32, v.shape, 0)
    rank = jnp.zeros(v.shape, F32)
    for j in range(nb):
        vj = v[j:j + 1, :]
        ge = (vj >= v).astype(F32)
        gt = (vj > v).astype(F32)
        rank = rank + jnp.where(blk > j, ge, gt)
    return rank


def _rms_kernel(x_ref, g_ref, o_ref):
    o_ref[...] = _rmsnorm(x_ref[...], g_ref[...]).astype(o_ref.dtype)


def _rms(x2d, g, out_dtype, tm=512):
    t, d = x2d.shape
    return pl.pallas_call(
        _rms_kernel,
        grid=(t // tm,),
        in_specs=[pl.BlockSpec((tm, d), lambda i: (i, 0)), pl.BlockSpec((1, d), lambda i: (0, 0))],
        out_specs=pl.BlockSpec((tm, d), lambda i: (i, 0)),
        out_shape=jax.ShapeDtypeStruct((t, d), out_dtype),
        compiler_params=_cparams(("parallel",)),
        name="rmsnorm",
    )(x2d, g.reshape(1, d))


def _ffn_up_kernel(h_ref, wg_ref, wu_ref, wd_ref, o_ref, wd_bf_out_ref, wg_bf_ref, wu_bf_ref):
    @pl.when(pl.program_id(1) == 0)
    def _():
        wg_bf_ref[...] = wg_ref[...].astype(BF16)
        wu_bf_ref[...] = wu_ref[...].astype(BF16)
        wd_bf_out_ref[...] = wd_ref[...].astype(BF16)

    h = h_ref[...]
    a = _dot_nn(h, wg_bf_ref[...])
    u = _dot_nn(h, wu_bf_ref[...])
    o_ref[...] = (a * _sigmoid(a) * u).astype(o_ref.dtype)


def _ffn_down_kernel(a_ref, wd_ref, x_ref, o_ref):
    o_ref[...] = x_ref[...] + 0.5 * _dot_nn(a_ref[...], wd_ref[...])


def _ffn(x2d, g, wg, wu, wd, layer, tm=1024, tf=512, tm_down=512, tn=1024):
    t, d = x2d.shape
    f = wg.shape[2]
    h = _rms(x2d, g, BF16)
    act, wd_bf = pl.pallas_call(
        _ffn_up_kernel,
        grid=(f // tf, t // tm),
        in_specs=[
            pl.BlockSpec((tm, d), lambda j, i: (i, 0)),
            pl.BlockSpec((None, d, tf), lambda j, i: (layer, 0, j)),
            pl.BlockSpec((None, d, tf), lambda j, i: (layer, 0, j)),
            pl.BlockSpec((None, tf, d), lambda j, i: (layer, j, 0)),
        ],
        out_specs=[
            pl.BlockSpec((tm, tf), lambda j, i: (i, j)),
            pl.BlockSpec((tf, d), lambda j, i: (j, 0)),
        ],
        out_shape=[jax.ShapeDtypeStruct((t, f), BF16), jax.ShapeDtypeStruct((f, d), BF16)],
        scratch_shapes=[pltpu.VMEM((d, tf), BF16), pltpu.VMEM((d, tf), BF16)],
        compiler_params=_cparams(("parallel", "arbitrary")),
        name="ffn_up",
    )(h, wg, wu, wd)
    return pl.pallas_call(
        _ffn_down_kernel,
        grid=(d // tn, t // tm_down),
        in_specs=[
            pl.BlockSpec((tm_down, f), lambda n, i: (i, 0)),
            pl.BlockSpec((f, tn), lambda n, i: (0, n)),
            pl.BlockSpec((tm_down, tn), lambda n, i: (i, n)),
        ],
        out_specs=pl.BlockSpec((tm_down, tn), lambda n, i: (i, n)),
        out_shape=jax.ShapeDtypeStruct((t, d), F32),
        compiler_params=_cparams(("parallel", "parallel")),
        name="ffn_down",
    )(act, wd_bf, x2d)


def _in_proj_kernel(x_ref, g_ref, w_ref, z_ref, gate_ref, h_ref, *, n_z):
    j = pl.program_id(1)

    @pl.when(j == 0)
    def _():
        h_ref[...] = _rmsnorm(x_ref[...], g_ref[...]).astype(BF16)

    @pl.when(j < n_z)
    def _():
        z_ref[...] = _dot_nt(h_ref[...], w_ref[...])

    @pl.when(j >= n_z)
    def _():
        gate_ref[...] = _sigmoid(_dot_nt(h_ref[...], w_ref[...])).astype(gate_ref.dtype)


def _in_proj(x2d, g, w, layer, tm=1024, tn=1024):
    t, d = x2d.shape
    n = w.shape[1]
    n_z = Z_COLS // tn
    n_g = (n - Z_COLS) // tn
    return pl.pallas_call(
        functools.partial(_in_proj_kernel, n_z=n_z),
        grid=(t // tm, n_z + n_g),
        in_specs=[
            pl.BlockSpec((tm, d), lambda i, j: (i, 0)),
            pl.BlockSpec((1, d), lambda i, j: (0, 0)),
            pl.BlockSpec((None, tn, d), lambda i, j: (layer, j, 0)),
        ],
        out_specs=[
            pl.BlockSpec((tm, tn), lambda i, j: (i, jnp.minimum(j, n_z - 1))),
            pl.BlockSpec((tm, tn), lambda i, j: (i, jnp.maximum(j - n_z, 0))),
        ],
        out_shape=[jax.ShapeDtypeStruct((t, Z_COLS), F32), jax.ShapeDtypeStruct((t, n - Z_COLS), BF16)],
        scratch_shapes=[pltpu.VMEM((tm, d), BF16)],
        compiler_params=_cparams(("parallel", "arbitrary")),
        name="in_proj",
    )(x2d, g.reshape(1, d), w)


def _moba_kernel(q_ref, k_ref, v_ref, cosf_ref, sinf_ref, cosq_ref, sinq_ref, o_ref,
                 kr_ref, km_ref, vt_ref, *, n_blk, scale):
    qi = pl.program_id(2)
    tq = q_ref.shape[1]
    hp = q_ref.shape[2] // LANES
    head = lambda hh: slice(hh * LANES, (hh + 1) * LANES)

    @pl.when(qi == 0)
    def _():
        for hh in range(hp):
            kr = _rope_full(k_ref[0, :, head(hh)], cosf_ref[...], sinf_ref[...])
            kr_ref[hh] = kr.astype(BF16)
            km_ref[hh] = jnp.concatenate(
                [jnp.mean(kr[j * MOBA_BLOCK:(j + 1) * MOBA_BLOCK], axis=0, keepdims=True) for j in range(n_blk)],
                axis=0)
            vt_ref[hh] = v_ref[0, :, head(hh)].T.astype(BF16)

    def causal(a, n):
        return (lax.broadcasted_iota(jnp.int32, (n, tq), 0) + a
                <= lax.broadcasted_iota(jnp.int32, (n, tq), 1))

    for n in range(n_blk):
        @pl.when(qi == n)
        def _(n=n):
            for hh in range(hp):
                qr = _rope_full(q_ref[0, :, head(hh)], cosq_ref[...], sinq_ref[...])
                q_bf = (qr * (scale * LOG2E)).astype(BF16)
                s = _dot_nt(kr_ref[hh, 0:(n + 1) * tq], q_bf)
                parts = []
                if n > MOBA_TOPK:
                    gate = _dot3_nt(km_ref[hh], qr)
                    past = lax.broadcasted_iota(jnp.int32, (n_blk, tq), 0) < n
                    gate = jnp.where(past, gate, -jnp.inf)
                    drop = jnp.where(_rank_t(gate, n) < MOBA_TOPK, 0.0, NEG_BIG)
                    for j in range(n):
                        parts.append((s[j * tq:(j + 1) * tq] + drop[j:j + 1, :], None))
                elif n > 0:
                    parts.append((s[0:n * tq], None))
                parts.append((s[n * tq:], causal))
                e, l = _softmax_parts(parts)
                o_t = _dot_nn(vt_ref[hh, :, 0:(n + 1) * tq], e) / l
                o_ref[0, :, head(hh)] = o_t.T.astype(o_ref.dtype)


def _moba(z3, cos, sin_s, tq=MOBA_BLOCK, hp=MOBA_HEADS):
    b, s, _ = z3.shape
    n_blk = s // MOBA_BLOCK
    w = hp * LANES
    qb, kb, vb = OFF_AQ // w, OFF_AK // w, OFF_AV // w
    return pl.pallas_call(
        functools.partial(_moba_kernel, n_blk=n_blk, scale=HEAD_DIM ** -0.5),
        grid=(b, MOBA_HEADS // hp, s // tq),
        in_specs=[
            pl.BlockSpec((1, tq, w), lambda bi, h, qi: (bi, qi, qb + h)),
            pl.BlockSpec((1, s, w), lambda bi, h, qi: (bi, 0, kb + h)),
            pl.BlockSpec((1, s, w), lambda bi, h, qi: (bi, 0, vb + h)),
            pl.BlockSpec((s, LANES), lambda bi, h, qi: (0, 0)),
            pl.BlockSpec((s, LANES), lambda bi, h, qi: (0, 0)),
            pl.BlockSpec((tq, LANES), lambda bi, h, qi: (qi, 0)),
            pl.BlockSpec((tq, LANES), lambda bi, h, qi: (qi, 0)),
        ],
        out_specs=pl.BlockSpec((1, tq, w), lambda bi, h, qi: (bi, qi, h)),
        out_shape=jax.ShapeDtypeStruct((b, s, MOBA_HEADS * HEAD_DIM), BF16),
        scratch_shapes=[pltpu.VMEM((hp, s, LANES), BF16), pltpu.VMEM((hp, n_blk, LANES), F32),
                        pltpu.VMEM((hp, LANES, s), BF16)],
        compiler_params=_cparams(("parallel", "parallel", "arbitrary")),
        name="moba",
    )(z3, z3, z3, cos, sin_s, cos, sin_s)


def _mla_kernel(cq_ref, ckv_ref, kr_ref, qg_ref, kvg_ref, wqn_ref, wqr_ref, wkn_ref, wv_ref,
                cosf_ref, sinf_ref, cosq_ref, sinq_ref, o_ref, kf_ref, vt_ref, *, scale):
    qi = pl.program_id(2)
    tq = cq_ref.shape[1]
    s_len = ckv_ref.shape[1]
    hp = wqn_ref.shape[1] // LANES
    head = lambda hh: slice(hh * LANES, (hh + 1) * LANES)

    @pl.when(qi == 0)
    def _():
        ckv = _rmsnorm(ckv_ref[0], kvg_ref[...]).astype(BF16)
        k_nope = _dot_nn(ckv, wkn_ref[...])
        v = _dot_nn(ckv, wv_ref[...])
        k_rope = _rope_half(kr_ref[0], cosf_ref[...], sinf_ref[...]).astype(BF16)
        for hh in range(hp):
            kf_ref[hh, :, :LANES] = k_nope[:, head(hh)].astype(BF16)
            kf_ref[hh, :, LANES:] = k_rope
            vt_ref[hh] = v[:, head(hh)].T.astype(BF16)

    cq = _rmsnorm(cq_ref[0], qg_ref[...]).astype(BF16)
    q_nope = _dot_nn(cq, wqn_ref[...])
    q_rope = _dot_nn(cq, wqr_ref[...])
    def causal(a, n):
        return (lax.broadcasted_iota(jnp.int32, (n, tq), 0) + a
                <= lax.broadcasted_iota(jnp.int32, (n, tq), 1))

    for n in range(s_len // tq):
        @pl.when(qi == n)
        def _(n=n):
            for hh in range(hp):
                q = jnp.concatenate([q_nope[:, head(hh)],
                                     _rope_half(q_rope[:, head(hh)], cosq_ref[...], sinq_ref[...])], axis=1)
                q_bf = (q * (scale * LOG2E)).astype(BF16)
                s = _dot_nt(kf_ref[hh, 0:(n + 1) * tq], q_bf)
                parts = [(s[0:n * tq], None)] if n > 0 else []
                parts.append((s[n * tq:], causal))
                e, l = _softmax_parts(parts)
                o_t = _dot_nn(vt_ref[hh, :, 0:(n + 1) * tq], e) / l
                o_ref[0, :, head(hh)] = o_t.T.astype(o_ref.dtype)


def _mla(z3, q_gain, kv_gain, w_uq, w_ukv, layer, cos, sin_s, tq=256, hp=MLA_HEADS):
    b, s, _ = z3.shape
    w = hp * LANES
    ng = MLA_HEADS // hp
    return pl.pallas_call(
        functools.partial(_mla_kernel, scale=(MLA_NOPE + MLA_ROPE) ** -0.5),
        grid=(b, ng, s // tq),
        in_specs=[
            pl.BlockSpec((1, tq, MLA_Q_RANK), lambda bi, h, qi: (bi, qi, OFF_BCQ // MLA_Q_RANK)),
            pl.BlockSpec((1, s, MLA_KV_RANK), lambda bi, h, qi: (bi, 0, OFF_BCKV // MLA_KV_RANK)),
            pl.BlockSpec((1, s, LANES), lambda bi, h, qi: (bi, 0, OFF_BKR // LANES)),
            pl.BlockSpec((1, MLA_Q_RANK), lambda bi, h, qi: (0, 0)),
            pl.BlockSpec((1, MLA_KV_RANK), lambda bi, h, qi: (0, 0)),
            pl.BlockSpec((None, MLA_Q_RANK, w), lambda bi, h, qi: (layer, 0, h)),
            pl.BlockSpec((None, MLA_Q_RANK, w), lambda bi, h, qi: (layer, 0, ng + h)),
            pl.BlockSpec((None, MLA_KV_RANK, w), lambda bi, h, qi: (layer, 0, h)),
            pl.BlockSpec((None, MLA_KV_RANK, w), lambda bi, h, qi: (layer, 0, ng + h)),
            pl.BlockSpec((s, LANES), lambda bi, h, qi: (0, 0)),
            pl.BlockSpec((s, LANES), lambda bi, h, qi: (0, 0)),
            pl.BlockSpec((tq, LANES), lambda bi, h, qi: (qi, 0)),
            pl.BlockSpec((tq, LANES), lambda bi, h, qi: (qi, 0)),
        ],
        out_specs=pl.BlockSpec((1, tq, w), lambda bi, h, qi: (bi, qi, h)),
        out_shape=jax.ShapeDtypeStruct((b, s, MLA_HEADS * MLA_V), BF16),
        scratch_shapes=[pltpu.VMEM((hp, s, 2 * LANES), BF16), pltpu.VMEM((hp, LANES, s), BF16)],
        compiler_params=_cparams(("parallel", "parallel", "arbitrary")),
        name="mla",
    )(z3, z3, z3, q_gain.reshape(1, -1), kv_gain.reshape(1, -1), w_uq, w_uq, w_ukv, w_ukv,
      cos, sin_s, cos, sin_s)


def _cmp_kernel(x_ref, w1_ref, w2_ref, pos_ref, cos_ref, sin_ref, o_ref):
    kv = pl.program_id(0)
    n16 = o_ref.shape[3]
    x = jnp.concatenate(
        [x_ref[0, pl.ds(l, n16, stride=NSA_CMP_STRIDE), :] for l in range(NSA_CMP_STRIDE)], axis=1)
    half = x.shape[1]
    w1 = w1_ref[0]
    first = _dot3_nn(x, w1[:half])
    second = _dot3_nn(x, w1[half:])
    bias = _dot3_nn(jnp.broadcast_to(pos_ref[0], (8, 2 * half)), w1)[0:1]
    hid = first + pltpu.roll(second, n16 - 1, axis=0) + bias
    y = _dot3_nn(jax.nn.gelu(hid), w2_ref[0])
    roped = _rope_full(y, cos_ref[...], sin_ref[...])
    o_ref[0, 0, 0] = jnp.where(kv == 0, roped, y)


def _nsa_compress(z3, w1, w2, posflat, cos_c, sin_c):
    b, s, _ = z3.shape
    g = NSA_KV_GROUPS
    n16 = s // NSA_CMP_STRIDE
    width = NSA_CMP_STRIDE * HEAD_DIM
    hid = w1.shape[2]
    base = OFF_CKC // LANES
    return pl.pallas_call(
        _cmp_kernel,
        grid=(2, b, g),
        in_specs=[
            pl.BlockSpec((1, s, LANES), lambda kv, bi, gi: (bi, 0, base + g * kv + gi)),
            pl.BlockSpec((1, 2 * width, hid), lambda kv, bi, gi: (kv, 0, 0)),
            pl.BlockSpec((1, hid, HEAD_DIM), lambda kv, bi, gi: (kv, 0, 0)),
            pl.BlockSpec((1, 1, 2 * width), lambda kv, bi, gi: (kv, 0, 0)),
            pl.BlockSpec((n16, HEAD_DIM), lambda kv, bi, gi: (0, 0)),
            pl.BlockSpec((n16, HEAD_DIM), lambda kv, bi, gi: (0, 0)),
        ],
        out_specs=pl.BlockSpec((1, 1, 1, n16, HEAD_DIM), lambda kv, bi, gi: (kv, bi, gi, 0, 0)),
        out_shape=jax.ShapeDtypeStruct((2, b, g, n16, HEAD_DIM), F32),
        compiler_params=_cparams(("parallel", "parallel", "parallel")),
        name="nsa_compress",
    )(z3, w1, w2, posflat, cos_c, sin_c)


def _nsa_kernel(q_ref, ks_ref, vs_ref, kw_ref, vw_ref, kc_ref, vc_ref, gate_ref,
                cosf_ref, sinf_ref, cosq_ref, sinq_ref, ovl_ref, o_ref,
                ksr_ref, vst_ref, kw3_ref, vwt3_ref, vct_ref, os_ref, *, n_sel, scale):
    qi = pl.program_id(1)
    tq = q_ref.shape[1]
    s_len = ks_ref.shape[1]
    n_grp = ks_ref.shape[2] // HEAD_DIM
    rep = NSA_REP
    lanes = rep * tq
    n_chunk = s_len // tq
    pad_chunks = NSA_WINDOW // tq
    n_cmp = kc_ref.shape[3]
    grp = lambda gg: slice(gg * HEAD_DIM, (gg + 1) * HEAD_DIM)
    groups = range(n_grp)

    @pl.when(qi == 0)
    def _():
        for gg in groups:
            ksr_ref[gg] = _rope_full(ks_ref[0, :, grp(gg)], cosf_ref[...], sinf_ref[...]).astype(BF16)
            vst_ref[gg] = vs_ref[0, :, grp(gg)].T.astype(BF16)
            kw3_ref[gg, 0:pad_chunks] = jnp.zeros((pad_chunks, tq, HEAD_DIM), BF16)
            vwt3_ref[gg, 0:pad_chunks] = jnp.zeros((pad_chunks, HEAD_DIM, tq), BF16)
            kwr = _rope_full(kw_ref[0, :, grp(gg)], cosf_ref[...], sinf_ref[...]).astype(BF16)
            for c in range(n_chunk):
                kw3_ref[gg, pad_chunks + c] = kwr[c * tq:(c + 1) * tq]
                vwt3_ref[gg, pad_chunks + c] = vw_ref[0, c * tq:(c + 1) * tq, grp(gg)].T.astype(BF16)
            vct_ref[gg] = vc_ref[0, 0, gg].T.astype(BF16)

    cos_q = cosq_ref[...]
    sin_q = sinq_ref[...]
    lane = lax.broadcasted_iota(jnp.int32, (1, lanes), 1)
    t_loc = lane % tq
    t = qi * tq + t_loc
    cur = t[:, 0:tq] // NSA_SEL_BLOCK
    blk = lax.broadcasted_iota(jnp.int32, (n_sel, tq), 0)
    forced = (blk == 0) | (blk == cur) | (blk == cur - 1)
    visible = blk <= cur
    cmp_end = (lax.broadcasted_iota(jnp.int32, (n_cmp, lanes), 0) * NSA_CMP_STRIDE + (NSA_CMP_LEN - 1))
    ovl = ovl_ref[...]

    q_bf, o_c, imp = [], [], []
    for gg in groups:
        q_all = jnp.concatenate(
            [_rope_full(q_ref[0, :, grp(gg * rep + r)], cos_q, sin_q) for r in range(rep)],
            axis=0)
        q_bf.append((q_all * (scale * LOG2E)).astype(BF16))

        s_c = _dot3_nt(kc_ref[0, 0, gg], q_all)
        e_c, l_c = _softmax_t(s_c, cmp_end <= t, scale)
        p_c = e_c / l_c
        o_c.append(_dot_nn(vct_ref[gg], p_c.astype(BF16)))

        p_hi, p_lo = _split(p_c)
        imp3 = _dot_nn(ovl, p_hi) + _dot_nn(ovl, p_lo)
        imp_g = imp3[:, 0:tq]
        for r in range(1, rep):
            imp_g = imp_g + imp3[:, r * tq:(r + 1) * tq]
        imp_g = jnp.where(forced, FORCE_SCORE, imp_g)
        imp.append(jnp.where(visible, imp_g, -jnp.inf))

    unit = tq
    sb = NSA_SEL_BLOCK
    for v in range(s_len // unit):
        @pl.when(qi == v)
        def _(v=v):
            n_keys = (v + 1) * unit
            lo = n_keys - unit

            def causal(base, a, n):
                return lax.broadcasted_iota(jnp.int32, (n, lanes), 0) <= (t - (base + a))

            n_vis = n_keys // sb
            for gg in groups:
                s = _dot_nt(ksr_ref[gg, 0:n_keys], q_bf[gg])
                if n_vis > NSA_SEL_TOPK:
                    rank = _rank_t(imp[gg], n_vis)
                    rank3 = jnp.concatenate([rank] * rep, axis=1)
                    drop = jnp.where(rank3 < NSA_SEL_TOPK, 0.0, NEG_BIG)
                    parts = []
                    for j in range(n_vis):
                        parts.append((s[j * sb:(j + 1) * sb] + drop[j:j + 1, :],
                                      functools.partial(causal, j * sb) if j * sb >= lo else None))
                else:
                    parts = [(s[0:lo], None)] if lo > 0 else []
                    parts.append((s[lo:], functools.partial(causal, lo)))
                e_s, l_s = _softmax_parts(parts)
                os_ref[gg] = _dot_nn(vst_ref[gg, :, 0:n_keys], e_s) / l_s

    n_win = pad_chunks + 1
    def oldest(a, n):
        return lax.broadcasted_iota(jnp.int32, (n, lanes), 0) + a > t_loc

    def newest(a, n):
        return lax.broadcasted_iota(jnp.int32, (n, lanes), 0) + a <= t_loc

    for gg in groups:
        parts = []
        for c in range(n_win):
            s_w = _dot_nt(kw3_ref[gg, qi + c], q_bf[gg])
            if c < pad_chunks:
                s_w = s_w + jnp.where(qi + c >= pad_chunks, 0.0, NEG_BIG)
            parts.append((s_w, oldest if c == 0 else (newest if c == n_win - 1 else None)))
        e_w, l_w = _softmax_parts(parts)
        o_w = _dot_nn(vwt3_ref[gg, qi], e_w[0:tq])
        for c in range(1, n_win):
            o_w = o_w + _dot_nn(vwt3_ref[gg, qi + c], e_w[c * tq:(c + 1) * tq])
        o_w = o_w / l_w

        g_t = _sigmoid(gate_ref[0, :, grp(gg)].T)

        def gate_row(branch, g_t=g_t):
            return jnp.concatenate([g_t[3 * r + branch:3 * r + branch + 1, :] for r in range(rep)], axis=1)

        out_t = gate_row(0) * o_c[gg] + gate_row(1) * os_ref[gg] + gate_row(2) * o_w
        for r in range(rep):
            o_ref[0, :, grp(gg * rep + r)] = out_t[:, r * tq:(r + 1) * tq].T.astype(o_ref.dtype)


def _nsa(z3, cmp_kv, cos, sin_s, ovl_t, tq=256):
    b, s, _ = z3.shape
    g = NSA_KV_GROUPS
    n_sel = s // NSA_SEL_BLOCK
    n_cmp = cmp_kv.shape[3]
    qw = NSA_HEADS * HEAD_DIM
    kvw = g * HEAD_DIM
    n_chunk_pad = (s + NSA_WINDOW) // tq

    def col(off):
        return lambda bi, qi: (bi, 0, off // kvw)

    return pl.pallas_call(
        functools.partial(_nsa_kernel, n_sel=n_sel, scale=HEAD_DIM ** -0.5),
        grid=(b, s // tq),
        in_specs=[
            pl.BlockSpec((1, tq, qw), lambda bi, qi: (bi, qi, OFF_CQ // qw)),
            pl.BlockSpec((1, s, kvw), col(OFF_CKS)),
            pl.BlockSpec((1, s, kvw), col(OFF_CVS)),
            pl.BlockSpec((1, s, kvw), col(OFF_CKW)),
            pl.BlockSpec((1, s, kvw), col(OFF_CVW)),
            pl.BlockSpec((1, 1, g, n_cmp, HEAD_DIM), lambda bi, qi: (0, bi, 0, 0, 0)),
            pl.BlockSpec((1, 1, g, n_cmp, HEAD_DIM), lambda bi, qi: (1, bi, 0, 0, 0)),
            pl.BlockSpec((1, tq, g * LANES), lambda bi, qi: (bi, qi, OFF_CGATE // (g * LANES))),
            pl.BlockSpec((s, LANES), lambda bi, qi: (0, 0)),
            pl.BlockSpec((s, LANES), lambda bi, qi: (0, 0)),
            pl.BlockSpec((tq, LANES), lambda bi, qi: (qi, 0)),
            pl.BlockSpec((tq, LANES), lambda bi, qi: (qi, 0)),
            pl.BlockSpec((n_sel, n_cmp), lambda bi, qi: (0, 0)),
        ],
        out_specs=pl.BlockSpec((1, tq, qw), lambda bi, qi: (bi, qi, 0)),
        out_shape=jax.ShapeDtypeStruct((b, s, qw), BF16),
        scratch_shapes=[
            pltpu.VMEM((g, s, HEAD_DIM), BF16),
            pltpu.VMEM((g, HEAD_DIM, s), BF16),
            pltpu.VMEM((g, n_chunk_pad, tq, HEAD_DIM), BF16),
            pltpu.VMEM((g, n_chunk_pad, HEAD_DIM, tq), BF16),
            pltpu.VMEM((g, HEAD_DIM, n_cmp), BF16),
            pltpu.VMEM((g, HEAD_DIM, NSA_REP * tq), F32),
        ],
        compiler_params=_cparams(("parallel", "arbitrary")),
        name="nsa",
    )(z3, z3, z3, z3, z3, cmp_kv, cmp_kv, z3, cos, sin_s, cos, sin_s, ovl_t)


def _merge_kernel(ya_ref, yb_ref, yc_ref, ga_ref, gb_ref, gc_ref, wa_ref, wb_ref, wc_ref, o_ref,
                  wa_bf_ref, wb_bf_ref, wc_bf_ref):
    @pl.when(pl.program_id(1) == 0)
    def _():
        wa_bf_ref[...] = wa_ref[...].astype(BF16)
        wb_bf_ref[...] = wb_ref[...].astype(BF16)
        wc_bf_ref[...] = wc_ref[...].astype(BF16)

    m = ga_ref[...].astype(F32) * _dot_nn(ya_ref[...], wa_bf_ref[...])
    m = m + gb_ref[...].astype(F32) * _dot_nn(yb_ref[...], wb_bf_ref[...])
    m = m + gc_ref[...].astype(F32) * _dot_nn(yc_ref[...], wc_bf_ref[...])
    o_ref[...] = m.astype(o_ref.dtype)


def _merge(ya, yb, yc, gates, wa, wb, wc, layer, tm=1024, tn=1024):
    t = ya.shape[0]
    d = wa.shape[2]
    nj = d // tn

    def gate_spec(branch):
        return pl.BlockSpec((tm, tn), lambda j, i: (i, branch * nj + j))

    def y_spec(y):
        return pl.BlockSpec((tm, y.shape[1]), lambda j, i: (i, 0))

    def w_spec(w):
        return pl.BlockSpec((None, w.shape[1], tn), lambda j, i: (layer, 0, j), pipeline_mode=pl.Buffered(1))

    return pl.pallas_call(
        _merge_kernel,
        grid=(nj, t // tm),
        in_specs=[y_spec(ya), y_spec(yb), y_spec(yc), gate_spec(0), gate_spec(1), gate_spec(2),
                  w_spec(wa), w_spec(wb), w_spec(wc)],
        out_specs=pl.BlockSpec((tm, tn), lambda j, i: (i, j)),
        out_shape=jax.ShapeDtypeStruct((t, d), BF16),
        scratch_shapes=[pltpu.VMEM((w.shape[1], tn), BF16) for w in (wa, wb, wc)],
        compiler_params=_cparams(("parallel", "arbitrary")),
        name="merge",
    )(ya, yb, yc, gates, gates, gates, wa, wb, wc)


def _proj_residual_kernel(m_ref, w_ref, x_ref, o_ref, w_bf_ref):
    @pl.when(pl.program_id(1) == 0)
    def _():
        w_bf_ref[...] = w_ref[...].astype(BF16)

    o_ref[...] = x_ref[...] + _dot_nn(m_ref[...], w_bf_ref[...])


def _proj_residual(m, w, x2d, layer, tm=1024, tn=1024):
    t, k = m.shape
    d = w.shape[2]
    return pl.pallas_call(
        _proj_residual_kernel,
        grid=(d // tn, t // tm),
        in_specs=[
            pl.BlockSpec((tm, k), lambda j, i: (i, 0)),
            pl.BlockSpec((None, k, tn), lambda j, i: (layer, 0, j), pipeline_mode=pl.Buffered(1)),
            pl.BlockSpec((tm, tn), lambda j, i: (i, j)),
        ],
        out_specs=pl.BlockSpec((tm, tn), lambda j, i: (i, j)),
        out_shape=jax.ShapeDtypeStruct((t, d), F32),
        scratch_shapes=[pltpu.VMEM((k, tn), BF16)],
        compiler_params=_cparams(("parallel", "arbitrary")),
        name="out_proj",
    )(m, w, x2d)


def _rope_tables(pos, dim, width):
    inv = 1.0 / (ROPE_THETA ** (jnp.arange(0, dim, 2, dtype=F32) / dim))
    ang = pos.astype(F32)[:, None] * inv[None, :]
    cos = jnp.cos(ang)
    sin = jnp.sin(ang)
    cos = jnp.concatenate([cos, cos], axis=-1)
    sin_s = jnp.concatenate([-sin, sin], axis=-1)
    pad = ((0, 0), (0, width - dim))
    return jnp.pad(cos, pad), jnp.pad(sin_s, pad)


def _repack_kernel(srcp_ref, srcq_ref, start_ref, valid_ref, p_ref, q_ref, o_ref, *, n_src_cols):
    j = pl.program_id(1)
    start = start_ref[j]
    valid = valid_ref[j]
    row = lax.broadcasted_iota(jnp.int32, (LANES, LANES), 0)
    col = lax.broadcasted_iota(jnp.int32, (LANES, LANES), 1)
    lane = lax.broadcasted_iota(jnp.int32, (1, LANES), 1)

    def contribution(blk_ref, src_blk):
        take = jnp.where((src_blk * LANES + row == start + col) & (col < valid), 1.0, 0.0).astype(BF16)
        blk = jnp.where(src_blk * LANES + lane < n_src_cols, blk_ref[...], 0.0).astype(BF16)
        return _dot_nn(blk, take)

    o_ref[...] = (contribution(p_ref, srcp_ref[j]) + contribution(q_ref, srcq_ref[j])).astype(o_ref.dtype)


def _repack_tables(pieces, n_out_blocks, n_src_blocks):
    starts, valids = [], []
    for start, width in pieces:
        if width == 0:
            starts.append(0)
            valids.append(0)
        for off in range(0, width, LANES):
            starts.append(start + off)
            valids.append(min(LANES, width - off))
    starts += [0] * (n_out_blocks - len(starts))
    valids += [0] * (n_out_blocks - len(valids))
    assert n_src_blocks >= 2
    slots = [0, 1]
    srcp, srcq = [], []
    for start, valid in zip(starts, valids):
        need = []
        if valid > 0:
            need = list(range(start // LANES, (start + valid - 1) // LANES + 1))
        for blk in need:
            if blk not in slots:
                slots[0 if slots[0] not in need else 1] = blk
        srcp.append(slots[0])
        srcq.append(slots[1])
    return tuple(jnp.asarray(np.asarray(v, np.int32)) for v in (srcp, srcq, starts, valids))


def _repack(w, pieces, n_out_blocks):
    depth, k, n = w.shape
    tables = _repack_tables(pieces, n_out_blocks, -(-n // LANES))
    grid_spec = pltpu.PrefetchScalarGridSpec(
        num_scalar_prefetch=4,
        grid=(depth, n_out_blocks),
        in_specs=[
            pl.BlockSpec((None, k, LANES), lambda d, j, sp, sq, st, va: (d, 0, sp[j])),
            pl.BlockSpec((None, k, LANES), lambda d, j, sp, sq, st, va: (d, 0, sq[j])),
        ],
        out_specs=pl.BlockSpec((None, k, LANES), lambda d, j, sp, sq, st, va: (d, 0, j)),
    )
    return pl.pallas_call(
        functools.partial(_repack_kernel, n_src_cols=n),
        grid_spec=grid_spec,
        out_shape=jax.ShapeDtypeStruct((depth, k, n_out_blocks * LANES), BF16),
        compiler_params=_cparams(("parallel", "arbitrary")),
        name="repack",
    )(*tables, w, w)


def _repack_t_kernel(srcp_ref, srcq_ref, start_ref, valid_ref, p_ref, q_ref, o_ref, *, n_src_cols):
    j = pl.program_id(0)
    start = start_ref[j]
    valid = valid_ref[j]
    out_row = lax.broadcasted_iota(jnp.int32, (LANES, LANES), 0)
    src_row = lax.broadcasted_iota(jnp.int32, (LANES, LANES), 1)
    sub = lax.broadcasted_iota(jnp.int32, (LANES, 1), 0)

    def contribution(blk_ref, src_blk, layer):
        take = jnp.where((src_blk * LANES + src_row == start + out_row) & (out_row < valid), 1.0, 0.0).astype(BF16)
        blk = jnp.where(src_blk * LANES + sub < n_src_cols, blk_ref[:, layer, :], 0.0).astype(BF16)
        return _dot_nn(take, blk)

    for layer in range(o_ref.shape[0]):
        o_ref[layer] = (contribution(p_ref, srcp_ref[j], layer)
                        + contribution(q_ref, srcq_ref[j], layer)).astype(o_ref.dtype)


def _repack_t(wt, pieces, n_out_blocks):
    n, depth, k = wt.shape
    tables = _repack_tables(pieces, n_out_blocks, -(-n // LANES))
    grid_spec = pltpu.PrefetchScalarGridSpec(
        num_scalar_prefetch=4,
        grid=(n_out_blocks,),
        in_specs=[
            pl.BlockSpec((LANES, depth, k), lambda j, sp, sq, st, va: (sp[j], 0, 0)),
            pl.BlockSpec((LANES, depth, k), lambda j, sp, sq, st, va: (sq[j], 0, 0)),
        ],
        out_specs=pl.BlockSpec((depth, LANES, k), lambda j, sp, sq, st, va: (0, j, 0)),
    )
    return pl.pallas_call(
        functools.partial(_repack_t_kernel, n_src_cols=n),
        grid_spec=grid_spec,
        out_shape=jax.ShapeDtypeStruct((depth, n_out_blocks * LANES, k), BF16),
        compiler_params=_cparams(("arbitrary",)),
        name="repack_t",
    )(*tables, wt, wt)


def _pack_w_in(w):
    sizes = [MOBA_HEADS * HEAD_DIM] * 3 + [MLA_Q_RANK, MLA_KV_RANK, MLA_ROPE, NSA_HEADS * HEAD_DIM] \
        + [NSA_KV_GROUPS * HEAD_DIM] * 6
    offs = [int(v) for v in np.concatenate([[0], np.cumsum(sizes)])]
    (a_q, a_k, a_v, b_cq, b_ckv, b_kr, c_q, c_kc, c_vc, c_ks, c_vs, c_kw, c_vw) = [
        (offs[i], sizes[i]) for i in range(len(sizes))]
    gate0 = offs[-1]
    per_group = NSA_REP * 3
    gates = [(gate0 + gi * per_group, per_group) for gi in range(NSA_KV_GROUPS)]
    merge = (gate0 + NSA_HEADS * 3, w.shape[2] - gate0 - NSA_HEADS * 3)
    pieces = [c_q, b_ckv, b_cq, a_q, a_k, a_v, c_kc, c_vc, c_ks, c_vs, c_kw, c_vw] + gates + [b_kr]
    used = sum(-(-width // LANES) for _, width in pieces)
    pieces += [(0, 0)] * (Z_COLS // LANES - used) + [merge]
    return _repack_t(jnp.transpose(w, (2, 0, 1)), pieces, (Z_COLS + merge[1]) // LANES)


def _pack_w_uq(w):
    per_head = MLA_NOPE + MLA_ROPE
    pieces = [(h * per_head, MLA_NOPE) for h in range(MLA_HEADS)] \
        + [(h * per_head + MLA_NOPE, MLA_ROPE) for h in range(MLA_HEADS)]
    return _repack(w, pieces, 2 * MLA_HEADS)


def _pack_w_ukv(w):
    per_head = MLA_NOPE + MLA_V
    pieces = [(h * per_head, MLA_NOPE) for h in range(MLA_HEADS)] \
        + [(h * per_head + MLA_NOPE, MLA_V) for h in range(MLA_HEADS)]
    return _repack(w, pieces, 2 * MLA_HEADS)


def kernel(x, ffn1_norm, ffn1_w_gate, ffn1_w_up, ffn1_w_down, mix_norm, w_in, mla_q_norm, mla_w_uq, mla_kv_norm, mla_w_ukv, nsa_cmp_pos_k, nsa_cmp_w1_k, nsa_cmp_w2_k, nsa_cmp_pos_v, nsa_cmp_w1_v, nsa_cmp_w2_v, w_branch_moba, w_branch_mla, w_branch_nsa, w_out, ffn2_norm, ffn2_w_gate, ffn2_w_up, ffn2_w_down, final_norm):
    b, s, d = x.shape
    depth = w_in.shape[0]
    t = b * s
    assert NSA_CMP_LEN == 2 * NSA_CMP_STRIDE and s % MOBA_BLOCK == 0 and s % NSA_SEL_BLOCK == 0

    pos = jnp.arange(s)
    cos_h, sin_h = _rope_tables(pos, HEAD_DIM, LANES)
    cos_r, sin_r = _rope_tables(pos, MLA_ROPE, LANES)
    n16 = s // NSA_CMP_STRIDE
    cmp_end = np.arange(n16) * NSA_CMP_STRIDE + NSA_CMP_LEN - 1
    cos_c, sin_c = _rope_tables(jnp.asarray(cmp_end), HEAD_DIM, LANES)
    n_sel = s // NSA_SEL_BLOCK
    sel_start = np.arange(n_sel) * NSA_SEL_BLOCK
    cmp_start = np.arange(n16) * NSA_CMP_STRIDE
    ovl_t = ((cmp_start[None, :] <= sel_start[:, None] + NSA_SEL_BLOCK - 1)
             & (cmp_end[None, :] >= sel_start[:, None])).astype(np.float32)
    ovl_t = jnp.asarray(ovl_t, BF16)

    w_in_p = _pack_w_in(w_in)
    w_uq_p = _pack_w_uq(mla_w_uq)
    w_ukv_p = _pack_w_ukv(mla_w_ukv)

    xf = x.reshape(t, d)
    for i in range(depth):
        xf = _ffn(xf, ffn1_norm[i], ffn1_w_gate, ffn1_w_up, ffn1_w_down, i)

        z, gates = _in_proj(xf, mix_norm[i], w_in_p, i)
        z3 = z.reshape(b, s, Z_COLS)
        y_a = _moba(z3, cos_h, sin_h)
        y_b = _mla(z3, mla_q_norm[i], mla_kv_norm[i], w_uq_p, w_ukv_p, i, cos_r, sin_r)

        w1 =jnp.stack([nsa_cmp_w1_k[i], nsa_cmp_w1_v[i]])
        w2 = jnp.stack([nsa_cmp_w2_k[i], nsa_cmp_w2_v[i]])
        posflat = jnp.stack([nsa_cmp_pos_k[i].reshape(1, -1), nsa_cmp_pos_v[i].reshape(1, -1)])
        cmp_kv = _nsa_compress(z3, w1, w2, posflat, cos_c, sin_c)
        y_c = _nsa(z3, cmp_kv, cos_h, sin_h, ovl_t)

        merged = _merge(y_a.reshape(t, -1), y_b.reshape(t, -1), y_c.reshape(t, -1), gates,
                        w_branch_moba, w_branch_mla, w_branch_nsa, i)
        xf = _proj_residual(merged, w_out, xf, i)

        xf = _ffn(xf, ffn2_norm[i], ffn2_w_gate, ffn2_w_up, ffn2_w_down, i)
    return _rms(xf, final_norm, F32).reshape(b, s, d)
```

```python
import functools

import numpy as np
import jax
import jax.numpy as jnp
from jax import lax
from jax.experimental import pallas as pl
from jax.experimental.pallas import tpu as pltpu

F32 = jnp.float32
BF16 = jnp.bfloat16

HEAD_DIM = 128
ROPE_THETA = 10000.0
NORM_EPS = 1e-6
NEG_BIG = -1e30
TINY = 1e-30
LOG2E = 1.4426950408889634
N_BRANCHES = 3

MOBA_HEADS = 4
MOBA_BLOCK = 256
MOBA_TOPK = 3

MLA_HEADS = 6
MLA_Q_RANK = 512
MLA_KV_RANK = 256
MLA_NOPE = 128
MLA_ROPE = 64
MLA_V = 128

NSA_HEADS = 6
NSA_KV_GROUPS = 2
NSA_REP = NSA_HEADS // NSA_KV_GROUPS
NSA_CMP_LEN = 32
NSA_CMP_STRIDE = 16
NSA_CMP_HIDDEN = 256
NSA_SEL_BLOCK = 64
NSA_SEL_TOPK = 16
NSA_WINDOW = 512
FORCE_SCORE = 1e9

LANES = 128
SUBLANES = 8
SOFTMAX_CHUNK = 64
VMEM_LIMIT = 56 * 2**20

OFF_CQ = 0
OFF_BCKV = 768
OFF_BCQ = 1024
OFF_AQ = 1536
OFF_AK = 2048
OFF_AV = 2560
OFF_CKC = 3072
OFF_CVC = 3328
OFF_CKS = 3584
OFF_CVS = 3840
OFF_CKW = 4096
OFF_CVW = 4352
OFF_CGATE = 4608
OFF_BKR = 4864
Z_COLS = 5120


def _cparams(sem):
    return pltpu.CompilerParams(dimension_semantics=sem, vmem_limit_bytes=VMEM_LIMIT)


def _rmsnorm(x, g):
    ms = jnp.mean(x * x, axis=-1, keepdims=True)
    return x * lax.rsqrt(ms + NORM_EPS) * g


def _sigmoid(x):
    return 0.5 * jnp.tanh(0.5 * x) + 0.5


def _rope_full(x, cos, sin_signed):
    return x * cos + pltpu.roll(x, HEAD_DIM // 2, axis=1) * sin_signed


def _rope_half(x, cos, sin_signed):
    lane = lax.broadcasted_iota(jnp.int32, x.shape, 1)
    q = MLA_ROPE // 2
    swapped = jnp.where(lane < q, pltpu.roll(x, LANES - q, axis=1), pltpu.roll(x, q, axis=1))
    return x * cos + swapped * sin_signed


def _dot_nt(a, b):
    return lax.dot_general(a, b, (((1,), (1,)), ((), ())), preferred_element_type=F32)


def _dot_nn(a, b):
    return jnp.dot(a, b, preferred_element_type=F32)


def _split(a):
    hi = a.astype(BF16)
    lo = (a - hi.astype(F32)).astype(BF16)
    return hi, lo


def _dot3_nt(a, b):
    ah, al = _split(a)
    bh, bl = _split(b)
    return _dot_nt(ah, bh) + _dot_nt(ah, bl) + _dot_nt(al, bh)


def _dot3_nn(a, b):
    ah, al = _split(a)
    bh, bl = _split(b)
    return _dot_nn(ah, bh) + _dot_nn(ah, bl) + _dot_nn(al, bh)


def _softmax_t(s, mask, scale):
    s = jnp.where(mask, s * scale, NEG_BIG)
    m = jnp.max(s, axis=0, keepdims=True)
    e = jnp.where(mask, jnp.exp(s - m), 0.0)
    l = jnp.maximum(jnp.sum(e, axis=0, keepdims=True), TINY)
    return e, l


def _softmax_parts(parts, chunk=SOFTMAX_CHUNK):
    lanes = parts[0][0].shape[1]
    chunk = max(SUBLANES, chunk * 384 // max(lanes, 384))
    pieces = []
    for s, mask_fn in parts:
        for a in range(0, s.shape[0], chunk):
            n = min(chunk, s.shape[0] - a)
            p = s[a:a + n]
            pieces.append(p if mask_fn is None else jnp.where(mask_fn(a, n), p, NEG_BIG))

    def fold(x, op):
        return op(x.reshape(x.shape[0] // SUBLANES, SUBLANES, lanes), axis=0)

    m8 = functools.reduce(jnp.maximum, [fold(p, jnp.max) for p in pieces])
    mx = jnp.max(m8, axis=0, keepdims=True)
    es, l8 = [], None
    for p in pieces:
        e = jnp.exp2(p - mx)
        l8 = fold(e, jnp.sum) if l8 is None else l8 + fold(e, jnp.sum)
        es.append(e.astype(BF16))
    l = jnp.sum(l8, axis=0, keepdims=True)
    e = jnp.concatenate(es, axis=0) if len(es) > 1 else es[0]
    return e, jnp.maximum(l, TINY)


def _rank_t(v, nb):
    blk = lax.broadcasted_iota(jnp.int32, v.shape, 0)
    rank = jnp.zeros(v.shape, F32)
    for j in range(nb):
        vj = v[j:j + 1, :]
        ge = (vj >= v).astype(F32)
        gt = (vj > v).astype(F32)
        rank = rank + jnp.where(blk > j, ge, gt)
    return rank


def _rms_kernel(x_ref, g_ref, o_ref):
    o_ref[...] = _rmsnorm(x_ref[...], g_ref[...]).astype(o_ref.dtype)


def _rms_cast_kernel(x_ref, g_ref, w_ref, o_ref, w_bf_ref):
    o_ref[...] = _rmsnorm(x_ref[...], g_ref[...]).astype(o_ref.dtype)
    w_bf_ref[...] = w_ref[...].astype(w_bf_ref.dtype)


def _rms(x2d, g, out_dtype, tm=512, cast=None):
    t, d = x2d.shape
    steps = t // tm
    in_specs = [pl.BlockSpec((tm, d), lambda i: (i, 0)), pl.BlockSpec((1, d), lambda i: (0, 0))]
    out_specs = pl.BlockSpec((tm, d), lambda i: (i, 0))
    out_shape = jax.ShapeDtypeStruct((t, d), out_dtype)
    if cast is None:
        return pl.pallas_call(
            _rms_kernel, grid=(steps,), in_specs=in_specs, out_specs=out_specs, out_shape=out_shape,
            compiler_params=_cparams(("parallel",)), name="rmsnorm",
        )(x2d, g.reshape(1, d))
    w, layer = cast
    rows, cols = w.shape[1:]
    slab = rows // steps
    assert slab * steps == rows and slab % (2 * SUBLANES) == 0
    return pl.pallas_call(
        _rms_cast_kernel,
        grid=(steps,),
        in_specs=in_specs + [pl.BlockSpec((None, slab, cols), lambda i: (layer, i, 0))],
        out_specs=[out_specs, pl.BlockSpec((slab, cols), lambda i: (i, 0))],
        out_shape=[out_shape, jax.ShapeDtypeStruct((rows, cols), BF16)],
        compiler_params=_cparams(("parallel",)),
        name="rmsnorm_cast",
    )(x2d, g.reshape(1, d), w)


def _ffn_up_kernel(h_ref, wg_ref, wu_ref, o_ref, wg_bf_ref, wu_bf_ref):
    @pl.when(pl.program_id(1) == 0)
    def _():
        wg_bf_ref[...] = wg_ref[...].astype(BF16)
        wu_bf_ref[...] = wu_ref[...].astype(BF16)

    h = h_ref[...]
    a = _dot_nn(h, wg_bf_ref[...])
    u = _dot_nn(h, wu_bf_ref[...])
    o_ref[...] = (a * _sigmoid(a) * u).astype(o_ref.dtype)


def _ffn_down_kernel(a_ref, wd_ref, x_ref, o_ref):
    o_ref[...] = x_ref[...] + 0.5 * _dot_nn(a_ref[...], wd_ref[...])


def _ffn(x2d, g, wg, wu, wd, layer, tm=1024, tf=512, tm_down=512, tn=1024):
    t, d = x2d.shape
    f = wg.shape[2]
    h, wd_bf = _rms(x2d, g, BF16, cast=(wd, layer))
    act = pl.pallas_call(
        _ffn_up_kernel,
        grid=(f // tf, t // tm),
        in_specs=[
            pl.BlockSpec((tm, d), lambda j, i: (i, 0)),
            pl.BlockSpec((None, d, tf), lambda j, i: (layer, 0, j)),
            pl.BlockSpec((None, d, tf), lambda j, i: (layer, 0, j)),
        ],
        out_specs=pl.BlockSpec((tm, tf), lambda j, i: (i, j)),
        out_shape=jax.ShapeDtypeStruct((t, f), BF16),
        scratch_shapes=[pltpu.VMEM((d, tf), BF16), pltpu.VMEM((d, tf), BF16)],
        compiler_params=_cparams(("parallel", "arbitrary")),
        name="ffn_up",
    )(h, wg, wu)
    return pl.pallas_call(
        _ffn_down_kernel,
        grid=(d // tn, t // tm_down),
        in_specs=[
            pl.BlockSpec((tm_down, f), lambda n, i: (i, 0)),
            pl.BlockSpec((f, tn), lambda n, i: (0, n)),
            pl.BlockSpec((tm_down, tn), lambda n, i: (i, n)),
        ],
        out_specs=pl.BlockSpec((tm_down, tn), lambda n, i: (i, n)),
        out_shape=jax.ShapeDtypeStruct((t, d), F32),
        compiler_params=_cparams(("parallel", "parallel")),
        name="ffn_down",
    )(act, wd_bf, x2d)


def _in_proj_kernel(x_ref, g_ref, w_ref, z_ref, gate_ref, h_ref, *, n_z):
    j = pl.program_id(1)

    @pl.when(j == 0)
    def _():
        h_ref[...] = _rmsnorm(x_ref[...], g_ref[...]).astype(BF16)

    @pl.when(j < n_z)
    def _():
        z_ref[...] = _dot_nt(h_ref[...], w_ref[...])

    @pl.when(j >= n_z)
    def _():
        gate_ref[...] = _sigmoid(_dot_nt(h_ref[...], w_ref[...])).astype(gate_ref.dtype)


def _in_proj(x2d, g, w, layer, tm=1024, tn=1024):
    t, d = x2d.shape
    n = w.shape[1]
    n_z = Z_COLS // tn
    n_g = (n - Z_COLS) // tn
    return pl.pallas_call(
        functools.partial(_in_proj_kernel, n_z=n_z),
        grid=(t // tm, n_z + n_g),
        in_specs=[
            pl.BlockSpec((tm, d), lambda i, j: (i, 0)),
            pl.BlockSpec((1, d), lambda i, j: (0, 0)),
            pl.BlockSpec((None, tn, d), lambda i, j: (layer, j, 0)),
        ],
        out_specs=[
            pl.BlockSpec((tm, tn), lambda i, j: (i, jnp.minimum(j, n_z - 1))),
            pl.BlockSpec((tm, tn), lambda i, j: (i, jnp.maximum(j - n_z, 0))),
        ],
        out_shape=[jax.ShapeDtypeStruct((t, Z_COLS), F32), jax.ShapeDtypeStruct((t, n - Z_COLS), BF16)],
        scratch_shapes=[pltpu.VMEM((tm, d), BF16)],
        compiler_params=_cparams(("parallel", "arbitrary")),
        name="in_proj",
    )(x2d, g.reshape(1, d), w)


def _moba_kernel(q_ref, k_ref, v_ref, cosf_ref, sinf_ref, cosq_ref, sinq_ref, o_ref,
                 kr_ref, km_ref, vt_ref, *, n_blk, scale):
    qi = pl.program_id(2)
    tq = q_ref.shape[1]
    hp = q_ref.shape[2] // LANES
    head = lambda hh: slice(hh * LANES, (hh + 1) * LANES)

    @pl.when(qi == 0)
    def _():
        for hh in range(hp):
            kr = _rope_full(k_ref[0, :, head(hh)], cosf_ref[...], sinf_ref[...])
            kr_ref[hh] = kr.astype(BF16)
            km_ref[hh] = jnp.concatenate(
                [jnp.mean(kr[j * MOBA_BLOCK:(j + 1) * MOBA_BLOCK], axis=0, keepdims=True) for j in range(n_blk)],
                axis=0)
            vt_ref[hh] = v_ref[0, :, head(hh)].T.astype(BF16)

    def causal(a, n):
        return (lax.broadcasted_iota(jnp.int32, (n, tq), 0) + a
                <= lax.broadcasted_iota(jnp.int32, (n, tq), 1))

    for n in range(n_blk):
        @pl.when(qi == n)
        def _(n=n):
            for hh in range(hp):
                qr = _rope_full(q_ref[0, :, head(hh)], cosq_ref[...], sinq_ref[...])
                q_bf = (qr * (scale * LOG2E)).astype(BF16)
                s = _dot_nt(kr_ref[hh, 0:(n + 1) * tq], q_bf)
                parts = []
                if n > MOBA_TOPK:
                    gate = _dot3_nt(km_ref[hh], qr)
                    past = lax.broadcasted_iota(jnp.int32, (n_blk, tq), 0) < n
                    gate = jnp.where(past, gate, -jnp.inf)
                    drop = jnp.where(_rank_t(gate, n) < MOBA_TOPK, 0.0, NEG_BIG)
                    for j in range(n):
                        parts.append((s[j * tq:(j + 1) * tq] + drop[j:j + 1, :], None))
                elif n > 0:
                    parts.append((s[0:n * tq], None))
                parts.append((s[n * tq:], causal))
                e, l = _softmax_parts(parts)
                o_t = _dot_nn(vt_ref[hh, :, 0:(n + 1) * tq], e) / l
                o_ref[0, :, head(hh)] = o_t.T.astype(o_ref.dtype)


def _moba(z3, cos, sin_s, tq=MOBA_BLOCK, hp=MOBA_HEADS):
    b, s, _ = z3.shape
    n_blk = s // MOBA_BLOCK
    w = hp * LANES
    qb, kb, vb = OFF_AQ // w, OFF_AK // w, OFF_AV // w
    return pl.pallas_call(
        functools.partial(_moba_kernel, n_blk=n_blk, scale=HEAD_DIM ** -0.5),
        grid=(b, MOBA_HEADS // hp, s // tq),
        in_specs=[
            pl.BlockSpec((1, tq, w), lambda bi, h, qi: (bi, qi, qb + h)),
            pl.BlockSpec((1, s, w), lambda bi, h, qi: (bi, 0, kb + h)),
            pl.BlockSpec((1, s, w), lambda bi, h, qi: (bi, 0, vb + h)),
            pl.BlockSpec((s, LANES), lambda bi, h, qi: (0, 0)),
            pl.BlockSpec((s, LANES), lambda bi, h, qi: (0, 0)),
            pl.BlockSpec((tq, LANES), lambda bi, h, qi: (qi, 0)),
            pl.BlockSpec((tq, LANES), lambda bi, h, qi: (qi, 0)),
        ],
        out_specs=pl.BlockSpec((1, tq, w), lambda bi, h, qi: (bi, qi, h)),
        out_shape=jax.ShapeDtypeStruct((b, s, MOBA_HEADS * HEAD_DIM), BF16),
        scratch_shapes=[pltpu.VMEM((hp, s, LANES), BF16), pltpu.VMEM((hp, n_blk, LANES), F32),
                        pltpu.VMEM((hp, LANES, s), BF16)],
        compiler_params=_cparams(("parallel", "parallel", "arbitrary")),
        name="moba",
    )(z3, z3, z3, cos, sin_s, cos, sin_s)


def _mla_kernel(cq_ref, ckv_ref, kr_ref, qg_ref, kvg_ref, wqn_ref, wqr_ref, wkn_ref, wv_ref,
                cosf_ref, sinf_ref, cosq_ref, sinq_ref, o_ref, kf_ref, vt_ref, *, scale):
    qi = pl.program_id(2)
    tq = cq_ref.shape[1]
    s_len = ckv_ref.shape[1]
    hp = wqn_ref.shape[1] // LANES
    head = lambda hh: slice(hh * LANES, (hh + 1) * LANES)

    @pl.when(qi == 0)
    def _():
        ckv = _rmsnorm(ckv_ref[0], kvg_ref[...]).astype(BF16)
        k_nope = _dot_nn(ckv, wkn_ref[...])
        v = _dot_nn(ckv, wv_ref[...])
        k_rope = _rope_half(kr_ref[0], cosf_ref[...], sinf_ref[...]).astype(BF16)
        for hh in range(hp):
            kf_ref[hh, :, :LANES] = k_nope[:, head(hh)].astype(BF16)
            kf_ref[hh, :, LANES:] = k_rope
            vt_ref[hh] = v[:, head(hh)].T.astype(BF16)

    cq = _rmsnorm(cq_ref[0], qg_ref[...]).astype(BF16)
    q_nope = _dot_nn(cq, wqn_ref[...])
    q_rope = _dot_nn(cq, wqr_ref[...])
    def causal(a, n):
        return (lax.broadcasted_iota(jnp.int32, (n, tq), 0) + a
                <= lax.broadcasted_iota(jnp.int32, (n, tq), 1))

    for n in range(s_len // tq):
        @pl.when(qi == n)
        def _(n=n):
            for hh in range(hp):
                q = jnp.concatenate([q_nope[:, head(hh)],
                                     _rope_half(q_rope[:, head(hh)], cosq_ref[...], sinq_ref[...])], axis=1)
                q_bf = (q * (scale * LOG2E)).astype(BF16)
                s = _dot_nt(kf_ref[hh, 0:(n + 1) * tq], q_bf)
                parts = [(s[0:n * tq], None)] if n > 0 else []
                parts.append((s[n * tq:], causal))
                e, l = _softmax_parts(parts)
                o_t = _dot_nn(vt_ref[hh, :, 0:(n + 1) * tq], e) / l
                o_ref[0, :, head(hh)] = o_t.T.astype(o_ref.dtype)


def _mla(z3, q_gain, kv_gain, w_uq, w_ukv, layer, cos, sin_s, tq=512, hp=MLA_HEADS):
    b, s, _ = z3.shape
    w = hp * LANES
    ng = MLA_HEADS // hp
    return pl.pallas_call(
        functools.partial(_mla_kernel, scale=(MLA_NOPE + MLA_ROPE) ** -0.5),
        grid=(b, ng, s // tq),
        in_specs=[
            pl.BlockSpec((1, tq, MLA_Q_RANK), lambda bi, h, qi: (bi, qi, OFF_BCQ // MLA_Q_RANK)),
            pl.BlockSpec((1, s, MLA_KV_RANK), lambda bi, h, qi: (bi, 0, OFF_BCKV // MLA_KV_RANK)),
            pl.BlockSpec((1, s, LANES), lambda bi, h, qi: (bi, 0, OFF_BKR // LANES)),
            pl.BlockSpec((1, MLA_Q_RANK), lambda bi, h, qi: (0, 0)),
            pl.BlockSpec((1, MLA_KV_RANK), lambda bi, h, qi: (0, 0)),
            pl.BlockSpec((None, MLA_Q_RANK, w), lambda bi, h, qi: (layer, 0, h)),
            pl.BlockSpec((None, MLA_Q_RANK, w), lambda bi, h, qi: (layer, 0, ng + h)),
            pl.BlockSpec((None, MLA_KV_RANK, w), lambda bi, h, qi: (layer, 0, h)),
            pl.BlockSpec((None, MLA_KV_RANK, w), lambda bi, h, qi: (layer, 0, ng + h)),
            pl.BlockSpec((s, LANES), lambda bi, h, qi: (0, 0)),
            pl.BlockSpec((s, LANES), lambda bi, h, qi: (0, 0)),
            pl.BlockSpec((tq, LANES), lambda bi, h, qi: (qi, 0)),
            pl.BlockSpec((tq, LANES), lambda bi, h, qi: (qi, 0)),
        ],
        out_specs=pl.BlockSpec((1, tq, w), lambda bi, h, qi: (bi, qi, h)),
        out_shape=jax.ShapeDtypeStruct((b, s, MLA_HEADS * MLA_V), BF16),
        scratch_shapes=[pltpu.VMEM((hp, s, 2 * LANES), BF16), pltpu.VMEM((hp, LANES, s), BF16)],
        compiler_params=_cparams(("parallel", "parallel", "arbitrary")),
        name="mla",
    )(z3, z3, z3, q_gain.reshape(1, -1), kv_gain.reshape(1, -1), w_uq, w_uq, w_ukv, w_ukv,
      cos, sin_s, cos, sin_s)


def _cmp_kernel(x_ref, w1_ref, w2_ref, pos_ref, cos_ref, sin_ref, o_ref):
    kv = pl.program_id(0)
    n16 = o_ref.shape[3]
    x = jnp.concatenate(
        [x_ref[0, pl.ds(l, n16, stride=NSA_CMP_STRIDE), :] for l in range(NSA_CMP_STRIDE)], axis=1)
    half = x.shape[1]
    w1 = w1_ref[0]
    first = _dot3_nn(x, w1[:half])
    second = _dot3_nn(x, w1[half:])
    bias = _dot3_nn(jnp.broadcast_to(pos_ref[0], (8, 2 * half)), w1)[0:1]
    hid = first + pltpu.roll(second, n16 - 1, axis=0) + bias
    y = _dot3_nn(jax.nn.gelu(hid), w2_ref[0])
    roped = _rope_full(y, cos_ref[...], sin_ref[...])
    o_ref[0, 0, 0] = jnp.where(kv == 0, roped, y)


def _nsa_compress(z3, w1, w2, posflat, cos_c, sin_c):
    b, s, _ = z3.shape
    g = NSA_KV_GROUPS
    n16 = s // NSA_CMP_STRIDE
    width = NSA_CMP_STRIDE * HEAD_DIM
    hid = w1.shape[2]
    base = OFF_CKC // LANES
    return pl.pallas_call(
        _cmp_kernel,
        grid=(2, b, g),
        in_specs=[
            pl.BlockSpec((1, s, LANES), lambda kv, bi, gi: (bi, 0, base + g * kv + gi)),
            pl.BlockSpec((1, 2 * width, hid), lambda kv, bi, gi: (kv, 0, 0)),
            pl.BlockSpec((1, hid, HEAD_DIM), lambda kv, bi, gi: (kv, 0, 0)),
            pl.BlockSpec((1, 1, 2 * width), lambda kv, bi, gi: (kv, 0, 0)),
            pl.BlockSpec((n16, HEAD_DIM), lambda kv, bi, gi: (0, 0)),
            pl.BlockSpec((n16, HEAD_DIM), lambda kv, bi, gi: (0, 0)),
        ],
        out_specs=pl.BlockSpec((1, 1, 1, n16, HEAD_DIM), lambda kv, bi, gi: (kv, bi, gi, 0, 0)),
        out_shape=jax.ShapeDtypeStruct((2, b, g, n16, HEAD_DIM), F32),
        compiler_params=_cparams(("parallel", "parallel", "parallel")),
        name="nsa_compress",
    )(z3, w1, w2, posflat, cos_c, sin_c)


def _nsa_kernel(q_ref, ks_ref, vs_ref, kw_ref, vw_ref, kc_ref, vc_ref, gate_ref,
                cosf_ref, sinf_ref, cosq_ref, sinq_ref, ovl_ref, o_ref,
                ksr_ref, vst_ref, kw3_ref, vwt3_ref, vct_ref, os_ref, *, n_sel, scale):
    qi = pl.program_id(1)
    tq = q_ref.shape[1]
    s_len = ks_ref.shape[1]
    n_grp = ks_ref.shape[2] // HEAD_DIM
    rep = NSA_REP
    lanes = rep * tq
    n_chunk = s_len // tq
    pad_chunks = NSA_WINDOW // tq
    n_cmp = kc_ref.shape[3]
    grp = lambda gg: slice(gg * HEAD_DIM, (gg + 1) * HEAD_DIM)
    groups = range(n_grp)

    @pl.when(qi == 0)
    def _():
        for gg in groups:
            ksr_ref[gg] = _rope_full(ks_ref[0, :, grp(gg)], cosf_ref[...], sinf_ref[...]).astype(BF16)
            vst_ref[gg] = vs_ref[0, :, grp(gg)].T.astype(BF16)
            kw3_ref[gg, 0:pad_chunks] = jnp.zeros((pad_chunks, tq, HEAD_DIM), BF16)
            vwt3_ref[gg, 0:pad_chunks] = jnp.zeros((pad_chunks, HEAD_DIM, tq), BF16)
            kwr = _rope_full(kw_ref[0, :, grp(gg)], cosf_ref[...], sinf_ref[...]).astype(BF16)
            for c in range(n_chunk):
                kw3_ref[gg, pad_chunks + c] = kwr[c * tq:(c + 1) * tq]
                vwt3_ref[gg, pad_chunks + c] = vw_ref[0, c * tq:(c + 1) * tq, grp(gg)].T.astype(BF16)
            vct_ref[gg] = vc_ref[0, 0, gg].T.astype(BF16)

    cos_q = cosq_ref[...]
    sin_q = sinq_ref[...]
    lane = lax.broadcasted_iota(jnp.int32, (1, lanes), 1)
    t_loc = lane % tq
    t = qi * tq + t_loc
    cur = t[:, 0:tq] // NSA_SEL_BLOCK
    blk = lax.broadcasted_iota(jnp.int32, (n_sel, tq), 0)
    forced = (blk == 0) | (blk == cur) | (blk == cur - 1)
    visible = blk <= cur
    cmp_end = (lax.broadcasted_iota(jnp.int32, (n_cmp, lanes), 0) * NSA_CMP_STRIDE + (NSA_CMP_LEN - 1))
    ovl = ovl_ref[...]

    q_bf, o_c, imp = [], [], []
    for gg in groups:
        q_all = jnp.concatenate(
            [_rope_full(q_ref[0, :, grp(gg * rep + r)], cos_q, sin_q) for r in range(rep)],
            axis=0)
        q_bf.append((q_all * (scale * LOG2E)).astype(BF16))

        s_c = _dot3_nt(kc_ref[0, 0, gg], q_all)
        e_c, l_c = _softmax_t(s_c, cmp_end <= t, scale)
        p_c = e_c / l_c
        o_c.append(_dot_nn(vct_ref[gg], p_c.astype(BF16)))

        p_hi, p_lo = _split(p_c)
        imp3 = _dot_nn(ovl, p_hi) + _dot_nn(ovl, p_lo)
        imp_g = imp3[:, 0:tq]
        for r in range(1, rep):
            imp_g = imp_g + imp3[:, r * tq:(r + 1) * tq]
        imp_g = jnp.where(forced, FORCE_SCORE, imp_g)
        imp.append(jnp.where(visible, imp_g, -jnp.inf))

    unit = tq
    sb = NSA_SEL_BLOCK
    for v in range(s_len // unit):
        @pl.when(qi == v)
        def _(v=v):
            n_keys = (v + 1) * unit
            lo = n_keys - unit

            def causal(base, a, n):
                return lax.broadcasted_iota(jnp.int32, (n, lanes), 0) <= (t - (base + a))

            n_vis = n_keys // sb
            for gg in groups:
                s = _dot_nt(ksr_ref[gg, 0:n_keys], q_bf[gg])
                if n_vis > NSA_SEL_TOPK:
                    rank = _rank_t(imp[gg], n_vis)
                    rank3 = jnp.concatenate([rank] * rep, axis=1)
                    drop = jnp.where(rank3 < NSA_SEL_TOPK, 0.0, NEG_BIG)
                    parts = []
                    for j in range(n_vis):
                        parts.append((s[j * sb:(j + 1) * sb] + drop[j:j + 1, :],
                                      functools.partial(causal, j * sb) if j * sb >= lo else None))
                else:
                    parts = [(s[0:lo], None)] if lo > 0 else []
                    parts.append((s[lo:], functools.partial(causal, lo)))
                e_s, l_s = _softmax_parts(parts)
                os_ref[gg] = _dot_nn(vst_ref[gg, :, 0:n_keys], e_s) / l_s

    n_win = pad_chunks + 1
    def oldest(a, n):
        return lax.broadcasted_iota(jnp.int32, (n, lanes), 0) + a > t_loc

    def newest(a, n):
        return lax.broadcasted_iota(jnp.int32, (n, lanes), 0) + a <= t_loc

    for gg in groups:
        parts = []
        for c in range(n_win):
            s_w = _dot_nt(kw3_ref[gg, qi + c], q_bf[gg])
            if c < pad_chunks:
                s_w = s_w + jnp.where(qi + c >= pad_chunks, 0.0, NEG_BIG)
            parts.append((s_w, oldest if c == 0 else (newest if c == n_win - 1 else None)))
        e_w, l_w = _softmax_parts(parts)
        o_w = _dot_nn(vwt3_ref[gg, qi], e_w[0:tq])
        for c in range(1, n_win):
            o_w = o_w + _dot_nn(vwt3_ref[gg, qi + c], e_w[c * tq:(c + 1) * tq])
        o_w = o_w / l_w

        g_t = _sigmoid(gate_ref[0, :, grp(gg)].T)

        def gate_row(branch, g_t=g_t):
            return jnp.concatenate([g_t[3 * r + branch:3 * r + branch + 1, :] for r in range(rep)], axis=1)

        out_t = gate_row(0) * o_c[gg] + gate_row(1) * os_ref[gg] + gate_row(2) * o_w
        for r in range(rep):
            o_ref[0, :, grp(gg * rep + r)] = out_t[:, r * tq:(r + 1) * tq].T.astype(o_ref.dtype)


def _nsa(z3, cmp_kv, cos, sin_s, ovl_t, tq=256):
    b, s, _ = z3.shape
    g = NSA_KV_GROUPS
    n_sel = s // NSA_SEL_BLOCK
    n_cmp = cmp_kv.shape[3]
    qw = NSA_HEADS * HEAD_DIM
    kvw = g * HEAD_DIM
    n_chunk_pad = (s + NSA_WINDOW) // tq

    def col(off):
        return lambda bi, qi: (bi, 0, off // kvw)

    return pl.pallas_call(
        functools.partial(_nsa_kernel, n_sel=n_sel, scale=HEAD_DIM ** -0.5),
        grid=(b, s // tq),
        in_specs=[
            pl.BlockSpec((1, tq, qw), lambda bi, qi: (bi, qi, OFF_CQ // qw)),
            pl.BlockSpec((1, s, kvw), col(OFF_CKS)),
            pl.BlockSpec((1, s, kvw), col(OFF_CVS)),
            pl.BlockSpec((1, s, kvw), col(OFF_CKW)),
            pl.BlockSpec((1, s, kvw), col(OFF_CVW)),
            pl.BlockSpec((1, 1, g, n_cmp, HEAD_DIM), lambda bi, qi: (0, bi, 0, 0, 0)),
            pl.BlockSpec((1, 1, g, n_cmp, HEAD_DIM), lambda bi, qi: (1, bi, 0, 0, 0)),
            pl.BlockSpec((1, tq, g * LANES), lambda bi, qi: (bi, qi, OFF_CGATE // (g * LANES))),
            pl.BlockSpec((s, LANES), lambda bi, qi: (0, 0)),
            pl.BlockSpec((s, LANES), lambda bi, qi: (0, 0)),
            pl.BlockSpec((tq, LANES), lambda bi, qi: (qi, 0)),
            pl.BlockSpec((tq, LANES), lambda bi, qi: (qi, 0)),
            pl.BlockSpec((n_sel, n_cmp), lambda bi, qi: (0, 0)),
        ],
        out_specs=pl.BlockSpec((1, tq, qw), lambda bi, qi: (bi, qi, 0)),
        out_shape=jax.ShapeDtypeStruct((b, s, qw), BF16),
        scratch_shapes=[
            pltpu.VMEM((g, s, HEAD_DIM), BF16),
            pltpu.VMEM((g, HEAD_DIM, s), BF16),
            pltpu.VMEM((g, n_chunk_pad, tq, HEAD_DIM), BF16),
            pltpu.VMEM((g, n_chunk_pad, HEAD_DIM, tq), BF16),
            pltpu.VMEM((g, HEAD_DIM, n_cmp), BF16),
            pltpu.VMEM((g, HEAD_DIM, NSA_REP * tq), F32),
        ],
        compiler_params=_cparams(("parallel", "arbitrary")),
        name="nsa",
    )(z3, z3, z3, z3, z3, cmp_kv, cmp_kv, z3, cos, sin_s, cos, sin_s, ovl_t)


def _merge_kernel(ya_ref, yb_ref, yc_ref, ga_ref, gb_ref, gc_ref, wa_ref, wb_ref, wc_ref, o_ref,
                  wa_bf_ref, wb_bf_ref, wc_bf_ref):
    @pl.when(pl.program_id(1) == 0)
    def _():
        wa_bf_ref[...] = wa_ref[...].astype(BF16)
        wb_bf_ref[...] = wb_ref[...].astype(BF16)
        wc_bf_ref[...] = wc_ref[...].astype(BF16)

    m = ga_ref[...].astype(F32) * _dot_nn(ya_ref[...], wa_bf_ref[...])
    m = m + gb_ref[...].astype(F32) * _dot_nn(yb_ref[...], wb_bf_ref[...])
    m = m + gc_ref[...].astype(F32) * _dot_nn(yc_ref[...], wc_bf_ref[...])
    o_ref[...] = m.astype(o_ref.dtype)


def _merge(ya, yb, yc, gates, wa, wb, wc, layer, tm=1024, tn=1024):
    t = ya.shape[0]
    d = wa.shape[2]
    nj = d // tn

    def gate_spec(branch):
        return pl.BlockSpec((tm, tn), lambda j, i: (i, branch * nj + j))

    def y_spec(y):
        return pl.BlockSpec((tm, y.shape[1]), lambda j, i: (i, 0))

    def w_spec(w):
        return pl.BlockSpec((None, w.shape[1], tn), lambda j, i: (layer, 0, j), pipeline_mode=pl.Buffered(1))

    return pl.pallas_call(
        _merge_kernel,
        grid=(nj, t // tm),
        in_specs=[y_spec(ya), y_spec(yb), y_spec(yc), gate_spec(0), gate_spec(1), gate_spec(2),
                  w_spec(wa), w_spec(wb), w_spec(wc)],
        out_specs=pl.BlockSpec((tm, tn), lambda j, i: (i, j)),
        out_shape=jax.ShapeDtypeStruct((t, d), BF16),
        scratch_shapes=[pltpu.VMEM((w.shape[1], tn), BF16) for w in (wa, wb, wc)],
        compiler_params=_cparams(("parallel", "arbitrary")),
        name="merge",
    )(ya, yb, yc, gates, gates, gates, wa, wb, wc)


def _proj_residual_kernel(m_ref, w_ref, x_ref, o_ref, w_bf_ref):
    @pl.when(pl.program_id(1) == 0)
    def _():
        w_bf_ref[...] = w_ref[...].astype(BF16)

    o_ref[...] = x_ref[...] + _dot_nn(m_ref[...], w_bf_ref[...])


def _proj_residual(m, w, x2d, layer, tm=1024, tn=1024):
    t, k = m.shape
    d = w.shape[2]
    return pl.pallas_call(
        _proj_residual_kernel,
        grid=(d // tn, t // tm),
        in_specs=[
            pl.BlockSpec((tm, k), lambda j, i: (i, 0)),
            pl.BlockSpec((None, k, tn), lambda j, i: (layer, 0, j), pipeline_mode=pl.Buffered(1)),
            pl.BlockSpec((tm, tn), lambda j, i: (i, j)),
        ],
        out_specs=pl.BlockSpec((tm, tn), lambda j, i: (i, j)),
        out_shape=jax.ShapeDtypeStruct((t, d), F32),
        scratch_shapes=[pltpu.VMEM((k, tn), BF16)],
        compiler_params=_cparams(("parallel", "arbitrary")),
        name="out_proj",
    )(m, w, x2d)


def _rope_tables(pos, dim, width):
    inv = 1.0 / (ROPE_THETA ** (jnp.arange(0, dim, 2, dtype=F32) / dim))
    ang = pos.astype(F32)[:, None] * inv[None, :]
    cos = jnp.cos(ang)
    sin = jnp.sin(ang)
    cos = jnp.concatenate([cos, cos], axis=-1)
    sin_s = jnp.concatenate([-sin, sin], axis=-1)
    pad = ((0, 0), (0, width - dim))
    return jnp.pad(cos, pad), jnp.pad(sin_s, pad)


def _repack_kernel(srcp_ref, srcq_ref, start_ref, valid_ref, p_ref, q_ref, o_ref, *, n_src_cols):
    j = pl.program_id(1)
    start = start_ref[j]
    valid = valid_ref[j]
    row = lax.broadcasted_iota(jnp.int32, (LANES, LANES), 0)
    col = lax.broadcasted_iota(jnp.int32, (LANES, LANES), 1)
    lane = lax.broadcasted_iota(jnp.int32, (1, LANES), 1)

    def contribution(blk_ref, src_blk):
        take = jnp.where((src_blk * LANES + row == start + col) & (col < valid), 1.0, 0.0).astype(BF16)
        blk = jnp.where(src_blk * LANES + lane < n_src_cols, blk_ref[...], 0.0).astype(BF16)
        return _dot_nn(blk, take)

    o_ref[...] = (contribution(p_ref, srcp_ref[j]) + contribution(q_ref, srcq_ref[j])).astype(o_ref.dtype)


def _repack_tables(pieces, n_out_blocks, n_src_blocks):
    starts, valids = [], []
    for start, width in pieces:
        if width == 0:
            starts.append(0)
            valids.append(0)
        for off in range(0, width, LANES):
            starts.append(start + off)
            valids.append(min(LANES, width - off))
    starts += [0] * (n_out_blocks - len(starts))
    valids += [0] * (n_out_blocks - len(valids))
    assert n_src_blocks >= 2
    slots = [0, 1]
    srcp, srcq = [], []
    for start, valid in zip(starts, valids):
        need = []
        if valid > 0:
            need = list(range(start // LANES, (start + valid - 1) // LANES + 1))
        for blk in need:
            if blk not in slots:
                slots[0 if slots[0] not in need else 1] = blk
        srcp.append(slots[0])
        srcq.append(slots[1])
    return tuple(jnp.asarray(np.asarray(v, np.int32)) for v in (srcp, srcq, starts, valids))


def _repack(w, pieces, n_out_blocks):
    depth, k, n = w.shape
    tables = _repack_tables(pieces, n_out_blocks, -(-n // LANES))
    grid_spec = pltpu.PrefetchScalarGridSpec(
        num_scalar_prefetch=4,
        grid=(depth, n_out_blocks),
        in_specs=[
            pl.BlockSpec((None, k, LANES), lambda d, j, sp, sq, st, va: (d, 0, sp[j])),
            pl.BlockSpec((None, k, LANES), lambda d, j, sp, sq, st, va: (d, 0, sq[j])),
        ],
        out_specs=pl.BlockSpec((None, k, LANES), lambda d, j, sp, sq, st, va: (d, 0, j)),
    )
    return pl.pallas_call(
        functools.partial(_repack_kernel, n_src_cols=n),
        grid_spec=grid_spec,
        out_shape=jax.ShapeDtypeStruct((depth, k, n_out_blocks * LANES), BF16),
        compiler_params=_cparams(("parallel", "arbitrary")),
        name="repack",
    )(*tables, w, w)


def _repack_t_kernel(srcp_ref, srcq_ref, start_ref, valid_ref, p_ref, q_ref, o_ref, *, n_src_cols):
    j = pl.program_id(0)
    start = start_ref[j]
    valid = valid_ref[j]
    out_row = lax.broadcasted_iota(jnp.int32, (LANES, LANES), 0)
    src_row = lax.broadcasted_iota(jnp.int32, (LANES, LANES), 1)
    sub = lax.broadcasted_iota(jnp.int32, (LANES, 1), 0)

    def contribution(blk_ref, src_blk, layer):
        take = jnp.where((src_blk * LANES + src_row == start + out_row) & (out_row < valid), 1.0, 0.0).astype(BF16)
        blk = jnp.where(src_blk * LANES + sub < n_src_cols, blk_ref[:, layer, :], 0.0).astype(BF16)
        return _dot_nn(take, blk)

    for layer in range(o_ref.shape[0]):
        o_ref[layer] = (contribution(p_ref, srcp_ref[j], layer)
                        + contribution(q_ref, srcq_ref[j], layer)).astype(o_ref.dtype)


def _repack_t(wt, pieces, n_out_blocks):
    n, depth, k = wt.shape
    tables = _repack_tables(pieces, n_out_blocks, -(-n // LANES))
    grid_spec = pltpu.PrefetchScalarGridSpec(
        num_scalar_prefetch=4,
        grid=(n_out_blocks,),
        in_specs=[
            pl.BlockSpec((LANES, depth, k), lambda j, sp, sq, st, va: (sp[j], 0, 0)),
            pl.BlockSpec((LANES, depth, k), lambda j, sp, sq, st, va: (sq[j], 0, 0)),
        ],
        out_specs=pl.BlockSpec((depth, LANES, k), lambda j, sp, sq, st, va: (0, j, 0)),
    )
    return pl.pallas_call(
        functools.partial(_repack_t_kernel, n_src_cols=n),
        grid_spec=grid_spec,
        out_shape=jax.ShapeDtypeStruct((depth, n_out_blocks * LANES, k), BF16),
        compiler_params=_cparams(("arbitrary",)),
        name="repack_t",
    )(*tables, wt, wt)


def _pack_w_in(w):
    sizes = [MOBA_HEADS * HEAD_DIM] * 3 + [MLA_Q_RANK, MLA_KV_RANK, MLA_ROPE, NSA_HEADS * HEAD_DIM] \
        + [NSA_KV_GROUPS * HEAD_DIM] * 6
    offs = [int(v) for v in np.concatenate([[0], np.cumsum(sizes)])]
    (a_q, a_k, a_v, b_cq, b_ckv, b_kr, c_q, c_kc, c_vc, c_ks, c_vs, c_kw, c_vw) = [
        (offs[i], sizes[i]) for i in range(len(sizes))]
    gate0 = offs[-1]
    per_group = NSA_REP * 3
    gates = [(gate0 + gi * per_group, per_group) for gi in range(NSA_KV_GROUPS)]
    merge = (gate0 + NSA_HEADS * 3, w.shape[2] - gate0 - NSA_HEADS * 3)
    pieces = [c_q, b_ckv, b_cq, a_q, a_k, a_v, c_kc, c_vc, c_ks, c_vs, c_kw, c_vw] + gates + [b_kr]
    used = sum(-(-width // LANES) for _, width in pieces)
    pieces += [(0, 0)] * (Z_COLS // LANES - used) + [merge]
    return _repack_t(jnp.transpose(w, (2, 0, 1)), pieces, (Z_COLS + merge[1]) // LANES)


def _pack_w_uq(w):
    per_head = MLA_NOPE + MLA_ROPE
    pieces = [(h * per_head, MLA_NOPE) for h in range(MLA_HEADS)] \
        + [(h * per_head + MLA_NOPE, MLA_ROPE) for h in range(MLA_HEADS)]
    return _repack(w, pieces, 2 * MLA_HEADS)


def _pack_w_ukv(w):
    per_head = MLA_NOPE + MLA_V
    pieces = [(h * per_head, MLA_NOPE) for h in range(MLA_HEADS)] \
        + [(h * per_head + MLA_NOPE, MLA_V) for h in range(MLA_HEADS)]
    return _repack(w, pieces, 2 * MLA_HEADS)


def kernel(x, ffn1_norm, ffn1_w_gate, ffn1_w_up, ffn1_w_down, mix_norm, w_in, mla_q_norm, mla_w_uq, mla_kv_norm, mla_w_ukv, nsa_cmp_pos_k, nsa_cmp_w1_k, nsa_cmp_w2_k, nsa_cmp_pos_v, nsa_cmp_w1_v, nsa_cmp_w2_v, w_branch_moba, w_branch_mla, w_branch_nsa, w_out, ffn2_norm, ffn2_w_gate, ffn2_w_up, ffn2_w_down, final_norm):
    b, s, d = x.shape
    depth = w_in.shape[0]
    t = b * s
    assert NSA_CMP_LEN == 2 * NSA_CMP_STRIDE and s % MOBA_BLOCK == 0 and s % NSA_SEL_BLOCK == 0

    pos = jnp.arange(s)
    cos_h, sin_h = _rope_tables(pos, HEAD_DIM, LANES)
    cos_r, sin_r = _rope_tables(pos, MLA_ROPE, LANES)
    n16 = s // NSA_CMP_STRIDE
    cmp_end = np.arange(n16) * NSA_CMP_STRIDE + NSA_CMP_LEN - 1
    cos_c, sin_c = _rope_tables(jnp.asarray(cmp_end), HEAD_DIM, LANES)
    n_sel = s // NSA_SEL_BLOCK
    sel_start = np.arange(n_sel) * NSA_SEL_BLOCK
    cmp_start = np.arange(n16) * NSA_CMP_STRIDE
    ovl_t = ((cmp_start[None, :] <= sel_start[:, None] + NSA_SEL_BLOCK - 1)
             & (cmp_end[None, :] >= sel_start[:, None])).astype(np.float32)
    ovl_t = jnp.asarray(ovl_t, BF16)

    w_in_p = _pack_w_in(w_in)
    w_uq_p = _pack_w_uq(mla_w_uq)
    w_ukv_p = _pack_w_ukv(mla_w_ukv)

    xf = x.reshape(t, d)
    for i in range(depth):
        xf = _ffn(xf, ffn1_norm[i], ffn1_w_gate, ffn1_w_up, ffn1_w_down, i)

        z, gates = _in_proj(xf, mix_norm[i], w_in_p, i)
        z3 = z.reshape(b, s, Z_COLS)
        y_a = _moba(z3, cos_h, sin_h)
        y_b = _mla(z3, mla_q_norm[i], mla_kv_norm[i], w_uq_p, w_ukv_p, i, cos_r, sin_r)

        w1 =jnp.stack([nsa_cmp_w1_k[i], nsa_cmp_w1_v[i]])
        w2 = jnp.stack([nsa_cmp_w2_k[i], nsa_cmp_w2_v[i]])
        posflat = jnp.stack([nsa_cmp_pos_k[i].reshape(1, -1), nsa_cmp_pos_v[i].reshape(1, -1)])
        cmp_kv = _nsa_compress(z3, w1, w2, posflat, cos_c, sin_c)
        y_c = _nsa(z3, cmp_kv, cos_h, sin_h, ovl_t)

        merged = _merge(y_a.reshape(t, -1), y_b.reshape(t, -1), y_c.reshape(t, -1), gates,
                        w_branch_moba, w_branch_mla, w_branch_nsa, i)
        xf = _proj_residual(merged, w_out, xf, i)

        xf = _ffn(xf, ffn2_norm[i], ffn2_w_gate, ffn2_w_up, ffn2_w_down, i)
    return _rms(xf, final_norm, F32).reshape(b, s, d)
```

```python
import functools

import numpy as np
import jax
import jax.numpy as jnp
from jax import lax
from jax.experimental import pallas as pl
from jax.experimental.pallas import tpu as pltpu

F32 = jnp.float32
BF16 = jnp.bfloat16

HEAD_DIM = 128
ROPE_THETA = 10000.0
NORM_EPS = 1e-6
NEG_BIG = -1e30
TINY = 1e-30
LOG2E = 1.4426950408889634
N_BRANCHES = 3

MOBA_HEADS = 4
MOBA_BLOCK = 256
MOBA_TOPK = 3

MLA_HEADS = 6
MLA_Q_RANK = 512
MLA_KV_RANK = 256
MLA_NOPE = 128
MLA_ROPE = 64
MLA_V = 128

NSA_HEADS = 6
NSA_KV_GROUPS = 2
NSA_REP = NSA_HEADS // NSA_KV_GROUPS
NSA_CMP_LEN = 32
NSA_CMP_STRIDE = 16
NSA_CMP_HIDDEN = 256
NSA_SEL_BLOCK = 64
NSA_SEL_TOPK = 16
NSA_WINDOW = 512
FORCE_SCORE = 1e9

LANES = 128
SUBLANES = 8
SOFTMAX_CHUNK = 64
VMEM_LIMIT = 56 * 2**20

OFF_CQ = 0
OFF_BCKV = 768
OFF_BCQ = 1024
OFF_AQ = 1536
OFF_AK = 2048
OFF_AV = 2560
OFF_CKC = 3072
OFF_CVC = 3328
OFF_CKS = 3584
OFF_CVS = 3840
OFF_CKW = 4096
OFF_CVW = 4352
OFF_CGATE = 4608
OFF_BKR = 4864
Z_COLS = 5120


def _cparams(sem):
    return pltpu.CompilerParams(dimension_semantics=sem, vmem_limit_bytes=VMEM_LIMIT)


def _rmsnorm(x, g):
    ms = jnp.mean(x * x, axis=-1, keepdims=True)
    return x * lax.rsqrt(ms + NORM_EPS) * g


def _sigmoid(x):
    return 0.5 * jnp.tanh(0.5 * x) + 0.5


def _rope_full(x, cos, sin_signed):
    return x * cos + pltpu.roll(x, HEAD_DIM // 2, axis=1) * sin_signed


def _rope_half(x, cos, sin_signed):
    lane = lax.broadcasted_iota(jnp.int32, x.shape, 1)
    q = MLA_ROPE // 2
    swapped = jnp.where(lane < q, pltpu.roll(x, LANES - q, axis=1), pltpu.roll(x, q, axis=1))
    return x * cos + swapped * sin_signed


def _dot_nt(a, b):
    return lax.dot_general(a, b, (((1,), (1,)), ((), ())), preferred_element_type=F32)


def _dot_nn(a, b):
    return jnp.dot(a, b, preferred_element_type=F32)


def _split(a):
    hi = a.astype(BF16)
    lo = (a - hi.astype(F32)).astype(BF16)
    return hi, lo


def _dot3_nt(a, b):
    ah, al = _split(a)
    bh, bl = _split(b)
    return _dot_nt(ah, bh) + _dot_nt(ah, bl) + _dot_nt(al, bh)


def _dot3_nn(a, b):
    ah, al = _split(a)
    bh, bl = _split(b)
    return _dot_nn(ah, bh) + _dot_nn(ah, bl) + _dot_nn(al, bh)


def _softmax_t(s, mask, scale):
    s = jnp.where(mask, s * scale, NEG_BIG)
    m = jnp.max(s, axis=0, keepdims=True)
    e = jnp.where(mask, jnp.exp(s - m), 0.0)
    l = jnp.maximum(jnp.sum(e, axis=0, keepdims=True), TINY)
    return e, l


def _softmax_parts(parts, chunk=SOFTMAX_CHUNK):
    lanes = parts[0][0].shape[1]
    chunk = max(SUBLANES, chunk * 384 // max(lanes, 384))
    pieces = []
    for s, mask_fn in parts:
        for a in range(0, s.shape[0], chunk):
            n = min(chunk, s.shape[0] - a)
            p = s[a:a + n]
            pieces.append(p if mask_fn is None else jnp.where(mask_fn(a, n), p, NEG_BIG))

    def fold(x, op):
        return op(x.reshape(x.shape[0] // SUBLANES, SUBLANES, lanes), axis=0)

    m8 = functools.reduce(jnp.maximum, [fold(p, jnp.max) for p in pieces])
    mx = jnp.max(m8, axis=0, keepdims=True)
    es, l8 = [], None
    for p in pieces:
        e = jnp.exp2(p - mx)
        l8 = fold(e, jnp.sum) if l8 is None else l8 + fold(e, jnp.sum)
        es.append(e.astype(BF16))
    l = jnp.sum(l8, axis=0, keepdims=True)
    e = jnp.concatenate(es, axis=0) if len(es) > 1 else es[0]
    return e, jnp.maximum(l, TINY)


def _rank_t(v, nb):
    blk = lax.broadcasted_iota(jnp.int32, v.shape, 0)
    rank = jnp.zeros(v.shape, F32)
    for j in range(nb):
        vj = v[j:j + 1, :]
        ge = (vj >= v).astype(F32)
        gt = (vj > v).astype(F32)
        rank = rank + jnp.where(blk > j, ge, gt)
    return rank


def _rms_kernel(x_ref, g_ref, o_ref):
    o_ref[...] = _rmsnorm(x_ref[...], g_ref[...]).astype(o_ref.dtype)


def _rms_cast_kernel(x_ref, g_ref, w_ref, o_ref, w_bf_ref):
    o_ref[...] = _rmsnorm(x_ref[...], g_ref[...]).astype(o_ref.dtype)
    w_bf_ref[...] = w_ref[...].astype(w_bf_ref.dtype)


def _rms(x2d, g, out_dtype, tm=512, cast=None):
    t, d = x2d.shape
    steps = t // tm
    in_specs = [pl.BlockSpec((tm, d), lambda i: (i, 0)), pl.BlockSpec((1, d), lambda i: (0, 0))]
    out_specs = pl.BlockSpec((tm, d), lambda i: (i, 0))
    out_shape = jax.ShapeDtypeStruct((t, d), out_dtype)
    if cast is None:
        return pl.pallas_call(
            _rms_kernel, grid=(steps,), in_specs=in_specs, out_specs=out_specs, out_shape=out_shape,
            compiler_params=_cparams(("parallel",)), name="rmsnorm",
        )(x2d, g.reshape(1, d))
    w, layer = cast
    rows, cols = w.shape[1:]
    slab = rows // steps
    assert slab * steps == rows and slab % (2 * SUBLANES) == 0
    return pl.pallas_call(
        _rms_cast_kernel,
        grid=(steps,),
        in_specs=in_specs + [pl.BlockSpec((None, slab, cols), lambda i: (layer, i, 0))],
        out_specs=[out_specs, pl.BlockSpec((slab, cols), lambda i: (i, 0))],
        out_shape=[out_shape, jax.ShapeDtypeStruct((rows, cols), BF16)],
        compiler_params=_cparams(("parallel",)),
        name="rmsnorm_cast",
    )(x2d, g.reshape(1, d), w)


def _ffn_up_kernel(h_ref, wg_ref, wu_ref, o_ref, wg_bf_ref, wu_bf_ref):
    @pl.when(pl.program_id(1) == 0)
    def _():
        wg_bf_ref[...] = wg_ref[...].astype(BF16)
        wu_bf_ref[...] = wu_ref[...].astype(BF16)

    h = h_ref[...]
    a = _dot_nn(h, wg_bf_ref[...])
    u = _dot_nn(h, wu_bf_ref[...])
    o_ref[...] = (a * _sigmoid(a) * u).astype(o_ref.dtype)


def _ffn_down_kernel(a_ref, wd_ref, x_ref, o_ref):
    o_ref[...] = x_ref[...] + 0.5 * _dot_nn(a_ref[...], wd_ref[...])


def _ffn(x2d, g, wg, wu, wd, layer, tm=1024, tf=512, tm_down=512, tn=1024):
    t, d = x2d.shape
    f = wg.shape[2]
    h, wd_bf = _rms(x2d, g, BF16, cast=(wd, layer))
    act = pl.pallas_call(
        _ffn_up_kernel,
        grid=(f // tf, t // tm),
        in_specs=[
            pl.BlockSpec((tm, d), lambda j, i: (i, 0)),
            pl.BlockSpec((None, d, tf), lambda j, i: (layer, 0, j)),
            pl.BlockSpec((None, d, tf), lambda j, i: (layer, 0, j)),
        ],
        out_specs=pl.BlockSpec((tm, tf), lambda j, i: (i, j)),
        out_shape=jax.ShapeDtypeStruct((t, f), BF16),
        scratch_shapes=[pltpu.VMEM((d, tf), BF16), pltpu.VMEM((d, tf), BF16)],
        compiler_params=_cparams(("parallel", "arbitrary")),
        name="ffn_up",
    )(h, wg, wu)
    return pl.pallas_call(
        _ffn_down_kernel,
        grid=(d // tn, t // tm_down),
        in_specs=[
            pl.BlockSpec((tm_down, f), lambda n, i: (i, 0)),
            pl.BlockSpec((f, tn), lambda n, i: (0, n)),
            pl.BlockSpec((tm_down, tn), lambda n, i: (i, n)),
        ],
        out_specs=pl.BlockSpec((tm_down, tn), lambda n, i: (i, n)),
        out_shape=jax.ShapeDtypeStruct((t, d), F32),
        compiler_params=_cparams(("parallel", "parallel")),
        name="ffn_down",
    )(act, wd_bf, x2d)


def _in_proj_kernel(x_ref, g_ref, w_ref, z_ref, gate_ref, h_ref, *, n_z):
    j = pl.program_id(1)

    @pl.when(j == 0)
    def _():
        h_ref[...] = _rmsnorm(x_ref[...], g_ref[...]).astype(BF16)

    @pl.when(j < n_z)
    def _():
        z_ref[...] = _dot_nt(h_ref[...], w_ref[...])

    @pl.when(j >= n_z)
    def _():
        gate_ref[...] = _sigmoid(_dot_nt(h_ref[...], w_ref[...])).astype(gate_ref.dtype)


def _in_proj(x2d, g, w, layer, tm=1024, tn=1024):
    t, d = x2d.shape
    n = w.shape[1]
    n_z = Z_COLS // tn
    n_g = (n - Z_COLS) // tn
    return pl.pallas_call(
        functools.partial(_in_proj_kernel, n_z=n_z),
        grid=(t // tm, n_z + n_g),
        in_specs=[
            pl.BlockSpec((tm, d), lambda i, j: (i, 0)),
            pl.BlockSpec((1, d), lambda i, j: (0, 0)),
            pl.BlockSpec((None, tn, d), lambda i, j: (layer, j, 0)),
        ],
        out_specs=[
            pl.BlockSpec((tm, tn), lambda i, j: (i, jnp.minimum(j, n_z - 1))),
            pl.BlockSpec((tm, tn), lambda i, j: (i, jnp.maximum(j - n_z, 0))),
        ],
        out_shape=[jax.ShapeDtypeStruct((t, Z_COLS), F32), jax.ShapeDtypeStruct((t, n - Z_COLS), BF16)],
        scratch_shapes=[pltpu.VMEM((tm, d), BF16)],
        compiler_params=_cparams(("parallel", "arbitrary")),
        name="in_proj",
    )(x2d, g.reshape(1, d), w)


def _moba_kernel(q_ref, k_ref, v_ref, cosf_ref, sinf_ref, cosq_ref, sinq_ref, o_ref,
                 kr_ref, km_ref, vt_ref, *, n_blk, scale):
    qi = pl.program_id(2)
    tq = q_ref.shape[1]
    hp = q_ref.shape[2] // LANES
    head = lambda hh: slice(hh * LANES, (hh + 1) * LANES)

    @pl.when(qi == 0)
    def _():
        for hh in range(hp):
            kr = _rope_full(k_ref[0, :, head(hh)], cosf_ref[...], sinf_ref[...])
            kr_ref[hh] = kr.astype(BF16)
            km_ref[hh] = jnp.concatenate(
                [jnp.mean(kr[j * MOBA_BLOCK:(j + 1) * MOBA_BLOCK], axis=0, keepdims=True) for j in range(n_blk)],
                axis=0)
            vt_ref[hh] = v_ref[0, :, head(hh)].T.astype(BF16)

    blk = MOBA_BLOCK
    bpt = tq // blk
    lane = lax.broadcasted_iota(jnp.int32, (1, tq), 1)
    lane_sub = lane // blk
    lane_loc = lane % blk

    for n in range(n_blk // bpt):
        @pl.when(qi == n)
        def _(n=n):
            first = n * bpt
            ranked = first + bpt - 1 > MOBA_TOPK
            for hh in range(hp):
                qr = _rope_full(q_ref[0, :, head(hh)], cosq_ref[...], sinq_ref[...])
                q_bf = (qr * (scale * LOG2E)).astype(BF16)
                s = _dot_nt(kr_ref[hh, 0:(n + 1) * tq], q_bf)
                drop = None
                if ranked:
                    gate = _dot3_nt(km_ref[hh], qr)
                    past = lax.broadcasted_iota(jnp.int32, (n_blk, tq), 0) < first + lane_sub
                    gate = jnp.where(past, gate, -jnp.inf)
                    drop = jnp.where(_rank_t(gate, first + bpt - 1) < MOBA_TOPK, 0.0, NEG_BIG)
                parts = []
                if drop is not None:
                    for j in range(first):
                        parts.append((s[j * blk:(j + 1) * blk] + drop[j:j + 1, :], None))
                elif first > 0:
                    parts.append((s[0:first * blk], None))
                for i2 in range(bpt):
                    j = first + i2
                    s_j = s[j * blk:(j + 1) * blk]
                    if drop is not None and i2 < bpt - 1:
                        s_j = s_j + jnp.where(lane_sub > i2, drop[j:j + 1, :], 0.0)

                    limit = jnp.where(lane_sub > i2, blk, jnp.where(lane_sub == i2, lane_loc, -1))

                    def mask_fn(a, rows, limit=limit):
                        return lax.broadcasted_iota(jnp.int32, (rows, tq), 0) + a <= limit

                    parts.append((s_j, mask_fn))
                e, l = _softmax_parts(parts)
                o_t = _dot_nn(vt_ref[hh, :, 0:(n + 1) * tq], e) / l
                o_ref[0, :, head(hh)] = o_t.T.astype(o_ref.dtype)


def _moba(z3, cos, sin_s, tq=2 * MOBA_BLOCK, hp=MOBA_HEADS):
    b, s, _ = z3.shape
    n_blk = s // MOBA_BLOCK
    w = hp * LANES
    qb, kb, vb = OFF_AQ // w, OFF_AK // w, OFF_AV // w
    return pl.pallas_call(
        functools.partial(_moba_kernel, n_blk=n_blk, scale=HEAD_DIM ** -0.5),
        grid=(b, MOBA_HEADS // hp, s // tq),
        in_specs=[
            pl.BlockSpec((1, tq, w), lambda bi, h, qi: (bi, qi, qb + h)),
            pl.BlockSpec((1, s, w), lambda bi, h, qi: (bi, 0, kb + h)),
            pl.BlockSpec((1, s, w), lambda bi, h, qi: (bi, 0, vb + h)),
            pl.BlockSpec((s, LANES), lambda bi, h, qi: (0, 0)),
            pl.BlockSpec((s, LANES), lambda bi, h, qi: (0, 0)),
            pl.BlockSpec((tq, LANES), lambda bi, h, qi: (qi, 0)),
            pl.BlockSpec((tq, LANES), lambda bi, h, qi: (qi, 0)),
        ],
        out_specs=pl.BlockSpec((1, tq, w), lambda bi, h, qi: (bi, qi, h)),
        out_shape=jax.ShapeDtypeStruct((b, s, MOBA_HEADS * HEAD_DIM), BF16),
        scratch_shapes=[pltpu.VMEM((hp, s, LANES), BF16), pltpu.VMEM((hp, n_blk, LANES), F32),
                        pltpu.VMEM((hp, LANES, s), BF16)],
        compiler_params=_cparams(("parallel", "parallel", "arbitrary")),
        name="moba",
    )(z3, z3, z3, cos, sin_s, cos, sin_s)


def _mla_kernel(cq_ref, ckv_ref, kr_ref, qg_ref, kvg_ref, wqn_ref, wqr_ref, wkn_ref, wv_ref,
                cosf_ref, sinf_ref, cosq_ref, sinq_ref, o_ref, kf_ref, vt_ref, *, scale):
    qi = pl.program_id(2)
    tq = cq_ref.shape[1]
    s_len = ckv_ref.shape[1]
    hp = wqn_ref.shape[1] // LANES
    head = lambda hh: slice(hh * LANES, (hh + 1) * LANES)

    @pl.when(qi == 0)
    def _():
        ckv = _rmsnorm(ckv_ref[0], kvg_ref[...]).astype(BF16)
        k_nope = _dot_nn(ckv, wkn_ref[...])
        v = _dot_nn(ckv, wv_ref[...])
        k_rope = _rope_half(kr_ref[0], cosf_ref[...], sinf_ref[...]).astype(BF16)
        for hh in range(hp):
            kf_ref[hh, :, :LANES] = k_nope[:, head(hh)].astype(BF16)
            kf_ref[hh, :, LANES:] = k_rope
            vt_ref[hh] = v[:, head(hh)].T.astype(BF16)

    cq = _rmsnorm(cq_ref[0], qg_ref[...]).astype(BF16)
    q_nope = _dot_nn(cq, wqn_ref[...])
    q_rope = _dot_nn(cq, wqr_ref[...])
    def causal(a, n):
        return (lax.broadcasted_iota(jnp.int32, (n, tq), 0) + a
                <= lax.broadcasted_iota(jnp.int32, (n, tq), 1))

    for n in range(s_len // tq):
        @pl.when(qi == n)
        def _(n=n):
            for hh in range(hp):
                q = jnp.concatenate([q_nope[:, head(hh)],
                                     _rope_half(q_rope[:, head(hh)], cosq_ref[...], sinq_ref[...])], axis=1)
                q_bf = (q * (scale * LOG2E)).astype(BF16)
                s = _dot_nt(kf_ref[hh, 0:(n + 1) * tq], q_bf)
                parts = [(s[0:n * tq], None)] if n > 0 else []
                parts.append((s[n * tq:], causal))
                e, l = _softmax_parts(parts)
                o_t = _dot_nn(vt_ref[hh, :, 0:(n + 1) * tq], e) / l
                o_ref[0, :, head(hh)] = o_t.T.astype(o_ref.dtype)


def _mla(z3, q_gain, kv_gain, w_uq, w_ukv, layer, cos, sin_s, tq=512, hp=MLA_HEADS):
    b, s, _ = z3.shape
    w = hp * LANES
    ng = MLA_HEADS // hp
    return pl.pallas_call(
        functools.partial(_mla_kernel, scale=(MLA_NOPE + MLA_ROPE) ** -0.5),
        grid=(b, ng, s // tq),
        in_specs=[
            pl.BlockSpec((1, tq, MLA_Q_RANK), lambda bi, h, qi: (bi, qi, OFF_BCQ // MLA_Q_RANK)),
            pl.BlockSpec((1, s, MLA_KV_RANK), lambda bi, h, qi: (bi, 0, OFF_BCKV // MLA_KV_RANK)),
            pl.BlockSpec((1, s, LANES), lambda bi, h, qi: (bi, 0, OFF_BKR // LANES)),
            pl.BlockSpec((1, MLA_Q_RANK), lambda bi, h, qi: (0, 0)),
            pl.BlockSpec((1, MLA_KV_RANK), lambda bi, h, qi: (0, 0)),
            pl.BlockSpec((None, MLA_Q_RANK, w), lambda bi, h, qi: (layer, 0, h)),
            pl.BlockSpec((None, MLA_Q_RANK, w), lambda bi, h, qi: (layer, 0, ng + h)),
            pl.BlockSpec((None, MLA_KV_RANK, w), lambda bi, h, qi: (layer, 0, h)),
            pl.BlockSpec((None, MLA_KV_RANK, w), lambda bi, h, qi: (layer, 0, ng + h)),
            pl.BlockSpec((s, LANES), lambda bi, h, qi: (0, 0)),
            pl.BlockSpec((s, LANES), lambda bi, h, qi: (0, 0)),
            pl.BlockSpec((tq, LANES), lambda bi, h, qi: (qi, 0)),
            pl.BlockSpec((tq, LANES), lambda bi, h, qi: (qi, 0)),
        ],
        out_specs=pl.BlockSpec((1, tq, w), lambda bi, h, qi: (bi, qi, h)),
        out_shape=jax.ShapeDtypeStruct((b, s, MLA_HEADS * MLA_V), BF16),
        scratch_shapes=[pltpu.VMEM((hp, s, 2 * LANES), BF16), pltpu.VMEM((hp, LANES, s), BF16)],
        compiler_params=_cparams(("parallel", "parallel", "arbitrary")),
        name="mla",
    )(z3, z3, z3, q_gain.reshape(1, -1), kv_gain.reshape(1, -1), w_uq, w_uq, w_ukv, w_ukv,
      cos, sin_s, cos, sin_s)


def _dot3_presplit(a, b_hi, b_lo):
    ah, al = _split(a)
    return _dot_nn(ah, b_hi) + _dot_nn(ah, b_lo) + _dot_nn(al, b_hi)


def _cmp_kernel(*refs):
    n_grp = len(refs) - 9
    x_refs = refs[:n_grp]
    w1_ref, w2_ref, pos_ref, cos_ref, sin_ref, o_ref, w1_hi_ref, w1_lo_ref, bias_ref = refs[n_grp:]
    kv = pl.program_id(0)
    n16 = o_ref.shape[3]
    half = NSA_CMP_STRIDE * HEAD_DIM

    @pl.when(pl.program_id(1) == 0)
    def _():
        hi, lo = _split(w1_ref[0])
        w1_hi_ref[...] = hi
        w1_lo_ref[...] = lo
        bias_ref[...] = _dot3_presplit(jnp.broadcast_to(pos_ref[0], (SUBLANES, 2 * half)), hi, lo)

    for gi, x_ref in enumerate(x_refs):
        x = jnp.concatenate(
            [x_ref[0, pl.ds(l, n16, stride=NSA_CMP_STRIDE), :] for l in range(NSA_CMP_STRIDE)], axis=1)
        first = _dot3_presplit(x, w1_hi_ref[:half], w1_lo_ref[:half])
        second = _dot3_presplit(x, w1_hi_ref[half:], w1_lo_ref[half:])
        hid = first + pltpu.roll(second, n16 - 1, axis=0) + bias_ref[0:1]
        y = _dot3_nn(jax.nn.gelu(hid), w2_ref[0])
        roped = _rope_full(y, cos_ref[...], sin_ref[...])
        o_ref[0, 0, gi] = jnp.where(kv == 0, roped, y)


def _nsa_compress(z3, w1, w2, posflat, cos_c, sin_c):
    b, s, _ = z3.shape
    g = NSA_KV_GROUPS
    n16 = s // NSA_CMP_STRIDE
    width = NSA_CMP_STRIDE * HEAD_DIM
    hid = w1.shape[2]
    base = OFF_CKC // LANES
    x_specs = [pl.BlockSpec((1, s, LANES), lambda kv, bi, gi=gi: (bi, 0, base + g * kv + gi)) for gi in range(g)]
    return pl.pallas_call(
        _cmp_kernel,
        grid=(2, b),
        in_specs=x_specs + [
            pl.BlockSpec((1, 2 * width, hid), lambda kv, bi: (kv, 0, 0)),
            pl.BlockSpec((1, hid, HEAD_DIM), lambda kv, bi: (kv, 0, 0)),
            pl.BlockSpec((1, 1, 2 * width), lambda kv, bi: (kv, 0, 0)),
            pl.BlockSpec((n16, HEAD_DIM), lambda kv, bi: (0, 0)),
            pl.BlockSpec((n16, HEAD_DIM), lambda kv, bi: (0, 0)),
        ],
        out_specs=pl.BlockSpec((1, 1, g, n16, HEAD_DIM), lambda kv, bi: (kv, bi, 0, 0, 0)),
        out_shape=jax.ShapeDtypeStruct((2, b, g, n16, HEAD_DIM), F32),
        scratch_shapes=[pltpu.VMEM((2 * width, hid), BF16), pltpu.VMEM((2 * width, hid), BF16),
                        pltpu.VMEM((SUBLANES, hid), F32)],
        compiler_params=_cparams(("parallel", "arbitrary")),
        name="nsa_compress",
    )(*([z3] * g), w1, w2, posflat, cos_c, sin_c)


def _nsa_kernel(q_ref, ks_ref, vs_ref, kw_ref, vw_ref, kc_ref, vc_ref, gate_ref,
                cosf_ref, sinf_ref, cosq_ref, sinq_ref, ovl_ref, o_ref,
                ksr_ref, vst_ref, kw3_ref, vwt3_ref, vct_ref, os_ref, *, n_sel, scale):
    qi = pl.program_id(1)
    tq = q_ref.shape[1]
    s_len = ks_ref.shape[1]
    n_grp = ks_ref.shape[2] // HEAD_DIM
    rep = NSA_REP
    lanes = rep * tq
    n_chunk = s_len // tq
    pad_chunks = NSA_WINDOW // tq
    n_cmp = kc_ref.shape[3]
    grp = lambda gg: slice(gg * HEAD_DIM, (gg + 1) * HEAD_DIM)
    groups = range(n_grp)

    @pl.when(qi == 0)
    def _():
        for gg in groups:
            ksr_ref[gg] = _rope_full(ks_ref[0, :, grp(gg)], cosf_ref[...], sinf_ref[...]).astype(BF16)
            vst_ref[gg] = vs_ref[0, :, grp(gg)].T.astype(BF16)
            kw3_ref[gg, 0:pad_chunks] = jnp.zeros((pad_chunks, tq, HEAD_DIM), BF16)
            vwt3_ref[gg, 0:pad_chunks] = jnp.zeros((pad_chunks, HEAD_DIM, tq), BF16)
            kwr = _rope_full(kw_ref[0, :, grp(gg)], cosf_ref[...], sinf_ref[...]).astype(BF16)
            for c in range(n_chunk):
                kw3_ref[gg, pad_chunks + c] = kwr[c * tq:(c + 1) * tq]
                vwt3_ref[gg, pad_chunks + c] = vw_ref[0, c * tq:(c + 1) * tq, grp(gg)].T.astype(BF16)
            vct_ref[gg] = vc_ref[0, 0, gg].T.astype(BF16)

    cos_q = cosq_ref[...]
    sin_q = sinq_ref[...]
    lane = lax.broadcasted_iota(jnp.int32, (1, lanes), 1)
    t_loc = lane % tq
    t = qi * tq + t_loc
    cur = t[:, 0:tq] // NSA_SEL_BLOCK
    blk = lax.broadcasted_iota(jnp.int32, (n_sel, tq), 0)
    forced = (blk == 0) | (blk == cur) | (blk == cur - 1)
    visible = blk <= cur
    cmp_end = (lax.broadcasted_iota(jnp.int32, (n_cmp, lanes), 0) * NSA_CMP_STRIDE + (NSA_CMP_LEN - 1))
    ovl = ovl_ref[...]

    q_bf, o_c, imp = [], [], []
    for gg in groups:
        q_all = jnp.concatenate(
            [_rope_full(q_ref[0, :, grp(gg * rep + r)], cos_q, sin_q) for r in range(rep)],
            axis=0)
        q_bf.append((q_all * (scale * LOG2E)).astype(BF16))

        s_c = _dot3_nt(kc_ref[0, 0, gg], q_all)
        e_c, l_c = _softmax_t(s_c, cmp_end <= t, scale)
        p_c = e_c / l_c
        o_c.append(_dot_nn(vct_ref[gg], p_c.astype(BF16)))

        p_hi, p_lo = _split(p_c)
        imp3 = _dot_nn(ovl, p_hi) + _dot_nn(ovl, p_lo)
        imp_g = imp3[:, 0:tq]
        for r in range(1, rep):
            imp_g = imp_g + imp3[:, r * tq:(r + 1) * tq]
        imp_g = jnp.where(forced, FORCE_SCORE, imp_g)
        imp.append(jnp.where(visible, imp_g, -jnp.inf))

    unit = tq
    sb = NSA_SEL_BLOCK
    for v in range(s_len // unit):
        @pl.when(qi == v)
        def _(v=v):
            n_keys = (v + 1) * unit
            lo = n_keys - unit

            def causal(base, a, n):
                return lax.broadcasted_iota(jnp.int32, (n, lanes), 0) <= (t - (base + a))

            n_vis = n_keys // sb
            for gg in groups:
                s = _dot_nt(ksr_ref[gg, 0:n_keys], q_bf[gg])
                if n_vis > NSA_SEL_TOPK:
                    rank = _rank_t(imp[gg], n_vis)
                    rank3 = jnp.concatenate([rank] * rep, axis=1)
                    drop = jnp.where(rank3 < NSA_SEL_TOPK, 0.0, NEG_BIG)
                    parts = []
                    for j in range(n_vis):
                        parts.append((s[j * sb:(j + 1) * sb] + drop[j:j + 1, :],
                                      functools.partial(causal, j * sb) if j * sb >= lo else None))
                else:
                    parts = [(s[0:lo], None)] if lo > 0 else []
                    parts.append((s[lo:], functools.partial(causal, lo)))
                e_s, l_s = _softmax_parts(parts)
                os_ref[gg] = _dot_nn(vst_ref[gg, :, 0:n_keys], e_s) / l_s

    n_win = pad_chunks + 1
    def oldest(a, n):
        return lax.broadcasted_iota(jnp.int32, (n, lanes), 0) + a > t_loc

    def newest(a, n):
        return lax.broadcasted_iota(jnp.int32, (n, lanes), 0) + a <= t_loc

    for gg in groups:
        parts = []
        for c in range(n_win):
            s_w = _dot_nt(kw3_ref[gg, qi + c], q_bf[gg])
            if c < pad_chunks:
                s_w = s_w + jnp.where(qi + c >= pad_chunks, 0.0, NEG_BIG)
            parts.append((s_w, oldest if c == 0 else (newest if c == n_win - 1 else None)))
        e_w, l_w = _softmax_parts(parts)
        o_w = _dot_nn(vwt3_ref[gg, qi], e_w[0:tq])
        for c in range(1, n_win):
            o_w = o_w + _dot_nn(vwt3_ref[gg, qi + c], e_w[c * tq:(c + 1) * tq])
        o_w = o_w / l_w

        g_t = _sigmoid(gate_ref[0, :, grp(gg)].T)

        def gate_row(branch, g_t=g_t):
            return jnp.concatenate([g_t[3 * r + branch:3 * r + branch + 1, :] for r in range(rep)], axis=1)

        out_t = gate_row(0) * o_c[gg] + gate_row(1) * os_ref[gg] + gate_row(2) * o_w
        for r in range(rep):
            o_ref[0, :, grp(gg * rep + r)] = out_t[:, r * tq:(r + 1) * tq].T.astype(o_ref.dtype)


def _nsa(z3, cmp_kv, cos, sin_s, ovl_t, tq=256):
    b, s, _ = z3.shape
    g = NSA_KV_GROUPS
    n_sel = s // NSA_SEL_BLOCK
    n_cmp = cmp_kv.shape[3]
    qw = NSA_HEADS * HEAD_DIM
    kvw = g * HEAD_DIM
    n_chunk_pad = (s + NSA_WINDOW) // tq

    def col(off):
        return lambda bi, qi: (bi, 0, off // kvw)

    return pl.pallas_call(
        functools.partial(_nsa_kernel, n_sel=n_sel, scale=HEAD_DIM ** -0.5),
        grid=(b, s // tq),
        in_specs=[
            pl.BlockSpec((1, tq, qw), lambda bi, qi: (bi, qi, OFF_CQ // qw)),
            pl.BlockSpec((1, s, kvw), col(OFF_CKS)),
            pl.BlockSpec((1, s, kvw), col(OFF_CVS)),
            pl.BlockSpec((1, s, kvw), col(OFF_CKW)),
            pl.BlockSpec((1, s, kvw), col(OFF_CVW)),
            pl.BlockSpec((1, 1, g, n_cmp, HEAD_DIM), lambda bi, qi: (0, bi, 0, 0, 0)),
            pl.BlockSpec((1, 1, g, n_cmp, HEAD_DIM), lambda bi, qi: (1, bi, 0, 0, 0)),
            pl.BlockSpec((1, tq, g * LANES), lambda bi, qi: (bi, qi, OFF_CGATE // (g * LANES))),
            pl.BlockSpec((s, LANES), lambda bi, qi: (0, 0)),
            pl.BlockSpec((s, LANES), lambda bi, qi: (0, 0)),
            pl.BlockSpec((tq, LANES), lambda bi, qi: (qi, 0)),
            pl.BlockSpec((tq, LANES), lambda bi, qi: (qi, 0)),
            pl.BlockSpec((n_sel, n_cmp), lambda bi, qi: (0, 0)),
        ],
        out_specs=pl.BlockSpec((1, tq, qw), lambda bi, qi: (bi, qi, 0)),
        out_shape=jax.ShapeDtypeStruct((b, s, qw), BF16),
        scratch_shapes=[
            pltpu.VMEM((g, s, HEAD_DIM), BF16),
            pltpu.VMEM((g, HEAD_DIM, s), BF16),
            pltpu.VMEM((g, n_chunk_pad, tq, HEAD_DIM), BF16),
            pltpu.VMEM((g, n_chunk_pad, HEAD_DIM, tq), BF16),
            pltpu.VMEM((g, HEAD_DIM, n_cmp), BF16),
            pltpu.VMEM((g, HEAD_DIM, NSA_REP * tq), F32),
        ],
        compiler_params=_cparams(("parallel", "arbitrary")),
        name="nsa",
    )(z3, z3, z3, z3, z3, cmp_kv, cmp_kv, z3, cos, sin_s, cos, sin_s, ovl_t)


def _merge_kernel(ya_ref, yb_ref, yc_ref, ga_ref, gb_ref, gc_ref, wa_ref, wb_ref, wc_ref, o_ref,
                  wa_bf_ref, wb_bf_ref, wc_bf_ref):
    @pl.when(pl.program_id(1) == 0)
    def _():
        wa_bf_ref[...] = wa_ref[...].astype(BF16)
        wb_bf_ref[...] = wb_ref[...].astype(BF16)
        wc_bf_ref[...] = wc_ref[...].astype(BF16)

    m = ga_ref[...].astype(F32) * _dot_nn(ya_ref[...], wa_bf_ref[...])
    m = m + gb_ref[...].astype(F32) * _dot_nn(yb_ref[...], wb_bf_ref[...])
    m = m + gc_ref[...].astype(F32) * _dot_nn(yc_ref[...], wc_bf_ref[...])
    o_ref[...] = m.astype(o_ref.dtype)


def _merge(ya, yb, yc, gates, wa, wb, wc, layer, tm=1024, tn=1024):
    t = ya.shape[0]
    d = wa.shape[2]
    nj = d // tn

    def gate_spec(branch):
        return pl.BlockSpec((tm, tn), lambda j, i: (i, branch * nj + j))

    def y_spec(y):
        return pl.BlockSpec((tm, y.shape[1]), lambda j, i: (i, 0))

    def w_spec(w):
        return pl.BlockSpec((None, w.shape[1], tn), lambda j, i: (layer, 0, j), pipeline_mode=pl.Buffered(1))

    return pl.pallas_call(
        _merge_kernel,
        grid=(nj, t // tm),
        in_specs=[y_spec(ya), y_spec(yb), y_spec(yc), gate_spec(0), gate_spec(1), gate_spec(2),
                  w_spec(wa), w_spec(wb), w_spec(wc)],
        out_specs=pl.BlockSpec((tm, tn), lambda j, i: (i, j)),
        out_shape=jax.ShapeDtypeStruct((t, d), BF16),
        scratch_shapes=[pltpu.VMEM((w.shape[1], tn), BF16) for w in (wa, wb, wc)],
        compiler_params=_cparams(("parallel", "arbitrary")),
        name="merge",
    )(ya, yb, yc, gates, gates, gates, wa, wb, wc)


def _proj_residual_kernel(m_ref, w_ref, x_ref, o_ref, w_bf_ref):
    @pl.when(pl.program_id(1) == 0)
    def _():
        w_bf_ref[...] = w_ref[...].astype(BF16)

    o_ref[...] = x_ref[...] + _dot_nn(m_ref[...], w_bf_ref[...])


def _proj_residual(m, w, x2d, layer, tm=1024, tn=1024):
    t, k = m.shape
    d = w.shape[2]
    return pl.pallas_call(
        _proj_residual_kernel,
        grid=(d // tn, t // tm),
        in_specs=[
            pl.BlockSpec((tm, k), lambda j, i: (i, 0)),
            pl.BlockSpec((None, k, tn), lambda j, i: (layer, 0, j), pipeline_mode=pl.Buffered(1)),
            pl.BlockSpec((tm, tn), lambda j, i: (i, j)),
        ],
        out_specs=pl.BlockSpec((tm, tn), lambda j, i: (i, j)),
        out_shape=jax.ShapeDtypeStruct((t, d), F32),
        scratch_shapes=[pltpu.VMEM((k, tn), BF16)],
        compiler_params=_cparams(("parallel", "arbitrary")),
        name="out_proj",
    )(m, w, x2d)


def _rope_tables(pos, dim, width):
    inv = 1.0 / (ROPE_THETA ** (jnp.arange(0, dim, 2, dtype=F32) / dim))
    ang = pos.astype(F32)[:, None] * inv[None, :]
    cos = jnp.cos(ang)
    sin = jnp.sin(ang)
    cos = jnp.concatenate([cos, cos], axis=-1)
    sin_s = jnp.concatenate([-sin, sin], axis=-1)
    pad = ((0, 0), (0, width - dim))
    return jnp.pad(cos, pad), jnp.pad(sin_s, pad)


def _repack_kernel(srcp_ref, srcq_ref, start_ref, valid_ref, p_ref, q_ref, o_ref, *, n_src_cols):
    j = pl.program_id(1)
    start = start_ref[j]
    valid = valid_ref[j]
    row = lax.broadcasted_iota(jnp.int32, (LANES, LANES), 0)
    col = lax.broadcasted_iota(jnp.int32, (LANES, LANES), 1)
    lane = lax.broadcasted_iota(jnp.int32, (1, LANES), 1)

    def contribution(blk_ref, src_blk):
        take = jnp.where((src_blk * LANES + row == start + col) & (col < valid), 1.0, 0.0).astype(BF16)
        blk = jnp.where(src_blk * LANES + lane < n_src_cols, blk_ref[...], 0.0).astype(BF16)
        return _dot_nn(blk, take)

    o_ref[...] = (contribution(p_ref, srcp_ref[j]) + contribution(q_ref, srcq_ref[j])).astype(o_ref.dtype)


def _repack_tables(pieces, n_out_blocks, n_src_blocks):
    starts, valids = [], []
    for start, width in pieces:
        if width == 0:
            starts.append(0)
            valids.append(0)
        for off in range(0, width, LANES):
            starts.append(start + off)
            valids.append(min(LANES, width - off))
    starts += [0] * (n_out_blocks - len(starts))
    valids += [0] * (n_out_blocks - len(valids))
    assert n_src_blocks >= 2
    slots = [0, 1]
    srcp, srcq = [], []
    for start, valid in zip(starts, valids):
        need = []
        if valid > 0:
            need = list(range(start // LANES, (start + valid - 1) // LANES + 1))
        for blk in need:
            if blk not in slots:
                slots[0 if slots[0] not in need else 1] = blk
        srcp.append(slots[0])
        srcq.append(slots[1])
    return tuple(jnp.asarray(np.asarray(v, np.int32)) for v in (srcp, srcq, starts, valids))


def _repack(w, pieces, n_out_blocks):
    depth, k, n = w.shape
    tables = _repack_tables(pieces, n_out_blocks, -(-n // LANES))
    grid_spec = pltpu.PrefetchScalarGridSpec(
        num_scalar_prefetch=4,
        grid=(depth, n_out_blocks),
        in_specs=[
            pl.BlockSpec((None, k, LANES), lambda d, j, sp, sq, st, va: (d, 0, sp[j])),
            pl.BlockSpec((None, k, LANES), lambda d, j, sp, sq, st, va: (d, 0, sq[j])),
        ],
        out_specs=pl.BlockSpec((None, k, LANES), lambda d, j, sp, sq, st, va: (d, 0, j)),
    )
    return pl.pallas_call(
        functools.partial(_repack_kernel, n_src_cols=n),
        grid_spec=grid_spec,
        out_shape=jax.ShapeDtypeStruct((depth, k, n_out_blocks * LANES), BF16),
        compiler_params=_cparams(("parallel", "arbitrary")),
        name="repack",
    )(*tables, w, w)


def _repack_t_kernel(srcp_ref, srcq_ref, start_ref, valid_ref, p_ref, q_ref, o_ref, *, n_src_cols):
    j = pl.program_id(0)
    start = start_ref[j]
    valid = valid_ref[j]
    out_row = lax.broadcasted_iota(jnp.int32, (LANES, LANES), 0)
    src_row = lax.broadcasted_iota(jnp.int32, (LANES, LANES), 1)
    sub = lax.broadcasted_iota(jnp.int32, (LANES, 1), 0)

    def contribution(blk_ref, src_blk, layer):
        take = jnp.where((src_blk * LANES + src_row == start + out_row) & (out_row < valid), 1.0, 0.0).astype(BF16)
        blk = jnp.where(src_blk * LANES + sub < n_src_cols, blk_ref[:, layer, :], 0.0).astype(BF16)
        return _dot_nn(take, blk)

    for layer in range(o_ref.shape[0]):
        o_ref[layer] = (contribution(p_ref, srcp_ref[j], layer)
                        + contribution(q_ref, srcq_ref[j], layer)).astype(o_ref.dtype)


def _repack_t(wt, pieces, n_out_blocks):
    n, depth, k = wt.shape
    tables = _repack_tables(pieces, n_out_blocks, -(-n // LANES))
    grid_spec = pltpu.PrefetchScalarGridSpec(
        num_scalar_prefetch=4,
        grid=(n_out_blocks,),
        in_specs=[
            pl.BlockSpec((LANES, depth, k), lambda j, sp, sq, st, va: (sp[j], 0, 0)),
            pl.BlockSpec((LANES, depth, k), lambda j, sp, sq, st, va: (sq[j], 0, 0)),
        ],
        out_specs=pl.BlockSpec((depth, LANES, k), lambda j, sp, sq, st, va: (0, j, 0)),
    )
    return pl.pallas_call(
        functools.partial(_repack_t_kernel, n_src_cols=n),
        grid_spec=grid_spec,
        out_shape=jax.ShapeDtypeStruct((depth, n_out_blocks * LANES, k), BF16),
        compiler_params=_cparams(("arbitrary",)),
        name="repack_t",
    )(*tables, wt, wt)


def _pack_w_in(w):
    sizes = [MOBA_HEADS * HEAD_DIM] * 3 + [MLA_Q_RANK, MLA_KV_RANK, MLA_ROPE, NSA_HEADS * HEAD_DIM] \
        + [NSA_KV_GROUPS * HEAD_DIM] * 6
    offs = [int(v) for v in np.concatenate([[0], np.cumsum(sizes)])]
    (a_q, a_k, a_v, b_cq, b_ckv, b_kr, c_q, c_kc, c_vc, c_ks, c_vs, c_kw, c_vw) = [
        (offs[i], sizes[i]) for i in range(len(sizes))]
    gate0 = offs[-1]
    per_group = NSA_REP * 3
    gates = [(gate0 + gi * per_group, per_group) for gi in range(NSA_KV_GROUPS)]
    merge = (gate0 + NSA_HEADS * 3, w.shape[2] - gate0 - NSA_HEADS * 3)
    pieces = [c_q, b_ckv, b_cq, a_q, a_k, a_v, c_kc, c_vc, c_ks, c_vs, c_kw, c_vw] + gates + [b_kr]
    used = sum(-(-width // LANES) for _, width in pieces)
    pieces += [(0, 0)] * (Z_COLS // LANES - used) + [merge]
    return _repack_t(jnp.transpose(w, (2, 0, 1)), pieces, (Z_COLS + merge[1]) // LANES)


def _pack_w_uq(w):
    per_head = MLA_NOPE + MLA_ROPE
    pieces = [(h * per_head, MLA_NOPE) for h in range(MLA_HEADS)] \
        + [(h * per_head + MLA_NOPE, MLA_ROPE) for h in range(MLA_HEADS)]
    return _repack(w, pieces, 2 * MLA_HEADS)


def _pack_w_ukv(w):
    per_head = MLA_NOPE + MLA_V
    pieces = [(h * per_head, MLA_NOPE) for h in range(MLA_HEADS)] \
        + [(h * per_head + MLA_NOPE, MLA_V) for h in range(MLA_HEADS)]
    return _repack(w, pieces, 2 * MLA_HEADS)


def kernel(x, ffn1_norm, ffn1_w_gate, ffn1_w_up, ffn1_w_down, mix_norm, w_in, mla_q_norm, mla_w_uq, mla_kv_norm, mla_w_ukv, nsa_cmp_pos_k, nsa_cmp_w1_k, nsa_cmp_w2_k, nsa_cmp_pos_v, nsa_cmp_w1_v, nsa_cmp_w2_v, w_branch_moba, w_branch_mla, w_branch_nsa, w_out, ffn2_norm, ffn2_w_gate, ffn2_w_up, ffn2_w_down, final_norm):
    b, s, d = x.shape
    depth = w_in.shape[0]
    t = b * s
    assert NSA_CMP_LEN == 2 * NSA_CMP_STRIDE and s % MOBA_BLOCK == 0 and s % NSA_SEL_BLOCK == 0

    pos = jnp.arange(s)
    cos_h, sin_h = _rope_tables(pos, HEAD_DIM, LANES)
    cos_r, sin_r = _rope_tables(pos, MLA_ROPE, LANES)
    n16 = s // NSA_CMP_STRIDE
    cmp_end = np.arange(n16) * NSA_CMP_STRIDE + NSA_CMP_LEN - 1
    cos_c, sin_c = _rope_tables(jnp.asarray(cmp_end), HEAD_DIM, LANES)
    n_sel = s // NSA_SEL_BLOCK
    sel_start = np.arange(n_sel) * NSA_SEL_BLOCK
    cmp_start = np.arange(n16) * NSA_CMP_STRIDE
    ovl_t = ((cmp_start[None, :] <= sel_start[:, None] + NSA_SEL_BLOCK - 1)
             & (cmp_end[None, :] >= sel_start[:, None])).astype(np.float32)
    ovl_t = jnp.asarray(ovl_t, BF16)

    w_in_p = _pack_w_in(w_in)
    w_uq_p = _pack_w_uq(mla_w_uq)
    w_ukv_p = _pack_w_ukv(mla_w_ukv)

    xf = x.reshape(t, d)
    for i in range(depth):
        xf = _ffn(xf, ffn1_norm[i], ffn1_w_gate, ffn1_w_up, ffn1_w_down, i)

        z, gates = _in_proj(xf, mix_norm[i], w_in_p, i)
        z3 = z.reshape(b, s, Z_COLS)
        y_a = _moba(z3, cos_h, sin_h)
        y_b = _mla(z3, mla_q_norm[i], mla_kv_norm[i], w_uq_p, w_ukv_p, i, cos_r, sin_r)

        w1 =jnp.stack([nsa_cmp_w1_k[i], nsa_cmp_w1_v[i]])
        w2 = jnp.stack([nsa_cmp_w2_k[i], nsa_cmp_w2_v[i]])
        posflat = jnp.stack([nsa_cmp_pos_k[i].reshape(1, -1), nsa_cmp_pos_v[i].reshape(1, -1)])
        cmp_kv = _nsa_compress(z3, w1, w2, posflat, cos_c, sin_c)
        y_c = _nsa(z3, cmp_kv, cos_h, sin_h, ovl_t)

        merged = _merge(y_a.reshape(t, -1), y_b.reshape(t, -1), y_c.reshape(t, -1), gates,
                        w_branch_moba, w_branch_mla, w_branch_nsa, i)
        xf = _proj_residual(merged, w_out, xf, i)

        xf = _ffn(xf, ffn2_norm[i], ffn2_w_gate, ffn2_w_up, ffn2_w_down, i)
    return _rms(xf, final_norm, F32).reshape(b, s, d)
```

```python
import functools

import numpy as np
import jax
import jax.numpy as jnp
from jax import lax
from jax.experimental import pallas as pl
from jax.experimental.pallas import tpu as pltpu

F32 = jnp.float32
BF16 = jnp.bfloat16

HEAD_DIM = 128
ROPE_THETA = 10000.0
NORM_EPS = 1e-6
NEG_BIG = -1e30
TINY = 1e-30
LOG2E = 1.4426950408889634
N_BRANCHES = 3

MOBA_HEADS = 4
MOBA_BLOCK = 256
MOBA_TOPK = 3

MLA_HEADS = 6
MLA_Q_RANK = 512
MLA_KV_RANK = 256
MLA_NOPE = 128
MLA_ROPE = 64
MLA_V = 128

NSA_HEADS = 6
NSA_KV_GROUPS = 2
NSA_REP = NSA_HEADS // NSA_KV_GROUPS
NSA_CMP_LEN = 32
NSA_CMP_STRIDE = 16
NSA_CMP_HIDDEN = 256
NSA_SEL_BLOCK = 64
NSA_SEL_TOPK = 16
NSA_WINDOW = 512
FORCE_SCORE = 1e9

LANES = 128
SUBLANES = 8
SOFTMAX_CHUNK = 64
VMEM_LIMIT = 56 * 2**20

OFF_CQ = 0
OFF_BCKV = 768
OFF_BCQ = 1024
OFF_AQ = 1536
OFF_AK = 2048
OFF_AV = 2560
OFF_CKC = 3072
OFF_CVC = 3328
OFF_CKS = 3584
OFF_CVS = 3840
OFF_CKW = 4096
OFF_CVW = 4352
OFF_CGATE = 4608
OFF_BKR = 4864
Z_COLS = 5120


def _cparams(sem):
    return pltpu.CompilerParams(dimension_semantics=sem, vmem_limit_bytes=VMEM_LIMIT)


def _rmsnorm(x, g):
    ms = jnp.mean(x * x, axis=-1, keepdims=True)
    return x * lax.rsqrt(ms + NORM_EPS) * g


def _sigmoid(x):
    return 0.5 * jnp.tanh(0.5 * x) + 0.5


def _rope_full(x, cos, sin_signed):
    return x * cos + pltpu.roll(x, HEAD_DIM // 2, axis=1) * sin_signed


def _rope_half(x, cos, sin_signed):
    lane = lax.broadcasted_iota(jnp.int32, x.shape, 1)
    q = MLA_ROPE // 2
    swapped = jnp.where(lane < q, pltpu.roll(x, LANES - q, axis=1), pltpu.roll(x, q, axis=1))
    return x * cos + swapped * sin_signed


def _dot_nt(a, b):
    return lax.dot_general(a, b, (((1,), (1,)), ((), ())), preferred_element_type=F32)


def _dot_nn(a, b):
    return jnp.dot(a, b, preferred_element_type=F32)


def _split(a):
    hi = a.astype(BF16)
    lo = (a - hi.astype(F32)).astype(BF16)
    return hi, lo


def _dot3_nt(a, b):
    ah, al = _split(a)
    bh, bl = _split(b)
    return _dot_nt(ah, bh) + _dot_nt(ah, bl) + _dot_nt(al, bh)


def _dot3_nn(a, b):
    ah, al = _split(a)
    bh, bl = _split(b)
    return _dot_nn(ah, bh) + _dot_nn(ah, bl) + _dot_nn(al, bh)


def _softmax_t(s, mask, scale):
    s = jnp.where(mask, s * scale, NEG_BIG)
    m = jnp.max(s, axis=0, keepdims=True)
    e = jnp.where(mask, jnp.exp(s - m), 0.0)
    l = jnp.maximum(jnp.sum(e, axis=0, keepdims=True), TINY)
    return e, l


def _softmax_parts(parts, chunk=SOFTMAX_CHUNK):
    lanes = parts[0][0].shape[1]
    chunk = max(SUBLANES, chunk * 384 // max(lanes, 384))
    pieces = []
    for s, mask_fn, pen in parts:
        for a in range(0, s.shape[0], chunk):
            n = min(chunk, s.shape[0] - a)
            p = s[a:a + n]
            pieces.append((p if mask_fn is None else jnp.where(mask_fn(a, n), p, NEG_BIG), pen))

    def fold(x, op):
        return op(x.reshape(x.shape[0] // SUBLANES, SUBLANES, lanes), axis=0)

    def col_max(p, pen):
        m = fold(p, jnp.max)
        return m if pen is None else m + pen

    m8 = functools.reduce(jnp.maximum, [col_max(p, pen) for p, pen in pieces])
    mx = jnp.max(m8, axis=0, keepdims=True)
    es, l8 = [], None
    for p, pen in pieces:
        e = jnp.exp2(p - (mx if pen is None else mx - pen))
        l8 = fold(e, jnp.sum) if l8 is None else l8 + fold(e, jnp.sum)
        es.append(e.astype(BF16))
    l = jnp.sum(l8, axis=0, keepdims=True)
    e = jnp.concatenate(es, axis=0) if len(es) > 1 else es[0]
    return e, jnp.maximum(l, TINY)


def _rank_t(v, nb):
    blk = lax.broadcasted_iota(jnp.int32, v.shape, 0)
    rank = jnp.zeros(v.shape, F32)
    for j in range(nb):
        vj = v[j:j + 1, :]
        ge = (vj >= v).astype(F32)
        gt = (vj > v).astype(F32)
        rank = rank + jnp.where(blk > j, ge, gt)
    return rank


def _rms_kernel(x_ref, g_ref, o_ref):
    o_ref[...] = _rmsnorm(x_ref[...], g_ref[...]).astype(o_ref.dtype)


def _rms_cast_kernel(x_ref, g_ref, w_ref, o_ref, w_bf_ref):
    o_ref[...] = _rmsnorm(x_ref[...], g_ref[...]).astype(o_ref.dtype)
    w_bf_ref[...] = w_ref[...].astype(w_bf_ref.dtype)


def _rms(x2d, g, out_dtype, tm=512, cast=None):
    t, d = x2d.shape
    steps = t // tm
    in_specs = [pl.BlockSpec((tm, d), lambda i: (i, 0)), pl.BlockSpec((1, d), lambda i: (0, 0))]
    out_specs = pl.BlockSpec((tm, d), lambda i: (i, 0))
    out_shape = jax.ShapeDtypeStruct((t, d), out_dtype)
    if cast is None:
        return pl.pallas_call(
            _rms_kernel, grid=(steps,), in_specs=in_specs, out_specs=out_specs, out_shape=out_shape,
            compiler_params=_cparams(("parallel",)), name="rmsnorm",
        )(x2d, g.reshape(1, d))
    w, layer = cast
    rows, cols = w.shape[1:]
    slab = rows // steps
    assert slab * steps == rows and slab % (2 * SUBLANES) == 0
    return pl.pallas_call(
        _rms_cast_kernel,
        grid=(steps,),
        in_specs=in_specs + [pl.BlockSpec((None, slab, cols), lambda i: (layer, i, 0))],
        out_specs=[out_specs, pl.BlockSpec((slab, cols), lambda i: (i, 0))],
        out_shape=[out_shape, jax.ShapeDtypeStruct((rows, cols), BF16)],
        compiler_params=_cparams(("parallel",)),
        name="rmsnorm_cast",
    )(x2d, g.reshape(1, d), w)


def _ffn_up_kernel(h_ref, wg_ref, wu_ref, o_ref, wg_bf_ref, wu_bf_ref):
    @pl.when(pl.program_id(1) == 0)
    def _():
        wg_bf_ref[...] = wg_ref[...].astype(BF16)
        wu_bf_ref[...] = wu_ref[...].astype(BF16)

    h = h_ref[...]
    a = _dot_nn(h, wg_bf_ref[...])
    u = _dot_nn(h, wu_bf_ref[...])
    o_ref[...] = (a * _sigmoid(a) * u).astype(o_ref.dtype)


def _ffn_down_kernel(a_ref, wd_ref, x_ref, o_ref):
    o_ref[...] = x_ref[...] + 0.5 * _dot_nn(a_ref[...], wd_ref[...])


def _ffn(x2d, g, wg, wu, wd, layer, tm=1024, tf=512, tm_down=512, tn=1024):
    t, d = x2d.shape
    f = wg.shape[2]
    h, wd_bf = _rms(x2d, g, BF16, cast=(wd, layer))
    act = pl.pallas_call(
        _ffn_up_kernel,
        grid=(f // tf, t // tm),
        in_specs=[
            pl.BlockSpec((tm, d), lambda j, i: (i, 0)),
            pl.BlockSpec((None, d, tf), lambda j, i: (layer, 0, j)),
            pl.BlockSpec((None, d, tf), lambda j, i: (layer, 0, j)),
        ],
        out_specs=pl.BlockSpec((tm, tf), lambda j, i: (i, j)),
        out_shape=jax.ShapeDtypeStruct((t, f), BF16),
        scratch_shapes=[pltpu.VMEM((d, tf), BF16), pltpu.VMEM((d, tf), BF16)],
        compiler_params=_cparams(("parallel", "arbitrary")),
        name="ffn_up",
    )(h, wg, wu)
    return pl.pallas_call(
        _ffn_down_kernel,
        grid=(d // tn, t // tm_down),
        in_specs=[
            pl.BlockSpec((tm_down, f), lambda n, i: (i, 0)),
            pl.BlockSpec((f, tn), lambda n, i: (0, n)),
            pl.BlockSpec((tm_down, tn), lambda n, i: (i, n)),
        ],
        out_specs=pl.BlockSpec((tm_down, tn), lambda n, i: (i, n)),
        out_shape=jax.ShapeDtypeStruct((t, d), F32),
        compiler_params=_cparams(("parallel", "parallel")),
        name="ffn_down",
    )(act, wd_bf, x2d)


def _in_proj_kernel(x_ref, g_ref, w_ref, z_ref, gate_ref, h_ref, *, n_z):
    j = pl.program_id(1)

    @pl.when(j == 0)
    def _():
        h_ref[...] = _rmsnorm(x_ref[...], g_ref[...]).astype(BF16)

    @pl.when(j < n_z)
    def _():
        z_ref[...] = _dot_nt(h_ref[...], w_ref[...])

    @pl.when(j >= n_z)
    def _():
        gate_ref[...] = _sigmoid(_dot_nt(h_ref[...], w_ref[...])).astype(gate_ref.dtype)


def _in_proj(x2d, g, w, layer, tm=1024, tn=1024):
    t, d = x2d.shape
    n = w.shape[1]
    n_z = Z_COLS // tn
    n_g = (n - Z_COLS) // tn
    return pl.pallas_call(
        functools.partial(_in_proj_kernel, n_z=n_z),
        grid=(t // tm, n_z + n_g),
        in_specs=[
            pl.BlockSpec((tm, d), lambda i, j: (i, 0)),
            pl.BlockSpec((1, d), lambda i, j: (0, 0)),
            pl.BlockSpec((None, tn, d), lambda i, j: (layer, j, 0)),
        ],
        out_specs=[
            pl.BlockSpec((tm, tn), lambda i, j: (i, jnp.minimum(j, n_z - 1))),
            pl.BlockSpec((tm, tn), lambda i, j: (i, jnp.maximum(j - n_z, 0))),
        ],
        out_shape=[jax.ShapeDtypeStruct((t, Z_COLS), F32), jax.ShapeDtypeStruct((t, n - Z_COLS), BF16)],
        scratch_shapes=[pltpu.VMEM((tm, d), BF16)],
        compiler_params=_cparams(("parallel", "arbitrary")),
        name="in_proj",
    )(x2d, g.reshape(1, d), w)


def _moba_kernel(q_ref, k_ref, v_ref, cosf_ref, sinf_ref, cosq_ref, sinq_ref, o_ref,
                 kr_ref, km_ref, vt_ref, *, n_blk, scale):
    qi = pl.program_id(2)
    tq = q_ref.shape[1]
    hp = q_ref.shape[2] // LANES
    head = lambda hh: slice(hh * LANES, (hh + 1) * LANES)

    @pl.when(qi == 0)
    def _():
        for hh in range(hp):
            kr = _rope_full(k_ref[0, :, head(hh)], cosf_ref[...], sinf_ref[...])
            kr_ref[hh] = kr.astype(BF16)
            km_ref[hh] = jnp.concatenate(
                [jnp.mean(kr[j * MOBA_BLOCK:(j + 1) * MOBA_BLOCK], axis=0, keepdims=True) for j in range(n_blk)],
                axis=0)
            vt_ref[hh] = v_ref[0, :, head(hh)].T.astype(BF16)

    blk = MOBA_BLOCK
    bpt = tq // blk
    lane = lax.broadcasted_iota(jnp.int32, (1, tq), 1)
    lane_sub = lane // blk
    lane_loc = lane % blk

    for n in range(n_blk // bpt):
        @pl.when(qi == n)
        def _(n=n):
            first = n * bpt
            ranked = first + bpt - 1 > MOBA_TOPK
            for hh in range(hp):
                qr = _rope_full(q_ref[0, :, head(hh)], cosq_ref[...], sinq_ref[...])
                q_bf = (qr * (scale * LOG2E)).astype(BF16)
                s = _dot_nt(kr_ref[hh, 0:(n + 1) * tq], q_bf)
                drop = None
                if ranked:
                    gate = _dot3_nt(km_ref[hh], qr)
                    past = lax.broadcasted_iota(jnp.int32, (n_blk, tq), 0) < first + lane_sub
                    gate = jnp.where(past, gate, -jnp.inf)
                    drop = jnp.where(_rank_t(gate, first + bpt - 1) < MOBA_TOPK, 0.0, NEG_BIG)
                parts = []
                if drop is not None:
                    for j in range(first):
                        parts.append((s[j * blk:(j + 1) * blk], None, drop[j:j + 1, :]))
                elif first > 0:
                    parts.append((s[0:first * blk], None, None))
                for i2 in range(bpt):
                    j = first + i2
                    pen = None
                    if drop is not None and i2 < bpt - 1:
                        pen = jnp.where(lane_sub > i2, drop[j:j + 1, :], 0.0)

                    limit = jnp.where(lane_sub > i2, blk, jnp.where(lane_sub == i2, lane_loc, -1))

                    def mask_fn(a, rows, limit=limit):
                        return lax.broadcasted_iota(jnp.int32, (rows, tq), 0) + a <= limit

                    parts.append((s[j * blk:(j + 1) * blk], mask_fn, pen))
                e, l = _softmax_parts(parts)
                o_t = _dot_nn(vt_ref[hh, :, 0:(n + 1) * tq], e) / l
                o_ref[0, :, head(hh)] = o_t.T.astype(o_ref.dtype)


def _moba(z3, cos, sin_s, tq=2 * MOBA_BLOCK, hp=MOBA_HEADS):
    b, s, _ = z3.shape
    n_blk = s // MOBA_BLOCK
    w = hp * LANES
    qb, kb, vb = OFF_AQ // w, OFF_AK // w, OFF_AV // w
    return pl.pallas_call(
        functools.partial(_moba_kernel, n_blk=n_blk, scale=HEAD_DIM ** -0.5),
        grid=(b, MOBA_HEADS // hp, s // tq),
        in_specs=[
            pl.BlockSpec((1, tq, w), lambda bi, h, qi: (bi, qi, qb + h)),
            pl.BlockSpec((1, s, w), lambda bi, h, qi: (bi, 0, kb + h)),
            pl.BlockSpec((1, s, w), lambda bi, h, qi: (bi, 0, vb + h)),
            pl.BlockSpec((s, LANES), lambda bi, h, qi: (0, 0)),
            pl.BlockSpec((s, LANES), lambda bi, h, qi: (0, 0)),
            pl.BlockSpec((tq, LANES), lambda bi, h, qi: (qi, 0)),
            pl.BlockSpec((tq, LANES), lambda bi, h, qi: (qi, 0)),
        ],
        out_specs=pl.BlockSpec((1, tq, w), lambda bi, h, qi: (bi, qi, h)),
        out_shape=jax.ShapeDtypeStruct((b, s, MOBA_HEADS * HEAD_DIM), BF16),
        scratch_shapes=[pltpu.VMEM((hp, s, LANES), BF16), pltpu.VMEM((hp, n_blk, LANES), F32),
                        pltpu.VMEM((hp, LANES, s), BF16)],
        compiler_params=_cparams(("parallel", "parallel", "arbitrary")),
        name="moba",
    )(z3, z3, z3, cos, sin_s, cos, sin_s)


def _mla_kernel(cq_ref, ckv_ref, kr_ref, qg_ref, kvg_ref, wqn_ref, wqr_ref, wkn_ref, wv_ref,
                cosf_ref, sinf_ref, cosq_ref, sinq_ref, o_ref, kf_ref, vt_ref, *, scale):
    qi = pl.program_id(2)
    tq = cq_ref.shape[1]
    s_len = ckv_ref.shape[1]
    hp = wqn_ref.shape[1] // LANES
    head = lambda hh: slice(hh * LANES, (hh + 1) * LANES)

    @pl.when(qi == 0)
    def _():
        ckv = _rmsnorm(ckv_ref[0], kvg_ref[...]).astype(BF16)
        k_nope = _dot_nn(ckv, wkn_ref[...])
        v = _dot_nn(ckv, wv_ref[...])
        k_rope = _rope_half(kr_ref[0], cosf_ref[...], sinf_ref[...]).astype(BF16)
        for hh in range(hp):
            kf_ref[hh, :, :LANES] = k_nope[:, head(hh)].astype(BF16)
            kf_ref[hh, :, LANES:] = k_rope
            vt_ref[hh] = v[:, head(hh)].T.astype(BF16)

    cq = _rmsnorm(cq_ref[0], qg_ref[...]).astype(BF16)
    q_nope = _dot_nn(cq, wqn_ref[...])
    q_rope = _dot_nn(cq, wqr_ref[...])
    def causal(a, n):
        return (lax.broadcasted_iota(jnp.int32, (n, tq), 0) + a
                <= lax.broadcasted_iota(jnp.int32, (n, tq), 1))

    for n in range(s_len // tq):
        @pl.when(qi == n)
        def _(n=n):
            for hh in range(hp):
                q = jnp.concatenate([q_nope[:, head(hh)],
                                     _rope_half(q_rope[:, head(hh)], cosq_ref[...], sinq_ref[...])], axis=1)
                q_bf = (q * (scale * LOG2E)).astype(BF16)
                s = _dot_nt(kf_ref[hh, 0:(n + 1) * tq], q_bf)
                parts = [(s[0:n * tq], None, None)] if n > 0 else []
                parts.append((s[n * tq:], causal, None))
                e, l = _softmax_parts(parts)
                o_t = _dot_nn(vt_ref[hh, :, 0:(n + 1) * tq], e) / l
                o_ref[0, :, head(hh)] = o_t.T.astype(o_ref.dtype)


def _mla(z3, q_gain, kv_gain, w_uq, w_ukv, layer, cos, sin_s, tq=512, hp=MLA_HEADS):
    b, s, _ = z3.shape
    w = hp * LANES
    ng = MLA_HEADS // hp
    return pl.pallas_call(
        functools.partial(_mla_kernel, scale=(MLA_NOPE + MLA_ROPE) ** -0.5),
        grid=(b, ng, s // tq),
        in_specs=[
            pl.BlockSpec((1, tq, MLA_Q_RANK), lambda bi, h, qi: (bi, qi, OFF_BCQ // MLA_Q_RANK)),
            pl.BlockSpec((1, s, MLA_KV_RANK), lambda bi, h, qi: (bi, 0, OFF_BCKV // MLA_KV_RANK)),
            pl.BlockSpec((1, s, LANES), lambda bi, h, qi: (bi, 0, OFF_BKR // LANES)),
            pl.BlockSpec((1, MLA_Q_RANK), lambda bi, h, qi: (0, 0)),
            pl.BlockSpec((1, MLA_KV_RANK), lambda bi, h, qi: (0, 0)),
            pl.BlockSpec((None, MLA_Q_RANK, w), lambda bi, h, qi: (layer, 0, h)),
            pl.BlockSpec((None, MLA_Q_RANK, w), lambda bi, h, qi: (layer, 0, ng + h)),
            pl.BlockSpec((None, MLA_KV_RANK, w), lambda bi, h, qi: (layer, 0, h)),
            pl.BlockSpec((None, MLA_KV_RANK, w), lambda bi, h, qi: (layer, 0, ng + h)),
            pl.BlockSpec((s, LANES), lambda bi, h, qi: (0, 0)),
            pl.BlockSpec((s, LANES), lambda bi, h, qi: (0, 0)),
            pl.BlockSpec((tq, LANES), lambda bi, h, qi: (qi, 0)),
            pl.BlockSpec((tq, LANES), lambda bi, h, qi: (qi, 0)),
        ],
        out_specs=pl.BlockSpec((1, tq, w), lambda bi, h, qi: (bi, qi, h)),
        out_shape=jax.ShapeDtypeStruct((b, s, MLA_HEADS * MLA_V), BF16),
        scratch_shapes=[pltpu.VMEM((hp, s, 2 * LANES), BF16), pltpu.VMEM((hp, LANES, s), BF16)],
        compiler_params=_cparams(("parallel", "parallel", "arbitrary")),
        name="mla",
    )(z3, z3, z3, q_gain.reshape(1, -1), kv_gain.reshape(1, -1), w_uq, w_uq, w_ukv, w_ukv,
      cos, sin_s, cos, sin_s)


def _dot3_presplit(a, b_hi, b_lo):
    ah, al = _split(a)
    return _dot_nn(ah, b_hi) + _dot_nn(ah, b_lo) + _dot_nn(al, b_hi)


def _cmp_kernel(*refs):
    n_grp = len(refs) - 9
    x_refs = refs[:n_grp]
    w1_ref, w2_ref, pos_ref, cos_ref, sin_ref, o_ref, w1_hi_ref, w1_lo_ref, bias_ref = refs[n_grp:]
    kv = pl.program_id(0)
    n16 = o_ref.shape[3]
    half = NSA_CMP_STRIDE * HEAD_DIM

    @pl.when(pl.program_id(1) == 0)
    def _():
        hi, lo = _split(w1_ref[0])
        w1_hi_ref[...] = hi
        w1_lo_ref[...] = lo
        bias_ref[...] = _dot3_presplit(jnp.broadcast_to(pos_ref[0], (SUBLANES, 2 * half)), hi, lo)

    for gi, x_ref in enumerate(x_refs):
        x = jnp.concatenate(
            [x_ref[0, pl.ds(l, n16, stride=NSA_CMP_STRIDE), :] for l in range(NSA_CMP_STRIDE)], axis=1)
        first = _dot3_presplit(x, w1_hi_ref[:half], w1_lo_ref[:half])
        second = _dot3_presplit(x, w1_hi_ref[half:], w1_lo_ref[half:])
        hid = first + pltpu.roll(second, n16 - 1, axis=0) + bias_ref[0:1]
        y = _dot3_nn(jax.nn.gelu(hid), w2_ref[0])
        roped = _rope_full(y, cos_ref[...], sin_ref[...])
        o_ref[0, 0, gi] = jnp.where(kv == 0, roped, y)


def _nsa_compress(z3, w1, w2, posflat, cos_c, sin_c):
    b, s, _ = z3.shape
    g = NSA_KV_GROUPS
    n16 = s // NSA_CMP_STRIDE
    width = NSA_CMP_STRIDE * HEAD_DIM
    hid = w1.shape[2]
    base = OFF_CKC // LANES
    x_specs = [pl.BlockSpec((1, s, LANES), lambda kv, bi, gi=gi: (bi, 0, base + g * kv + gi)) for gi in range(g)]
    return pl.pallas_call(
        _cmp_kernel,
        grid=(2, b),
        in_specs=x_specs + [
            pl.BlockSpec((1, 2 * width, hid), lambda kv, bi: (kv, 0, 0)),
            pl.BlockSpec((1, hid, HEAD_DIM), lambda kv, bi: (kv, 0, 0)),
            pl.BlockSpec((1, 1, 2 * width), lambda kv, bi: (kv, 0, 0)),
            pl.BlockSpec((n16, HEAD_DIM), lambda kv, bi: (0, 0)),
            pl.BlockSpec((n16, HEAD_DIM), lambda kv, bi: (0, 0)),
        ],
        out_specs=pl.BlockSpec((1, 1, g, n16, HEAD_DIM), lambda kv, bi: (kv, bi, 0, 0, 0)),
        out_shape=jax.ShapeDtypeStruct((2, b, g, n16, HEAD_DIM), F32),
        scratch_shapes=[pltpu.VMEM((2 * width, hid), BF16), pltpu.VMEM((2 * width, hid), BF16),
                        pltpu.VMEM((SUBLANES, hid), F32)],
        compiler_params=_cparams(("parallel", "arbitrary")),
        name="nsa_compress",
    )(*([z3] * g), w1, w2, posflat, cos_c, sin_c)


def _nsa_kernel(q_ref, ks_ref, vs_ref, kw_ref, vw_ref, kc_ref, vc_ref, gate_ref,
                cosf_ref, sinf_ref, cosq_ref, sinq_ref, ovl_ref, o_ref,
                ksr_ref, vst_ref, kw3_ref, vwt3_ref, vct_ref, os_ref, *, n_sel, scale):
    qi = pl.program_id(1)
    tq = q_ref.shape[1]
    s_len = ks_ref.shape[1]
    n_grp = ks_ref.shape[2] // HEAD_DIM
    rep = NSA_REP
    lanes = rep * tq
    n_chunk = s_len // tq
    pad_chunks = NSA_WINDOW // tq
    n_cmp = kc_ref.shape[3]
    grp = lambda gg: slice(gg * HEAD_DIM, (gg + 1) * HEAD_DIM)
    groups = range(n_grp)

    @pl.when(qi == 0)
    def _():
        for gg in groups:
            ksr_ref[gg] = _rope_full(ks_ref[0, :, grp(gg)], cosf_ref[...], sinf_ref[...]).astype(BF16)
            vst_ref[gg] = vs_ref[0, :, grp(gg)].T.astype(BF16)
            kw3_ref[gg, 0:pad_chunks] = jnp.zeros((pad_chunks, tq, HEAD_DIM), BF16)
            vwt3_ref[gg, 0:pad_chunks] = jnp.zeros((pad_chunks, HEAD_DIM, tq), BF16)
            kwr = _rope_full(kw_ref[0, :, grp(gg)], cosf_ref[...], sinf_ref[...]).astype(BF16)
            for c in range(n_chunk):
                kw3_ref[gg, pad_chunks + c] = kwr[c * tq:(c + 1) * tq]
                vwt3_ref[gg, pad_chunks + c] = vw_ref[0, c * tq:(c + 1) * tq, grp(gg)].T.astype(BF16)
            vct_ref[gg] = vc_ref[0, 0, gg].T.astype(BF16)

    cos_q = cosq_ref[...]
    sin_q = sinq_ref[...]
    lane = lax.broadcasted_iota(jnp.int32, (1, lanes), 1)
    t_loc = lane % tq
    t = qi * tq + t_loc
    cur = t[:, 0:tq] // NSA_SEL_BLOCK
    blk = lax.broadcasted_iota(jnp.int32, (n_sel, tq), 0)
    forced = (blk == 0) | (blk == cur) | (blk == cur - 1)
    visible = blk <= cur
    cmp_end = (lax.broadcasted_iota(jnp.int32, (n_cmp, lanes), 0) * NSA_CMP_STRIDE + (NSA_CMP_LEN - 1))
    ovl = ovl_ref[...]

    q_bf, o_c, imp = [], [], []
    for gg in groups:
        q_all = jnp.concatenate(
            [_rope_full(q_ref[0, :, grp(gg * rep + r)], cos_q, sin_q) for r in range(rep)],
            axis=0)
        q_bf.append((q_all * (scale * LOG2E)).astype(BF16))

        s_c = _dot3_nt(kc_ref[0, 0, gg], q_all)
        e_c, l_c = _softmax_t(s_c, cmp_end <= t, scale)
        p_c = e_c / l_c
        o_c.append(_dot_nn(vct_ref[gg], p_c.astype(BF16)))

        p_hi, p_lo = _split(p_c)
        imp3 = _dot_nn(ovl, p_hi) + _dot_nn(ovl, p_lo)
        imp_g = imp3[:, 0:tq]
        for r in range(1, rep):
            imp_g = imp_g + imp3[:, r * tq:(r + 1) * tq]
        imp_g = jnp.where(forced, FORCE_SCORE, imp_g)
        imp.append(jnp.where(visible, imp_g, -jnp.inf))

    unit = tq
    sb = NSA_SEL_BLOCK
    for v in range(s_len // unit):
        @pl.when(qi == v)
        def _(v=v):
            n_keys = (v + 1) * unit
            lo = n_keys - unit

            def causal(base, a, n):
                return lax.broadcasted_iota(jnp.int32, (n, lanes), 0) <= (t - (base + a))

            n_vis = n_keys // sb
            for gg in groups:
                s = _dot_nt(ksr_ref[gg, 0:n_keys], q_bf[gg])
                if n_vis > NSA_SEL_TOPK:
                    rank = _rank_t(imp[gg], n_vis)
                    rank3 = jnp.concatenate([rank] * rep, axis=1)
                    drop = jnp.where(rank3 < NSA_SEL_TOPK, 0.0, NEG_BIG)
                    parts = []
                    for j in range(n_vis):
                        parts.append((s[j * sb:(j + 1) * sb],
                                      functools.partial(causal, j * sb) if j * sb >= lo else None,
                                      drop[j:j + 1, :]))
                else:
                    parts = [(s[0:lo], None, None)] if lo > 0 else []
                    parts.append((s[lo:], functools.partial(causal, lo), None))
                e_s, l_s = _softmax_parts(parts)
                os_ref[gg] = _dot_nn(vst_ref[gg, :, 0:n_keys], e_s) / l_s

    n_win = pad_chunks + 1
    def oldest(a, n):
        return lax.broadcasted_iota(jnp.int32, (n, lanes), 0) + a > t_loc

    def newest(a, n):
        return lax.broadcasted_iota(jnp.int32, (n, lanes), 0) + a <= t_loc

    for gg in groups:
        parts = []
        for c in range(n_win):
            s_w = _dot_nt(kw3_ref[gg, qi + c], q_bf[gg])
            pen = None
            if c < pad_chunks:
                pen = jnp.full((1, lanes), jnp.where(qi + c >= pad_chunks, 0.0, NEG_BIG), F32)
            parts.append((s_w, oldest if c == 0 else (newest if c == n_win - 1 else None), pen))
        e_w, l_w = _softmax_parts(parts)
        o_w = _dot_nn(vwt3_ref[gg, qi], e_w[0:tq])
        for c in range(1, n_win):
            o_w = o_w + _dot_nn(vwt3_ref[gg, qi + c], e_w[c * tq:(c + 1) * tq])
        o_w = o_w / l_w

        g_t = _sigmoid(gate_ref[0, :, grp(gg)].T)

        def gate_row(branch, g_t=g_t):
            return jnp.concatenate([g_t[3 * r + branch:3 * r + branch + 1, :] for r in range(rep)], axis=1)

        out_t = gate_row(0) * o_c[gg] + gate_row(1) * os_ref[gg] + gate_row(2) * o_w
        for r in range(rep):
            o_ref[0, :, grp(gg * rep + r)] = out_t[:, r * tq:(r + 1) * tq].T.astype(o_ref.dtype)


def _nsa(z3, cmp_kv, cos, sin_s, ovl_t, tq=256):
    b, s, _ = z3.shape
    g = NSA_KV_GROUPS
    n_sel = s // NSA_SEL_BLOCK
    n_cmp = cmp_kv.shape[3]
    qw = NSA_HEADS * HEAD_DIM
    kvw = g * HEAD_DIM
    n_chunk_pad = (s + NSA_WINDOW) // tq

    def col(off):
        return lambda bi, qi: (bi, 0, off // kvw)

    return pl.pallas_call(
        functools.partial(_nsa_kernel, n_sel=n_sel, scale=HEAD_DIM ** -0.5),
        grid=(b, s // tq),
        in_specs=[
            pl.BlockSpec((1, tq, qw), lambda bi, qi: (bi, qi, OFF_CQ // qw)),
            pl.BlockSpec((1, s, kvw), col(OFF_CKS)),
            pl.BlockSpec((1, s, kvw), col(OFF_CVS)),
            pl.BlockSpec((1, s, kvw), col(OFF_CKW)),
            pl.BlockSpec((1, s, kvw), col(OFF_CVW)),
            pl.BlockSpec((1, 1, g, n_cmp, HEAD_DIM), lambda bi, qi: (0, bi, 0, 0, 0)),
            pl.BlockSpec((1, 1, g, n_cmp, HEAD_DIM), lambda bi, qi: (1, bi, 0, 0, 0)),
            pl.BlockSpec((1, tq, g * LANES), lambda bi, qi: (bi, qi, OFF_CGATE // (g * LANES))),
            pl.BlockSpec((s, LANES), lambda bi, qi: (0, 0)),
            pl.BlockSpec((s, LANES), lambda bi, qi: (0, 0)),
            pl.BlockSpec((tq, LANES), lambda bi, qi: (qi, 0)),
            pl.BlockSpec((tq, LANES), lambda bi, qi: (qi, 0)),
            pl.BlockSpec((n_sel, n_cmp), lambda bi, qi: (0, 0)),
        ],
        out_specs=pl.BlockSpec((1, tq, qw), lambda bi, qi: (bi, qi, 0)),
        out_shape=jax.ShapeDtypeStruct((b, s, qw), BF16),
        scratch_shapes=[
            pltpu.VMEM((g, s, HEAD_DIM), BF16),
            pltpu.VMEM((g, HEAD_DIM, s), BF16),
            pltpu.VMEM((g, n_chunk_pad, tq, HEAD_DIM), BF16),
            pltpu.VMEM((g, n_chunk_pad, HEAD_DIM, tq), BF16),
            pltpu.VMEM((g, HEAD_DIM, n_cmp), BF16),
            pltpu.VMEM((g, HEAD_DIM, NSA_REP * tq), F32),
        ],
        compiler_params=_cparams(("parallel", "arbitrary")),
        name="nsa",
    )(z3, z3, z3, z3, z3, cmp_kv, cmp_kv, z3, cos, sin_s, cos, sin_s, ovl_t)


def _merge_kernel(ya_ref, yb_ref, yc_ref, ga_ref, gb_ref, gc_ref, wa_ref, wb_ref, wc_ref, o_ref,
                  wa_bf_ref, wb_bf_ref, wc_bf_ref):
    @pl.when(pl.program_id(1) == 0)
    def _():
        wa_bf_ref[...] = wa_ref[...].astype(BF16)
        wb_bf_ref[...] = wb_ref[...].astype(BF16)
        wc_bf_ref[...] = wc_ref[...].astype(BF16)

    m = ga_ref[...].astype(F32) * _dot_nn(ya_ref[...], wa_bf_ref[...])
    m = m + gb_ref[...].astype(F32) * _dot_nn(yb_ref[...], wb_bf_ref[...])
    m = m + gc_ref[...].astype(F32) * _dot_nn(yc_ref[...], wc_bf_ref[...])
    o_ref[...] = m.astype(o_ref.dtype)


def _merge(ya, yb, yc, gates, wa, wb, wc, layer, tm=1024, tn=1024):
    t = ya.shape[0]
    d = wa.shape[2]
    nj = d // tn

    def gate_spec(branch):
        return pl.BlockSpec((tm, tn), lambda j, i: (i, branch * nj + j))

    def y_spec(y):
        return pl.BlockSpec((tm, y.shape[1]), lambda j, i: (i, 0))

    def w_spec(w):
        return pl.BlockSpec((None, w.shape[1], tn), lambda j, i: (layer, 0, j), pipeline_mode=pl.Buffered(1))

    return pl.pallas_call(
        _merge_kernel,
        grid=(nj, t // tm),
        in_specs=[y_spec(ya), y_spec(yb), y_spec(yc), gate_spec(0), gate_spec(1), gate_spec(2),
                  w_spec(wa), w_spec(wb), w_spec(wc)],
        out_specs=pl.BlockSpec((tm, tn), lambda j, i: (i, j)),
        out_shape=jax.ShapeDtypeStruct((t, d), BF16),
        scratch_shapes=[pltpu.VMEM((w.shape[1], tn), BF16) for w in (wa, wb, wc)],
        compiler_params=_cparams(("parallel", "arbitrary")),
        name="merge",
    )(ya, yb, yc, gates, gates, gates, wa, wb, wc)


def _proj_residual_kernel(m_ref, w_ref, x_ref, o_ref, w_bf_ref):
    @pl.when(pl.program_id(1) == 0)
    def _():
        w_bf_ref[...] = w_ref[...].astype(BF16)

    o_ref[...] = x_ref[...] + _dot_nn(m_ref[...], w_bf_ref[...])


def _proj_residual(m, w, x2d, layer, tm=1024, tn=1024):
    t, k = m.shape
    d = w.shape[2]
    return pl.pallas_call(
        _proj_residual_kernel,
        grid=(d // tn, t // tm),
        in_specs=[
            pl.BlockSpec((tm, k), lambda j, i: (i, 0)),
            pl.BlockSpec((None, k, tn), lambda j, i: (layer, 0, j), pipeline_mode=pl.Buffered(1)),
            pl.BlockSpec((tm, tn), lambda j, i: (i, j)),
        ],
        out_specs=pl.BlockSpec((tm, tn), lambda j, i: (i, j)),
        out_shape=jax.ShapeDtypeStruct((t, d), F32),
        scratch_shapes=[pltpu.VMEM((k, tn), BF16)],
        compiler_params=_cparams(("parallel", "arbitrary")),
        name="out_proj",
    )(m, w, x2d)


def _rope_tables(pos, dim, width):
    inv = 1.0 / (ROPE_THETA ** (jnp.arange(0, dim, 2, dtype=F32) / dim))
    ang = pos.astype(F32)[:, None] * inv[None, :]
    cos = jnp.cos(ang)
    sin = jnp.sin(ang)
    cos = jnp.concatenate([cos, cos], axis=-1)
    sin_s = jnp.concatenate([-sin, sin], axis=-1)
    pad = ((0, 0), (0, width - dim))
    return jnp.pad(cos, pad), jnp.pad(sin_s, pad)


def _repack_kernel(srcp_ref, srcq_ref, start_ref, valid_ref, p_ref, q_ref, o_ref, *, n_src_cols):
    j = pl.program_id(1)
    start = start_ref[j]
    valid = valid_ref[j]
    row = lax.broadcasted_iota(jnp.int32, (LANES, LANES), 0)
    col = lax.broadcasted_iota(jnp.int32, (LANES, LANES), 1)
    lane = lax.broadcasted_iota(jnp.int32, (1, LANES), 1)

    def contribution(blk_ref, src_blk):
        take = jnp.where((src_blk * LANES + row == start + col) & (col < valid), 1.0, 0.0).astype(BF16)
        blk = jnp.where(src_blk * LANES + lane < n_src_cols, blk_ref[...], 0.0).astype(BF16)
        return _dot_nn(blk, take)

    o_ref[...] = (contribution(p_ref, srcp_ref[j]) + contribution(q_ref, srcq_ref[j])).astype(o_ref.dtype)


def _repack_tables(pieces, n_out_blocks, n_src_blocks):
    starts, valids = [], []
    for start, width in pieces:
        if width == 0:
            starts.append(0)
            valids.append(0)
        for off in range(0, width, LANES):
            starts.append(start + off)
            valids.append(min(LANES, width - off))
    starts += [0] * (n_out_blocks - len(starts))
    valids += [0] * (n_out_blocks - len(valids))
    assert n_src_blocks >= 2
    slots = [0, 1]
    srcp, srcq = [], []
    for start, valid in zip(starts, valids):
        need = []
        if valid > 0:
            need = list(range(start // LANES, (start + valid - 1) // LANES + 1))
        for blk in need:
            if blk not in slots:
                slots[0 if slots[0] not in need else 1] = blk
        srcp.append(slots[0])
        srcq.append(slots[1])
    return tuple(jnp.asarray(np.asarray(v, np.int32)) for v in (srcp, srcq, starts, valids))


def _repack(w, pieces, n_out_blocks):
    depth, k, n = w.shape
    tables = _repack_tables(pieces, n_out_blocks, -(-n // LANES))
    grid_spec = pltpu.PrefetchScalarGridSpec(
        num_scalar_prefetch=4,
        grid=(depth, n_out_blocks),
        in_specs=[
            pl.BlockSpec((None, k, LANES), lambda d, j, sp, sq, st, va: (d, 0, sp[j])),
            pl.BlockSpec((None, k, LANES), lambda d, j, sp, sq, st, va: (d, 0, sq[j])),
        ],
        out_specs=pl.BlockSpec((None, k, LANES), lambda d, j, sp, sq, st, va: (d, 0, j)),
    )
    return pl.pallas_call(
        functools.partial(_repack_kernel, n_src_cols=n),
        grid_spec=grid_spec,
        out_shape=jax.ShapeDtypeStruct((depth, k, n_out_blocks * LANES), BF16),
        compiler_params=_cparams(("parallel", "arbitrary")),
        name="repack",
    )(*tables, w, w)


def _repack_t_kernel(srcp_ref, srcq_ref, start_ref, valid_ref, p_ref, q_ref, o_ref, *, n_src_cols):
    j = pl.program_id(0)
    start = start_ref[j]
    valid = valid_ref[j]
    depth = o_ref.shape[0]
    rows = LANES * depth
    out_row = lax.broadcasted_iota(jnp.int32, (LANES, rows), 0)
    flat = lax.broadcasted_iota(jnp.int32, (LANES, rows), 1)
    sub = lax.broadcasted_iota(jnp.int32, (rows, 1), 0)

    def flattened(blk_ref, src_blk):
        x = blk_ref[...].reshape(rows, blk_ref.shape[2])
        return jnp.where(src_blk * LANES + sub // depth < n_src_cols, x, 0.0).astype(BF16)

    def take(src_blk, layer):
        hit = (flat == (start + out_row - src_blk * LANES) * depth + layer) & (out_row < valid)
        return jnp.where(hit, 1.0, 0.0).astype(BF16)

    xp, xq = flattened(p_ref, srcp_ref[j]), flattened(q_ref, srcq_ref[j])
    for layer in range(depth):
        o_ref[layer] = (_dot_nn(take(srcp_ref[j], layer), xp)
                        + _dot_nn(take(srcq_ref[j], layer), xq)).astype(o_ref.dtype)


def _repack_t(wt, pieces, n_out_blocks):
    n, depth, k = wt.shape
    tables = _repack_tables(pieces, n_out_blocks, -(-n // LANES))
    grid_spec = pltpu.PrefetchScalarGridSpec(
        num_scalar_prefetch=4,
        grid=(n_out_blocks,),
        in_specs=[
            pl.BlockSpec((LANES, depth, k), lambda j, sp, sq, st, va: (sp[j], 0, 0)),
            pl.BlockSpec((LANES, depth, k), lambda j, sp, sq, st, va: (sq[j], 0, 0)),
        ],
        out_specs=pl.BlockSpec((depth, LANES, k), lambda j, sp, sq, st, va: (0, j, 0)),
    )
    return pl.pallas_call(
        functools.partial(_repack_t_kernel, n_src_cols=n),
        grid_spec=grid_spec,
        out_shape=jax.ShapeDtypeStruct((depth, n_out_blocks * LANES, k), BF16),
        compiler_params=_cparams(("arbitrary",)),
        name="repack_t",
    )(*tables, wt, wt)


def _pack_w_in(w):
    sizes = [MOBA_HEADS * HEAD_DIM] * 3 + [MLA_Q_RANK, MLA_KV_RANK, MLA_ROPE, NSA_HEADS * HEAD_DIM] \
        + [NSA_KV_GROUPS * HEAD_DIM] * 6
    offs = [int(v) for v in np.concatenate([[0], np.cumsum(sizes)])]
    (a_q, a_k, a_v, b_cq, b_ckv, b_kr, c_q, c_kc, c_vc, c_ks, c_vs, c_kw, c_vw) = [
        (offs[i], sizes[i]) for i in range(len(sizes))]
    gate0 = offs[-1]
    per_group = NSA_REP * 3
    gates = [(gate0 + gi * per_group, per_group) for gi in range(NSA_KV_GROUPS)]
    merge = (gate0 + NSA_HEADS * 3, w.shape[2] - gate0 - NSA_HEADS * 3)
    pieces = [c_q, b_ckv, b_cq, a_q, a_k, a_v, c_kc, c_vc, c_ks, c_vs, c_kw, c_vw] + gates + [b_kr]
    used = sum(-(-width // LANES) for _, width in pieces)
    pieces += [(0, 0)] * (Z_COLS // LANES - used) + [merge]
    return _repack_t(jnp.transpose(w, (2, 0, 1)), pieces, (Z_COLS + merge[1]) // LANES)


def _pack_w_uq(w):
    per_head = MLA_NOPE + MLA_ROPE
    pieces = [(h * per_head, MLA_NOPE) for h in range(MLA_HEADS)] \
        + [(h * per_head + MLA_NOPE, MLA_ROPE) for h in range(MLA_HEADS)]
    return _repack(w, pieces, 2 * MLA_HEADS)


def _pack_w_ukv(w):
    per_head = MLA_NOPE + MLA_V
    pieces = [(h * per_head, MLA_NOPE) for h in range(MLA_HEADS)] \
        + [(h * per_head + MLA_NOPE, MLA_V) for h in range(MLA_HEADS)]
    return _repack(w, pieces, 2 * MLA_HEADS)


def kernel(x, ffn1_norm, ffn1_w_gate, ffn1_w_up, ffn1_w_down, mix_norm, w_in, mla_q_norm, mla_w_uq, mla_kv_norm, mla_w_ukv, nsa_cmp_pos_k, nsa_cmp_w1_k, nsa_cmp_w2_k, nsa_cmp_pos_v, nsa_cmp_w1_v, nsa_cmp_w2_v, w_branch_moba, w_branch_mla, w_branch_nsa, w_out, ffn2_norm, ffn2_w_gate, ffn2_w_up, ffn2_w_down, final_norm):
    b, s, d = x.shape
    depth = w_in.shape[0]
    t = b * s
    assert NSA_CMP_LEN == 2 * NSA_CMP_STRIDE and s % MOBA_BLOCK == 0 and s % NSA_SEL_BLOCK == 0

    pos = jnp.arange(s)
    cos_h, sin_h = _rope_tables(pos, HEAD_DIM, LANES)
    cos_r, sin_r = _rope_tables(pos, MLA_ROPE, LANES)
    n16 = s // NSA_CMP_STRIDE
    cmp_end = np.arange(n16) * NSA_CMP_STRIDE + NSA_CMP_LEN - 1
    cos_c, sin_c = _rope_tables(jnp.asarray(cmp_end), HEAD_DIM, LANES)
    n_sel = s // NSA_SEL_BLOCK
    sel_start = np.arange(n_sel) * NSA_SEL_BLOCK
    cmp_start = np.arange(n16) * NSA_CMP_STRIDE
    ovl_t = ((cmp_start[None, :] <= sel_start[:, None] + NSA_SEL_BLOCK - 1)
             & (cmp_end[None, :] >= sel_start[:, None])).astype(np.float32)
    ovl_t = jnp.asarray(ovl_t, BF16)

    w_in_p = _pack_w_in(w_in)
    w_uq_p = _pack_w_uq(mla_w_uq)
    w_ukv_p = _pack_w_ukv(mla_w_ukv)

    xf = x.reshape(t, d)
    for i in range(depth):
        xf = _ffn(xf, ffn1_norm[i], ffn1_w_gate, ffn1_w_up, ffn1_w_down, i)

        z, gates = _in_proj(xf, mix_norm[i], w_in_p, i)
        z3 = z.reshape(b, s, Z_COLS)
        y_a = _moba(z3, cos_h, sin_h)
        y_b = _mla(z3, mla_q_norm[i], mla_kv_norm[i], w_uq_p, w_ukv_p, i, cos_r, sin_r)

        w1 =jnp.stack([nsa_cmp_w1_k[i], nsa_cmp_w1_v[i]])
        w2 = jnp.stack([nsa_cmp_w2_k[i], nsa_cmp_w2_v[i]])
        posflat = jnp.stack([nsa_cmp_pos_k[i].reshape(1, -1), nsa_cmp_pos_v[i].reshape(1, -1)])
        cmp_kv = _nsa_compress(z3, w1, w2, posflat, cos_c, sin_c)
        y_c = _nsa(z3, cmp_kv, cos_h, sin_h, ovl_t)

        merged = _merge(y_a.reshape(t, -1), y_b.reshape(t, -1), y_c.reshape(t, -1), gates,
                        w_branch_moba, w_branch_mla, w_branch_nsa, i)
        xf = _proj_residual(merged, w_out, xf, i)

        xf = _ffn(xf, ffn2_norm[i], ffn2_w_gate, ffn2_w_up, ffn2_w_down, i)
    return _rms(xf, final_norm, F32).reshape(b, s, d)
```

```python
import functools

import numpy as np
import jax
import jax.numpy as jnp
from jax import lax
from jax.experimental import pallas as pl
from jax.experimental.pallas import tpu as pltpu

F32 = jnp.float32
BF16 = jnp.bfloat16

HEAD_DIM = 128
ROPE_THETA = 10000.0
NORM_EPS = 1e-6
NEG_BIG = -1e30
TINY = 1e-30
LOG2E = 1.4426950408889634
N_BRANCHES = 3

MOBA_HEADS = 4
MOBA_BLOCK = 256
MOBA_TOPK = 3

MLA_HEADS = 6
MLA_Q_RANK = 512
MLA_KV_RANK = 256
MLA_NOPE = 128
MLA_ROPE = 64
MLA_V = 128

NSA_HEADS = 6
NSA_KV_GROUPS = 2
NSA_REP = NSA_HEADS // NSA_KV_GROUPS
NSA_CMP_LEN = 32
NSA_CMP_STRIDE = 16
NSA_CMP_HIDDEN = 256
NSA_SEL_BLOCK = 64
NSA_SEL_TOPK = 16
NSA_WINDOW = 512
FORCE_SCORE = 1e9

LANES = 128
SUBLANES = 8
SOFTMAX_CHUNK = 64
VMEM_LIMIT = 56 * 2**20

OFF_CQ = 0
OFF_BCKV = 768
OFF_BCQ = 1024
OFF_AQ = 1536
OFF_AK = 2048
OFF_AV = 2560
OFF_CKC = 3072
OFF_CVC = 3328
OFF_CKS = 3584
OFF_CVS = 3840
OFF_CKW = 4096
OFF_CVW = 4352
OFF_CGATE = 4608
OFF_BKR = 4864
Z_COLS = 5120


def _cparams(sem):
    return pltpu.CompilerParams(dimension_semantics=sem, vmem_limit_bytes=VMEM_LIMIT)


def _rmsnorm(x, g):
    ms = jnp.mean(x * x, axis=-1, keepdims=True)
    return x * lax.rsqrt(ms + NORM_EPS) * g


def _sigmoid(x):
    return 0.5 * jnp.tanh(0.5 * x) + 0.5


def _rope_full(x, cos, sin_signed):
    return x * cos + pltpu.roll(x, HEAD_DIM // 2, axis=1) * sin_signed


def _rope_half(x, cos, sin_signed):
    lane = lax.broadcasted_iota(jnp.int32, x.shape, 1)
    q = MLA_ROPE // 2
    swapped = jnp.where(lane < q, pltpu.roll(x, LANES - q, axis=1), pltpu.roll(x, q, axis=1))
    return x * cos + swapped * sin_signed


def _dot_nt(a, b):
    return lax.dot_general(a, b, (((1,), (1,)), ((), ())), preferred_element_type=F32)


def _dot_nn(a, b):
    return jnp.dot(a, b, preferred_element_type=F32)


def _split(a):
    hi = a.astype(BF16)
    lo = (a - hi.astype(F32)).astype(BF16)
    return hi, lo


def _dot3_nt(a, b):
    ah, al = _split(a)
    bh, bl = _split(b)
    return _dot_nt(ah, bh) + _dot_nt(ah, bl) + _dot_nt(al, bh)


def _dot3_nn(a, b):
    ah, al = _split(a)
    bh, bl = _split(b)
    return _dot_nn(ah, bh) + _dot_nn(ah, bl) + _dot_nn(al, bh)


def _softmax_t(s, mask, scale):
    s = jnp.where(mask, s * scale, NEG_BIG)
    m = jnp.max(s, axis=0, keepdims=True)
    e = jnp.where(mask, jnp.exp(s - m), 0.0)
    l = jnp.maximum(jnp.sum(e, axis=0, keepdims=True), TINY)
    return e, l


def _softmax_parts(parts, chunk=SOFTMAX_CHUNK):
    lanes = parts[0][0].shape[1]
    chunk = max(SUBLANES, chunk * 384 // max(lanes, 384))
    pieces = []
    for s, mask_fn, pen in parts:
        for a in range(0, s.shape[0], chunk):
            n = min(chunk, s.shape[0] - a)
            p = s[a:a + n]
            pieces.append((p if mask_fn is None else jnp.where(mask_fn(a, n), p, NEG_BIG), pen))

    def fold(x, op):
        return op(x.reshape(x.shape[0] // SUBLANES, SUBLANES, lanes), axis=0)

    def col_max(p, pen):
        m = fold(p, jnp.max)
        return m if pen is None else m + pen

    m8 = functools.reduce(jnp.maximum, [col_max(p, pen) for p, pen in pieces])
    mx = jnp.max(m8, axis=0, keepdims=True)
    es, l8 = [], None
    for p, pen in pieces:
        e = jnp.exp2(p - (mx if pen is None else mx - pen))
        l8 = fold(e, jnp.sum) if l8 is None else l8 + fold(e, jnp.sum)
        es.append(e.astype(BF16))
    l = jnp.sum(l8, axis=0, keepdims=True)
    e = jnp.concatenate(es, axis=0) if len(es) > 1 else es[0]
    return e, jnp.maximum(l, TINY)


def _rank_t(v, nb):
    blk = lax.broadcasted_iota(jnp.int32, v.shape, 0)
    rank = jnp.zeros(v.shape, F32)
    for j in range(nb):
        vj = v[j:j + 1, :]
        ge = (vj >= v).astype(F32)
        gt = (vj > v).astype(F32)
        rank = rank + jnp.where(blk > j, ge, gt)
    return rank


def _rms_kernel(x_ref, g_ref, o_ref):
    o_ref[...] = _rmsnorm(x_ref[...], g_ref[...]).astype(o_ref.dtype)


def _rms_cast_kernel(x_ref, g_ref, w_ref, o_ref, w_bf_ref):
    o_ref[...] = _rmsnorm(x_ref[...], g_ref[...]).astype(o_ref.dtype)
    w_bf_ref[...] = w_ref[...].astype(w_bf_ref.dtype)


def _rms(x2d, g, out_dtype, tm=512, cast=None):
    t, d = x2d.shape
    steps = t // tm
    in_specs = [pl.BlockSpec((tm, d), lambda i: (i, 0)), pl.BlockSpec((1, d), lambda i: (0, 0))]
    out_specs = pl.BlockSpec((tm, d), lambda i: (i, 0))
    out_shape = jax.ShapeDtypeStruct((t, d), out_dtype)
    if cast is None:
        return pl.pallas_call(
            _rms_kernel, grid=(steps,), in_specs=in_specs, out_specs=out_specs, out_shape=out_shape,
            compiler_params=_cparams(("parallel",)), name="rmsnorm",
        )(x2d, g.reshape(1, d))
    w, layer = cast
    rows, cols = w.shape[1:]
    slab = rows // steps
    assert slab * steps == rows and slab % (2 * SUBLANES) == 0
    return pl.pallas_call(
        _rms_cast_kernel,
        grid=(steps,),
        in_specs=in_specs + [pl.BlockSpec((None, slab, cols), lambda i: (layer, i, 0))],
        out_specs=[out_specs, pl.BlockSpec((slab, cols), lambda i: (i, 0))],
        out_shape=[out_shape, jax.ShapeDtypeStruct((rows, cols), BF16)],
        compiler_params=_cparams(("parallel",)),
        name="rmsnorm_cast",
    )(x2d, g.reshape(1, d), w)


def _ffn_up_kernel(h_ref, wg_ref, wu_ref, o_ref, wg_bf_ref, wu_bf_ref):
    @pl.when(pl.program_id(1) == 0)
    def _():
        wg_bf_ref[...] = wg_ref[...].astype(BF16)
        wu_bf_ref[...] = wu_ref[...].astype(BF16)

    h = h_ref[...]
    a = _dot_nn(h, wg_bf_ref[...])
    u = _dot_nn(h, wu_bf_ref[...])
    o_ref[...] = (a * _sigmoid(a) * u).astype(o_ref.dtype)


def _ffn_down_kernel(a_ref, wd_ref, x_ref, o_ref):
    o_ref[...] = x_ref[...] + 0.5 * _dot_nn(a_ref[...], wd_ref[...])


def _ffn(x2d, g, wg, wu, wd, layer, tm=1024, tf=512, tm_down=512, tn=1024):
    t, d = x2d.shape
    f = wg.shape[2]
    h, wd_bf = _rms(x2d, g, BF16, cast=(wd, layer))
    act = pl.pallas_call(
        _ffn_up_kernel,
        grid=(f // tf, t // tm),
        in_specs=[
            pl.BlockSpec((tm, d), lambda j, i: (i, 0)),
            pl.BlockSpec((None, d, tf), lambda j, i: (layer, 0, j)),
            pl.BlockSpec((None, d, tf), lambda j, i: (layer, 0, j)),
        ],
        out_specs=pl.BlockSpec((tm, tf), lambda j, i: (i, j)),
        out_shape=jax.ShapeDtypeStruct((t, f), BF16),
        scratch_shapes=[pltpu.VMEM((d, tf), BF16), pltpu.VMEM((d, tf), BF16)],
        compiler_params=_cparams(("parallel", "arbitrary")),
        name="ffn_up",
    )(h, wg, wu)
    return pl.pallas_call(
        _ffn_down_kernel,
        grid=(d // tn, t // tm_down),
        in_specs=[
            pl.BlockSpec((tm_down, f), lambda n, i: (i, 0)),
            pl.BlockSpec((f, tn), lambda n, i: (0, n)),
            pl.BlockSpec((tm_down, tn), lambda n, i: (i, n)),
        ],
        out_specs=pl.BlockSpec((tm_down, tn), lambda n, i: (i, n)),
        out_shape=jax.ShapeDtypeStruct((t, d), F32),
        compiler_params=_cparams(("parallel", "parallel")),
        name="ffn_down",
    )(act, wd_bf, x2d)


def _in_proj_kernel(x_ref, g_ref, w_ref, z_ref, gate_ref, h_ref, *, n_z):
    j = pl.program_id(1)

    @pl.when(j == 0)
    def _():
        h_ref[...] = _rmsnorm(x_ref[...], g_ref[...]).astype(BF16)

    @pl.when(j < n_z)
    def _():
        z_ref[...] = _dot_nt(h_ref[...], w_ref[...])

    @pl.when(j >= n_z)
    def _():
        gate_ref[...] = _sigmoid(_dot_nt(h_ref[...], w_ref[...])).astype(gate_ref.dtype)


def _in_proj(x2d, g, w, layer, tm=1024, tn=1024):
    t, d = x2d.shape
    n = w.shape[1]
    n_z = Z_COLS // tn
    n_g = (n - Z_COLS) // tn
    return pl.pallas_call(
        functools.partial(_in_proj_kernel, n_z=n_z),
        grid=(t // tm, n_z + n_g),
        in_specs=[
            pl.BlockSpec((tm, d), lambda i, j: (i, 0)),
            pl.BlockSpec((1, d), lambda i, j: (0, 0)),
            pl.BlockSpec((None, tn, d), lambda i, j: (layer, j, 0)),
        ],
        out_specs=[
            pl.BlockSpec((tm, tn), lambda i, j: (i, jnp.minimum(j, n_z - 1))),
            pl.BlockSpec((tm, tn), lambda i, j: (i, jnp.maximum(j - n_z, 0))),
        ],
        out_shape=[jax.ShapeDtypeStruct((t, Z_COLS), F32), jax.ShapeDtypeStruct((t, n - Z_COLS), BF16)],
        scratch_shapes=[pltpu.VMEM((tm, d), BF16)],
        compiler_params=_cparams(("parallel", "arbitrary")),
        name="in_proj",
    )(x2d, g.reshape(1, d), w)


def _moba_kernel(q_ref, k_ref, v_ref, cosf_ref, sinf_ref, cosq_ref, sinq_ref, o_ref,
                 kr_ref, km_ref, vt_ref, *, n_blk, scale):
    qi = pl.program_id(2)
    tq = q_ref.shape[1]
    hp = q_ref.shape[2] // LANES
    head = lambda hh: slice(hh * LANES, (hh + 1) * LANES)

    @pl.when(qi == 0)
    def _():
        for hh in range(hp):
            kr = _rope_full(k_ref[0, :, head(hh)], cosf_ref[...], sinf_ref[...])
            kr_ref[hh] = kr.astype(BF16)
            km_ref[hh] = jnp.concatenate(
                [jnp.mean(kr[j * MOBA_BLOCK:(j + 1) * MOBA_BLOCK], axis=0, keepdims=True) for j in range(n_blk)],
                axis=0)
            vt_ref[hh] = v_ref[0, :, head(hh)].T.astype(BF16)

    blk = MOBA_BLOCK
    bpt = tq // blk
    lane = lax.broadcasted_iota(jnp.int32, (1, tq), 1)
    lane_sub = lane // blk
    lane_loc = lane % blk

    for n in range(n_blk // bpt):
        @pl.when(qi == n)
        def _(n=n):
            first = n * bpt
            ranked = first + bpt - 1 > MOBA_TOPK
            for hh in range(hp):
                qr = _rope_full(q_ref[0, :, head(hh)], cosq_ref[...], sinq_ref[...])
                q_bf = (qr * (scale * LOG2E)).astype(BF16)
                s = _dot_nt(kr_ref[hh, 0:(n + 1) * tq], q_bf)
                drop = None
                if ranked:
                    gate = _dot3_nt(km_ref[hh], qr)
                    past = lax.broadcasted_iota(jnp.int32, (n_blk, tq), 0) < first + lane_sub
                    gate = jnp.where(past, gate, -jnp.inf)
                    drop = jnp.where(_rank_t(gate, first + bpt - 1) < MOBA_TOPK, 0.0, NEG_BIG)
                parts = []
                if drop is not None:
                    for j in range(first):
                        parts.append((s[j * blk:(j + 1) * blk], None, drop[j:j + 1, :]))
                elif first > 0:
                    parts.append((s[0:first * blk], None, None))
                for i2 in range(bpt):
                    j = first + i2
                    pen = None
                    if drop is not None and i2 < bpt - 1:
                        pen = jnp.where(lane_sub > i2, drop[j:j + 1, :], 0.0)

                    limit = jnp.where(lane_sub > i2, blk, jnp.where(lane_sub == i2, lane_loc, -1))

                    def mask_fn(a, rows, limit=limit):
                        return lax.broadcasted_iota(jnp.int32, (rows, tq), 0) + a <= limit

                    parts.append((s[j * blk:(j + 1) * blk], mask_fn, pen))
                e, l = _softmax_parts(parts)
                o_t = _dot_nn(vt_ref[hh, :, 0:(n + 1) * tq], e) / l
                o_ref[0, :, head(hh)] = o_t.T.astype(o_ref.dtype)


def _moba(z3, cos, sin_s, tq=2 * MOBA_BLOCK, hp=2):
    b, s, _ = z3.shape
    n_blk = s // MOBA_BLOCK
    w = hp * LANES
    qb, kb, vb = OFF_AQ // w, OFF_AK // w, OFF_AV // w
    return pl.pallas_call(
        functools.partial(_moba_kernel, n_blk=n_blk, scale=HEAD_DIM ** -0.5),
        grid=(b, MOBA_HEADS // hp, s // tq),
        in_specs=[
            pl.BlockSpec((1, tq, w), lambda bi, h, qi: (bi, qi, qb + h)),
            pl.BlockSpec((1, s, w), lambda bi, h, qi: (bi, 0, kb + h)),
            pl.BlockSpec((1, s, w), lambda bi, h, qi: (bi, 0, vb + h)),
            pl.BlockSpec((s, LANES), lambda bi, h, qi: (0, 0)),
            pl.BlockSpec((s, LANES), lambda bi, h, qi: (0, 0)),
            pl.BlockSpec((tq, LANES), lambda bi, h, qi: (qi, 0)),
            pl.BlockSpec((tq, LANES), lambda bi, h, qi: (qi, 0)),
        ],
        out_specs=pl.BlockSpec((1, tq, w), lambda bi, h, qi: (bi, qi, h)),
        out_shape=jax.ShapeDtypeStruct((b, s, MOBA_HEADS * HEAD_DIM), BF16),
        scratch_shapes=[pltpu.VMEM((hp, s, LANES), BF16), pltpu.VMEM((hp, n_blk, LANES), F32),
                        pltpu.VMEM((hp, LANES, s), BF16)],
        compiler_params=_cparams(("parallel", "parallel", "arbitrary")),
        name="moba",
    )(z3, z3, z3, cos, sin_s, cos, sin_s)


def _mla_kernel(cq_ref, ckv_ref, kr_ref, qg_ref, kvg_ref, wqn_ref, wqr_ref, wkn_ref, wv_ref,
                cosf_ref, sinf_ref, cosq_ref, sinq_ref, o_ref, kf_ref, vt_ref, *, scale):
    qi = pl.program_id(2)
    tq = cq_ref.shape[1]
    s_len = ckv_ref.shape[1]
    hp = wqn_ref.shape[1] // LANES
    head = lambda hh: slice(hh * LANES, (hh + 1) * LANES)

    @pl.when(qi == 0)
    def _():
        ckv = _rmsnorm(ckv_ref[0], kvg_ref[...]).astype(BF16)
        k_nope = _dot_nn(ckv, wkn_ref[...])
        v = _dot_nn(ckv, wv_ref[...])
        k_rope = _rope_half(kr_ref[0], cosf_ref[...], sinf_ref[...]).astype(BF16)
        for hh in range(hp):
            kf_ref[hh, :, :LANES] = k_nope[:, head(hh)].astype(BF16)
            kf_ref[hh, :, LANES:] = k_rope
            vt_ref[hh] = v[:, head(hh)].T.astype(BF16)

    cq = _rmsnorm(cq_ref[0], qg_ref[...]).astype(BF16)
    q_nope = _dot_nn(cq, wqn_ref[...])
    q_rope = _dot_nn(cq, wqr_ref[...])
    def causal(a, n):
        return (lax.broadcasted_iota(jnp.int32, (n, tq), 0) + a
                <= lax.broadcasted_iota(jnp.int32, (n, tq), 1))

    for n in range(s_len // tq):
        @pl.when(qi == n)
        def _(n=n):
            for hh in range(hp):
                q = jnp.concatenate([q_nope[:, head(hh)],
                                     _rope_half(q_rope[:, head(hh)], cosq_ref[...], sinq_ref[...])], axis=1)
                q_bf = (q * (scale * LOG2E)).astype(BF16)
                s = _dot_nt(kf_ref[hh, 0:(n + 1) * tq], q_bf)
                parts = [(s[0:n * tq], None, None)] if n > 0 else []
                parts.append((s[n * tq:], causal, None))
                e, l = _softmax_parts(parts)
                o_t = _dot_nn(vt_ref[hh, :, 0:(n + 1) * tq], e) / l
                o_ref[0, :, head(hh)] = o_t.T.astype(o_ref.dtype)


def _mla(z3, q_gain, kv_gain, w_uq, w_ukv, layer, cos, sin_s, tq=512, hp=3):
    b, s, _ = z3.shape
    w = hp * LANES
    ng = MLA_HEADS // hp
    return pl.pallas_call(
        functools.partial(_mla_kernel, scale=(MLA_NOPE + MLA_ROPE) ** -0.5),
        grid=(b, ng, s // tq),
        in_specs=[
            pl.BlockSpec((1, tq, MLA_Q_RANK), lambda bi, h, qi: (bi, qi, OFF_BCQ // MLA_Q_RANK)),
            pl.BlockSpec((1, s, MLA_KV_RANK), lambda bi, h, qi: (bi, 0, OFF_BCKV // MLA_KV_RANK)),
            pl.BlockSpec((1, s, LANES), lambda bi, h, qi: (bi, 0, OFF_BKR // LANES)),
            pl.BlockSpec((1, MLA_Q_RANK), lambda bi, h, qi: (0, 0)),
            pl.BlockSpec((1, MLA_KV_RANK), lambda bi, h, qi: (0, 0)),
            pl.BlockSpec((None, MLA_Q_RANK, w), lambda bi, h, qi: (layer, 0, h)),
            pl.BlockSpec((None, MLA_Q_RANK, w), lambda bi, h, qi: (layer, 0, ng + h)),
            pl.BlockSpec((None, MLA_KV_RANK, w), lambda bi, h, qi: (layer, 0, h)),
            pl.BlockSpec((None, MLA_KV_RANK, w), lambda bi, h, qi: (layer, 0, ng + h)),
            pl.BlockSpec((s, LANES), lambda bi, h, qi: (0, 0)),
            pl.BlockSpec((s, LANES), lambda bi, h, qi: (0, 0)),
            pl.BlockSpec((tq, LANES), lambda bi, h, qi: (qi, 0)),
            pl.BlockSpec((tq, LANES), lambda bi, h, qi: (qi, 0)),
        ],
        out_specs=pl.BlockSpec((1, tq, w), lambda bi, h, qi: (bi, qi, h)),
        out_shape=jax.ShapeDtypeStruct((b, s, MLA_HEADS * MLA_V), BF16),
        scratch_shapes=[pltpu.VMEM((hp, s, 2 * LANES), BF16), pltpu.VMEM((hp, LANES, s), BF16)],
        compiler_params=_cparams(("parallel", "parallel", "arbitrary")),
        name="mla",
    )(z3, z3, z3, q_gain.reshape(1, -1), kv_gain.reshape(1, -1), w_uq, w_uq, w_ukv, w_ukv,
      cos, sin_s, cos, sin_s)


def _dot3_presplit(a, b_hi, b_lo):
    ah, al = _split(a)
    return _dot_nn(ah, b_hi) + _dot_nn(ah, b_lo) + _dot_nn(al, b_hi)


def _cmp_kernel(*refs):
    n_grp = len(refs) - 9
    x_refs = refs[:n_grp]
    w1_ref, w2_ref, pos_ref, cos_ref, sin_ref, o_ref, w1_hi_ref, w1_lo_ref, bias_ref = refs[n_grp:]
    kv = pl.program_id(0)
    n16 = o_ref.shape[3]
    half = NSA_CMP_STRIDE * HEAD_DIM

    @pl.when(pl.program_id(1) == 0)
    def _():
        hi, lo = _split(w1_ref[0])
        w1_hi_ref[...] = hi
        w1_lo_ref[...] = lo
        bias_ref[...] = _dot3_presplit(jnp.broadcast_to(pos_ref[0], (SUBLANES, 2 * half)), hi, lo)

    for gi, x_ref in enumerate(x_refs):
        x = jnp.concatenate(
            [x_ref[0, pl.ds(l, n16, stride=NSA_CMP_STRIDE), :] for l in range(NSA_CMP_STRIDE)], axis=1)
        first = _dot3_presplit(x, w1_hi_ref[:half], w1_lo_ref[:half])
        second = _dot3_presplit(x, w1_hi_ref[half:], w1_lo_ref[half:])
        hid = first + pltpu.roll(second, n16 - 1, axis=0) + bias_ref[0:1]
        y = _dot3_nn(jax.nn.gelu(hid), w2_ref[0])
        roped = _rope_full(y, cos_ref[...], sin_ref[...])
        o_ref[0, 0, gi] = jnp.where(kv == 0, roped, y)


def _nsa_compress(z3, w1, w2, posflat, cos_c, sin_c):
    b, s, _ = z3.shape
    g = NSA_KV_GROUPS
    n16 = s // NSA_CMP_STRIDE
    width = NSA_CMP_STRIDE * HEAD_DIM
    hid = w1.shape[2]
    base = OFF_CKC // LANES
    x_specs = [pl.BlockSpec((1, s, LANES), lambda kv, bi, gi=gi: (bi, 0, base + g * kv + gi)) for gi in range(g)]
    return pl.pallas_call(
        _cmp_kernel,
        grid=(2, b),
        in_specs=x_specs + [
            pl.BlockSpec((1, 2 * width, hid), lambda kv, bi: (kv, 0, 0)),
            pl.BlockSpec((1, hid, HEAD_DIM), lambda kv, bi: (kv, 0, 0)),
            pl.BlockSpec((1, 1, 2 * width), lambda kv, bi: (kv, 0, 0)),
            pl.BlockSpec((n16, HEAD_DIM), lambda kv, bi: (0, 0)),
            pl.BlockSpec((n16, HEAD_DIM), lambda kv, bi: (0, 0)),
        ],
        out_specs=pl.BlockSpec((1, 1, g, n16, HEAD_DIM), lambda kv, bi: (kv, bi, 0, 0, 0)),
        out_shape=jax.ShapeDtypeStruct((2, b, g, n16, HEAD_DIM), F32),
        scratch_shapes=[pltpu.VMEM((2 * width, hid), BF16), pltpu.VMEM((2 * width, hid), BF16),
                        pltpu.VMEM((SUBLANES, hid), F32)],
        compiler_params=_cparams(("parallel", "arbitrary")),
        name="nsa_compress",
    )(*([z3] * g), w1, w2, posflat, cos_c, sin_c)


def _nsa_kernel(q_ref, ks_ref, vs_ref, kw_ref, vw_ref, kc_ref, vc_ref, gate_ref,
                cosf_ref, sinf_ref, cosq_ref, sinq_ref, ovl_ref, o_ref,
                ksr_ref, vst_ref, kw3_ref, vwt3_ref, vct_ref, os_ref, *, n_sel, scale):
    qi = pl.program_id(1)
    tq = q_ref.shape[1]
    s_len = ks_ref.shape[1]
    n_grp = ks_ref.shape[2] // HEAD_DIM
    rep = NSA_REP
    lanes = rep * tq
    n_chunk = s_len // tq
    pad_chunks = NSA_WINDOW // tq
    n_cmp = kc_ref.shape[3]
    grp = lambda gg: slice(gg * HEAD_DIM, (gg + 1) * HEAD_DIM)
    groups = range(n_grp)

    @pl.when(qi == 0)
    def _():
        for gg in groups:
            ksr_ref[gg] = _rope_full(ks_ref[0, :, grp(gg)], cosf_ref[...], sinf_ref[...]).astype(BF16)
            vst_ref[gg] = vs_ref[0, :, grp(gg)].T.astype(BF16)
            kw3_ref[gg, 0:pad_chunks] = jnp.zeros((pad_chunks, tq, HEAD_DIM), BF16)
            vwt3_ref[gg, 0:pad_chunks] = jnp.zeros((pad_chunks, HEAD_DIM, tq), BF16)
            kwr = _rope_full(kw_ref[0, :, grp(gg)], cosf_ref[...], sinf_ref[...]).astype(BF16)
            for c in range(n_chunk):
                kw3_ref[gg, pad_chunks + c] = kwr[c * tq:(c + 1) * tq]
                vwt3_ref[gg, pad_chunks + c] = vw_ref[0, c * tq:(c + 1) * tq, grp(gg)].T.astype(BF16)
            vct_ref[gg] = vc_ref[0, 0, gg].T.astype(BF16)

    cos_q = cosq_ref[...]
    sin_q = sinq_ref[...]
    lane = lax.broadcasted_iota(jnp.int32, (1, lanes), 1)
    t_loc = lane % tq
    t = qi * tq + t_loc
    cur = t[:, 0:tq] // NSA_SEL_BLOCK
    blk = lax.broadcasted_iota(jnp.int32, (n_sel, tq), 0)
    forced = (blk == 0) | (blk == cur) | (blk == cur - 1)
    visible = blk <= cur
    cmp_end = (lax.broadcasted_iota(jnp.int32, (n_cmp, lanes), 0) * NSA_CMP_STRIDE + (NSA_CMP_LEN - 1))
    ovl = ovl_ref[...]

    q_bf, o_c, imp = [], [], []
    for gg in groups:
        q_all = jnp.concatenate(
            [_rope_full(q_ref[0, :, grp(gg * rep + r)], cos_q, sin_q) for r in range(rep)],
            axis=0)
        q_bf.append((q_all * (scale * LOG2E)).astype(BF16))

        s_c = _dot3_nt(kc_ref[0, 0, gg], q_all)
        e_c, l_c = _softmax_t(s_c, cmp_end <= t, scale)
        p_c = e_c / l_c
        o_c.append(_dot_nn(vct_ref[gg], p_c.astype(BF16)))

        p_hi, p_lo = _split(p_c)
        imp3 = _dot_nn(ovl, p_hi) + _dot_nn(ovl, p_lo)
        imp_g = imp3[:, 0:tq]
        for r in range(1, rep):
            imp_g = imp_g + imp3[:, r * tq:(r + 1) * tq]
        imp_g = jnp.where(forced, FORCE_SCORE, imp_g)
        imp.append(jnp.where(visible, imp_g, -jnp.inf))

    unit = tq
    sb = NSA_SEL_BLOCK
    for v in range(s_len // unit):
        @pl.when(qi == v)
        def _(v=v):
            n_keys = (v + 1) * unit
            lo = n_keys - unit

            def causal(base, a, n):
                return lax.broadcasted_iota(jnp.int32, (n, lanes), 0) <= (t - (base + a))

            n_vis = n_keys // sb
            for gg in groups:
                s = _dot_nt(ksr_ref[gg, 0:n_keys], q_bf[gg])
                if n_vis > NSA_SEL_TOPK:
                    rank = _rank_t(imp[gg], n_vis)
                    rank3 = jnp.concatenate([rank] * rep, axis=1)
                    drop = jnp.where(rank3 < NSA_SEL_TOPK, 0.0, NEG_BIG)
                    parts = []
                    for j in range(n_vis):
                        parts.append((s[j * sb:(j + 1) * sb],
                                      functools.partial(causal, j * sb) if j * sb >= lo else None,
                                      drop[j:j + 1, :]))
                else:
                    parts = [(s[0:lo], None, None)] if lo > 0 else []
                    parts.append((s[lo:], functools.partial(causal, lo), None))
                e_s, l_s = _softmax_parts(parts)
                os_ref[gg] = _dot_nn(vst_ref[gg, :, 0:n_keys], e_s) / l_s

    n_win = pad_chunks + 1
    def oldest(a, n):
        return lax.broadcasted_iota(jnp.int32, (n, lanes), 0) + a > t_loc

    def newest(a, n):
        return lax.broadcasted_iota(jnp.int32, (n, lanes), 0) + a <= t_loc

    for gg in groups:
        parts = []
        for c in range(n_win):
            s_w = _dot_nt(kw3_ref[gg, qi + c], q_bf[gg])
            pen = None
            if c < pad_chunks:
                pen = jnp.full((1, lanes), jnp.where(qi + c >= pad_chunks, 0.0, NEG_BIG), F32)
            parts.append((s_w, oldest if c == 0 else (newest if c == n_win - 1 else None), pen))
        e_w, l_w = _softmax_parts(parts)
        o_w = _dot_nn(vwt3_ref[gg, qi], e_w[0:tq])
        for c in range(1, n_win):
            o_w = o_w + _dot_nn(vwt3_ref[gg, qi + c], e_w[c * tq:(c + 1) * tq])
        o_w = o_w / l_w

        g_t = _sigmoid(gate_ref[0, :, grp(gg)].T)

        def gate_row(branch, g_t=g_t):
            return jnp.concatenate([g_t[3 * r + branch:3 * r + branch + 1, :] for r in range(rep)], axis=1)

        out_t = gate_row(0) * o_c[gg] + gate_row(1) * os_ref[gg] + gate_row(2) * o_w
        for r in range(rep):
            o_ref[0, :, grp(gg * rep + r)] = out_t[:, r * tq:(r + 1) * tq].T.astype(o_ref.dtype)


def _nsa(z3, cmp_kv, cos, sin_s, ovl_t, tq=256):
    b, s, _ = z3.shape
    g = NSA_KV_GROUPS
    n_sel = s // NSA_SEL_BLOCK
    n_cmp = cmp_kv.shape[3]
    qw = NSA_HEADS * HEAD_DIM
    kvw = g * HEAD_DIM
    n_chunk_pad = (s + NSA_WINDOW) // tq

    def col(off):
        return lambda bi, qi: (bi, 0, off // kvw)

    return pl.pallas_call(
        functools.partial(_nsa_kernel, n_sel=n_sel, scale=HEAD_DIM ** -0.5),
        grid=(b, s // tq),
        in_specs=[
            pl.BlockSpec((1, tq, qw), lambda bi, qi: (bi, qi, OFF_CQ // qw)),
            pl.BlockSpec((1, s, kvw), col(OFF_CKS)),
            pl.BlockSpec((1, s, kvw), col(OFF_CVS)),
            pl.BlockSpec((1, s, kvw), col(OFF_CKW)),
            pl.BlockSpec((1, s, kvw), col(OFF_CVW)),
            pl.BlockSpec((1, 1, g, n_cmp, HEAD_DIM), lambda bi, qi: (0, bi, 0, 0, 0)),
            pl.BlockSpec((1, 1, g, n_cmp, HEAD_DIM), lambda bi, qi: (1, bi, 0, 0, 0)),
            pl.BlockSpec((1, tq, g * LANES), lambda bi, qi: (bi, qi, OFF_CGATE // (g * LANES))),
            pl.BlockSpec((s, LANES), lambda bi, qi: (0, 0)),
            pl.BlockSpec((s, LANES), lambda bi, qi: (0, 0)),
            pl.BlockSpec((tq, LANES), lambda bi, qi: (qi, 0)),
            pl.BlockSpec((tq, LANES), lambda bi, qi: (qi, 0)),
            pl.BlockSpec((n_sel, n_cmp), lambda bi, qi: (0, 0)),
        ],
        out_specs=pl.BlockSpec((1, tq, qw), lambda bi, qi: (bi, qi, 0)),
        out_shape=jax.ShapeDtypeStruct((b, s, qw), BF16),
        scratch_shapes=[
            pltpu.VMEM((g, s, HEAD_DIM), BF16),
            pltpu.VMEM((g, HEAD_DIM, s), BF16),
            pltpu.VMEM((g, n_chunk_pad, tq, HEAD_DIM), BF16),
            pltpu.VMEM((g, n_chunk_pad, HEAD_DIM, tq), BF16),
            pltpu.VMEM((g, HEAD_DIM, n_cmp), BF16),
            pltpu.VMEM((g, HEAD_DIM, NSA_REP * tq), F32),
        ],
        compiler_params=_cparams(("parallel", "arbitrary")),
        name="nsa",
    )(z3, z3, z3, z3, z3, cmp_kv, cmp_kv, z3, cos, sin_s, cos, sin_s, ovl_t)


def _merge_kernel(ya_ref, yb_ref, yc_ref, ga_ref, gb_ref, gc_ref, wa_ref, wb_ref, wc_ref, o_ref,
                  wa_bf_ref, wb_bf_ref, wc_bf_ref):
    @pl.when(pl.program_id(1) == 0)
    def _():
        wa_bf_ref[...] = wa_ref[...].astype(BF16)
        wb_bf_ref[...] = wb_ref[...].astype(BF16)
        wc_bf_ref[...] = wc_ref[...].astype(BF16)

    m = ga_ref[...].astype(F32) * _dot_nn(ya_ref[...], wa_bf_ref[...])
    m = m + gb_ref[...].astype(F32) * _dot_nn(yb_ref[...], wb_bf_ref[...])
    m = m + gc_ref[...].astype(F32) * _dot_nn(yc_ref[...], wc_bf_ref[...])
    o_ref[...] = m.astype(o_ref.dtype)


def _merge(ya, yb, yc, gates, wa, wb, wc, layer, tm=1024, tn=1024):
    t = ya.shape[0]
    d = wa.shape[2]
    nj = d // tn

    def gate_spec(branch):
        return pl.BlockSpec((tm, tn), lambda j, i: (i, branch * nj + j))

    def y_spec(y):
        return pl.BlockSpec((tm, y.shape[1]), lambda j, i: (i, 0))

    def w_spec(w):
        return pl.BlockSpec((None, w.shape[1], tn), lambda j, i: (layer, 0, j), pipeline_mode=pl.Buffered(1))

    return pl.pallas_call(
        _merge_kernel,
        grid=(nj, t // tm),
        in_specs=[y_spec(ya), y_spec(yb), y_spec(yc), gate_spec(0), gate_spec(1), gate_spec(2),
                  w_spec(wa), w_spec(wb), w_spec(wc)],
        out_specs=pl.BlockSpec((tm, tn), lambda j, i: (i, j)),
        out_shape=jax.ShapeDtypeStruct((t, d), BF16),
        scratch_shapes=[pltpu.VMEM((w.shape[1], tn), BF16) for w in (wa, wb, wc)],
        compiler_params=_cparams(("parallel", "arbitrary")),
        name="merge",
    )(ya, yb, yc, gates, gates, gates, wa, wb, wc)


def _proj_residual_kernel(m_ref, w_ref, x_ref, o_ref, w_bf_ref):
    @pl.when(pl.program_id(1) == 0)
    def _():
        w_bf_ref[...] = w_ref[...].astype(BF16)

    o_ref[...] = x_ref[...] + _dot_nn(m_ref[...], w_bf_ref[...])


def _proj_residual(m, w, x2d, layer, tm=1024, tn=1024):
    t, k = m.shape
    d = w.shape[2]
    return pl.pallas_call(
        _proj_residual_kernel,
        grid=(d // tn, t // tm),
        in_specs=[
            pl.BlockSpec((tm, k), lambda j, i: (i, 0)),
            pl.BlockSpec((None, k, tn), lambda j, i: (layer, 0, j), pipeline_mode=pl.Buffered(1)),
            pl.BlockSpec((tm, tn), lambda j, i: (i, j)),
        ],
        out_specs=pl.BlockSpec((tm, tn), lambda j, i: (i, j)),
        out_shape=jax.ShapeDtypeStruct((t, d), F32),
        scratch_shapes=[pltpu.VMEM((k, tn), BF16)],
        compiler_params=_cparams(("parallel", "arbitrary")),
        name="out_proj",
    )(m, w, x2d)


def _rope_tables(pos, dim, width):
    inv = 1.0 / (ROPE_THETA ** (jnp.arange(0, dim, 2, dtype=F32) / dim))
    ang = pos.astype(F32)[:, None] * inv[None, :]
    cos = jnp.cos(ang)
    sin = jnp.sin(ang)
    cos = jnp.concatenate([cos, cos], axis=-1)
    sin_s = jnp.concatenate([-sin, sin], axis=-1)
    pad = ((0, 0), (0, width - dim))
    return jnp.pad(cos, pad), jnp.pad(sin_s, pad)


def _repack_kernel(srcp_ref, srcq_ref, start_ref, valid_ref, p_ref, q_ref, o_ref, *, n_src_cols):
    j = pl.program_id(1)
    start = start_ref[j]
    valid = valid_ref[j]
    row = lax.broadcasted_iota(jnp.int32, (LANES, LANES), 0)
    col = lax.broadcasted_iota(jnp.int32, (LANES, LANES), 1)
    lane = lax.broadcasted_iota(jnp.int32, (1, LANES), 1)

    def contribution(blk_ref, src_blk):
        take = jnp.where((src_blk * LANES + row == start + col) & (col < valid), 1.0, 0.0).astype(BF16)
        blk = jnp.where(src_blk * LANES + lane < n_src_cols, blk_ref[...], 0.0).astype(BF16)
        return _dot_nn(blk, take)

    o_ref[...] = (contribution(p_ref, srcp_ref[j]) + contribution(q_ref, srcq_ref[j])).astype(o_ref.dtype)


def _repack_tables(pieces, n_out_blocks, n_src_blocks):
    starts, valids = [], []
    for start, width in pieces:
        if width == 0:
            starts.append(0)
            valids.append(0)
        for off in range(0, width, LANES):
            starts.append(start + off)
            valids.append(min(LANES, width - off))
    starts += [0] * (n_out_blocks - len(starts))
    valids += [0] * (n_out_blocks - len(valids))
    assert n_src_blocks >= 2
    slots = [0, 1]
    srcp, srcq = [], []
    for start, valid in zip(starts, valids):
        need = []
        if valid > 0:
            need = list(range(start // LANES, (start + valid - 1) // LANES + 1))
        for blk in need:
            if blk not in slots:
                slots[0 if slots[0] not in need else 1] = blk
        srcp.append(slots[0])
        srcq.append(slots[1])
    return tuple(jnp.asarray(np.asarray(v, np.int32)) for v in (srcp, srcq, starts, valids))


def _repack(w, pieces, n_out_blocks):
    depth, k, n = w.shape
    tables = _repack_tables(pieces, n_out_blocks, -(-n // LANES))
    grid_spec = pltpu.PrefetchScalarGridSpec(
        num_scalar_prefetch=4,
        grid=(depth, n_out_blocks),
        in_specs=[
            pl.BlockSpec((None, k, LANES), lambda d, j, sp, sq, st, va: (d, 0, sp[j])),
            pl.BlockSpec((None, k, LANES), lambda d, j, sp, sq, st, va: (d, 0, sq[j])),
        ],
        out_specs=pl.BlockSpec((None, k, LANES), lambda d, j, sp, sq, st, va: (d, 0, j)),
    )
    return pl.pallas_call(
        functools.partial(_repack_kernel, n_src_cols=n),
        grid_spec=grid_spec,
        out_shape=jax.ShapeDtypeStruct((depth, k, n_out_blocks * LANES), BF16),
        compiler_params=_cparams(("parallel", "arbitrary")),
        name="repack",
    )(*tables, w, w)


def _repack_t_kernel(srcp_ref, srcq_ref, start_ref, valid_ref, p_ref, q_ref, o_ref, *, n_src_cols):
    j = pl.program_id(0)
    start = start_ref[j]
    valid = valid_ref[j]
    depth = o_ref.shape[0]
    rows = LANES * depth
    out_row = lax.broadcasted_iota(jnp.int32, (LANES, rows), 0)
    flat = lax.broadcasted_iota(jnp.int32, (LANES, rows), 1)
    sub = lax.broadcasted_iota(jnp.int32, (rows, 1), 0)

    def flattened(blk_ref, src_blk):
        x = blk_ref[...].reshape(rows, blk_ref.shape[2])
        return jnp.where(src_blk * LANES + sub // depth < n_src_cols, x, 0.0).astype(BF16)

    def take(src_blk, layer):
        hit = (flat == (start + out_row - src_blk * LANES) * depth + layer) & (out_row < valid)
        return jnp.where(hit, 1.0, 0.0).astype(BF16)

    xp, xq = flattened(p_ref, srcp_ref[j]), flattened(q_ref, srcq_ref[j])
    for layer in range(depth):
        o_ref[layer] = (_dot_nn(take(srcp_ref[j], layer), xp)
                        + _dot_nn(take(srcq_ref[j], layer), xq)).astype(o_ref.dtype)


def _repack_t(wt, pieces, n_out_blocks):
    n, depth, k = wt.shape
    tables = _repack_tables(pieces, n_out_blocks, -(-n // LANES))
    grid_spec = pltpu.PrefetchScalarGridSpec(
        num_scalar_prefetch=4,
        grid=(n_out_blocks,),
        in_specs=[
            pl.BlockSpec((LANES, depth, k), lambda j, sp, sq, st, va: (sp[j], 0, 0)),
            pl.BlockSpec((LANES, depth, k), lambda j, sp, sq, st, va: (sq[j], 0, 0)),
        ],
        out_specs=pl.BlockSpec((depth, LANES, k), lambda j, sp, sq, st, va: (0, j, 0)),
    )
    return pl.pallas_call(
        functools.partial(_repack_t_kernel, n_src_cols=n),
        grid_spec=grid_spec,
        out_shape=jax.ShapeDtypeStruct((depth, n_out_blocks * LANES, k), BF16),
        compiler_params=_cparams(("arbitrary",)),
        name="repack_t",
    )(*tables, wt, wt)


def _pack_w_in(w):
    sizes = [MOBA_HEADS * HEAD_DIM] * 3 + [MLA_Q_RANK, MLA_KV_RANK, MLA_ROPE, NSA_HEADS * HEAD_DIM] \
        + [NSA_KV_GROUPS * HEAD_DIM] * 6
    offs = [int(v) for v in np.concatenate([[0], np.cumsum(sizes)])]
    (a_q, a_k, a_v, b_cq, b_ckv, b_kr, c_q, c_kc, c_vc, c_ks, c_vs, c_kw, c_vw) = [
        (offs[i], sizes[i]) for i in range(len(sizes))]
    gate0 = offs[-1]
    per_group = NSA_REP * 3
    gates = [(gate0 + gi * per_group, per_group) for gi in range(NSA_KV_GROUPS)]
    merge = (gate0 + NSA_HEADS * 3, w.shape[2] - gate0 - NSA_HEADS * 3)
    pieces = [c_q, b_ckv, b_cq, a_q, a_k, a_v, c_kc, c_vc, c_ks, c_vs, c_kw, c_vw] + gates + [b_kr]
    used = sum(-(-width // LANES) for _, width in pieces)
    pieces += [(0, 0)] * (Z_COLS // LANES - used) + [merge]
    return _repack_t(jnp.transpose(w, (2, 0, 1)), pieces, (Z_COLS + merge[1]) // LANES)


def _pack_w_uq(w):
    per_head = MLA_NOPE + MLA_ROPE
    pieces = [(h * per_head, MLA_NOPE) for h in range(MLA_HEADS)] \
        + [(h * per_head + MLA_NOPE, MLA_ROPE) for h in range(MLA_HEADS)]
    return _repack(w, pieces, 2 * MLA_HEADS)


def _pack_w_ukv(w):
    per_head = MLA_NOPE + MLA_V
    pieces = [(h * per_head, MLA_NOPE) for h in range(MLA_HEADS)] \
        + [(h * per_head + MLA_NOPE, MLA_V) for h in range(MLA_HEADS)]
    return _repack(w, pieces, 2 * MLA_HEADS)


def kernel(x, ffn1_norm, ffn1_w_gate, ffn1_w_up, ffn1_w_down, mix_norm, w_in, mla_q_norm, mla_w_uq, mla_kv_norm, mla_w_ukv, nsa_cmp_pos_k, nsa_cmp_w1_k, nsa_cmp_w2_k, nsa_cmp_pos_v, nsa_cmp_w1_v, nsa_cmp_w2_v, w_branch_moba, w_branch_mla, w_branch_nsa, w_out, ffn2_norm, ffn2_w_gate, ffn2_w_up, ffn2_w_down, final_norm):
    b, s, d = x.shape
    depth = w_in.shape[0]
    t = b * s
    assert NSA_CMP_LEN == 2 * NSA_CMP_STRIDE and s % MOBA_BLOCK == 0 and s % NSA_SEL_BLOCK == 0

    pos = jnp.arange(s)
    cos_h, sin_h = _rope_tables(pos, HEAD_DIM, LANES)
    cos_r, sin_r = _rope_tables(pos, MLA_ROPE, LANES)
    n16 = s // NSA_CMP_STRIDE
    cmp_end = np.arange(n16) * NSA_CMP_STRIDE + NSA_CMP_LEN - 1
    cos_c, sin_c = _rope_tables(jnp.asarray(cmp_end), HEAD_DIM, LANES)
    n_sel = s // NSA_SEL_BLOCK
    sel_start = np.arange(n_sel) * NSA_SEL_BLOCK
    cmp_start = np.arange(n16) * NSA_CMP_STRIDE
    ovl_t = ((cmp_start[None, :] <= sel_start[:, None] + NSA_SEL_BLOCK - 1)
             & (cmp_end[None, :] >= sel_start[:, None])).astype(np.float32)
    ovl_t = jnp.asarray(ovl_t, BF16)

    w_in_p = _pack_w_in(w_in)
    w_uq_p = _pack_w_uq(mla_w_uq)
    w_ukv_p = _pack_w_ukv(mla_w_ukv)

    xf = x.reshape(t, d)
    for i in range(depth):
        xf = _ffn(xf, ffn1_norm[i], ffn1_w_gate, ffn1_w_up, ffn1_w_down, i)

        z, gates = _in_proj(xf, mix_norm[i], w_in_p, i)
        z3 = z.reshape(b, s, Z_COLS)
        y_a = _moba(z3, cos_h, sin_h)
        y_b = _mla(z3, mla_q_norm[i], mla_kv_norm[i], w_uq_p, w_ukv_p, i, cos_r, sin_r)

        w1 =jnp.stack([nsa_cmp_w1_k[i], nsa_cmp_w1_v[i]])
        w2 = jnp.stack([nsa_cmp_w2_k[i], nsa_cmp_w2_v[i]])
        posflat = jnp.stack([nsa_cmp_pos_k[i].reshape(1, -1), nsa_cmp_pos_v[i].reshape(1, -1)])
        cmp_kv = _nsa_compress(z3, w1, w2, posflat, cos_c, sin_c)
        y_c = _nsa(z3, cmp_kv, cos_h, sin_h, ovl_t)

        merged = _merge(y_a.reshape(t, -1), y_b.reshape(t, -1), y_c.reshape(t, -1), gates,
                        w_branch_moba, w_branch_mla, w_branch_nsa, i)
        xf = _proj_residual(merged, w_out, xf, i)

        xf = _ffn(xf, ffn2_norm[i], ffn2_w_gate, ffn2_w_up, ffn2_w_down, i)
    return _rms(xf, final_norm, F32).reshape(b, s, d)
```

```python
import functools

import numpy as np
import jax
import jax.numpy as jnp
from jax import lax
from jax.experimental import pallas as pl
from jax.experimental.pallas import tpu as pltpu

F32 = jnp.float32
BF16 = jnp.bfloat16

HEAD_DIM = 128
ROPE_THETA = 10000.0
NORM_EPS = 1e-6
NEG_BIG = -1e30
TINY = 1e-30
LOG2E = 1.4426950408889634
N_BRANCHES = 3

MOBA_HEADS = 4
MOBA_BLOCK = 256
MOBA_TOPK = 3

MLA_HEADS = 6
MLA_Q_RANK = 512
MLA_KV_RANK = 256
MLA_NOPE = 128
MLA_ROPE = 64
MLA_V = 128

NSA_HEADS = 6
NSA_KV_GROUPS = 2
NSA_REP = NSA_HEADS // NSA_KV_GROUPS
NSA_CMP_LEN = 32
NSA_CMP_STRIDE = 16
NSA_CMP_HIDDEN = 256
NSA_SEL_BLOCK = 64
NSA_SEL_TOPK = 16
NSA_WINDOW = 512
FORCE_SCORE = 1e9

LANES = 128
SUBLANES = 8
SOFTMAX_CHUNK = 64
SOFTMAX_LANES = 384
VMEM_LIMIT = 56 * 2**20

OFF_CQ = 0
OFF_BCKV = 768
OFF_BCQ = 1024
OFF_AQ = 1536
OFF_AK = 2048
OFF_AV = 2560
OFF_CKC = 3072
OFF_CVC = 3328
OFF_CKS = 3584
OFF_CVS = 3840
OFF_CKW = 4096
OFF_CVW = 4352
OFF_CGATE = 4608
OFF_BKR = 4864
Z_COLS = 5120


def _cparams(sem):
    return pltpu.CompilerParams(dimension_semantics=sem, vmem_limit_bytes=VMEM_LIMIT)


def _rmsnorm(x, g):
    ms = jnp.mean(x * x, axis=-1, keepdims=True)
    return x * lax.rsqrt(ms + NORM_EPS) * g


def _sigmoid(x):
    return 0.5 * jnp.tanh(0.5 * x) + 0.5


def _rope_full(x, cos, sin_signed):
    return x * cos + pltpu.roll(x, HEAD_DIM // 2, axis=1) * sin_signed


def _rope_half(x, cos, sin_signed):
    lane = lax.broadcasted_iota(jnp.int32, x.shape, 1)
    q = MLA_ROPE // 2
    swapped = jnp.where(lane < q, pltpu.roll(x, LANES - q, axis=1), pltpu.roll(x, q, axis=1))
    return x * cos + swapped * sin_signed


def _dot_nt(a, b):
    return lax.dot_general(a, b, (((1,), (1,)), ((), ())), preferred_element_type=F32)


def _dot_nn(a, b):
    return jnp.dot(a, b, preferred_element_type=F32)


def _split(a):
    hi = a.astype(BF16)
    lo = (a - hi.astype(F32)).astype(BF16)
    return hi, lo


def _dot3_nt(a, b):
    ah, al = _split(a)
    bh, bl = _split(b)
    return _dot_nt(ah, bh) + _dot_nt(ah, bl) + _dot_nt(al, bh)


def _dot3_nn(a, b):
    ah, al = _split(a)
    bh, bl = _split(b)
    return _dot_nn(ah, bh) + _dot_nn(ah, bl) + _dot_nn(al, bh)


def _softmax_t(s, mask, scale):
    s = jnp.where(mask, s * scale, NEG_BIG)
    m = jnp.max(s, axis=0, keepdims=True)
    e = jnp.where(mask, jnp.exp(s - m), 0.0)
    l = jnp.maximum(jnp.sum(e, axis=0, keepdims=True), TINY)
    return e, l


def _softmax_parts(parts, chunk=SOFTMAX_CHUNK):
    lanes = parts[0][0].shape[1]
    chunk = max(SUBLANES, chunk * SOFTMAX_LANES // max(lanes, SOFTMAX_LANES))
    pieces = []
    for s, mask_fn, pen in parts:
        for a in range(0, s.shape[0], chunk):
            n = min(chunk, s.shape[0] - a)
            p = s[a:a + n]
            pieces.append((p if mask_fn is None else jnp.where(mask_fn(a, n), p, NEG_BIG), pen))

    def fold(x, op):
        return op(x.reshape(x.shape[0] // SUBLANES, SUBLANES, lanes), axis=0)

    def col_max(p, pen):
        m = fold(p, jnp.max)
        return m if pen is None else m + pen

    m8 = functools.reduce(jnp.maximum, [col_max(p, pen) for p, pen in pieces])
    mx = jnp.max(m8, axis=0, keepdims=True)
    es, l8 = [], None
    for p, pen in pieces:
        e = jnp.exp2(p - (mx if pen is None else mx - pen))
        l8 = fold(e, jnp.sum) if l8 is None else l8 + fold(e, jnp.sum)
        es.append(e.astype(BF16))
    l = jnp.sum(l8, axis=0, keepdims=True)
    e = jnp.concatenate(es, axis=0) if len(es) > 1 else es[0]
    return e, jnp.maximum(l, TINY)


def _rank_t(v, nb):
    blk = lax.broadcasted_iota(jnp.int32, v.shape, 0)
    rank = jnp.zeros(v.shape, F32)
    for j in range(nb):
        vj = v[j:j + 1, :]
        ge = (vj >= v).astype(F32)
        gt = (vj > v).astype(F32)
        rank = rank + jnp.where(blk > j, ge, gt)
    return rank


def _rms_kernel(x_ref, g_ref, o_ref):
    o_ref[...] = _rmsnorm(x_ref[...], g_ref[...]).astype(o_ref.dtype)


def _rms_cast_kernel(x_ref, g_ref, w_ref, o_ref, w_bf_ref):
    o_ref[...] = _rmsnorm(x_ref[...], g_ref[...]).astype(o_ref.dtype)
    w_bf_ref[...] = w_ref[...].astype(w_bf_ref.dtype)


def _rms(x2d, g, out_dtype, tm=1024, cast=None):
    t, d = x2d.shape
    steps = t // tm
    in_specs = [pl.BlockSpec((tm, d), lambda i: (i, 0)), pl.BlockSpec((1, d), lambda i: (0, 0))]
    out_specs = pl.BlockSpec((tm, d), lambda i: (i, 0))
    out_shape = jax.ShapeDtypeStruct((t, d), out_dtype)
    if cast is None:
        return pl.pallas_call(
            _rms_kernel, grid=(steps,), in_specs=in_specs, out_specs=out_specs, out_shape=out_shape,
            compiler_params=_cparams(("parallel",)), name="rmsnorm",
        )(x2d, g.reshape(1, d))
    w, layer = cast
    rows, cols = w.shape[1:]
    slab = rows // steps
    assert slab * steps == rows and slab % (2 * SUBLANES) == 0
    return pl.pallas_call(
        _rms_cast_kernel,
        grid=(steps,),
        in_specs=in_specs + [pl.BlockSpec((None, slab, cols), lambda i: (layer, i, 0))],
        out_specs=[out_specs, pl.BlockSpec((slab, cols), lambda i: (i, 0))],
        out_shape=[out_shape, jax.ShapeDtypeStruct((rows, cols), BF16)],
        compiler_params=_cparams(("parallel",)),
        name="rmsnorm_cast",
    )(x2d, g.reshape(1, d), w)


def _ffn_up_kernel(h_ref, wg_ref, wu_ref, o_ref, wg_bf_ref, wu_bf_ref):
    @pl.when(pl.program_id(1) == 0)
    def _():
        wg_bf_ref[...] = wg_ref[...].astype(BF16)
        wu_bf_ref[...] = wu_ref[...].astype(BF16)

    h = h_ref[...]
    a = _dot_nn(h, wg_bf_ref[...])
    u = _dot_nn(h, wu_bf_ref[...])
    o_ref[...] = (a * _sigmoid(a) * u).astype(o_ref.dtype)


def _ffn_down_kernel(a_ref, wd_ref, x_ref, o_ref):
    o_ref[...] = x_ref[...] + 0.5 * _dot_nn(a_ref[...], wd_ref[...])


def _ffn(x2d, g, wg, wu, wd, layer, tm=1024, tf=512, tm_down=512, tn=1024):
    t, d = x2d.shape
    f = wg.shape[2]
    h, wd_bf = _rms(x2d, g, BF16, cast=(wd, layer))
    act = pl.pallas_call(
        _ffn_up_kernel,
        grid=(f // tf, t // tm),
        in_specs=[
            pl.BlockSpec((tm, d), lambda j, i: (i, 0)),
            pl.BlockSpec((None, d, tf), lambda j, i: (layer, 0, j)),
            pl.BlockSpec((None, d, tf), lambda j, i: (layer, 0, j)),
        ],
        out_specs=pl.BlockSpec((tm, tf), lambda j, i: (i, j)),
        out_shape=jax.ShapeDtypeStruct((t, f), BF16),
        scratch_shapes=[pltpu.VMEM((d, tf), BF16), pltpu.VMEM((d, tf), BF16)],
        compiler_params=_cparams(("parallel", "arbitrary")),
        name="ffn_up",
    )(h, wg, wu)
    return pl.pallas_call(
        _ffn_down_kernel,
        grid=(d // tn, t // tm_down),
        in_specs=[
            pl.BlockSpec((tm_down, f), lambda n, i: (i, 0)),
            pl.BlockSpec((f, tn), lambda n, i: (0, n)),
            pl.BlockSpec((tm_down, tn), lambda n, i: (i, n)),
        ],
        out_specs=pl.BlockSpec((tm_down, tn), lambda n, i: (i, n)),
        out_shape=jax.ShapeDtypeStruct((t, d), F32),
        compiler_params=_cparams(("parallel", "parallel")),
        name="ffn_down",
    )(act, wd_bf, x2d)


def _in_proj_kernel(x_ref, g_ref, w_ref, z_ref, gate_ref, h_ref, *, n_z):
    j = pl.program_id(1)

    @pl.when(j == 0)
    def _():
        h_ref[...] = _rmsnorm(x_ref[...], g_ref[...]).astype(BF16)

    @pl.when(j < n_z)
    def _():
        z_ref[...] = _dot_nt(h_ref[...], w_ref[...])

    @pl.when(j >= n_z)
    def _():
        gate_ref[...] = _sigmoid(_dot_nt(h_ref[...], w_ref[...])).astype(gate_ref.dtype)


def _in_proj(x2d, g, w, layer, tm=1024, tn=1024):
    t, d = x2d.shape
    n = w.shape[1]
    n_z = Z_COLS // tn
    n_g = (n - Z_COLS) // tn
    return pl.pallas_call(
        functools.partial(_in_proj_kernel, n_z=n_z),
        grid=(t // tm, n_z + n_g),
        in_specs=[
            pl.BlockSpec((tm, d), lambda i, j: (i, 0)),
            pl.BlockSpec((1, d), lambda i, j: (0, 0)),
            pl.BlockSpec((None, tn, d), lambda i, j: (layer, j, 0)),
        ],
        out_specs=[
            pl.BlockSpec((tm, tn), lambda i, j: (i, jnp.minimum(j, n_z - 1))),
            pl.BlockSpec((tm, tn), lambda i, j: (i, jnp.maximum(j - n_z, 0))),
        ],
        out_shape=[jax.ShapeDtypeStruct((t, Z_COLS), F32), jax.ShapeDtypeStruct((t, n - Z_COLS), BF16)],
        scratch_shapes=[pltpu.VMEM((tm, d), BF16)],
        compiler_params=_cparams(("parallel", "arbitrary")),
        name="in_proj",
    )(x2d, g.reshape(1, d), w)


def _moba_kernel(q_ref, k_ref, v_ref, cosf_ref, sinf_ref, cosq_ref, sinq_ref, o_ref,
                 kr_ref, km_ref, vt_ref, *, n_blk, scale):
    qi = pl.program_id(2)
    tq = q_ref.shape[1]
    hp = q_ref.shape[2] // LANES
    head = lambda hh: slice(hh * LANES, (hh + 1) * LANES)

    @pl.when(qi == 0)
    def _():
        for hh in range(hp):
            kr = _rope_full(k_ref[0, :, head(hh)], cosf_ref[...], sinf_ref[...])
            kr_ref[hh] = kr.astype(BF16)
            km_ref[hh] = jnp.concatenate(
                [jnp.mean(kr[j * MOBA_BLOCK:(j + 1) * MOBA_BLOCK], axis=0, keepdims=True) for j in range(n_blk)],
                axis=0)
            vt_ref[hh] = v_ref[0, :, head(hh)].T.astype(BF16)

    blk = MOBA_BLOCK
    bpt = tq // blk
    lane = lax.broadcasted_iota(jnp.int32, (1, tq), 1)
    lane_sub = lane // blk
    lane_loc = lane % blk

    for n in range(n_blk // bpt):
        @pl.when(qi == n)
        def _(n=n):
            first = n * bpt
            ranked = first + bpt - 1 > MOBA_TOPK
            for hh in range(hp):
                qr = _rope_full(q_ref[0, :, head(hh)], cosq_ref[...], sinq_ref[...])
                q_bf = (qr * (scale * LOG2E)).astype(BF16)
                s = _dot_nt(kr_ref[hh, 0:(n + 1) * tq], q_bf)
                drop = None
                if ranked:
                    gate = _dot3_nt(km_ref[hh], qr)
                    past = lax.broadcasted_iota(jnp.int32, (n_blk, tq), 0) < first + lane_sub
                    gate = jnp.where(past, gate, -jnp.inf)
                    drop = jnp.where(_rank_t(gate, first + bpt - 1) < MOBA_TOPK, 0.0, NEG_BIG)
                parts = []
                if drop is not None:
                    for j in range(first):
                        parts.append((s[j * blk:(j + 1) * blk], None, drop[j:j + 1, :]))
                elif first > 0:
                    parts.append((s[0:first * blk], None, None))
                for i2 in range(bpt):
                    j = first + i2
                    pen = None
                    if drop is not None and i2 < bpt - 1:
                        pen = jnp.where(lane_sub > i2, drop[j:j + 1, :], 0.0)

                    limit = jnp.where(lane_sub > i2, blk, jnp.where(lane_sub == i2, lane_loc, -1))

                    def mask_fn(a, rows, limit=limit):
                        return lax.broadcasted_iota(jnp.int32, (rows, tq), 0) + a <= limit

                    parts.append((s[j * blk:(j + 1) * blk], mask_fn, pen))
                e, l = _softmax_parts(parts)
                o_t = _dot_nn(vt_ref[hh, :, 0:(n + 1) * tq], e) / l
                o_ref[0, :, head(hh)] = o_t.T.astype(o_ref.dtype)


def _moba(z3, cos, sin_s, tq=2 * MOBA_BLOCK, hp=MOBA_HEADS):
    b, s, _ = z3.shape
    n_blk = s // MOBA_BLOCK
    w = hp * LANES
    qb, kb, vb = OFF_AQ // w, OFF_AK // w, OFF_AV // w
    return pl.pallas_call(
        functools.partial(_moba_kernel, n_blk=n_blk, scale=HEAD_DIM ** -0.5),
        grid=(b, MOBA_HEADS // hp, s // tq),
        in_specs=[
            pl.BlockSpec((1, tq, w), lambda bi, h, qi: (bi, qi, qb + h)),
            pl.BlockSpec((1, s, w), lambda bi, h, qi: (bi, 0, kb + h)),
            pl.BlockSpec((1, s, w), lambda bi, h, qi: (bi, 0, vb + h)),
            pl.BlockSpec((s, LANES), lambda bi, h, qi: (0, 0)),
            pl.BlockSpec((s, LANES), lambda bi, h, qi: (0, 0)),
            pl.BlockSpec((tq, LANES), lambda bi, h, qi: (qi, 0)),
            pl.BlockSpec((tq, LANES), lambda bi, h, qi: (qi, 0)),
        ],
        out_specs=pl.BlockSpec((1, tq, w), lambda bi, h, qi: (bi, qi, h)),
        out_shape=jax.ShapeDtypeStruct((b, s, MOBA_HEADS * HEAD_DIM), BF16),
        scratch_shapes=[pltpu.VMEM((hp, s, LANES), BF16), pltpu.VMEM((hp, n_blk, LANES), F32),
                        pltpu.VMEM((hp, LANES, s), BF16)],
        compiler_params=_cparams(("parallel", "parallel", "arbitrary")),
        name="moba",
    )(z3, z3, z3, cos, sin_s, cos, sin_s)


def _mla_kernel(cq_ref, ckv_ref, kr_ref, qg_ref, kvg_ref, wqn_ref, wqr_ref, wkn_ref, wv_ref,
                cosf_ref, sinf_ref, cosq_ref, sinq_ref, o_ref, kf_ref, vt_ref, *, scale):
    qi = pl.program_id(2)
    tq = cq_ref.shape[1]
    s_len = ckv_ref.shape[1]
    hp = wqn_ref.shape[1] // LANES
    head = lambda hh: slice(hh * LANES, (hh + 1) * LANES)

    @pl.when(qi == 0)
    def _():
        ckv = _rmsnorm(ckv_ref[0], kvg_ref[...]).astype(BF16)
        k_nope = _dot_nn(ckv, wkn_ref[...])
        v = _dot_nn(ckv, wv_ref[...])
        k_rope = _rope_half(kr_ref[0], cosf_ref[...], sinf_ref[...]).astype(BF16)
        for hh in range(hp):
            kf_ref[hh, :, :LANES] = k_nope[:, head(hh)].astype(BF16)
            kf_ref[hh, :, LANES:] = k_rope
            vt_ref[hh] = v[:, head(hh)].T.astype(BF16)

    cq = _rmsnorm(cq_ref[0], qg_ref[...]).astype(BF16)
    q_nope = _dot_nn(cq, wqn_ref[...])
    q_rope = _dot_nn(cq, wqr_ref[...])
    def causal(a, n):
        return (lax.broadcasted_iota(jnp.int32, (n, tq), 0) + a
                <= lax.broadcasted_iota(jnp.int32, (n, tq), 1))

    for n in range(s_len // tq):
        @pl.when(qi == n)
        def _(n=n):
            for hh in range(hp):
                q = jnp.concatenate([q_nope[:, head(hh)],
                                     _rope_half(q_rope[:, head(hh)], cosq_ref[...], sinq_ref[...])], axis=1)
                q_bf = (q * (scale * LOG2E)).astype(BF16)
                s = _dot_nt(kf_ref[hh, 0:(n + 1) * tq], q_bf)
                parts = [(s[0:n * tq], None, None)] if n > 0 else []
                parts.append((s[n * tq:], causal, None))
                e, l = _softmax_parts(parts)
                o_t = _dot_nn(vt_ref[hh, :, 0:(n + 1) * tq], e) / l
                o_ref[0, :, head(hh)] = o_t.T.astype(o_ref.dtype)


def _mla(z3, q_gain, kv_gain, w_uq, w_ukv, layer, cos, sin_s, tq=512, hp=MLA_HEADS):
    b, s, _ = z3.shape
    w = hp * LANES
    ng = MLA_HEADS // hp
    return pl.pallas_call(
        functools.partial(_mla_kernel, scale=(MLA_NOPE + MLA_ROPE) ** -0.5),
        grid=(b, ng, s // tq),
        in_specs=[
            pl.BlockSpec((1, tq, MLA_Q_RANK), lambda bi, h, qi: (bi, qi, OFF_BCQ // MLA_Q_RANK)),
            pl.BlockSpec((1, s, MLA_KV_RANK), lambda bi, h, qi: (bi, 0, OFF_BCKV // MLA_KV_RANK)),
            pl.BlockSpec((1, s, LANES), lambda bi, h, qi: (bi, 0, OFF_BKR // LANES)),
            pl.BlockSpec((1, MLA_Q_RANK), lambda bi, h, qi: (0, 0)),
            pl.BlockSpec((1, MLA_KV_RANK), lambda bi, h, qi: (0, 0)),
            pl.BlockSpec((None, MLA_Q_RANK, w), lambda bi, h, qi: (layer, 0, h)),
            pl.BlockSpec((None, MLA_Q_RANK, w), lambda bi, h, qi: (layer, 0, ng + h)),
            pl.BlockSpec((None, MLA_KV_RANK, w), lambda bi, h, qi: (layer, 0, h)),
            pl.BlockSpec((None, MLA_KV_RANK, w), lambda bi, h, qi: (layer, 0, ng + h)),
            pl.BlockSpec((s, LANES), lambda bi, h, qi: (0, 0)),
            pl.BlockSpec((s, LANES), lambda bi, h, qi: (0, 0)),
            pl.BlockSpec((tq, LANES), lambda bi, h, qi: (qi, 0)),
            pl.BlockSpec((tq, LANES), lambda bi, h, qi: (qi, 0)),
        ],
        out_specs=pl.BlockSpec((1, tq, w), lambda bi, h, qi: (bi, qi, h)),
        out_shape=jax.ShapeDtypeStruct((b, s, MLA_HEADS * MLA_V), BF16),
        scratch_shapes=[pltpu.VMEM((hp, s, 2 * LANES), BF16), pltpu.VMEM((hp, LANES, s), BF16)],
        compiler_params=_cparams(("parallel", "parallel", "arbitrary")),
        name="mla",
    )(z3, z3, z3, q_gain.reshape(1, -1), kv_gain.reshape(1, -1), w_uq, w_uq, w_ukv, w_ukv,
      cos, sin_s, cos, sin_s)


def _dot3_presplit(a, b_hi, b_lo):
    ah, al = _split(a)
    return _dot_nn(ah, b_hi) + _dot_nn(ah, b_lo) + _dot_nn(al, b_hi)


def _cmp_kernel(*refs):
    n_grp = len(refs) - 9
    x_refs = refs[:n_grp]
    w1_ref, w2_ref, pos_ref, cos_ref, sin_ref, o_ref, w1_hi_ref, w1_lo_ref, bias_ref = refs[n_grp:]
    kv = pl.program_id(0)
    n16 = o_ref.shape[3]
    half = NSA_CMP_STRIDE * HEAD_DIM

    @pl.when(pl.program_id(1) == 0)
    def _():
        hi, lo = _split(w1_ref[0])
        w1_hi_ref[...] = hi
        w1_lo_ref[...] = lo
        bias_ref[...] = _dot3_presplit(jnp.broadcast_to(pos_ref[0], (SUBLANES, 2 * half)), hi, lo)

    for gi, x_ref in enumerate(x_refs):
        x = jnp.concatenate(
            [x_ref[0, pl.ds(l, n16, stride=NSA_CMP_STRIDE), :] for l in range(NSA_CMP_STRIDE)], axis=1)
        first = _dot3_presplit(x, w1_hi_ref[:half], w1_lo_ref[:half])
        second = _dot3_presplit(x, w1_hi_ref[half:], w1_lo_ref[half:])
        hid = first + pltpu.roll(second, n16 - 1, axis=0) + bias_ref[0:1]
        y = _dot3_nn(jax.nn.gelu(hid), w2_ref[0])
        roped = _rope_full(y, cos_ref[...], sin_ref[...])
        o_ref[0, 0, gi] = jnp.where(kv == 0, roped, y)


def _nsa_compress(z3, w1, w2, posflat, cos_c, sin_c):
    b, s, _ = z3.shape
    g = NSA_KV_GROUPS
    n16 = s // NSA_CMP_STRIDE
    width = NSA_CMP_STRIDE * HEAD_DIM
    hid = w1.shape[2]
    base = OFF_CKC // LANES
    x_specs = [pl.BlockSpec((1, s, LANES), lambda kv, bi, gi=gi: (bi, 0, base + g * kv + gi)) for gi in range(g)]
    return pl.pallas_call(
        _cmp_kernel,
        grid=(2, b),
        in_specs=x_specs + [
            pl.BlockSpec((1, 2 * width, hid), lambda kv, bi: (kv, 0, 0)),
            pl.BlockSpec((1, hid, HEAD_DIM), lambda kv, bi: (kv, 0, 0)),
            pl.BlockSpec((1, 1, 2 * width), lambda kv, bi: (kv, 0, 0)),
            pl.BlockSpec((n16, HEAD_DIM), lambda kv, bi: (0, 0)),
            pl.BlockSpec((n16, HEAD_DIM), lambda kv, bi: (0, 0)),
        ],
        out_specs=pl.BlockSpec((1, 1, g, n16, HEAD_DIM), lambda kv, bi: (kv, bi, 0, 0, 0)),
        out_shape=jax.ShapeDtypeStruct((2, b, g, n16, HEAD_DIM), F32),
        scratch_shapes=[pltpu.VMEM((2 * width, hid), BF16), pltpu.VMEM((2 * width, hid), BF16),
                        pltpu.VMEM((SUBLANES, hid), F32)],
        compiler_params=_cparams(("parallel", "arbitrary")),
        name="nsa_compress",
    )(*([z3] * g), w1, w2, posflat, cos_c, sin_c)


def _nsa_kernel(q_ref, ks_ref, vs_ref, kw_ref, vw_ref, kc_ref, vc_ref, gate_ref,
                cosf_ref, sinf_ref, cosq_ref, sinq_ref, ovl_ref, o_ref,
                ksr_ref, vst_ref, kw3_ref, vwt3_ref, vct_ref, os_ref, *, n_sel, scale):
    qi = pl.program_id(1)
    tq = q_ref.shape[1]
    s_len = ks_ref.shape[1]
    n_grp = ks_ref.shape[2] // HEAD_DIM
    rep = NSA_REP
    lanes = rep * tq
    n_chunk = s_len // tq
    pad_chunks = NSA_WINDOW // tq
    n_cmp = kc_ref.shape[3]
    grp = lambda gg: slice(gg * HEAD_DIM, (gg + 1) * HEAD_DIM)
    groups = range(n_grp)

    @pl.when(qi == 0)
    def _():
        for gg in groups:
            ksr_ref[gg] = _rope_full(ks_ref[0, :, grp(gg)], cosf_ref[...], sinf_ref[...]).astype(BF16)
            vst_ref[gg] = vs_ref[0, :, grp(gg)].T.astype(BF16)
            kw3_ref[gg, 0:pad_chunks] = jnp.zeros((pad_chunks, tq, HEAD_DIM), BF16)
            vwt3_ref[gg, 0:pad_chunks] = jnp.zeros((pad_chunks, HEAD_DIM, tq), BF16)
            kwr = _rope_full(kw_ref[0, :, grp(gg)], cosf_ref[...], sinf_ref[...]).astype(BF16)
            for c in range(n_chunk):
                kw3_ref[gg, pad_chunks + c] = kwr[c * tq:(c + 1) * tq]
                vwt3_ref[gg, pad_chunks + c] = vw_ref[0, c * tq:(c + 1) * tq, grp(gg)].T.astype(BF16)
            vct_ref[gg] = vc_ref[0, 0, gg].T.astype(BF16)

    cos_q = cosq_ref[...]
    sin_q = sinq_ref[...]
    lane = lax.broadcasted_iota(jnp.int32, (1, lanes), 1)
    t_loc = lane % tq
    t = qi * tq + t_loc
    cur = t[:, 0:tq] // NSA_SEL_BLOCK
    blk = lax.broadcasted_iota(jnp.int32, (n_sel, tq), 0)
    forced = (blk == 0) | (blk == cur) | (blk == cur - 1)
    visible = blk <= cur
    cmp_end = (lax.broadcasted_iota(jnp.int32, (n_cmp, lanes), 0) * NSA_CMP_STRIDE + (NSA_CMP_LEN - 1))
    ovl = ovl_ref[...]

    q_bf, o_c, imp = [], [], []
    for gg in groups:
        q_all = jnp.concatenate(
            [_rope_full(q_ref[0, :, grp(gg * rep + r)], cos_q, sin_q) for r in range(rep)],
            axis=0)
        q_bf.append((q_all * (scale * LOG2E)).astype(BF16))

        s_c = _dot3_nt(kc_ref[0, 0, gg], q_all)
        e_c, l_c = _softmax_t(s_c, cmp_end <= t, scale)
        p_c = e_c / l_c
        o_c.append(_dot_nn(vct_ref[gg], p_c.astype(BF16)))

        p_hi, p_lo = _split(p_c)
        imp3 = _dot_nn(ovl, p_hi) + _dot_nn(ovl, p_lo)
        imp_g = imp3[:, 0:tq]
        for r in range(1, rep):
            imp_g = imp_g + imp3[:, r * tq:(r + 1) * tq]
        imp_g = jnp.where(forced, FORCE_SCORE, imp_g)
        imp.append(jnp.where(visible, imp_g, -jnp.inf))

    unit = tq
    sb = NSA_SEL_BLOCK
    for v in range(s_len // unit):
        @pl.when(qi == v)
        def _(v=v):
            n_keys = (v + 1) * unit
            lo = n_keys - unit

            def causal(base, a, n):
                return lax.broadcasted_iota(jnp.int32, (n, lanes), 0) <= (t - (base + a))

            n_vis = n_keys // sb
            for gg in groups:
                s = _dot_nt(ksr_ref[gg, 0:n_keys], q_bf[gg])
                if n_vis > NSA_SEL_TOPK:
                    rank = _rank_t(imp[gg], n_vis)
                    rank3 = jnp.concatenate([rank] * rep, axis=1)
                    drop = jnp.where(rank3 < NSA_SEL_TOPK, 0.0, NEG_BIG)
                    parts = []
                    for j in range(n_vis):
                        parts.append((s[j * sb:(j + 1) * sb],
                                      functools.partial(causal, j * sb) if j * sb >= lo else None,
                                      drop[j:j + 1, :]))
                else:
                    parts = [(s[0:lo], None, None)] if lo > 0 else []
                    parts.append((s[lo:], functools.partial(causal, lo), None))
                e_s, l_s = _softmax_parts(parts)
                os_ref[gg] = _dot_nn(vst_ref[gg, :, 0:n_keys], e_s) / l_s

    n_win = pad_chunks + 1
    def oldest(a, n):
        return lax.broadcasted_iota(jnp.int32, (n, lanes), 0) + a > t_loc

    def newest(a, n):
        return lax.broadcasted_iota(jnp.int32, (n, lanes), 0) + a <= t_loc

    for gg in groups:
        parts = []
        for c in range(n_win):
            s_w = _dot_nt(kw3_ref[gg, qi + c], q_bf[gg])
            pen = None
            if c < pad_chunks:
                pen = jnp.full((1, lanes), jnp.where(qi + c >= pad_chunks, 0.0, NEG_BIG), F32)
            parts.append((s_w, oldest if c == 0 else (newest if c == n_win - 1 else None), pen))
        e_w, l_w = _softmax_parts(parts)
        o_w = _dot_nn(vwt3_ref[gg, qi], e_w[0:tq])
        for c in range(1, n_win):
            o_w = o_w + _dot_nn(vwt3_ref[gg, qi + c], e_w[c * tq:(c + 1) * tq])
        o_w = o_w / l_w

        g_t = _sigmoid(gate_ref[0, :, grp(gg)].T)

        def gate_row(branch, g_t=g_t):
            return jnp.concatenate([g_t[3 * r + branch:3 * r + branch + 1, :] for r in range(rep)], axis=1)

        out_t = gate_row(0) * o_c[gg] + gate_row(1) * os_ref[gg] + gate_row(2) * o_w
        for r in range(rep):
            o_ref[0, :, grp(gg * rep + r)] = out_t[:, r * tq:(r + 1) * tq].T.astype(o_ref.dtype)


def _nsa(z3, cmp_kv, cos, sin_s, ovl_t, tq=256):
    b, s, _ = z3.shape
    g = NSA_KV_GROUPS
    n_sel = s // NSA_SEL_BLOCK
    n_cmp = cmp_kv.shape[3]
    qw = NSA_HEADS * HEAD_DIM
    kvw = g * HEAD_DIM
    n_chunk_pad = (s + NSA_WINDOW) // tq

    def col(off):
        return lambda bi, qi: (bi, 0, off // kvw)

    return pl.pallas_call(
        functools.partial(_nsa_kernel, n_sel=n_sel, scale=HEAD_DIM ** -0.5),
        grid=(b, s // tq),
        in_specs=[
            pl.BlockSpec((1, tq, qw), lambda bi, qi: (bi, qi, OFF_CQ // qw)),
            pl.BlockSpec((1, s, kvw), col(OFF_CKS)),
            pl.BlockSpec((1, s, kvw), col(OFF_CVS)),
            pl.BlockSpec((1, s, kvw), col(OFF_CKW)),
            pl.BlockSpec((1, s, kvw), col(OFF_CVW)),
            pl.BlockSpec((1, 1, g, n_cmp, HEAD_DIM), lambda bi, qi: (0, bi, 0, 0, 0)),
            pl.BlockSpec((1, 1, g, n_cmp, HEAD_DIM), lambda bi, qi: (1, bi, 0, 0, 0)),
            pl.BlockSpec((1, tq, g * LANES), lambda bi, qi: (bi, qi, OFF_CGATE // (g * LANES))),
            pl.BlockSpec((s, LANES), lambda bi, qi: (0, 0)),
            pl.BlockSpec((s, LANES), lambda bi, qi: (0, 0)),
            pl.BlockSpec((tq, LANES), lambda bi, qi: (qi, 0)),
            pl.BlockSpec((tq, LANES), lambda bi, qi: (qi, 0)),
            pl.BlockSpec((n_sel, n_cmp), lambda bi, qi: (0, 0)),
        ],
        out_specs=pl.BlockSpec((1, tq, qw), lambda bi, qi: (bi, qi, 0)),
        out_shape=jax.ShapeDtypeStruct((b, s, qw), BF16),
        scratch_shapes=[
            pltpu.VMEM((g, s, HEAD_DIM), BF16),
            pltpu.VMEM((g, HEAD_DIM, s), BF16),
            pltpu.VMEM((g, n_chunk_pad, tq, HEAD_DIM), BF16),
            pltpu.VMEM((g, n_chunk_pad, HEAD_DIM, tq), BF16),
            pltpu.VMEM((g, HEAD_DIM, n_cmp), BF16),
            pltpu.VMEM((g, HEAD_DIM, NSA_REP * tq), F32),
        ],
        compiler_params=_cparams(("parallel", "arbitrary")),
        name="nsa",
    )(z3, z3, z3, z3, z3, cmp_kv, cmp_kv, z3, cos, sin_s, cos, sin_s, ovl_t)


def _merge_kernel(ya_ref, yb_ref, yc_ref, ga_ref, gb_ref, gc_ref, wa_ref, wb_ref, wc_ref, o_ref,
                  wa_bf_ref, wb_bf_ref, wc_bf_ref):
    @pl.when(pl.program_id(1) == 0)
    def _():
        wa_bf_ref[...] = wa_ref[...].astype(BF16)
        wb_bf_ref[...] = wb_ref[...].astype(BF16)
        wc_bf_ref[...] = wc_ref[...].astype(BF16)

    m = ga_ref[...].astype(F32) * _dot_nn(ya_ref[...], wa_bf_ref[...])
    m = m + gb_ref[...].astype(F32) * _dot_nn(yb_ref[...], wb_bf_ref[...])
    m = m + gc_ref[...].astype(F32) * _dot_nn(yc_ref[...], wc_bf_ref[...])
    o_ref[...] = m.astype(o_ref.dtype)


def _merge(ya, yb, yc, gates, wa, wb, wc, layer, tm=1024, tn=1024):
    t = ya.shape[0]
    d = wa.shape[2]
    nj = d // tn

    def gate_spec(branch):
        return pl.BlockSpec((tm, tn), lambda j, i: (i, branch * nj + j))

    def y_spec(y):
        return pl.BlockSpec((tm, y.shape[1]), lambda j, i: (i, 0))

    def w_spec(w):
        return pl.BlockSpec((None, w.shape[1], tn), lambda j, i: (layer, 0, j), pipeline_mode=pl.Buffered(1))

    return pl.pallas_call(
        _merge_kernel,
        grid=(nj, t // tm),
        in_specs=[y_spec(ya), y_spec(yb), y_spec(yc), gate_spec(0), gate_spec(1), gate_spec(2),
                  w_spec(wa), w_spec(wb), w_spec(wc)],
        out_specs=pl.BlockSpec((tm, tn), lambda j, i: (i, j)),
        out_shape=jax.ShapeDtypeStruct((t, d), BF16),
        scratch_shapes=[pltpu.VMEM((w.shape[1], tn), BF16) for w in (wa, wb, wc)],
        compiler_params=_cparams(("parallel", "arbitrary")),
        name="merge",
    )(ya, yb, yc, gates, gates, gates, wa, wb, wc)


def _proj_residual_kernel(m_ref, w_ref, x_ref, o_ref, w_bf_ref):
    @pl.when(pl.program_id(1) == 0)
    def _():
        w_bf_ref[...] = w_ref[...].astype(BF16)

    o_ref[...] = x_ref[...] + _dot_nn(m_ref[...], w_bf_ref[...])


def _proj_residual(m, w, x2d, layer, tm=1024, tn=1024):
    t, k = m.shape
    d = w.shape[2]
    return pl.pallas_call(
        _proj_residual_kernel,
        grid=(d // tn, t // tm),
        in_specs=[
            pl.BlockSpec((tm, k), lambda j, i: (i, 0)),
            pl.BlockSpec((None, k, tn), lambda j, i: (layer, 0, j), pipeline_mode=pl.Buffered(1)),
            pl.BlockSpec((tm, tn), lambda j, i: (i, j)),
        ],
        out_specs=pl.BlockSpec((tm, tn), lambda j, i: (i, j)),
        out_shape=jax.ShapeDtypeStruct((t, d), F32),
        scratch_shapes=[pltpu.VMEM((k, tn), BF16)],
        compiler_params=_cparams(("parallel", "arbitrary")),
        name="out_proj",
    )(m, w, x2d)


def _rope_tables(pos, dim, width):
    inv = 1.0 / (ROPE_THETA ** (jnp.arange(0, dim, 2, dtype=F32) / dim))
    ang = pos.astype(F32)[:, None] * inv[None, :]
    cos = jnp.cos(ang)
    sin = jnp.sin(ang)
    cos = jnp.concatenate([cos, cos], axis=-1)
    sin_s = jnp.concatenate([-sin, sin], axis=-1)
    pad = ((0, 0), (0, width - dim))
    return jnp.pad(cos, pad), jnp.pad(sin_s, pad)


def _repack_kernel(srcp_ref, srcq_ref, start_ref, valid_ref, p_ref, q_ref, o_ref, *, n_src_cols):
    j = pl.program_id(1)
    start = start_ref[j]
    valid = valid_ref[j]
    row = lax.broadcasted_iota(jnp.int32, (LANES, LANES), 0)
    col = lax.broadcasted_iota(jnp.int32, (LANES, LANES), 1)
    lane = lax.broadcasted_iota(jnp.int32, (1, LANES), 1)

    def contribution(blk_ref, src_blk):
        take = jnp.where((src_blk * LANES + row == start + col) & (col < valid), 1.0, 0.0).astype(BF16)
        blk = jnp.where(src_blk * LANES + lane < n_src_cols, blk_ref[...], 0.0).astype(BF16)
        return _dot_nn(blk, take)

    o_ref[...] = (contribution(p_ref, srcp_ref[j]) + contribution(q_ref, srcq_ref[j])).astype(o_ref.dtype)


def _repack_tables(pieces, n_out_blocks, n_src_blocks):
    starts, valids = [], []
    for start, width in pieces:
        if width == 0:
            starts.append(0)
            valids.append(0)
        for off in range(0, width, LANES):
            starts.append(start + off)
            valids.append(min(LANES, width - off))
    starts += [0] * (n_out_blocks - len(starts))
    valids += [0] * (n_out_blocks - len(valids))
    assert n_src_blocks >= 2
    slots = [0, 1]
    srcp, srcq = [], []
    for start, valid in zip(starts, valids):
        need = []
        if valid > 0:
            need = list(range(start // LANES, (start + valid - 1) // LANES + 1))
        for blk in need:
            if blk not in slots:
                slots[0 if slots[0] not in need else 1] = blk
        srcp.append(slots[0])
        srcq.append(slots[1])
    return tuple(jnp.asarray(np.asarray(v, np.int32)) for v in (srcp, srcq, starts, valids))


def _repack(w, pieces, n_out_blocks):
    depth, k, n = w.shape
    tables = _repack_tables(pieces, n_out_blocks, -(-n // LANES))
    grid_spec = pltpu.PrefetchScalarGridSpec(
        num_scalar_prefetch=4,
        grid=(depth, n_out_blocks),
        in_specs=[
            pl.BlockSpec((None, k, LANES), lambda d, j, sp, sq, st, va: (d, 0, sp[j])),
            pl.BlockSpec((None, k, LANES), lambda d, j, sp, sq, st, va: (d, 0, sq[j])),
        ],
        out_specs=pl.BlockSpec((None, k, LANES), lambda d, j, sp, sq, st, va: (d, 0, j)),
    )
    return pl.pallas_call(
        functools.partial(_repack_kernel, n_src_cols=n),
        grid_spec=grid_spec,
        out_shape=jax.ShapeDtypeStruct((depth, k, n_out_blocks * LANES), BF16),
        compiler_params=_cparams(("parallel", "arbitrary")),
        name="repack",
    )(*tables, w, w)


def _repack_t_kernel(srcp_ref, srcq_ref, start_ref, valid_ref, p_ref, q_ref, o_ref, *, n_src_cols):
    j = pl.program_id(0)
    start = start_ref[j]
    valid = valid_ref[j]
    depth = o_ref.shape[0]
    rows = LANES * depth
    out_row = lax.broadcasted_iota(jnp.int32, (LANES, rows), 0)
    flat = lax.broadcasted_iota(jnp.int32, (LANES, rows), 1)
    sub = lax.broadcasted_iota(jnp.int32, (rows, 1), 0)

    def flattened(blk_ref, src_blk):
        x = blk_ref[...].reshape(rows, blk_ref.shape[2])
        return jnp.where(src_blk * LANES + sub // depth < n_src_cols, x, 0.0).astype(BF16)

    def take(src_blk, layer):
        hit = (flat == (start + out_row - src_blk * LANES) * depth + layer) & (out_row < valid)
        return jnp.where(hit, 1.0, 0.0).astype(BF16)

    xp, xq = flattened(p_ref, srcp_ref[j]), flattened(q_ref, srcq_ref[j])
    for layer in range(depth):
        o_ref[layer] = (_dot_nn(take(srcp_ref[j], layer), xp)
                        + _dot_nn(take(srcq_ref[j], layer), xq)).astype(o_ref.dtype)


def _repack_t(wt, pieces, n_out_blocks):
    n, depth, k = wt.shape
    tables = _repack_tables(pieces, n_out_blocks, -(-n // LANES))
    grid_spec = pltpu.PrefetchScalarGridSpec(
        num_scalar_prefetch=4,
        grid=(n_out_blocks,),
        in_specs=[
            pl.BlockSpec((LANES, depth, k), lambda j, sp, sq, st, va: (sp[j], 0, 0)),
            pl.BlockSpec((LANES, depth, k), lambda j, sp, sq, st, va: (sq[j], 0, 0)),
        ],
        out_specs=pl.BlockSpec((depth, LANES, k), lambda j, sp, sq, st, va: (0, j, 0)),
    )
    return pl.pallas_call(
        functools.partial(_repack_t_kernel, n_src_cols=n),
        grid_spec=grid_spec,
        out_shape=jax.ShapeDtypeStruct((depth, n_out_blocks * LANES, k), BF16),
        compiler_params=_cparams(("arbitrary",)),
        name="repack_t",
    )(*tables, wt, wt)


def _pack_w_in(w):
    sizes = [MOBA_HEADS * HEAD_DIM] * 3 + [MLA_Q_RANK, MLA_KV_RANK, MLA_ROPE, NSA_HEADS * HEAD_DIM] \
        + [NSA_KV_GROUPS * HEAD_DIM] * 6
    offs = [int(v) for v in np.concatenate([[0], np.cumsum(sizes)])]
    (a_q, a_k, a_v, b_cq, b_ckv, b_kr, c_q, c_kc, c_vc, c_ks, c_vs, c_kw, c_vw) = [
        (offs[i], sizes[i]) for i in range(len(sizes))]
    gate0 = offs[-1]
    per_group = NSA_REP * 3
    gates = [(gate0 + gi * per_group, per_group) for gi in range(NSA_KV_GROUPS)]
    merge = (gate0 + NSA_HEADS * 3, w.shape[2] - gate0 - NSA_HEADS * 3)
    pieces = [c_q, b_ckv, b_cq, a_q, a_k, a_v, c_kc, c_vc, c_ks, c_vs, c_kw, c_vw] + gates + [b_kr]
    used = sum(-(-width // LANES) for _, width in pieces)
    pieces += [(0, 0)] * (Z_COLS // LANES - used) + [merge]
    return _repack_t(jnp.transpose(w, (2, 0, 1)), pieces, (Z_COLS + merge[1]) // LANES)


def _pack_w_uq(w):
    per_head = MLA_NOPE + MLA_ROPE
    pieces = [(h * per_head, MLA_NOPE) for h in range(MLA_HEADS)] \
        + [(h * per_head + MLA_NOPE, MLA_ROPE) for h in range(MLA_HEADS)]
    return _repack(w, pieces, 2 * MLA_HEADS)


def _pack_w_ukv(w):
    per_head = MLA_NOPE + MLA_V
    pieces = [(h * per_head, MLA_NOPE) for h in range(MLA_HEADS)] \
        + [(h * per_head + MLA_NOPE, MLA_V) for h in range(MLA_HEADS)]
    return _repack(w, pieces, 2 * MLA_HEADS)


def kernel(x, ffn1_norm, ffn1_w_gate, ffn1_w_up, ffn1_w_down, mix_norm, w_in, mla_q_norm, mla_w_uq, mla_kv_norm, mla_w_ukv, nsa_cmp_pos_k, nsa_cmp_w1_k, nsa_cmp_w2_k, nsa_cmp_pos_v, nsa_cmp_w1_v, nsa_cmp_w2_v, w_branch_moba, w_branch_mla, w_branch_nsa, w_out, ffn2_norm, ffn2_w_gate, ffn2_w_up, ffn2_w_down, final_norm):
    b, s, d = x.shape
    depth = w_in.shape[0]
    t = b * s
    assert NSA_CMP_LEN == 2 * NSA_CMP_STRIDE and s % MOBA_BLOCK == 0 and s % NSA_SEL_BLOCK == 0

    pos = jnp.arange(s)
    cos_h, sin_h = _rope_tables(pos, HEAD_DIM, LANES)
    cos_r, sin_r = _rope_tables(pos, MLA_ROPE, LANES)
    n16 = s // NSA_CMP_STRIDE
    cmp_end = np.arange(n16) * NSA_CMP_STRIDE + NSA_CMP_LEN - 1
    cos_c, sin_c = _rope_tables(jnp.asarray(cmp_end), HEAD_DIM, LANES)
    n_sel = s // NSA_SEL_BLOCK
    sel_start = np.arange(n_sel) * NSA_SEL_BLOCK
    cmp_start = np.arange(n16) * NSA_CMP_STRIDE
    ovl_t = ((cmp_start[None, :] <= sel_start[:, None] + NSA_SEL_BLOCK - 1)
             & (cmp_end[None, :] >= sel_start[:, None])).astype(np.float32)
    ovl_t = jnp.asarray(ovl_t, BF16)

    w_in_p = _pack_w_in(w_in)
    w_uq_p = _pack_w_uq(mla_w_uq)
    w_ukv_p = _pack_w_ukv(mla_w_ukv)

    xf = x.reshape(t, d)
    for i in range(depth):
        xf = _ffn(xf, ffn1_norm[i], ffn1_w_gate, ffn1_w_up, ffn1_w_down, i)

        z, gates = _in_proj(xf, mix_norm[i], w_in_p, i)
        z3 = z.reshape(b, s, Z_COLS)
        y_a = _moba(z3, cos_h, sin_h)
        y_b = _mla(z3, mla_q_norm[i], mla_kv_norm[i], w_uq_p, w_ukv_p, i, cos_r, sin_r)

        w1 =jnp.stack([nsa_cmp_w1_k[i], nsa_cmp_w1_v[i]])
        w2 = jnp.stack([nsa_cmp_w2_k[i], nsa_cmp_w2_v[i]])
        posflat = jnp.stack([nsa_cmp_pos_k[i].reshape(1, -1), nsa_cmp_pos_v[i].reshape(1, -1)])
        cmp_kv = _nsa_compress(z3, w1, w2, posflat, cos_c, sin_c)
        y_c = _nsa(z3, cmp_kv, cos_h, sin_h, ovl_t)

        merged = _merge(y_a.reshape(t, -1), y_b.reshape(t, -1), y_c.reshape(t, -1), gates,
                        w_branch_moba, w_branch_mla, w_branch_nsa, i)
        xf = _proj_residual(merged, w_out, xf, i)

        xf = _ffn(xf, ffn2_norm[i], ffn2_w_gate, ffn2_w_up, ffn2_w_down, i)
    return _rms(xf, final_norm, F32).reshape(b, s, d)
```

```python
import functools

import numpy as np
import jax
import jax.numpy as jnp
from jax import lax
from jax.experimental import pallas as pl
from jax.experimental.pallas import tpu as pltpu

F32 = jnp.float32
BF16 = jnp.bfloat16

HEAD_DIM = 128
ROPE_THETA = 10000.0
NORM_EPS = 1e-6
NEG_BIG = -1e30
TINY = 1e-30
LOG2E = 1.4426950408889634
N_BRANCHES = 3

MOBA_HEADS = 4
MOBA_BLOCK = 256
MOBA_TOPK = 3

MLA_HEADS = 6
MLA_Q_RANK = 512
MLA_KV_RANK = 256
MLA_NOPE = 128
MLA_ROPE = 64
MLA_V = 128

NSA_HEADS = 6
NSA_KV_GROUPS = 2
NSA_REP = NSA_HEADS // NSA_KV_GROUPS
NSA_CMP_LEN = 32
NSA_CMP_STRIDE = 16
NSA_CMP_HIDDEN = 256
NSA_SEL_BLOCK = 64
NSA_SEL_TOPK = 16
NSA_WINDOW = 512
FORCE_SCORE = 1e9

LANES = 128
SUBLANES = 8
SOFTMAX_CHUNK = 64
SOFTMAX_LANES = 384
VMEM_LIMIT = 56 * 2**20

OFF_CQ = 0
OFF_BCKV = 768
OFF_BCQ = 1024
OFF_AQ = 1536
OFF_AK = 2048
OFF_AV = 2560
OFF_CKC = 3072
OFF_CVC = 3328
OFF_CKS = 3584
OFF_CVS = 3840
OFF_CKW = 4096
OFF_CVW = 4352
OFF_CGATE = 4608
OFF_BKR = 4864
Z_COLS = 5120


def _cparams(sem):
    return pltpu.CompilerParams(dimension_semantics=sem, vmem_limit_bytes=VMEM_LIMIT)


def _rmsnorm(x, g):
    ms = jnp.mean(x * x, axis=-1, keepdims=True)
    return x * lax.rsqrt(ms + NORM_EPS) * g


def _sigmoid(x):
    return 0.5 * jnp.tanh(0.5 * x) + 0.5


def _rope_full(x, cos, sin_signed):
    return x * cos + pltpu.roll(x, HEAD_DIM // 2, axis=1) * sin_signed


def _rope_half(x, cos, sin_signed):
    lane = lax.broadcasted_iota(jnp.int32, x.shape, 1)
    q = MLA_ROPE // 2
    swapped = jnp.where(lane < q, pltpu.roll(x, LANES - q, axis=1), pltpu.roll(x, q, axis=1))
    return x * cos + swapped * sin_signed


def _dot_nt(a, b):
    return lax.dot_general(a, b, (((1,), (1,)), ((), ())), preferred_element_type=F32)


def _dot_nn(a, b):
    return jnp.dot(a, b, preferred_element_type=F32)


def _split(a):
    hi = a.astype(BF16)
    lo = (a - hi.astype(F32)).astype(BF16)
    return hi, lo


def _dot3_nt(a, b):
    ah, al = _split(a)
    bh, bl = _split(b)
    return _dot_nt(ah, bh) + _dot_nt(ah, bl) + _dot_nt(al, bh)


def _dot3_nn(a, b):
    ah, al = _split(a)
    bh, bl = _split(b)
    return _dot_nn(ah, bh) + _dot_nn(ah, bl) + _dot_nn(al, bh)


def _softmax_t(s, mask, scale):
    s = jnp.where(mask, s * scale, NEG_BIG)
    m = jnp.max(s, axis=0, keepdims=True)
    e = jnp.where(mask, jnp.exp(s - m), 0.0)
    l = jnp.maximum(jnp.sum(e, axis=0, keepdims=True), TINY)
    return e, l


def _softmax_parts(parts, chunk=SOFTMAX_CHUNK):
    lanes = parts[0][0].shape[1]
    chunk = max(SUBLANES, chunk * SOFTMAX_LANES // max(lanes, SOFTMAX_LANES))
    pieces = []
    for s, mask_fn, pen in parts:
        for a in range(0, s.shape[0], chunk):
            n = min(chunk, s.shape[0] - a)
            p = s[a:a + n]
            pieces.append((p if mask_fn is None else jnp.where(mask_fn(a, n), p, NEG_BIG), pen))

    def fold(x, op):
        return op(x.reshape(x.shape[0] // SUBLANES, SUBLANES, lanes), axis=0)

    def col_max(p, pen):
        m = fold(p, jnp.max)
        return m if pen is None else m + pen

    m8 = functools.reduce(jnp.maximum, [col_max(p, pen) for p, pen in pieces])
    mx = jnp.max(m8, axis=0, keepdims=True)
    es, l8 = [], None
    for p, pen in pieces:
        e = jnp.exp2(p - (mx if pen is None else mx - pen))
        l8 = fold(e, jnp.sum) if l8 is None else l8 + fold(e, jnp.sum)
        es.append(e.astype(BF16))
    l = jnp.sum(l8, axis=0, keepdims=True)
    e = jnp.concatenate(es, axis=0) if len(es) > 1 else es[0]
    return e, jnp.maximum(l, TINY)


def _rank_t(v, nb):
    blk = lax.broadcasted_iota(jnp.int32, v.shape, 0)
    rank = jnp.zeros(v.shape, F32)
    for j in range(nb):
        vj = v[j:j + 1, :]
        ge = (vj >= v).astype(F32)
        gt = (vj > v).astype(F32)
        rank = rank + jnp.where(blk > j, ge, gt)
    return rank


def _rms_kernel(x_ref, g_ref, o_ref):
    o_ref[...] = _rmsnorm(x_ref[...], g_ref[...]).astype(o_ref.dtype)


def _rms_cast_kernel(x_ref, g_ref, w_ref, o_ref, w_bf_ref):
    o_ref[...] = _rmsnorm(x_ref[...], g_ref[...]).astype(o_ref.dtype)
    w_bf_ref[...] = w_ref[...].astype(w_bf_ref.dtype)


def _rms(x2d, g, out_dtype, tm=1024, cast=None):
    t, d = x2d.shape
    steps = t // tm
    in_specs = [pl.BlockSpec((tm, d), lambda i: (i, 0)), pl.BlockSpec((1, d), lambda i: (0, 0))]
    out_specs = pl.BlockSpec((tm, d), lambda i: (i, 0))
    out_shape = jax.ShapeDtypeStruct((t, d), out_dtype)
    if cast is None:
        return pl.pallas_call(
            _rms_kernel, grid=(steps,), in_specs=in_specs, out_specs=out_specs, out_shape=out_shape,
            compiler_params=_cparams(("parallel",)), name="rmsnorm",
        )(x2d, g.reshape(1, d))
    w, layer = cast
    rows, cols = w.shape[1:]
    slab = rows // steps
    assert slab * steps == rows and slab % (2 * SUBLANES) == 0
    return pl.pallas_call(
        _rms_cast_kernel,
        grid=(steps,),
        in_specs=in_specs + [pl.BlockSpec((None, slab, cols), lambda i: (layer, i, 0))],
        out_specs=[out_specs, pl.BlockSpec((slab, cols), lambda i: (i, 0))],
        out_shape=[out_shape, jax.ShapeDtypeStruct((rows, cols), BF16)],
        compiler_params=_cparams(("parallel",)),
        name="rmsnorm_cast",
    )(x2d, g.reshape(1, d), w)


def _ffn_up_kernel(h_ref, wg_ref, wu_ref, o_ref, wg_bf_ref, wu_bf_ref):
    @pl.when(pl.program_id(1) == 0)
    def _():
        wg_bf_ref[...] = wg_ref[...].astype(BF16)
        wu_bf_ref[...] = wu_ref[...].astype(BF16)

    h = h_ref[...]
    a = _dot_nn(h, wg_bf_ref[...])
    u = _dot_nn(h, wu_bf_ref[...])
    o_ref[...] = (a * _sigmoid(a) * u).astype(o_ref.dtype)


def _ffn_down_kernel(a_ref, wd_ref, x_ref, o_ref):
    o_ref[...] = x_ref[...] + 0.5 * _dot_nn(a_ref[...], wd_ref[...])


def _ffn(x2d, g, wg, wu, wd, layer, tm=1024, tf=512, tm_down=1024, tn=1024):
    t, d = x2d.shape
    f = wg.shape[2]
    h, wd_bf = _rms(x2d, g, BF16, cast=(wd, layer))
    act = pl.pallas_call(
        _ffn_up_kernel,
        grid=(f // tf, t // tm),
        in_specs=[
            pl.BlockSpec((tm, d), lambda j, i: (i, 0)),
            pl.BlockSpec((None, d, tf), lambda j, i: (layer, 0, j)),
            pl.BlockSpec((None, d, tf), lambda j, i: (layer, 0, j)),
        ],
        out_specs=pl.BlockSpec((tm, tf), lambda j, i: (i, j)),
        out_shape=jax.ShapeDtypeStruct((t, f), BF16),
        scratch_shapes=[pltpu.VMEM((d, tf), BF16), pltpu.VMEM((d, tf), BF16)],
        compiler_params=_cparams(("parallel", "arbitrary")),
        name="ffn_up",
    )(h, wg, wu)
    return pl.pallas_call(
        _ffn_down_kernel,
        grid=(d // tn, t // tm_down),
        in_specs=[
            pl.BlockSpec((tm_down, f), lambda n, i: (i, 0)),
            pl.BlockSpec((f, tn), lambda n, i: (0, n), pipeline_mode=pl.Buffered(1)),
            pl.BlockSpec((tm_down, tn), lambda n, i: (i, n)),
        ],
        out_specs=pl.BlockSpec((tm_down, tn), lambda n, i: (i, n)),
        out_shape=jax.ShapeDtypeStruct((t, d), F32),
        compiler_params=_cparams(("parallel", "parallel")),
        name="ffn_down",
    )(act, wd_bf, x2d)


def _in_proj_kernel(x_ref, g_ref, w_ref, z_ref, gate_ref, h_ref, *, n_z):
    j = pl.program_id(1)

    @pl.when(j == 0)
    def _():
        h_ref[...] = _rmsnorm(x_ref[...], g_ref[...]).astype(BF16)

    @pl.when(j < n_z)
    def _():
        z_ref[...] = _dot_nt(h_ref[...], w_ref[...])

    @pl.when(j >= n_z)
    def _():
        gate_ref[...] = _sigmoid(_dot_nt(h_ref[...], w_ref[...])).astype(gate_ref.dtype)


def _in_proj(x2d, g, w, layer, tm=1024, tn=1024):
    t, d = x2d.shape
    n = w.shape[1]
    n_z = Z_COLS // tn
    n_g = (n - Z_COLS) // tn
    return pl.pallas_call(
        functools.partial(_in_proj_kernel, n_z=n_z),
        grid=(t // tm, n_z + n_g),
        in_specs=[
            pl.BlockSpec((tm, d), lambda i, j: (i, 0)),
            pl.BlockSpec((1, d), lambda i, j: (0, 0)),
            pl.BlockSpec((None, tn, d), lambda i, j: (layer, j, 0)),
        ],
        out_specs=[
            pl.BlockSpec((tm, tn), lambda i, j: (i, jnp.minimum(j, n_z - 1))),
            pl.BlockSpec((tm, tn), lambda i, j: (i, jnp.maximum(j - n_z, 0))),
        ],
        out_shape=[jax.ShapeDtypeStruct((t, Z_COLS), F32), jax.ShapeDtypeStruct((t, n - Z_COLS), BF16)],
        scratch_shapes=[pltpu.VMEM((tm, d), BF16)],
        compiler_params=_cparams(("parallel", "arbitrary")),
        name="in_proj",
    )(x2d, g.reshape(1, d), w)


def _moba_kernel(q_ref, k_ref, v_ref, cosf_ref, sinf_ref, cosq_ref, sinq_ref, o_ref,
                 kr_ref, km_ref, vt_ref, *, n_blk, scale):
    qi = pl.program_id(2)
    tq = q_ref.shape[1]
    hp = q_ref.shape[2] // LANES
    head = lambda hh: slice(hh * LANES, (hh + 1) * LANES)

    @pl.when(qi == 0)
    def _():
        for hh in range(hp):
            kr = _rope_full(k_ref[0, :, head(hh)], cosf_ref[...], sinf_ref[...])
            kr_ref[hh] = kr.astype(BF16)
            km_ref[hh] = jnp.concatenate(
                [jnp.mean(kr[j * MOBA_BLOCK:(j + 1) * MOBA_BLOCK], axis=0, keepdims=True) for j in range(n_blk)],
                axis=0)
            vt_ref[hh] = v_ref[0, :, head(hh)].T.astype(BF16)

    blk = MOBA_BLOCK
    bpt = tq // blk
    lane = lax.broadcasted_iota(jnp.int32, (1, tq), 1)
    lane_sub = lane // blk
    lane_loc = lane % blk

    for n in range(n_blk // bpt):
        @pl.when(qi == n)
        def _(n=n):
            first = n * bpt
            ranked = first + bpt - 1 > MOBA_TOPK
            for hh in range(hp):
                qr = _rope_full(q_ref[0, :, head(hh)], cosq_ref[...], sinq_ref[...])
                q_bf = (qr * (scale * LOG2E)).astype(BF16)
                s = _dot_nt(kr_ref[hh, 0:(n + 1) * tq], q_bf)
                drop = None
                if ranked:
                    gate = _dot3_nt(km_ref[hh], qr)
                    past = lax.broadcasted_iota(jnp.int32, (n_blk, tq), 0) < first + lane_sub
                    gate = jnp.where(past, gate, -jnp.inf)
                    drop = jnp.where(_rank_t(gate, first + bpt - 1) < MOBA_TOPK, 0.0, NEG_BIG)
                parts = []
                if drop is not None:
                    for j in range(first):
                        parts.append((s[j * blk:(j + 1) * blk], None, drop[j:j + 1, :]))
                elif first > 0:
                    parts.append((s[0:first * blk], None, None))
                for i2 in range(bpt):
                    j = first + i2
                    pen = None
                    if drop is not None and i2 < bpt - 1:
                        pen = jnp.where(lane_sub > i2, drop[j:j + 1, :], 0.0)

                    limit = jnp.where(lane_sub > i2, blk, jnp.where(lane_sub == i2, lane_loc, -1))

                    def mask_fn(a, rows, limit=limit):
                        return lax.broadcasted_iota(jnp.int32, (rows, tq), 0) + a <= limit

                    parts.append((s[j * blk:(j + 1) * blk], mask_fn, pen))
                e, l = _softmax_parts(parts)
                o_t = _dot_nn(vt_ref[hh, :, 0:(n + 1) * tq], e) / l
                o_ref[0, :, head(hh)] = o_t.T.astype(o_ref.dtype)


def _moba(z3, cos, sin_s, tq=2 * MOBA_BLOCK, hp=MOBA_HEADS):
    b, s, _ = z3.shape
    n_blk = s // MOBA_BLOCK
    w = hp * LANES
    qb, kb, vb = OFF_AQ // w, OFF_AK // w, OFF_AV // w
    return pl.pallas_call(
        functools.partial(_moba_kernel, n_blk=n_blk, scale=HEAD_DIM ** -0.5),
        grid=(b, MOBA_HEADS // hp, s // tq),
        in_specs=[
            pl.BlockSpec((1, tq, w), lambda bi, h, qi: (bi, qi, qb + h)),
            pl.BlockSpec((1, s, w), lambda bi, h, qi: (bi, 0, kb + h)),
            pl.BlockSpec((1, s, w), lambda bi, h, qi: (bi, 0, vb + h)),
            pl.BlockSpec((s, LANES), lambda bi, h, qi: (0, 0)),
            pl.BlockSpec((s, LANES), lambda bi, h, qi: (0, 0)),
            pl.BlockSpec((tq, LANES), lambda bi, h, qi: (qi, 0)),
            pl.BlockSpec((tq, LANES), lambda bi, h, qi: (qi, 0)),
        ],
        out_specs=pl.BlockSpec((1, tq, w), lambda bi, h, qi: (bi, qi, h)),
        out_shape=jax.ShapeDtypeStruct((b, s, MOBA_HEADS * HEAD_DIM), BF16),
        scratch_shapes=[pltpu.VMEM((hp, s, LANES), BF16), pltpu.VMEM((hp, n_blk, LANES), F32),
                        pltpu.VMEM((hp, LANES, s), BF16)],
        compiler_params=_cparams(("parallel", "parallel", "arbitrary")),
        name="moba",
    )(z3, z3, z3, cos, sin_s, cos, sin_s)


def _mla_kernel(cq_ref, ckv_ref, kr_ref, qg_ref, kvg_ref, wqn_ref, wqr_ref, wkn_ref, wv_ref,
                cosf_ref, sinf_ref, cosq_ref, sinq_ref, o_ref, kf_ref, vt_ref, *, scale):
    qi = pl.program_id(2)
    tq = cq_ref.shape[1]
    s_len = ckv_ref.shape[1]
    hp = wqn_ref.shape[1] // LANES
    head = lambda hh: slice(hh * LANES, (hh + 1) * LANES)

    @pl.when(qi == 0)
    def _():
        ckv = _rmsnorm(ckv_ref[0], kvg_ref[...]).astype(BF16)
        k_nope = _dot_nn(ckv, wkn_ref[...])
        v = _dot_nn(ckv, wv_ref[...])
        k_rope = _rope_half(kr_ref[0], cosf_ref[...], sinf_ref[...]).astype(BF16)
        for hh in range(hp):
            kf_ref[hh, :, :LANES] = k_nope[:, head(hh)].astype(BF16)
            kf_ref[hh, :, LANES:] = k_rope
            vt_ref[hh] = v[:, head(hh)].T.astype(BF16)

    cq = _rmsnorm(cq_ref[0], qg_ref[...]).astype(BF16)
    q_nope = _dot_nn(cq, wqn_ref[...])
    q_rope = _dot_nn(cq, wqr_ref[...])
    def causal(a, n):
        return (lax.broadcasted_iota(jnp.int32, (n, tq), 0) + a
                <= lax.broadcasted_iota(jnp.int32, (n, tq), 1))

    for n in range(s_len // tq):
        @pl.when(qi == n)
        def _(n=n):
            for hh in range(hp):
                q = jnp.concatenate([q_nope[:, head(hh)],
                                     _rope_half(q_rope[:, head(hh)], cosq_ref[...], sinq_ref[...])], axis=1)
                q_bf = (q * (scale * LOG2E)).astype(BF16)
                s = _dot_nt(kf_ref[hh, 0:(n + 1) * tq], q_bf)
                parts = [(s[0:n * tq], None, None)] if n > 0 else []
                parts.append((s[n * tq:], causal, None))
                e, l = _softmax_parts(parts)
                o_t = _dot_nn(vt_ref[hh, :, 0:(n + 1) * tq], e) / l
                o_ref[0, :, head(hh)] = o_t.T.astype(o_ref.dtype)


def _mla(z3, q_gain, kv_gain, w_uq, w_ukv, layer, cos, sin_s, tq=512, hp=MLA_HEADS):
    b, s, _ = z3.shape
    w = hp * LANES
    ng = MLA_HEADS // hp
    return pl.pallas_call(
        functools.partial(_mla_kernel, scale=(MLA_NOPE + MLA_ROPE) ** -0.5),
        grid=(b, ng, s // tq),
        in_specs=[
            pl.BlockSpec((1, tq, MLA_Q_RANK), lambda bi, h, qi: (bi, qi, OFF_BCQ // MLA_Q_RANK)),
            pl.BlockSpec((1, s, MLA_KV_RANK), lambda bi, h, qi: (bi, 0, OFF_BCKV // MLA_KV_RANK)),
            pl.BlockSpec((1, s, LANES), lambda bi, h, qi: (bi, 0, OFF_BKR // LANES)),
            pl.BlockSpec((1, MLA_Q_RANK), lambda bi, h, qi: (0, 0)),
            pl.BlockSpec((1, MLA_KV_RANK), lambda bi, h, qi: (0, 0)),
            pl.BlockSpec((None, MLA_Q_RANK, w), lambda bi, h, qi: (layer, 0, h)),
            pl.BlockSpec((None, MLA_Q_RANK, w), lambda bi, h, qi: (layer, 0, ng + h)),
            pl.BlockSpec((None, MLA_KV_RANK, w), lambda bi, h, qi: (layer, 0, h)),
            pl.BlockSpec((None, MLA_KV_RANK, w), lambda bi, h, qi: (layer, 0, ng + h)),
            pl.BlockSpec((s, LANES), lambda bi, h, qi: (0, 0)),
            pl.BlockSpec((s, LANES), lambda bi, h, qi: (0, 0)),
            pl.BlockSpec((tq, LANES), lambda bi, h, qi: (qi, 0)),
            pl.BlockSpec((tq, LANES), lambda bi, h, qi: (qi, 0)),
        ],
        out_specs=pl.BlockSpec((1, tq, w), lambda bi, h, qi: (bi, qi, h)),
        out_shape=jax.ShapeDtypeStruct((b, s, MLA_HEADS * MLA_V), BF16),
        scratch_shapes=[pltpu.VMEM((hp, s, 2 * LANES), BF16), pltpu.VMEM((hp, LANES, s), BF16)],
        compiler_params=_cparams(("parallel", "parallel", "arbitrary")),
        name="mla",
    )(z3, z3, z3, q_gain.reshape(1, -1), kv_gain.reshape(1, -1), w_uq, w_uq, w_ukv, w_ukv,
      cos, sin_s, cos, sin_s)


def _dot3_presplit(a, b_hi, b_lo):
    ah, al = _split(a)
    return _dot_nn(ah, b_hi) + _dot_nn(ah, b_lo) + _dot_nn(al, b_hi)


def _cmp_kernel(*refs):
    n_grp = len(refs) - 9
    x_refs = refs[:n_grp]
    w1_ref, w2_ref, pos_ref, cos_ref, sin_ref, o_ref, w1_hi_ref, w1_lo_ref, bias_ref = refs[n_grp:]
    kv = pl.program_id(0)
    n16 = o_ref.shape[3]
    half = NSA_CMP_STRIDE * HEAD_DIM

    @pl.when(pl.program_id(1) == 0)
    def _():
        hi, lo = _split(w1_ref[0])
        w1_hi_ref[...] = hi
        w1_lo_ref[...] = lo
        bias_ref[...] = _dot3_presplit(jnp.broadcast_to(pos_ref[0], (SUBLANES, 2 * half)), hi, lo)

    for gi, x_ref in enumerate(x_refs):
        x = jnp.concatenate(
            [x_ref[0, pl.ds(l, n16, stride=NSA_CMP_STRIDE), :] for l in range(NSA_CMP_STRIDE)], axis=1)
        first = _dot3_presplit(x, w1_hi_ref[:half], w1_lo_ref[:half])
        second = _dot3_presplit(x, w1_hi_ref[half:], w1_lo_ref[half:])
        hid = first + pltpu.roll(second, n16 - 1, axis=0) + bias_ref[0:1]
        y = _dot3_nn(jax.nn.gelu(hid), w2_ref[0])
        roped = _rope_full(y, cos_ref[...], sin_ref[...])
        o_ref[0, 0, gi] = jnp.where(kv == 0, roped, y)


def _nsa_compress(z3, w1, w2, posflat, cos_c, sin_c):
    b, s, _ = z3.shape
    g = NSA_KV_GROUPS
    n16 = s // NSA_CMP_STRIDE
    width = NSA_CMP_STRIDE * HEAD_DIM
    hid = w1.shape[2]
    base = OFF_CKC // LANES
    x_specs = [pl.BlockSpec((1, s, LANES), lambda kv, bi, gi=gi: (bi, 0, base + g * kv + gi)) for gi in range(g)]
    return pl.pallas_call(
        _cmp_kernel,
        grid=(2, b),
        in_specs=x_specs + [
            pl.BlockSpec((1, 2 * width, hid), lambda kv, bi: (kv, 0, 0)),
            pl.BlockSpec((1, hid, HEAD_DIM), lambda kv, bi: (kv, 0, 0)),
            pl.BlockSpec((1, 1, 2 * width), lambda kv, bi: (kv, 0, 0)),
            pl.BlockSpec((n16, HEAD_DIM), lambda kv, bi: (0, 0)),
            pl.BlockSpec((n16, HEAD_DIM), lambda kv, bi: (0, 0)),
        ],
        out_specs=pl.BlockSpec((1, 1, g, n16, HEAD_DIM), lambda kv, bi: (kv, bi, 0, 0, 0)),
        out_shape=jax.ShapeDtypeStruct((2, b, g, n16, HEAD_DIM), F32),
        scratch_shapes=[pltpu.VMEM((2 * width, hid), BF16), pltpu.VMEM((2 * width, hid), BF16),
                        pltpu.VMEM((SUBLANES, hid), F32)],
        compiler_params=_cparams(("parallel", "arbitrary")),
        name="nsa_compress",
    )(*([z3] * g), w1, w2, posflat, cos_c, sin_c)


def _nsa_kernel(q_ref, ks_ref, vs_ref, kw_ref, vw_ref, kc_ref, vc_ref, gate_ref,
                cosf_ref, sinf_ref, cosq_ref, sinq_ref, ovl_ref, o_ref,
                ksr_ref, vst_ref, kw3_ref, vwt3_ref, vct_ref, os_ref, *, n_sel, scale):
    qi = pl.program_id(1)
    tq = q_ref.shape[1]
    s_len = ks_ref.shape[1]
    n_grp = ks_ref.shape[2] // HEAD_DIM
    rep = NSA_REP
    lanes = rep * tq
    n_chunk = s_len // tq
    pad_chunks = NSA_WINDOW // tq
    n_cmp = kc_ref.shape[3]
    grp = lambda gg: slice(gg * HEAD_DIM, (gg + 1) * HEAD_DIM)
    groups = range(n_grp)

    @pl.when(qi == 0)
    def _():
        for gg in groups:
            ksr_ref[gg] = _rope_full(ks_ref[0, :, grp(gg)], cosf_ref[...], sinf_ref[...]).astype(BF16)
            vst_ref[gg] = vs_ref[0, :, grp(gg)].T.astype(BF16)
            kw3_ref[gg, 0:pad_chunks] = jnp.zeros((pad_chunks, tq, HEAD_DIM), BF16)
            vwt3_ref[gg, 0:pad_chunks] = jnp.zeros((pad_chunks, HEAD_DIM, tq), BF16)
            kwr = _rope_full(kw_ref[0, :, grp(gg)], cosf_ref[...], sinf_ref[...]).astype(BF16)
            for c in range(n_chunk):
                kw3_ref[gg, pad_chunks + c] = kwr[c * tq:(c + 1) * tq]
                vwt3_ref[gg, pad_chunks + c] = vw_ref[0, c * tq:(c + 1) * tq, grp(gg)].T.astype(BF16)
            vct_ref[gg] = vc_ref[0, 0, gg].T.astype(BF16)

    cos_q = cosq_ref[...]
    sin_q = sinq_ref[...]
    lane = lax.broadcasted_iota(jnp.int32, (1, lanes), 1)
    t_loc = lane % tq
    t = qi * tq + t_loc
    cur = t[:, 0:tq] // NSA_SEL_BLOCK
    blk = lax.broadcasted_iota(jnp.int32, (n_sel, tq), 0)
    forced = (blk == 0) | (blk == cur) | (blk == cur - 1)
    visible = blk <= cur
    cmp_end = (lax.broadcasted_iota(jnp.int32, (n_cmp, lanes), 0) * NSA_CMP_STRIDE + (NSA_CMP_LEN - 1))
    ovl = ovl_ref[...]

    q_bf, o_c, imp = [], [], []
    for gg in groups:
        q_all = jnp.concatenate(
            [_rope_full(q_ref[0, :, grp(gg * rep + r)], cos_q, sin_q) for r in range(rep)],
            axis=0)
        q_bf.append((q_all * (scale * LOG2E)).astype(BF16))

        s_c = _dot3_nt(kc_ref[0, 0, gg], q_all)
        e_c, l_c = _softmax_t(s_c, cmp_end <= t, scale)
        p_c = e_c / l_c
        o_c.append(_dot_nn(vct_ref[gg], p_c.astype(BF16)))

        p_hi, p_lo = _split(p_c)
        imp3 = _dot_nn(ovl, p_hi) + _dot_nn(ovl, p_lo)
        imp_g = imp3[:, 0:tq]
        for r in range(1, rep):
            imp_g = imp_g + imp3[:, r * tq:(r + 1) * tq]
        imp_g = jnp.where(forced, FORCE_SCORE, imp_g)
        imp.append(jnp.where(visible, imp_g, -jnp.inf))

    unit = tq
    sb = NSA_SEL_BLOCK
    for v in range(s_len // unit):
        @pl.when(qi == v)
        def _(v=v):
            n_keys = (v + 1) * unit
            lo = n_keys - unit

            def causal(base, a, n):
                return lax.broadcasted_iota(jnp.int32, (n, lanes), 0) <= (t - (base + a))

            n_vis = n_keys // sb
            for gg in groups:
                s = _dot_nt(ksr_ref[gg, 0:n_keys], q_bf[gg])
                if n_vis > NSA_SEL_TOPK:
                    rank = _rank_t(imp[gg], n_vis)
                    rank3 = jnp.concatenate([rank] * rep, axis=1)
                    drop = jnp.where(rank3 < NSA_SEL_TOPK, 0.0, NEG_BIG)
                    parts = []
                    for j in range(n_vis):
                        parts.append((s[j * sb:(j + 1) * sb],
                                      functools.partial(causal, j * sb) if j * sb >= lo else None,
                                      drop[j:j + 1, :]))
                else:
                    parts = [(s[0:lo], None, None)] if lo > 0 else []
                    parts.append((s[lo:], functools.partial(causal, lo), None))
                e_s, l_s = _softmax_parts(parts)
                os_ref[gg] = _dot_nn(vst_ref[gg, :, 0:n_keys], e_s) / l_s

    n_win = pad_chunks + 1
    def oldest(a, n):
        return lax.broadcasted_iota(jnp.int32, (n, lanes), 0) + a > t_loc

    def newest(a, n):
        return lax.broadcasted_iota(jnp.int32, (n, lanes), 0) + a <= t_loc

    for gg in groups:
        parts = []
        for c in range(n_win):
            s_w = _dot_nt(kw3_ref[gg, qi + c], q_bf[gg])
            pen = None
            if c < pad_chunks:
                pen = jnp.full((1, lanes), jnp.where(qi + c >= pad_chunks, 0.0, NEG_BIG), F32)
            parts.append((s_w, oldest if c == 0 else (newest if c == n_win - 1 else None), pen))
        e_w, l_w = _softmax_parts(parts)
        o_w = _dot_nn(vwt3_ref[gg, qi], e_w[0:tq])
        for c in range(1, n_win):
            o_w = o_w + _dot_nn(vwt3_ref[gg, qi + c], e_w[c * tq:(c + 1) * tq])
        o_w = o_w / l_w

        g_t = _sigmoid(gate_ref[0, :, grp(gg)].T)

        def gate_row(branch, g_t=g_t):
            return jnp.concatenate([g_t[3 * r + branch:3 * r + branch + 1, :] for r in range(rep)], axis=1)

        out_t = gate_row(0) * o_c[gg] + gate_row(1) * os_ref[gg] + gate_row(2) * o_w
        for r in range(rep):
            o_ref[0, :, grp(gg * rep + r)] = out_t[:, r * tq:(r + 1) * tq].T.astype(o_ref.dtype)


def _nsa(z3, cmp_kv, cos, sin_s, ovl_t, tq=256):
    b, s, _ = z3.shape
    g = NSA_KV_GROUPS
    n_sel = s // NSA_SEL_BLOCK
    n_cmp = cmp_kv.shape[3]
    qw = NSA_HEADS * HEAD_DIM
    kvw = g * HEAD_DIM
    n_chunk_pad = (s + NSA_WINDOW) // tq

    def col(off):
        return lambda bi, qi: (bi, 0, off // kvw)

    return pl.pallas_call(
        functools.partial(_nsa_kernel, n_sel=n_sel, scale=HEAD_DIM ** -0.5),
        grid=(b, s // tq),
        in_specs=[
            pl.BlockSpec((1, tq, qw), lambda bi, qi: (bi, qi, OFF_CQ // qw)),
            pl.BlockSpec((1, s, kvw), col(OFF_CKS)),
            pl.BlockSpec((1, s, kvw), col(OFF_CVS)),
            pl.BlockSpec((1, s, kvw), col(OFF_CKW)),
            pl.BlockSpec((1, s, kvw), col(OFF_CVW)),
            pl.BlockSpec((1, 1, g, n_cmp, HEAD_DIM), lambda bi, qi: (0, bi, 0, 0, 0)),
            pl.BlockSpec((1, 1, g, n_cmp, HEAD_DIM), lambda bi, qi: (1, bi, 0, 0, 0)),
            pl.BlockSpec((1, tq, g * LANES), lambda bi, qi: (bi, qi, OFF_CGATE // (g * LANES))),
            pl.BlockSpec((s, LANES), lambda bi, qi: (0, 0)),
            pl.BlockSpec((s, LANES), lambda bi, qi: (0, 0)),
            pl.BlockSpec((tq, LANES), lambda bi, qi: (qi, 0)),
            pl.BlockSpec((tq, LANES), lambda bi, qi: (qi, 0)),
            pl.BlockSpec((n_sel, n_cmp), lambda bi, qi: (0, 0)),
        ],
        out_specs=pl.BlockSpec((1, tq, qw), lambda bi, qi: (bi, qi, 0)),
        out_shape=jax.ShapeDtypeStruct((b, s, qw), BF16),
        scratch_shapes=[
            pltpu.VMEM((g, s, HEAD_DIM), BF16),
            pltpu.VMEM((g, HEAD_DIM, s), BF16),
            pltpu.VMEM((g, n_chunk_pad, tq, HEAD_DIM), BF16),
            pltpu.VMEM((g, n_chunk_pad, HEAD_DIM, tq), BF16),
            pltpu.VMEM((g, HEAD_DIM, n_cmp), BF16),
            pltpu.VMEM((g, HEAD_DIM, NSA_REP * tq), F32),
        ],
        compiler_params=_cparams(("parallel", "arbitrary")),
        name="nsa",
    )(z3, z3, z3, z3, z3, cmp_kv, cmp_kv, z3, cos, sin_s, cos, sin_s, ovl_t)


def _merge_kernel(ya_ref, yb_ref, yc_ref, ga_ref, gb_ref, gc_ref, wa_ref, wb_ref, wc_ref, o_ref,
                  wa_bf_ref, wb_bf_ref, wc_bf_ref):
    @pl.when(pl.program_id(1) == 0)
    def _():
        wa_bf_ref[...] = wa_ref[...].astype(BF16)
        wb_bf_ref[...] = wb_ref[...].astype(BF16)
        wc_bf_ref[...] = wc_ref[...].astype(BF16)

    m = ga_ref[...].astype(F32) * _dot_nn(ya_ref[...], wa_bf_ref[...])
    m = m + gb_ref[...].astype(F32) * _dot_nn(yb_ref[...], wb_bf_ref[...])
    m = m + gc_ref[...].astype(F32) * _dot_nn(yc_ref[...], wc_bf_ref[...])
    o_ref[...] = m.astype(o_ref.dtype)


def _merge(ya, yb, yc, gates, wa, wb, wc, layer, tm=1024, tn=1024):
    t = ya.shape[0]
    d = wa.shape[2]
    nj = d // tn

    def gate_spec(branch):
        return pl.BlockSpec((tm, tn), lambda j, i: (i, branch * nj + j))

    def y_spec(y):
        return pl.BlockSpec((tm, y.shape[1]), lambda j, i: (i, 0))

    def w_spec(w):
        return pl.BlockSpec((None, w.shape[1], tn), lambda j, i: (layer, 0, j), pipeline_mode=pl.Buffered(1))

    return pl.pallas_call(
        _merge_kernel,
        grid=(nj, t // tm),
        in_specs=[y_spec(ya), y_spec(yb), y_spec(yc), gate_spec(0), gate_spec(1), gate_spec(2),
                  w_spec(wa), w_spec(wb), w_spec(wc)],
        out_specs=pl.BlockSpec((tm, tn), lambda j, i: (i, j)),
        out_shape=jax.ShapeDtypeStruct((t, d), BF16),
        scratch_shapes=[pltpu.VMEM((w.shape[1], tn), BF16) for w in (wa, wb, wc)],
        compiler_params=_cparams(("parallel", "arbitrary")),
        name="merge",
    )(ya, yb, yc, gates, gates, gates, wa, wb, wc)


def _proj_residual_kernel(m_ref, w_ref, x_ref, o_ref, w_bf_ref):
    @pl.when(pl.program_id(1) == 0)
    def _():
        w_bf_ref[...] = w_ref[...].astype(BF16)

    o_ref[...] = x_ref[...] + _dot_nn(m_ref[...], w_bf_ref[...])


def _proj_residual(m, w, x2d, layer, tm=1024, tn=1024):
    t, k = m.shape
    d = w.shape[2]
    return pl.pallas_call(
        _proj_residual_kernel,
        grid=(d // tn, t // tm),
        in_specs=[
            pl.BlockSpec((tm, k), lambda j, i: (i, 0)),
            pl.BlockSpec((None, k, tn), lambda j, i: (layer, 0, j), pipeline_mode=pl.Buffered(1)),
            pl.BlockSpec((tm, tn), lambda j, i: (i, j)),
        ],
        out_specs=pl.BlockSpec((tm, tn), lambda j, i: (i, j)),
        out_shape=jax.ShapeDtypeStruct((t, d), F32),
        scratch_shapes=[pltpu.VMEM((k, tn), BF16)],
        compiler_params=_cparams(("parallel", "arbitrary")),
        name="out_proj",
    )(m, w, x2d)


def _rope_tables(pos, dim, width):
    inv = 1.0 / (ROPE_THETA ** (jnp.arange(0, dim, 2, dtype=F32) / dim))
    ang = pos.astype(F32)[:, None] * inv[None, :]
    cos = jnp.cos(ang)
    sin = jnp.sin(ang)
    cos = jnp.concatenate([cos, cos], axis=-1)
    sin_s = jnp.concatenate([-sin, sin], axis=-1)
    pad = ((0, 0), (0, width - dim))
    return jnp.pad(cos, pad), jnp.pad(sin_s, pad)


def _repack_kernel(srcp_ref, srcq_ref, start_ref, valid_ref, p_ref, q_ref, o_ref, *, n_src_cols):
    j = pl.program_id(1)
    start = start_ref[j]
    valid = valid_ref[j]
    row = lax.broadcasted_iota(jnp.int32, (LANES, LANES), 0)
    col = lax.broadcasted_iota(jnp.int32, (LANES, LANES), 1)
    lane = lax.broadcasted_iota(jnp.int32, (1, LANES), 1)

    def contribution(blk_ref, src_blk):
        take = jnp.where((src_blk * LANES + row == start + col) & (col < valid), 1.0, 0.0).astype(BF16)
        blk = jnp.where(src_blk * LANES + lane < n_src_cols, blk_ref[...], 0.0).astype(BF16)
        return _dot_nn(blk, take)

    o_ref[...] = (contribution(p_ref, srcp_ref[j]) + contribution(q_ref, srcq_ref[j])).astype(o_ref.dtype)


def _repack_tables(pieces, n_out_blocks, n_src_blocks):
    starts, valids = [], []
    for start, width in pieces:
        if width == 0:
            starts.append(0)
            valids.append(0)
        for off in range(0, width, LANES):
            starts.append(start + off)
            valids.append(min(LANES, width - off))
    starts += [0] * (n_out_blocks - len(starts))
    valids += [0] * (n_out_blocks - len(valids))
    assert n_src_blocks >= 2
    slots = [0, 1]
    srcp, srcq = [], []
    for start, valid in zip(starts, valids):
        need = []
        if valid > 0:
            need = list(range(start // LANES, (start + valid - 1) // LANES + 1))
        for blk in need:
            if blk not in slots:
                slots[0 if slots[0] not in need else 1] = blk
        srcp.append(slots[0])
        srcq.append(slots[1])
    return tuple(jnp.asarray(np.asarray(v, np.int32)) for v in (srcp, srcq, starts, valids))


def _repack(w, pieces, n_out_blocks):
    depth, k, n = w.shape
    tables = _repack_tables(pieces, n_out_blocks, -(-n // LANES))
    grid_spec = pltpu.PrefetchScalarGridSpec(
        num_scalar_prefetch=4,
        grid=(depth, n_out_blocks),
        in_specs=[
            pl.BlockSpec((None, k, LANES), lambda d, j, sp, sq, st, va: (d, 0, sp[j])),
            pl.BlockSpec((None, k, LANES), lambda d, j, sp, sq, st, va: (d, 0, sq[j])),
        ],
        out_specs=pl.BlockSpec((None, k, LANES), lambda d, j, sp, sq, st, va: (d, 0, j)),
    )
    return pl.pallas_call(
        functools.partial(_repack_kernel, n_src_cols=n),
        grid_spec=grid_spec,
        out_shape=jax.ShapeDtypeStruct((depth, k, n_out_blocks * LANES), BF16),
        compiler_params=_cparams(("parallel", "arbitrary")),
        name="repack",
    )(*tables, w, w)


def _repack_t_kernel(srcp_ref, srcq_ref, start_ref, valid_ref, p_ref, q_ref, o_ref, *, n_src_cols):
    j = pl.program_id(0)
    start = start_ref[j]
    valid = valid_ref[j]
    depth = o_ref.shape[0]
    rows = LANES * depth
    out_row = lax.broadcasted_iota(jnp.int32, (LANES, rows), 0)
    flat = lax.broadcasted_iota(jnp.int32, (LANES, rows), 1)
    sub = lax.broadcasted_iota(jnp.int32, (rows, 1), 0)

    def flattened(blk_ref, src_blk):
        x = blk_ref[...].reshape(rows, blk_ref.shape[2])
        return jnp.where(src_blk * LANES + sub // depth < n_src_cols, x, 0.0).astype(BF16)

    def take(src_blk, layer):
        hit = (flat == (start + out_row - src_blk * LANES) * depth + layer) & (out_row < valid)
        return jnp.where(hit, 1.0, 0.0).astype(BF16)

    xp, xq = flattened(p_ref, srcp_ref[j]), flattened(q_ref, srcq_ref[j])
    for layer in range(depth):
        o_ref[layer] = (_dot_nn(take(srcp_ref[j], layer), xp)
                        + _dot_nn(take(srcq_ref[j], layer), xq)).astype(o_ref.dtype)


def _repack_t(wt, pieces, n_out_blocks):
    n, depth, k = wt.shape
    tables = _repack_tables(pieces, n_out_blocks, -(-n // LANES))
    grid_spec = pltpu.PrefetchScalarGridSpec(
        num_scalar_prefetch=4,
        grid=(n_out_blocks,),
        in_specs=[
            pl.BlockSpec((LANES, depth, k), lambda j, sp, sq, st, va: (sp[j], 0, 0)),
            pl.BlockSpec((LANES, depth, k), lambda j, sp, sq, st, va: (sq[j], 0, 0)),
        ],
        out_specs=pl.BlockSpec((depth, LANES, k), lambda j, sp, sq, st, va: (0, j, 0)),
    )
    return pl.pallas_call(
        functools.partial(_repack_t_kernel, n_src_cols=n),
        grid_spec=grid_spec,
        out_shape=jax.ShapeDtypeStruct((depth, n_out_blocks * LANES, k), BF16),
        compiler_params=_cparams(("arbitrary",)),
        name="repack_t",
    )(*tables, wt, wt)


def _pack_w_in(w):
    sizes = [MOBA_HEADS * HEAD_DIM] * 3 + [MLA_Q_RANK, MLA_KV_RANK, MLA_ROPE, NSA_HEADS * HEAD_DIM] \
        + [NSA_KV_GROUPS * HEAD_DIM] * 6
    offs = [int(v) for v in np.concatenate([[0], np.cumsum(sizes)])]
    (a_q, a_k, a_v, b_cq, b_ckv, b_kr, c_q, c_kc, c_vc, c_ks, c_vs, c_kw, c_vw) = [
        (offs[i], sizes[i]) for i in range(len(sizes))]
    gate0 = offs[-1]
    per_group = NSA_REP * 3
    gates = [(gate0 + gi * per_group, per_group) for gi in range(NSA_KV_GROUPS)]
    merge = (gate0 + NSA_HEADS * 3, w.shape[2] - gate0 - NSA_HEADS * 3)
    pieces = [c_q, b_ckv, b_cq, a_q, a_k, a_v, c_kc, c_vc, c_ks, c_vs, c_kw, c_vw] + gates + [b_kr]
    used = sum(-(-width // LANES) for _, width in pieces)
    pieces += [(0, 0)] * (Z_COLS // LANES - used) + [merge]
    return _repack_t(jnp.transpose(w, (2, 0, 1)), pieces, (Z_COLS + merge[1]) // LANES)


def _pack_w_uq(w):
    per_head = MLA_NOPE + MLA_ROPE
    pieces = [(h * per_head, MLA_NOPE) for h in range(MLA_HEADS)] \
        + [(h * per_head + MLA_NOPE, MLA_ROPE) for h in range(MLA_HEADS)]
    return _repack(w, pieces, 2 * MLA_HEADS)


def _pack_w_ukv(w):
    per_head = MLA_NOPE + MLA_V
    pieces = [(h * per_head, MLA_NOPE) for h in range(MLA_HEADS)] \
        + [(h * per_head + MLA_NOPE, MLA_V) for h in range(MLA_HEADS)]
    return _repack(w, pieces, 2 * MLA_HEADS)


def kernel(x, ffn1_norm, ffn1_w_gate, ffn1_w_up, ffn1_w_down, mix_norm, w_in, mla_q_norm, mla_w_uq, mla_kv_norm, mla_w_ukv, nsa_cmp_pos_k, nsa_cmp_w1_k, nsa_cmp_w2_k, nsa_cmp_pos_v, nsa_cmp_w1_v, nsa_cmp_w2_v, w_branch_moba, w_branch_mla, w_branch_nsa, w_out, ffn2_norm, ffn2_w_gate, ffn2_w_up, ffn2_w_down, final_norm):
    b, s, d = x.shape
    depth = w_in.shape[0]
    t = b * s
    assert NSA_CMP_LEN == 2 * NSA_CMP_STRIDE and s % MOBA_BLOCK == 0 and s % NSA_SEL_BLOCK == 0

    pos = jnp.arange(s)
    cos_h, sin_h = _rope_tables(pos, HEAD_DIM, LANES)
    cos_r, sin_r = _rope_tables(pos, MLA_ROPE, LANES)
    n16 = s // NSA_CMP_STRIDE
    cmp_end = np.arange(n16) * NSA_CMP_STRIDE + NSA_CMP_LEN - 1
    cos_c, sin_c = _rope_tables(jnp.asarray(cmp_end), HEAD_DIM, LANES)
    n_sel = s // NSA_SEL_BLOCK
    sel_start = np.arange(n_sel) * NSA_SEL_BLOCK
    cmp_start = np.arange(n16) * NSA_CMP_STRIDE
    ovl_t = ((cmp_start[None, :] <= sel_start[:, None] + NSA_SEL_BLOCK - 1)
             & (cmp_end[None, :] >= sel_start[:, None])).astype(np.float32)
    ovl_t = jnp.asarray(ovl_t, BF16)

    w_in_p = _pack_w_in(w_in)
    w_uq_p = _pack_w_uq(mla_w_uq)
    w_ukv_p = _pack_w_ukv(mla_w_ukv)

    xf = x.reshape(t, d)
    for i in range(depth):
        xf = _ffn(xf, ffn1_norm[i], ffn1_w_gate, ffn1_w_up, ffn1_w_down, i)

        z, gates = _in_proj(xf, mix_norm[i], w_in_p, i)
        z3 = z.reshape(b, s, Z_COLS)
        y_a = _moba(z3, cos_h, sin_h)
        y_b = _mla(z3, mla_q_norm[i], mla_kv_norm[i], w_uq_p, w_ukv_p, i, cos_r, sin_r)

        w1 =jnp.stack([nsa_cmp_w1_k[i], nsa_cmp_w1_v[i]])
        w2 = jnp.stack([nsa_cmp_w2_k[i], nsa_cmp_w2_v[i]])
        posflat = jnp.stack([nsa_cmp_pos_k[i].reshape(1, -1), nsa_cmp_pos_v[i].reshape(1, -1)])
        cmp_kv = _nsa_compress(z3, w1, w2, posflat, cos_c, sin_c)
        y_c = _nsa(z3, cmp_kv, cos_h, sin_h, ovl_t)

        merged = _merge(y_a.reshape(t, -1), y_b.reshape(t, -1), y_c.reshape(t, -1), gates,
                        w_branch_moba, w_branch_mla, w_branch_nsa, i)
        xf = _proj_residual(merged, w_out, xf, i)

        xf = _ffn(xf, ffn2_norm[i], ffn2_w_gate, ffn2_w_up, ffn2_w_down, i)
    return _rms(xf, final_norm, F32).reshape(b, s, d)
```

```python
import functools

import numpy as np
import jax
import jax.numpy as jnp
from jax import lax
from jax.experimental import pallas as pl
from jax.experimental.pallas import tpu as pltpu

F32 = jnp.float32
BF16 = jnp.bfloat16

HEAD_DIM = 128
ROPE_THETA = 10000.0
NORM_EPS = 1e-6
NEG_BIG = -1e30
TINY = 1e-30
LOG2E = 1.4426950408889634
N_BRANCHES = 3

MOBA_HEADS = 4
MOBA_BLOCK = 256
MOBA_TOPK = 3

MLA_HEADS = 6
MLA_Q_RANK = 512
MLA_KV_RANK = 256
MLA_NOPE = 128
MLA_ROPE = 64
MLA_V = 128

NSA_HEADS = 6
NSA_KV_GROUPS = 2
NSA_REP = NSA_HEADS // NSA_KV_GROUPS
NSA_CMP_LEN = 32
NSA_CMP_STRIDE = 16
NSA_CMP_HIDDEN = 256
NSA_SEL_BLOCK = 64
NSA_SEL_TOPK = 16
NSA_WINDOW = 512
FORCE_SCORE = 1e9

LANES = 128
SUBLANES = 8
SOFTMAX_CHUNK = 64
SOFTMAX_LANES = 384
VMEM_LIMIT = 56 * 2**20

OFF_CQ = 0
OFF_BCKV = 768
OFF_BCQ = 1024
OFF_AQ = 1536
OFF_AK = 2048
OFF_AV = 2560
OFF_CKC = 3072
OFF_CVC = 3328
OFF_CKS = 3584
OFF_CVS = 3840
OFF_CKW = 4096
OFF_CVW = 4352
OFF_CGATE = 4608
OFF_BKR = 4864
Z_COLS = 5120


def _cparams(sem):
    return pltpu.CompilerParams(dimension_semantics=sem, vmem_limit_bytes=VMEM_LIMIT)


def _rmsnorm(x, g):
    ms = jnp.mean(x * x, axis=-1, keepdims=True)
    return x * lax.rsqrt(ms + NORM_EPS) * g


def _sigmoid(x):
    return 0.5 * jnp.tanh(0.5 * x) + 0.5


def _rope_full(x, cos, sin_signed):
    return x * cos + pltpu.roll(x, HEAD_DIM // 2, axis=1) * sin_signed


def _rope_half(x, cos, sin_signed):
    lane = lax.broadcasted_iota(jnp.int32, x.shape, 1)
    q = MLA_ROPE // 2
    swapped = jnp.where(lane < q, pltpu.roll(x, LANES - q, axis=1), pltpu.roll(x, q, axis=1))
    return x * cos + swapped * sin_signed


def _dot_nt(a, b):
    return lax.dot_general(a, b, (((1,), (1,)), ((), ())), preferred_element_type=F32)


def _dot_nn(a, b):
    return jnp.dot(a, b, preferred_element_type=F32)


def _split(a):
    hi = a.astype(BF16)
    lo = (a - hi.astype(F32)).astype(BF16)
    return hi, lo


def _dot3_nt(a, b):
    ah, al = _split(a)
    bh, bl = _split(b)
    return _dot_nt(ah, bh) + _dot_nt(ah, bl) + _dot_nt(al, bh)


def _dot3_nn(a, b):
    ah, al = _split(a)
    bh, bl = _split(b)
    return _dot_nn(ah, bh) + _dot_nn(ah, bl) + _dot_nn(al, bh)


def _softmax_t(s, mask, scale):
    s = jnp.where(mask, s * scale, NEG_BIG)
    m = jnp.max(s, axis=0, keepdims=True)
    e = jnp.where(mask, jnp.exp(s - m), 0.0)
    l = jnp.maximum(jnp.sum(e, axis=0, keepdims=True), TINY)
    return e, l


def _softmax_parts(parts, chunk=SOFTMAX_CHUNK):
    lanes = parts[0][0].shape[1]
    chunk = max(SUBLANES, chunk * SOFTMAX_LANES // max(lanes, SOFTMAX_LANES))
    pieces = []
    for s, mask_fn, pen in parts:
        for a in range(0, s.shape[0], chunk):
            n = min(chunk, s.shape[0] - a)
            p = s[a:a + n]
            pieces.append((p if mask_fn is None else jnp.where(mask_fn(a, n), p, NEG_BIG), pen))

    def fold(x, op):
        return op(x.reshape(x.shape[0] // SUBLANES, SUBLANES, lanes), axis=0)

    def col_max(p, pen):
        m = fold(p, jnp.max)
        return m if pen is None else m + pen

    m8 = functools.reduce(jnp.maximum, [col_max(p, pen) for p, pen in pieces])
    mx = jnp.max(m8, axis=0, keepdims=True)
    es, l8 = [], None
    for p, pen in pieces:
        e = jnp.exp2(p - (mx if pen is None else mx - pen))
        l8 = fold(e, jnp.sum) if l8 is None else l8 + fold(e, jnp.sum)
        es.append(e.astype(BF16))
    l = jnp.sum(l8, axis=0, keepdims=True)
    e = jnp.concatenate(es, axis=0) if len(es) > 1 else es[0]
    return e, jnp.maximum(l, TINY)


def _rank_t(v, nb):
    blk = lax.broadcasted_iota(jnp.int32, v.shape, 0)
    rank = jnp.zeros(v.shape, F32)
    for j in range(nb):
        vj = v[j:j + 1, :]
        ge = (vj >= v).astype(F32)
        gt = (vj > v).astype(F32)
        rank = rank + jnp.where(blk > j, ge, gt)
    return rank


def _rms_kernel(x_ref, g_ref, o_ref):
    o_ref[...] = _rmsnorm(x_ref[...], g_ref[...]).astype(o_ref.dtype)


def _rms_cast_kernel(x_ref, g_ref, w_ref, o_ref, w_bf_ref):
    o_ref[...] = _rmsnorm(x_ref[...], g_ref[...]).astype(o_ref.dtype)
    w_bf_ref[...] = w_ref[...].astype(w_bf_ref.dtype)


def _rms(x2d, g, out_dtype, tm=1024, cast=None):
    t, d = x2d.shape
    steps = t // tm
    in_specs = [pl.BlockSpec((tm, d), lambda i: (i, 0)), pl.BlockSpec((1, d), lambda i: (0, 0))]
    out_specs = pl.BlockSpec((tm, d), lambda i: (i, 0))
    out_shape = jax.ShapeDtypeStruct((t, d), out_dtype)
    if cast is None:
        return pl.pallas_call(
            _rms_kernel, grid=(steps,), in_specs=in_specs, out_specs=out_specs, out_shape=out_shape,
            compiler_params=_cparams(("parallel",)), name="rmsnorm",
        )(x2d, g.reshape(1, d))
    w, layer = cast
    rows, cols = w.shape[1:]
    slab = rows // steps
    assert slab * steps == rows and slab % (2 * SUBLANES) == 0
    return pl.pallas_call(
        _rms_cast_kernel,
        grid=(steps,),
        in_specs=in_specs + [pl.BlockSpec((None, slab, cols), lambda i: (layer, i, 0))],
        out_specs=[out_specs, pl.BlockSpec((slab, cols), lambda i: (i, 0))],
        out_shape=[out_shape, jax.ShapeDtypeStruct((rows, cols), BF16)],
        compiler_params=_cparams(("parallel",)),
        name="rmsnorm_cast",
    )(x2d, g.reshape(1, d), w)


def _ffn_up_kernel(h_ref, wg_ref, wu_ref, o_ref, wg_bf_ref, wu_bf_ref):
    @pl.when(pl.program_id(1) == 0)
    def _():
        wg_bf_ref[...] = wg_ref[...].astype(BF16)
        wu_bf_ref[...] = wu_ref[...].astype(BF16)

    h = h_ref[...]
    a = _dot_nn(h, wg_bf_ref[...])
    u = _dot_nn(h, wu_bf_ref[...])
    o_ref[...] = (a * _sigmoid(a) * u).astype(o_ref.dtype)


def _ffn_down_kernel(a_ref, wd_ref, x_ref, o_ref):
    o_ref[...] = x_ref[...] + 0.5 * _dot_nn(a_ref[...], wd_ref[...])


def _ffn(x2d, g, wg, wu, wd, layer, tm=1024, tf=512, tm_down=512, tn=1024):
    t, d = x2d.shape
    f = wg.shape[2]
    h, wd_bf = _rms(x2d, g, BF16, cast=(wd, layer))
    act = pl.pallas_call(
        _ffn_up_kernel,
        grid=(f // tf, t // tm),
        in_specs=[
            pl.BlockSpec((tm, d), lambda j, i: (i, 0)),
            pl.BlockSpec((None, d, tf), lambda j, i: (layer, 0, j)),
            pl.BlockSpec((None, d, tf), lambda j, i: (layer, 0, j)),
        ],
        out_specs=pl.BlockSpec((tm, tf), lambda j, i: (i, j)),
        out_shape=jax.ShapeDtypeStruct((t, f), BF16),
        scratch_shapes=[pltpu.VMEM((d, tf), BF16), pltpu.VMEM((d, tf), BF16)],
        compiler_params=_cparams(("parallel", "arbitrary")),
        name="ffn_up",
    )(h, wg, wu)
    return pl.pallas_call(
        _ffn_down_kernel,
        grid=(d // tn, t // tm_down),
        in_specs=[
            pl.BlockSpec((tm_down, f), lambda n, i: (i, 0)),
            pl.BlockSpec((f, tn), lambda n, i: (0, n)),
            pl.BlockSpec((tm_down, tn), lambda n, i: (i, n)),
        ],
        out_specs=pl.BlockSpec((tm_down, tn), lambda n, i: (i, n)),
        out_shape=jax.ShapeDtypeStruct((t, d), F32),
        compiler_params=_cparams(("parallel", "parallel")),
        name="ffn_down",
    )(act, wd_bf, x2d)


def _in_proj_kernel(x_ref, g_ref, w_ref, z_ref, gate_ref, h_ref, *, n_z):
    j = pl.program_id(1)

    @pl.when(j == 0)
    def _():
        h_ref[...] = _rmsnorm(x_ref[...], g_ref[...]).astype(BF16)

    @pl.when(j < n_z)
    def _():
        z_ref[...] = _dot_nt(h_ref[...], w_ref[...])

    @pl.when(j >= n_z)
    def _():
        gate_ref[...] = _sigmoid(_dot_nt(h_ref[...], w_ref[...])).astype(gate_ref.dtype)


def _in_proj(x2d, g, w, layer, tm=1024, tn=1024):
    t, d = x2d.shape
    n = w.shape[1]
    n_z = Z_COLS // tn
    n_g = (n - Z_COLS) // tn
    return pl.pallas_call(
        functools.partial(_in_proj_kernel, n_z=n_z),
        grid=(t // tm, n_z + n_g),
        in_specs=[
            pl.BlockSpec((tm, d), lambda i, j: (i, 0)),
            pl.BlockSpec((1, d), lambda i, j: (0, 0)),
            pl.BlockSpec((None, tn, d), lambda i, j: (layer, j, 0)),
        ],
        out_specs=[
            pl.BlockSpec((tm, tn), lambda i, j: (i, jnp.minimum(j, n_z - 1))),
            pl.BlockSpec((tm, tn), lambda i, j: (i, jnp.maximum(j - n_z, 0))),
        ],
        out_shape=[jax.ShapeDtypeStruct((t, Z_COLS), F32), jax.ShapeDtypeStruct((t, n - Z_COLS), BF16)],
        scratch_shapes=[pltpu.VMEM((tm, d), BF16)],
        compiler_params=_cparams(("parallel", "arbitrary")),
        name="in_proj",
    )(x2d, g.reshape(1, d), w)


def _moba_kernel(q_ref, k_ref, v_ref, cosf_ref, sinf_ref, cosq_ref, sinq_ref, o_ref,
                 kr_ref, km_ref, vt_ref, *, n_blk, scale):
    qi = pl.program_id(2)
    tq = q_ref.shape[1]
    hp = q_ref.shape[2] // LANES
    head = lambda hh: slice(hh * LANES, (hh + 1) * LANES)

    @pl.when(qi == 0)
    def _():
        for hh in range(hp):
            kr = _rope_full(k_ref[0, :, head(hh)], cosf_ref[...], sinf_ref[...])
            kr_ref[hh] = kr.astype(BF16)
            km_ref[hh] = jnp.concatenate(
                [jnp.mean(kr[j * MOBA_BLOCK:(j + 1) * MOBA_BLOCK], axis=0, keepdims=True) for j in range(n_blk)],
                axis=0)
            vt_ref[hh] = v_ref[0, :, head(hh)].T.astype(BF16)

    blk = MOBA_BLOCK
    bpt = tq // blk
    lane = lax.broadcasted_iota(jnp.int32, (1, tq), 1)
    lane_sub = lane // blk
    lane_loc = lane % blk

    for n in range(n_blk // bpt):
        @pl.when(qi == n)
        def _(n=n):
            first = n * bpt
            ranked = first + bpt - 1 > MOBA_TOPK
            for hh in range(hp):
                qr = _rope_full(q_ref[0, :, head(hh)], cosq_ref[...], sinq_ref[...])
                q_bf = (qr * (scale * LOG2E)).astype(BF16)
                s = _dot_nt(kr_ref[hh, 0:(n + 1) * tq], q_bf)
                drop = None
                if ranked:
                    gate = _dot3_nt(km_ref[hh], qr)
                    past = lax.broadcasted_iota(jnp.int32, (n_blk, tq), 0) < first + lane_sub
                    gate = jnp.where(past, gate, -jnp.inf)
                    drop = jnp.where(_rank_t(gate, first + bpt - 1) < MOBA_TOPK, 0.0, NEG_BIG)
                parts = []
                if drop is not None:
                    for j in range(first):
                        parts.append((s[j * blk:(j + 1) * blk], None, drop[j:j + 1, :]))
                elif first > 0:
                    parts.append((s[0:first * blk], None, None))
                for i2 in range(bpt):
                    j = first + i2
                    pen = None
                    if drop is not None and i2 < bpt - 1:
                        pen = jnp.where(lane_sub > i2, drop[j:j + 1, :], 0.0)

                    limit = jnp.where(lane_sub > i2, blk, jnp.where(lane_sub == i2, lane_loc, -1))

                    def mask_fn(a, rows, limit=limit):
                        return lax.broadcasted_iota(jnp.int32, (rows, tq), 0) + a <= limit

                    parts.append((s[j * blk:(j + 1) * blk], mask_fn, pen))
                e, l = _softmax_parts(parts)
                o_t = _dot_nn(vt_ref[hh, :, 0:(n + 1) * tq], e) / l
                o_ref[0, :, head(hh)] = o_t.T.astype(o_ref.dtype)


def _moba(z3, cos, sin_s, tq=2 * MOBA_BLOCK, hp=MOBA_HEADS):
    b, s, _ = z3.shape
    n_blk = s // MOBA_BLOCK
    w = hp * LANES
    qb, kb, vb = OFF_AQ // w, OFF_AK // w, OFF_AV // w
    return pl.pallas_call(
        functools.partial(_moba_kernel, n_blk=n_blk, scale=HEAD_DIM ** -0.5),
        grid=(b, MOBA_HEADS // hp, s // tq),
        in_specs=[
            pl.BlockSpec((1, tq, w), lambda bi, h, qi: (bi, qi, qb + h)),
            pl.BlockSpec((1, s, w), lambda bi, h, qi: (bi, 0, kb + h)),
            pl.BlockSpec((1, s, w), lambda bi, h, qi: (bi, 0, vb + h)),
            pl.BlockSpec((s, LANES), lambda bi, h, qi: (0, 0)),
            pl.BlockSpec((s, LANES), lambda bi, h, qi: (0, 0)),
            pl.BlockSpec((tq, LANES), lambda bi, h, qi: (qi, 0)),
            pl.BlockSpec((tq, LANES), lambda bi, h, qi: (qi, 0)),
        ],
        out_specs=pl.BlockSpec((1, tq, w), lambda bi, h, qi: (bi, qi, h)),
        out_shape=jax.ShapeDtypeStruct((b, s, MOBA_HEADS * HEAD_DIM), BF16),
        scratch_shapes=[pltpu.VMEM((hp, s, LANES), BF16), pltpu.VMEM((hp, n_blk, LANES), F32),
                        pltpu.VMEM((hp, LANES, s), BF16)],
        compiler_params=_cparams(("parallel", "parallel", "arbitrary")),
        name="moba",
    )(z3, z3, z3, cos, sin_s, cos, sin_s)


def _mla_kernel(cq_ref, ckv_ref, kr_ref, qg_ref, kvg_ref, wqn_ref, wqr_ref, wkn_ref, wv_ref,
                cosf_ref, sinf_ref, cosq_ref, sinq_ref, o_ref, kf_ref, vt_ref, *, scale):
    qi = pl.program_id(2)
    tq = cq_ref.shape[1]
    s_len = ckv_ref.shape[1]
    hp = wqn_ref.shape[1] // LANES
    head = lambda hh: slice(hh * LANES, (hh + 1) * LANES)

    @pl.when(qi == 0)
    def _():
        ckv = _rmsnorm(ckv_ref[0], kvg_ref[...]).astype(BF16)
        k_nope = _dot_nn(ckv, wkn_ref[...])
        v = _dot_nn(ckv, wv_ref[...])
        k_rope = _rope_half(kr_ref[0], cosf_ref[...], sinf_ref[...]).astype(BF16)
        for hh in range(hp):
            kf_ref[hh, :, :LANES] = k_nope[:, head(hh)].astype(BF16)
            kf_ref[hh, :, LANES:] = k_rope
            vt_ref[hh] = v[:, head(hh)].T.astype(BF16)

    cq = _rmsnorm(cq_ref[0], qg_ref[...]).astype(BF16)
    q_nope = _dot_nn(cq, wqn_ref[...])
    q_rope = _dot_nn(cq, wqr_ref[...])
    def causal(a, n):
        return (lax.broadcasted_iota(jnp.int32, (n, tq), 0) + a
                <= lax.broadcasted_iota(jnp.int32, (n, tq), 1))

    for n in range(s_len // tq):
        @pl.when(qi == n)
        def _(n=n):
            for hh in range(hp):
                q = jnp.concatenate([q_nope[:, head(hh)],
                                     _rope_half(q_rope[:, head(hh)], cosq_ref[...], sinq_ref[...])], axis=1)
                q_bf = (q * (scale * LOG2E)).astype(BF16)
                s = _dot_nt(kf_ref[hh, 0:(n + 1) * tq], q_bf)
                parts = [(s[0:n * tq], None, None)] if n > 0 else []
                parts.append((s[n * tq:], causal, None))
                e, l = _softmax_parts(parts)
                o_t = _dot_nn(vt_ref[hh, :, 0:(n + 1) * tq], e) / l
                o_ref[0, :, head(hh)] = o_t.T.astype(o_ref.dtype)


def _mla(z3, q_gain, kv_gain, w_uq, w_ukv, layer, cos, sin_s, tq=512, hp=MLA_HEADS):
    b, s, _ = z3.shape
    w = hp * LANES
    ng = MLA_HEADS // hp
    return pl.pallas_call(
        functools.partial(_mla_kernel, scale=(MLA_NOPE + MLA_ROPE) ** -0.5),
        grid=(b, ng, s // tq),
        in_specs=[
            pl.BlockSpec((1, tq, MLA_Q_RANK), lambda bi, h, qi: (bi, qi, OFF_BCQ // MLA_Q_RANK)),
            pl.BlockSpec((1, s, MLA_KV_RANK), lambda bi, h, qi: (bi, 0, OFF_BCKV // MLA_KV_RANK)),
            pl.BlockSpec((1, s, LANES), lambda bi, h, qi: (bi, 0, OFF_BKR // LANES)),
            pl.BlockSpec((1, MLA_Q_RANK), lambda bi, h, qi: (0, 0)),
            pl.BlockSpec((1, MLA_KV_RANK), lambda bi, h, qi: (0, 0)),
            pl.BlockSpec((None, MLA_Q_RANK, w), lambda bi, h, qi: (layer, 0, h)),
            pl.BlockSpec((None, MLA_Q_RANK, w), lambda bi, h, qi: (layer, 0, ng + h)),
            pl.BlockSpec((None, MLA_KV_RANK, w), lambda bi, h, qi: (layer, 0, h)),
            pl.BlockSpec((None, MLA_KV_RANK, w), lambda bi, h, qi: (layer, 0, ng + h)),
            pl.BlockSpec((s, LANES), lambda bi, h, qi: (0, 0)),
            pl.BlockSpec((s, LANES), lambda bi, h, qi: (0, 0)),
            pl.BlockSpec((tq, LANES), lambda bi, h, qi: (qi, 0)),
            pl.BlockSpec((tq, LANES), lambda bi, h, qi: (qi, 0)),
        ],
        out_specs=pl.BlockSpec((1, tq, w), lambda bi, h, qi: (bi, qi, h)),
        out_shape=jax.ShapeDtypeStruct((b, s, MLA_HEADS * MLA_V), BF16),
        scratch_shapes=[pltpu.VMEM((hp, s, 2 * LANES), BF16), pltpu.VMEM((hp, LANES, s), BF16)],
        compiler_params=_cparams(("parallel", "parallel", "arbitrary")),
        name="mla",
    )(z3, z3, z3, q_gain.reshape(1, -1), kv_gain.reshape(1, -1), w_uq, w_uq, w_ukv, w_ukv,
      cos, sin_s, cos, sin_s)


def _dot3_presplit(a, b_hi, b_lo):
    ah, al = _split(a)
    return _dot_nn(ah, b_hi) + _dot_nn(ah, b_lo) + _dot_nn(al, b_hi)


def _cmp_kernel(*refs):
    n_grp = len(refs) - 9
    x_refs = refs[:n_grp]
    w1_ref, w2_ref, pos_ref, cos_ref, sin_ref, o_ref, w1_hi_ref, w1_lo_ref, bias_ref = refs[n_grp:]
    kv = pl.program_id(0)
    n16 = o_ref.shape[3]
    half = NSA_CMP_STRIDE * HEAD_DIM

    @pl.when(pl.program_id(1) == 0)
    def _():
        hi, lo = _split(w1_ref[0])
        w1_hi_ref[...] = hi
        w1_lo_ref[...] = lo
        bias_ref[...] = _dot3_presplit(jnp.broadcast_to(pos_ref[0], (SUBLANES, 2 * half)), hi, lo)

    for gi, x_ref in enumerate(x_refs):
        x = jnp.concatenate(
            [x_ref[0, pl.ds(l, n16, stride=NSA_CMP_STRIDE), :] for l in range(NSA_CMP_STRIDE)], axis=1)
        first = _dot3_presplit(x, w1_hi_ref[:half], w1_lo_ref[:half])
        second = _dot3_presplit(x, w1_hi_ref[half:], w1_lo_ref[half:])
        hid = first + pltpu.roll(second, n16 - 1, axis=0) + bias_ref[0:1]
        y = _dot3_nn(jax.nn.gelu(hid), w2_ref[0])
        roped = _rope_full(y, cos_ref[...], sin_ref[...])
        o_ref[0, 0, gi] = jnp.where(kv == 0, roped, y)


def _nsa_compress(z3, w1, w2, posflat, cos_c, sin_c):
    b, s, _ = z3.shape
    g = NSA_KV_GROUPS
    n16 = s // NSA_CMP_STRIDE
    width = NSA_CMP_STRIDE * HEAD_DIM
    hid = w1.shape[2]
    base = OFF_CKC // LANES
    x_specs = [pl.BlockSpec((1, s, LANES), lambda kv, bi, gi=gi: (bi, 0, base + g * kv + gi)) for gi in range(g)]
    return pl.pallas_call(
        _cmp_kernel,
        grid=(2, b),
        in_specs=x_specs + [
            pl.BlockSpec((1, 2 * width, hid), lambda kv, bi: (kv, 0, 0)),
            pl.BlockSpec((1, hid, HEAD_DIM), lambda kv, bi: (kv, 0, 0)),
            pl.BlockSpec((1, 1, 2 * width), lambda kv, bi: (kv, 0, 0)),
            pl.BlockSpec((n16, HEAD_DIM), lambda kv, bi: (0, 0)),
            pl.BlockSpec((n16, HEAD_DIM), lambda kv, bi: (0, 0)),
        ],
        out_specs=pl.BlockSpec((1, 1, g, n16, HEAD_DIM), lambda kv, bi: (kv, bi, 0, 0, 0)),
        out_shape=jax.ShapeDtypeStruct((2, b, g, n16, HEAD_DIM), F32),
        scratch_shapes=[pltpu.VMEM((2 * width, hid), BF16), pltpu.VMEM((2 * width, hid), BF16),
                        pltpu.VMEM((SUBLANES, hid), F32)],
        compiler_params=_cparams(("parallel", "arbitrary")),
        name="nsa_compress",
    )(*([z3] * g), w1, w2, posflat, cos_c, sin_c)


def _nsa_kernel(q_ref, ks_ref, vs_ref, kw_ref, vw_ref, kc_ref, vc_ref, gate_ref,
                cosf_ref, sinf_ref, cosq_ref, sinq_ref, ovl_ref, o_ref,
                ksr_ref, vst_ref, kw3_ref, vwt3_ref, vct_ref, os_ref, *, n_sel, scale):
    qi = pl.program_id(1)
    tq = q_ref.shape[1]
    s_len = ks_ref.shape[1]
    n_grp = ks_ref.shape[2] // HEAD_DIM
    rep = NSA_REP
    lanes = rep * tq
    n_chunk = s_len // tq
    pad_chunks = NSA_WINDOW // tq
    n_cmp = kc_ref.shape[3]
    grp = lambda gg: slice(gg * HEAD_DIM, (gg + 1) * HEAD_DIM)
    groups = range(n_grp)

    @pl.when(qi == 0)
    def _():
        for gg in groups:
            ksr_ref[gg] = _rope_full(ks_ref[0, :, grp(gg)], cosf_ref[...], sinf_ref[...]).astype(BF16)
            vst_ref[gg] = vs_ref[0, :, grp(gg)].T.astype(BF16)
            kw3_ref[gg, 0:pad_chunks] = jnp.zeros((pad_chunks, tq, HEAD_DIM), BF16)
            vwt3_ref[gg, 0:pad_chunks] = jnp.zeros((pad_chunks, HEAD_DIM, tq), BF16)
            kwr = _rope_full(kw_ref[0, :, grp(gg)], cosf_ref[...], sinf_ref[...]).astype(BF16)
            for c in range(n_chunk):
                kw3_ref[gg, pad_chunks + c] = kwr[c * tq:(c + 1) * tq]
                vwt3_ref[gg, pad_chunks + c] = vw_ref[0, c * tq:(c + 1) * tq, grp(gg)].T.astype(BF16)
            vct_ref[gg] = vc_ref[0, 0, gg].T.astype(BF16)

    cos_q = cosq_ref[...]
    sin_q = sinq_ref[...]
    lane = lax.broadcasted_iota(jnp.int32, (1, lanes), 1)
    t_loc = lane % tq
    t = qi * tq + t_loc
    cur = t[:, 0:tq] // NSA_SEL_BLOCK
    blk = lax.broadcasted_iota(jnp.int32, (n_sel, tq), 0)
    forced = (blk == 0) | (blk == cur) | (blk == cur - 1)
    visible = blk <= cur
    cmp_end = (lax.broadcasted_iota(jnp.int32, (n_cmp, lanes), 0) * NSA_CMP_STRIDE + (NSA_CMP_LEN - 1))
    ovl = ovl_ref[...]

    q_bf, o_c, imp = [], [], []
    for gg in groups:
        q_all = jnp.concatenate(
            [_rope_full(q_ref[0, :, grp(gg * rep + r)], cos_q, sin_q) for r in range(rep)],
            axis=0)
        q_bf.append((q_all * (scale * LOG2E)).astype(BF16))

        s_c = _dot3_nt(kc_ref[0, 0, gg], q_all)
        e_c, l_c = _softmax_t(s_c, cmp_end <= t, scale)
        p_c = e_c / l_c
        o_c.append(_dot_nn(vct_ref[gg], p_c.astype(BF16)))

        p_hi, p_lo = _split(p_c)
        imp3 = _dot_nn(ovl, p_hi) + _dot_nn(ovl, p_lo)
        imp_g = imp3[:, 0:tq]
        for r in range(1, rep):
            imp_g = imp_g + imp3[:, r * tq:(r + 1) * tq]
        imp_g = jnp.where(forced, FORCE_SCORE, imp_g)
        imp.append(jnp.where(visible, imp_g, -jnp.inf))

    unit = tq
    sb = NSA_SEL_BLOCK
    for v in range(s_len // unit):
        @pl.when(qi == v)
        def _(v=v):
            n_keys = (v + 1) * unit
            lo = n_keys - unit

            def causal(base, a, n):
                return lax.broadcasted_iota(jnp.int32, (n, lanes), 0) <= (t - (base + a))

            n_vis = n_keys // sb
            for gg in groups:
                s = _dot_nt(ksr_ref[gg, 0:n_keys], q_bf[gg])
                if n_vis > NSA_SEL_TOPK:
                    rank = _rank_t(imp[gg], n_vis)
                    rank3 = jnp.concatenate([rank] * rep, axis=1)
                    drop = jnp.where(rank3 < NSA_SEL_TOPK, 0.0, NEG_BIG)
                    parts = []
                    for j in range(n_vis):
                        parts.append((s[j * sb:(j + 1) * sb],
                                      functools.partial(causal, j * sb) if j * sb >= lo else None,
                                      drop[j:j + 1, :]))
                else:
                    parts = [(s[0:lo], None, None)] if lo > 0 else []
                    parts.append((s[lo:], functools.partial(causal, lo), None))
                e_s, l_s = _softmax_parts(parts)
                os_ref[gg] = _dot_nn(vst_ref[gg, :, 0:n_keys], e_s) / l_s

    n_win = pad_chunks + 1
    def oldest(a, n):
        return lax.broadcasted_iota(jnp.int32, (n, lanes), 0) + a > t_loc

    def newest(a, n):
        return lax.broadcasted_iota(jnp.int32, (n, lanes), 0) + a <= t_loc

    for gg in groups:
        parts = []
        for c in range(n_win):
            s_w = _dot_nt(kw3_ref[gg, qi + c], q_bf[gg])
            pen = None
            if c < pad_chunks:
                pen = jnp.full((1, lanes), jnp.where(qi + c >= pad_chunks, 0.0, NEG_BIG), F32)
            parts.append((s_w, oldest if c == 0 else (newest if c == n_win - 1 else None), pen))
        e_w, l_w = _softmax_parts(parts)
        o_w = _dot_nn(vwt3_ref[gg, qi], e_w[0:tq])
        for c in range(1, n_win):
            o_w = o_w + _dot_nn(vwt3_ref[gg, qi + c], e_w[c * tq:(c + 1) * tq])
        o_w = o_w / l_w

        g_t = _sigmoid(gate_ref[0, :, grp(gg)].T)

        def gate_row(branch, g_t=g_t):
            return jnp.concatenate([g_t[3 * r + branch:3 * r + branch + 1, :] for r in range(rep)], axis=1)

        out_t = gate_row(0) * o_c[gg] + gate_row(1) * os_ref[gg] + gate_row(2) * o_w
        for r in range(rep):
            o_ref[0, :, grp(gg * rep + r)] = out_t[:, r * tq:(r + 1) * tq].T.astype(o_ref.dtype)


def _nsa(z3, cmp_kv, cos, sin_s, ovl_t, tq=256):
    b, s, _ = z3.shape
    g = NSA_KV_GROUPS
    n_sel = s // NSA_SEL_BLOCK
    n_cmp = cmp_kv.shape[3]
    qw = NSA_HEADS * HEAD_DIM
    kvw = g * HEAD_DIM
    n_chunk_pad = (s + NSA_WINDOW) // tq

    def col(off):
        return lambda bi, qi: (bi, 0, off // kvw)

    return pl.pallas_call(
        functools.partial(_nsa_kernel, n_sel=n_sel, scale=HEAD_DIM ** -0.5),
        grid=(b, s // tq),
        in_specs=[
            pl.BlockSpec((1, tq, qw), lambda bi, qi: (bi, qi, OFF_CQ // qw)),
            pl.BlockSpec((1, s, kvw), col(OFF_CKS)),
            pl.BlockSpec((1, s, kvw), col(OFF_CVS)),
            pl.BlockSpec((1, s, kvw), col(OFF_CKW)),
            pl.BlockSpec((1, s, kvw), col(OFF_CVW)),
            pl.BlockSpec((1, 1, g, n_cmp, HEAD_DIM), lambda bi, qi: (0, bi, 0, 0, 0)),
            pl.BlockSpec((1, 1, g, n_cmp, HEAD_DIM), lambda bi, qi: (1, bi, 0, 0, 0)),
            pl.BlockSpec((1, tq, g * LANES), lambda bi, qi: (bi, qi, OFF_CGATE // (g * LANES))),
            pl.BlockSpec((s, LANES), lambda bi, qi: (0, 0)),
            pl.BlockSpec((s, LANES), lambda bi, qi: (0, 0)),
            pl.BlockSpec((tq, LANES), lambda bi, qi: (qi, 0)),
            pl.BlockSpec((tq, LANES), lambda bi, qi: (qi, 0)),
            pl.BlockSpec((n_sel, n_cmp), lambda bi, qi: (0, 0)),
        ],
        out_specs=pl.BlockSpec((1, tq, qw), lambda bi, qi: (bi, qi, 0)),
        out_shape=jax.ShapeDtypeStruct((b, s, qw), BF16),
        scratch_shapes=[
            pltpu.VMEM((g, s, HEAD_DIM), BF16),
            pltpu.VMEM((g, HEAD_DIM, s), BF16),
            pltpu.VMEM((g, n_chunk_pad, tq, HEAD_DIM), BF16),
            pltpu.VMEM((g, n_chunk_pad, HEAD_DIM, tq), BF16),
            pltpu.VMEM((g, HEAD_DIM, n_cmp), BF16),
            pltpu.VMEM((g, HEAD_DIM, NSA_REP * tq), F32),
        ],
        compiler_params=_cparams(("parallel", "arbitrary")),
        name="nsa",
    )(z3, z3, z3, z3, z3, cmp_kv, cmp_kv, z3, cos, sin_s, cos, sin_s, ovl_t)


def _merge_kernel(ya_ref, yb_ref, yc_ref, ga_ref, gb_ref, gc_ref, wa_ref, wb_ref, wc_ref, o_ref,
                  wa_bf_ref, wb_bf_ref, wc_bf_ref):
    @pl.when(pl.program_id(1) == 0)
    def _():
        wa_bf_ref[...] = wa_ref[...].astype(BF16)
        wb_bf_ref[...] = wb_ref[...].astype(BF16)
        wc_bf_ref[...] = wc_ref[...].astype(BF16)

    m = ga_ref[...].astype(F32) * _dot_nn(ya_ref[...], wa_bf_ref[...])
    m = m + gb_ref[...].astype(F32) * _dot_nn(yb_ref[...], wb_bf_ref[...])
    m = m + gc_ref[...].astype(F32) * _dot_nn(yc_ref[...], wc_bf_ref[...])
    o_ref[...] = m.astype(o_ref.dtype)


def _merge(ya, yb, yc, gates, wa, wb, wc, layer, tm=1024, tn=1024):
    t = ya.shape[0]
    d = wa.shape[2]
    nj = d // tn

    def gate_spec(branch):
        return pl.BlockSpec((tm, tn), lambda j, i: (i, branch * nj + j))

    def y_spec(y):
        return pl.BlockSpec((tm, y.shape[1]), lambda j, i: (i, 0))

    def w_spec(w):
        return pl.BlockSpec((None, w.shape[1], tn), lambda j, i: (layer, 0, j))

    return pl.pallas_call(
        _merge_kernel,
        grid=(nj, t // tm),
        in_specs=[y_spec(ya), y_spec(yb), y_spec(yc), gate_spec(0), gate_spec(1), gate_spec(2),
                  w_spec(wa), w_spec(wb), w_spec(wc)],
        out_specs=pl.BlockSpec((tm, tn), lambda j, i: (i, j)),
        out_shape=jax.ShapeDtypeStruct((t, d), BF16),
        scratch_shapes=[pltpu.VMEM((w.shape[1], tn), BF16) for w in (wa, wb, wc)],
        compiler_params=_cparams(("parallel", "arbitrary")),
        name="merge",
    )(ya, yb, yc, gates, gates, gates, wa, wb, wc)


def _proj_residual_kernel(m_ref, w_ref, x_ref, o_ref, w_bf_ref):
    @pl.when(pl.program_id(1) == 0)
    def _():
        w_bf_ref[...] = w_ref[...].astype(BF16)

    o_ref[...] = x_ref[...] + _dot_nn(m_ref[...], w_bf_ref[...])


def _proj_residual(m, w, x2d, layer, tm=1024, tn=1024):
    t, k = m.shape
    d = w.shape[2]
    return pl.pallas_call(
        _proj_residual_kernel,
        grid=(d // tn, t // tm),
        in_specs=[
            pl.BlockSpec((tm, k), lambda j, i: (i, 0)),
            pl.BlockSpec((None, k, tn), lambda j, i: (layer, 0, j)),
            pl.BlockSpec((tm, tn), lambda j, i: (i, j)),
        ],
        out_specs=pl.BlockSpec((tm, tn), lambda j, i: (i, j)),
        out_shape=jax.ShapeDtypeStruct((t, d), F32),
        scratch_shapes=[pltpu.VMEM((k, tn), BF16)],
        compiler_params=_cparams(("parallel", "arbitrary")),
        name="out_proj",
    )(m, w, x2d)


def _rope_tables(pos, dim, width):
    inv = 1.0 / (ROPE_THETA ** (jnp.arange(0, dim, 2, dtype=F32) / dim))
    ang = pos.astype(F32)[:, None] * inv[None, :]
    cos = jnp.cos(ang)
    sin = jnp.sin(ang)
    cos = jnp.concatenate([cos, cos], axis=-1)
    sin_s = jnp.concatenate([-sin, sin], axis=-1)
    pad = ((0, 0), (0, width - dim))
    return jnp.pad(cos, pad), jnp.pad(sin_s, pad)


def _repack_kernel(srcp_ref, srcq_ref, start_ref, valid_ref, p_ref, q_ref, o_ref, *, n_src_cols):
    j = pl.program_id(1)
    start = start_ref[j]
    valid = valid_ref[j]
    row = lax.broadcasted_iota(jnp.int32, (LANES, LANES), 0)
    col = lax.broadcasted_iota(jnp.int32, (LANES, LANES), 1)
    lane = lax.broadcasted_iota(jnp.int32, (1, LANES), 1)

    def contribution(blk_ref, src_blk):
        take = jnp.where((src_blk * LANES + row == start + col) & (col < valid), 1.0, 0.0).astype(BF16)
        blk = jnp.where(src_blk * LANES + lane < n_src_cols, blk_ref[...], 0.0).astype(BF16)
        return _dot_nn(blk, take)

    o_ref[...] = (contribution(p_ref, srcp_ref[j]) + contribution(q_ref, srcq_ref[j])).astype(o_ref.dtype)


def _repack_tables(pieces, n_out_blocks, n_src_blocks):
    starts, valids = [], []
    for start, width in pieces:
        if width == 0:
            starts.append(0)
            valids.append(0)
        for off in range(0, width, LANES):
            starts.append(start + off)
            valids.append(min(LANES, width - off))
    starts += [0] * (n_out_blocks - len(starts))
    valids += [0] * (n_out_blocks - len(valids))
    assert n_src_blocks >= 2
    slots = [0, 1]
    srcp, srcq = [], []
    for start, valid in zip(starts, valids):
        need = []
        if valid > 0:
            need = list(range(start // LANES, (start + valid - 1) // LANES + 1))
        for blk in need:
            if blk not in slots:
                slots[0 if slots[0] not in need else 1] = blk
        srcp.append(slots[0])
        srcq.append(slots[1])
    return tuple(jnp.asarray(np.asarray(v, np.int32)) for v in (srcp, srcq, starts, valids))


def _repack(w, pieces, n_out_blocks):
    depth, k, n = w.shape
    tables = _repack_tables(pieces, n_out_blocks, -(-n // LANES))
    grid_spec = pltpu.PrefetchScalarGridSpec(
        num_scalar_prefetch=4,
        grid=(depth, n_out_blocks),
        in_specs=[
            pl.BlockSpec((None, k, LANES), lambda d, j, sp, sq, st, va: (d, 0, sp[j])),
            pl.BlockSpec((None, k, LANES), lambda d, j, sp, sq, st, va: (d, 0, sq[j])),
        ],
        out_specs=pl.BlockSpec((None, k, LANES), lambda d, j, sp, sq, st, va: (d, 0, j)),
    )
    return pl.pallas_call(
        functools.partial(_repack_kernel, n_src_cols=n),
        grid_spec=grid_spec,
        out_shape=jax.ShapeDtypeStruct((depth, k, n_out_blocks * LANES), BF16),
        compiler_params=_cparams(("parallel", "arbitrary")),
        name="repack",
    )(*tables, w, w)


def _repack_t_kernel(srcp_ref, srcq_ref, start_ref, valid_ref, p_ref, q_ref, o_ref, *, n_src_cols):
    j = pl.program_id(0)
    start = start_ref[j]
    valid = valid_ref[j]
    depth = o_ref.shape[0]
    rows = LANES * depth
    out_row = lax.broadcasted_iota(jnp.int32, (LANES, rows), 0)
    flat = lax.broadcasted_iota(jnp.int32, (LANES, rows), 1)
    sub = lax.broadcasted_iota(jnp.int32, (rows, 1), 0)

    def flattened(blk_ref, src_blk):
        x = blk_ref[...].reshape(rows, blk_ref.shape[2])
        return jnp.where(src_blk * LANES + sub // depth < n_src_cols, x, 0.0).astype(BF16)

    def take(src_blk, layer):
        hit = (flat == (start + out_row - src_blk * LANES) * depth + layer) & (out_row < valid)
        return jnp.where(hit, 1.0, 0.0).astype(BF16)

    xp, xq = flattened(p_ref, srcp_ref[j]), flattened(q_ref, srcq_ref[j])
    for layer in range(depth):
        o_ref[layer] = (_dot_nn(take(srcp_ref[j], layer), xp)
                        + _dot_nn(take(srcq_ref[j], layer), xq)).astype(o_ref.dtype)


def _repack_t(wt, pieces, n_out_blocks):
    n, depth, k = wt.shape
    tables = _repack_tables(pieces, n_out_blocks, -(-n // LANES))
    grid_spec = pltpu.PrefetchScalarGridSpec(
        num_scalar_prefetch=4,
        grid=(n_out_blocks,),
        in_specs=[
            pl.BlockSpec((LANES, depth, k), lambda j, sp, sq, st, va: (sp[j], 0, 0)),
            pl.BlockSpec((LANES, depth, k), lambda j, sp, sq, st, va: (sq[j], 0, 0)),
        ],
        out_specs=pl.BlockSpec((depth, LANES, k), lambda j, sp, sq, st, va: (0, j, 0)),
    )
    return pl.pallas_call(
        functools.partial(_repack_t_kernel, n_src_cols=n),
        grid_spec=grid_spec,
        out_shape=jax.ShapeDtypeStruct((depth, n_out_blocks * LANES, k), BF16),
        compiler_params=_cparams(("arbitrary",)),
        name="repack_t",
    )(*tables, wt, wt)


def _pack_w_in(w):
    sizes = [MOBA_HEADS * HEAD_DIM] * 3 + [MLA_Q_RANK, MLA_KV_RANK, MLA_ROPE, NSA_HEADS * HEAD_DIM] \
        + [NSA_KV_GROUPS * HEAD_DIM] * 6
    offs = [int(v) for v in np.concatenate([[0], np.cumsum(sizes)])]
    (a_q, a_k, a_v, b_cq, b_ckv, b_kr, c_q, c_kc, c_vc, c_ks, c_vs, c_kw, c_vw) = [
        (offs[i], sizes[i]) for i in range(len(sizes))]
    gate0 = offs[-1]
    per_group = NSA_REP * 3
    gates = [(gate0 + gi * per_group, per_group) for gi in range(NSA_KV_GROUPS)]
    merge = (gate0 + NSA_HEADS * 3, w.shape[2] - gate0 - NSA_HEADS * 3)
    pieces = [c_q, b_ckv, b_cq, a_q, a_k, a_v, c_kc, c_vc, c_ks, c_vs, c_kw, c_vw] + gates + [b_kr]
    used = sum(-(-width // LANES) for _, width in pieces)
    pieces += [(0, 0)] * (Z_COLS // LANES - used) + [merge]
    return _repack_t(jnp.transpose(w, (2, 0, 1)), pieces, (Z_COLS + merge[1]) // LANES)


def _pack_w_uq(w):
    per_head = MLA_NOPE + MLA_ROPE
    pieces = [(h * per_head, MLA_NOPE) for h in range(MLA_HEADS)] \
        + [(h * per_head + MLA_NOPE, MLA_ROPE) for h in range(MLA_HEADS)]
    return _repack(w, pieces, 2 * MLA_HEADS)


def _pack_w_ukv(w):
    per_head = MLA_NOPE + MLA_V
    pieces = [(h * per_head, MLA_NOPE) for h in range(MLA_HEADS)] \
        + [(h * per_head + MLA_NOPE, MLA_V) for h in range(MLA_HEADS)]
    return _repack(w, pieces, 2 * MLA_HEADS)


def kernel(x, ffn1_norm, ffn1_w_gate, ffn1_w_up, ffn1_w_down, mix_norm, w_in, mla_q_norm, mla_w_uq, mla_kv_norm, mla_w_ukv, nsa_cmp_pos_k, nsa_cmp_w1_k, nsa_cmp_w2_k, nsa_cmp_pos_v, nsa_cmp_w1_v, nsa_cmp_w2_v, w_branch_moba, w_branch_mla, w_branch_nsa, w_out, ffn2_norm, ffn2_w_gate, ffn2_w_up, ffn2_w_down, final_norm):
    b, s, d = x.shape
    depth = w_in.shape[0]
    t = b * s
    assert NSA_CMP_LEN == 2 * NSA_CMP_STRIDE and s % MOBA_BLOCK == 0 and s % NSA_SEL_BLOCK == 0

    pos = jnp.arange(s)
    cos_h, sin_h = _rope_tables(pos, HEAD_DIM, LANES)
    cos_r, sin_r = _rope_tables(pos, MLA_ROPE, LANES)
    n16 = s // NSA_CMP_STRIDE
    cmp_end = np.arange(n16) * NSA_CMP_STRIDE + NSA_CMP_LEN - 1
    cos_c, sin_c = _rope_tables(jnp.asarray(cmp_end), HEAD_DIM, LANES)
    n_sel = s // NSA_SEL_BLOCK
    sel_start = np.arange(n_sel) * NSA_SEL_BLOCK
    cmp_start = np.arange(n16) * NSA_CMP_STRIDE
    ovl_t = ((cmp_start[None, :] <= sel_start[:, None] + NSA_SEL_BLOCK - 1)
             & (cmp_end[None, :] >= sel_start[:, None])).astype(np.float32)
    ovl_t = jnp.asarray(ovl_t, BF16)

    w_in_p = _pack_w_in(w_in)
    w_uq_p = _pack_w_uq(mla_w_uq)
    w_ukv_p = _pack_w_ukv(mla_w_ukv)

    xf = x.reshape(t, d)
    for i in range(depth):
        xf = _ffn(xf, ffn1_norm[i], ffn1_w_gate, ffn1_w_up, ffn1_w_down, i)

        z, gates = _in_proj(xf, mix_norm[i], w_in_p, i)
        z3 = z.reshape(b, s, Z_COLS)
        y_a = _moba(z3, cos_h, sin_h)
        y_b = _mla(z3, mla_q_norm[i], mla_kv_norm[i], w_uq_p, w_ukv_p, i, cos_r, sin_r)

        w1 =jnp.stack([nsa_cmp_w1_k[i], nsa_cmp_w1_v[i]])
        w2 = jnp.stack([nsa_cmp_w2_k[i], nsa_cmp_w2_v[i]])
        posflat = jnp.stack([nsa_cmp_pos_k[i].reshape(1, -1), nsa_cmp_pos_v[i].reshape(1, -1)])
        cmp_kv = _nsa_compress(z3, w1, w2, posflat, cos_c, sin_c)
        y_c = _nsa(z3, cmp_kv, cos_h, sin_h, ovl_t)

        merged = _merge(y_a.reshape(t, -1), y_b.reshape(t, -1), y_c.reshape(t, -1), gates,
                        w_branch_moba, w_branch_mla, w_branch_nsa, i)
        xf = _proj_residual(merged, w_out, xf, i)

        xf = _ffn(xf, ffn2_norm[i], ffn2_w_gate, ffn2_w_up, ffn2_w_down, i)
    return _rms(xf, final_norm, F32).reshape(b, s, d)
```

```python
import functools

import numpy as np
import jax
import jax.numpy as jnp
from jax import lax
from jax.experimental import pallas as pl
from jax.experimental.pallas import tpu as pltpu

F32 = jnp.float32
BF16 = jnp.bfloat16

HEAD_DIM = 128
ROPE_THETA = 10000.0
NORM_EPS = 1e-6
NEG_BIG = -1e30
TINY = 1e-30
LOG2E = 1.4426950408889634
N_BRANCHES = 3

MOBA_HEADS = 4
MOBA_BLOCK = 256
MOBA_TOPK = 3

MLA_HEADS = 6
MLA_Q_RANK = 512
MLA_KV_RANK = 256
MLA_NOPE = 128
MLA_ROPE = 64
MLA_V = 128

NSA_HEADS = 6
NSA_KV_GROUPS = 2
NSA_REP = NSA_HEADS // NSA_KV_GROUPS
NSA_CMP_LEN = 32
NSA_CMP_STRIDE = 16
NSA_CMP_HIDDEN = 256
NSA_SEL_BLOCK = 64
NSA_SEL_TOPK = 16
NSA_WINDOW = 512
FORCE_SCORE = 1e9

LANES = 128
SUBLANES = 8
SOFTMAX_CHUNK = 64
SOFTMAX_LANES = 384
VMEM_LIMIT = 56 * 2**20

OFF_CQ = 0
OFF_BCKV = 768
OFF_BCQ = 1024
OFF_AQ = 1536
OFF_AK = 2048
OFF_AV = 2560
OFF_CKC = 3072
OFF_CVC = 3328
OFF_CKS = 3584
OFF_CVS = 3840
OFF_CKW = 4096
OFF_CVW = 4352
OFF_CGATE = 4608
OFF_BKR = 4864
Z_COLS = 5120


def _cparams(sem):
    return pltpu.CompilerParams(dimension_semantics=sem, vmem_limit_bytes=VMEM_LIMIT)


def _rmsnorm(x, g):
    ms = jnp.mean(x * x, axis=-1, keepdims=True)
    return x * lax.rsqrt(ms + NORM_EPS) * g


def _sigmoid(x):
    return 0.5 * jnp.tanh(0.5 * x) + 0.5


def _rope_full(x, cos, sin_signed):
    return x * cos + pltpu.roll(x, HEAD_DIM // 2, axis=1) * sin_signed


def _rope_half(x, cos, sin_signed):
    lane = lax.broadcasted_iota(jnp.int32, x.shape, 1)
    q = MLA_ROPE // 2
    swapped = jnp.where(lane < q, pltpu.roll(x, LANES - q, axis=1), pltpu.roll(x, q, axis=1))
    return x * cos + swapped * sin_signed


def _dot_nt(a, b):
    return lax.dot_general(a, b, (((1,), (1,)), ((), ())), preferred_element_type=F32)


def _dot_nn(a, b):
    return jnp.dot(a, b, preferred_element_type=F32)


def _split(a):
    hi = a.astype(BF16)
    lo = (a - hi.astype(F32)).astype(BF16)
    return hi, lo


def _dot3_nt(a, b):
    ah, al = _split(a)
    bh, bl = _split(b)
    return _dot_nt(ah, bh) + _dot_nt(ah, bl) + _dot_nt(al, bh)


def _dot3_nn(a, b):
    ah, al = _split(a)
    bh, bl = _split(b)
    return _dot_nn(ah, bh) + _dot_nn(ah, bl) + _dot_nn(al, bh)


def _softmax_t(s, mask, scale):
    s = jnp.where(mask, s * scale, NEG_BIG)
    m = jnp.max(s, axis=0, keepdims=True)
    e = jnp.where(mask, jnp.exp(s - m), 0.0)
    l = jnp.maximum(jnp.sum(e, axis=0, keepdims=True), TINY)
    return e, l


def _softmax_parts(parts, chunk=SOFTMAX_CHUNK):
    lanes = parts[0][0].shape[1]
    chunk = max(SUBLANES, chunk * SOFTMAX_LANES // max(lanes, SOFTMAX_LANES))
    pieces = []
    for s, mask_fn, pen in parts:
        for a in range(0, s.shape[0], chunk):
            n = min(chunk, s.shape[0] - a)
            p = s[a:a + n]
            pieces.append((p if mask_fn is None else jnp.where(mask_fn(a, n), p, NEG_BIG), pen))

    def fold(x, op):
        return op(x.reshape(x.shape[0] // SUBLANES, SUBLANES, lanes), axis=0)

    def col_max(p, pen):
        m = fold(p, jnp.max)
        return m if pen is None else m + pen

    m8 = functools.reduce(jnp.maximum, [col_max(p, pen) for p, pen in pieces])
    mx = jnp.max(m8, axis=0, keepdims=True)
    es, l8 = [], None
    for p, pen in pieces:
        e = jnp.exp2(p - (mx if pen is None else mx - pen))
        l8 = fold(e, jnp.sum) if l8 is None else l8 + fold(e, jnp.sum)
        es.append(e.astype(BF16))
    l = jnp.sum(l8, axis=0, keepdims=True)
    e = jnp.concatenate(es, axis=0) if len(es) > 1 else es[0]
    return e, jnp.maximum(l, TINY)


def _rank_t(v, nb):
    blk = lax.broadcasted_iota(jnp.int32, v.shape, 0)
    rank = jnp.zeros(v.shape, F32)
    for j in range(nb):
        vj = v[j:j + 1, :]
        ge = (vj >= v).astype(F32)
        gt = (vj > v).astype(F32)
        rank = rank + jnp.where(blk > j, ge, gt)
    return rank


def _rms_kernel(x_ref, g_ref, o_ref):
    o_ref[...] = _rmsnorm(x_ref[...], g_ref[...]).astype(o_ref.dtype)


def _rms_cast_kernel(x_ref, g_ref, w_ref, o_ref, w_bf_ref):
    o_ref[...] = _rmsnorm(x_ref[...], g_ref[...]).astype(o_ref.dtype)
    w_bf_ref[...] = w_ref[...].astype(w_bf_ref.dtype)


def _rms(x2d, g, out_dtype, tm=1024, cast=None):
    t, d = x2d.shape
    steps = t // tm
    in_specs = [pl.BlockSpec((tm, d), lambda i: (i, 0)), pl.BlockSpec((1, d), lambda i: (0, 0))]
    out_specs = pl.BlockSpec((tm, d), lambda i: (i, 0))
    out_shape = jax.ShapeDtypeStruct((t, d), out_dtype)
    if cast is None:
        return pl.pallas_call(
            _rms_kernel, grid=(steps,), in_specs=in_specs, out_specs=out_specs, out_shape=out_shape,
            compiler_params=_cparams(("parallel",)), name="rmsnorm",
        )(x2d, g.reshape(1, d))
    w, layer = cast
    rows, cols = w.shape[1:]
    slab = rows // steps
    assert slab * steps == rows and slab % (2 * SUBLANES) == 0
    return pl.pallas_call(
        _rms_cast_kernel,
        grid=(steps,),
        in_specs=in_specs + [pl.BlockSpec((None, slab, cols), lambda i: (layer, i, 0))],
        out_specs=[out_specs, pl.BlockSpec((slab, cols), lambda i: (i, 0))],
        out_shape=[out_shape, jax.ShapeDtypeStruct((rows, cols), BF16)],
        compiler_params=_cparams(("parallel",)),
        name="rmsnorm_cast",
    )(x2d, g.reshape(1, d), w)


def _ffn_up_kernel(h_ref, wg_ref, wu_ref, o_ref, wg_bf_ref, wu_bf_ref):
    @pl.when(pl.program_id(1) == 0)
    def _():
        wg_bf_ref[...] = wg_ref[...].astype(BF16)
        wu_bf_ref[...] = wu_ref[...].astype(BF16)

    h = h_ref[...]
    a = _dot_nn(h, wg_bf_ref[...])
    u = _dot_nn(h, wu_bf_ref[...])
    o_ref[...] = (a * _sigmoid(a) * u).astype(o_ref.dtype)


def _ffn_down_kernel(a_ref, wd_ref, x_ref, o_ref):
    o_ref[...] = x_ref[...] + 0.5 * _dot_nn(a_ref[...], wd_ref[...])


def _ffn(x2d, g, wg, wu, wd, layer, h=None, tm=1024, tf=512, tm_down=512, tn=1024):
    t, d = x2d.shape
    f = wg.shape[2]
    if h is None:
        h, wd_bf = _rms(x2d, g, BF16, cast=(wd, layer))
    else:
        wd_bf = _cast_bf16(wd, layer)
    act = pl.pallas_call(
        _ffn_up_kernel,
        grid=(f // tf, t // tm),
        in_specs=[
            pl.BlockSpec((tm, d), lambda j, i: (i, 0)),
            pl.BlockSpec((None, d, tf), lambda j, i: (layer, 0, j)),
            pl.BlockSpec((None, d, tf), lambda j, i: (layer, 0, j)),
        ],
        out_specs=pl.BlockSpec((tm, tf), lambda j, i: (i, j)),
        out_shape=jax.ShapeDtypeStruct((t, f), BF16),
        scratch_shapes=[pltpu.VMEM((d, tf), BF16), pltpu.VMEM((d, tf), BF16)],
        compiler_params=_cparams(("parallel", "arbitrary")),
        name="ffn_up",
    )(h, wg, wu)
    return pl.pallas_call(
        _ffn_down_kernel,
        grid=(d // tn, t // tm_down),
        in_specs=[
            pl.BlockSpec((tm_down, f), lambda n, i: (i, 0)),
            pl.BlockSpec((f, tn), lambda n, i: (0, n)),
            pl.BlockSpec((tm_down, tn), lambda n, i: (i, n)),
        ],
        out_specs=pl.BlockSpec((tm_down, tn), lambda n, i: (i, n)),
        out_shape=jax.ShapeDtypeStruct((t, d), F32),
        compiler_params=_cparams(("parallel", "parallel")),
        name="ffn_down",
    )(act, wd_bf, x2d)


def _in_proj_kernel(x_ref, g_ref, w_ref, z_ref, gate_ref, h_ref, *, n_z):
    j = pl.program_id(1)

    @pl.when(j == 0)
    def _():
        h_ref[...] = _rmsnorm(x_ref[...], g_ref[...]).astype(BF16)

    @pl.when(j < n_z)
    def _():
        z_ref[...] = _dot_nt(h_ref[...], w_ref[...])

    @pl.when(j >= n_z)
    def _():
        gate_ref[...] = _sigmoid(_dot_nt(h_ref[...], w_ref[...])).astype(gate_ref.dtype)


def _in_proj(x2d, g, w, layer, tm=1024, tn=1024):
    t, d = x2d.shape
    n = w.shape[1]
    n_z = Z_COLS // tn
    n_g = (n - Z_COLS) // tn
    return pl.pallas_call(
        functools.partial(_in_proj_kernel, n_z=n_z),
        grid=(t // tm, n_z + n_g),
        in_specs=[
            pl.BlockSpec((tm, d), lambda i, j: (i, 0)),
            pl.BlockSpec((1, d), lambda i, j: (0, 0)),
            pl.BlockSpec((None, tn, d), lambda i, j: (layer, j, 0)),
        ],
        out_specs=[
            pl.BlockSpec((tm, tn), lambda i, j: (i, jnp.minimum(j, n_z - 1))),
            pl.BlockSpec((tm, tn), lambda i, j: (i, jnp.maximum(j - n_z, 0))),
        ],
        out_shape=[jax.ShapeDtypeStruct((t, Z_COLS), F32), jax.ShapeDtypeStruct((t, n - Z_COLS), BF16)],
        scratch_shapes=[pltpu.VMEM((tm, d), BF16)],
        compiler_params=_cparams(("parallel", "arbitrary")),
        name="in_proj",
    )(x2d, g.reshape(1, d), w)


def _moba_kernel(q_ref, k_ref, v_ref, cosf_ref, sinf_ref, cosq_ref, sinq_ref, o_ref,
                 kr_ref, km_ref, vt_ref, *, n_blk, scale):
    qi = pl.program_id(2)
    tq = q_ref.shape[1]
    hp = q_ref.shape[2] // LANES
    head = lambda hh: slice(hh * LANES, (hh + 1) * LANES)

    @pl.when(qi == 0)
    def _():
        for hh in range(hp):
            kr = _rope_full(k_ref[0, :, head(hh)], cosf_ref[...], sinf_ref[...])
            kr_ref[hh] = kr.astype(BF16)
            km_ref[hh] = jnp.concatenate(
                [jnp.mean(kr[j * MOBA_BLOCK:(j + 1) * MOBA_BLOCK], axis=0, keepdims=True) for j in range(n_blk)],
                axis=0)
            vt_ref[hh] = v_ref[0, :, head(hh)].T.astype(BF16)

    blk = MOBA_BLOCK
    bpt = tq // blk
    lane = lax.broadcasted_iota(jnp.int32, (1, tq), 1)
    lane_sub = lane // blk
    lane_loc = lane % blk

    for n in range(n_blk // bpt):
        @pl.when(qi == n)
        def _(n=n):
            first = n * bpt
            ranked = first + bpt - 1 > MOBA_TOPK
            for hh in range(hp):
                qr = _rope_full(q_ref[0, :, head(hh)], cosq_ref[...], sinq_ref[...])
                q_bf = (qr * (scale * LOG2E)).astype(BF16)
                s = _dot_nt(kr_ref[hh, 0:(n + 1) * tq], q_bf)
                drop = None
                if ranked:
                    gate = _dot3_nt(km_ref[hh], qr)
                    past = lax.broadcasted_iota(jnp.int32, (n_blk, tq), 0) < first + lane_sub
                    gate = jnp.where(past, gate, -jnp.inf)
                    drop = jnp.where(_rank_t(gate, first + bpt - 1) < MOBA_TOPK, 0.0, NEG_BIG)
                parts = []
                if drop is not None:
                    for j in range(first):
                        parts.append((s[j * blk:(j + 1) * blk], None, drop[j:j + 1, :]))
                elif first > 0:
                    parts.append((s[0:first * blk], None, None))
                for i2 in range(bpt):
                    j = first + i2
                    pen = None
                    if drop is not None and i2 < bpt - 1:
                        pen = jnp.where(lane_sub > i2, drop[j:j + 1, :], 0.0)

                    limit = jnp.where(lane_sub > i2, blk, jnp.where(lane_sub == i2, lane_loc, -1))

                    def mask_fn(a, rows, limit=limit):
                        return lax.broadcasted_iota(jnp.int32, (rows, tq), 0) + a <= limit

                    parts.append((s[j * blk:(j + 1) * blk], mask_fn, pen))
                e, l = _softmax_parts(parts)
                o_t = _dot_nn(vt_ref[hh, :, 0:(n + 1) * tq], e) / l
                o_ref[0, :, head(hh)] = o_t.T.astype(o_ref.dtype)


def _moba(z3, cos, sin_s, tq=2 * MOBA_BLOCK, hp=MOBA_HEADS):
    b, s, _ = z3.shape
    n_blk = s // MOBA_BLOCK
    w = hp * LANES
    qb, kb, vb = OFF_AQ // w, OFF_AK // w, OFF_AV // w
    return pl.pallas_call(
        functools.partial(_moba_kernel, n_blk=n_blk, scale=HEAD_DIM ** -0.5),
        grid=(b, MOBA_HEADS // hp, s // tq),
        in_specs=[
            pl.BlockSpec((1, tq, w), lambda bi, h, qi: (bi, qi, qb + h)),
            pl.BlockSpec((1, s, w), lambda bi, h, qi: (bi, 0, kb + h)),
            pl.BlockSpec((1, s, w), lambda bi, h, qi: (bi, 0, vb + h)),
            pl.BlockSpec((s, LANES), lambda bi, h, qi: (0, 0)),
            pl.BlockSpec((s, LANES), lambda bi, h, qi: (0, 0)),
            pl.BlockSpec((tq, LANES), lambda bi, h, qi: (qi, 0)),
            pl.BlockSpec((tq, LANES), lambda bi, h, qi: (qi, 0)),
        ],
        out_specs=pl.BlockSpec((1, tq, w), lambda bi, h, qi: (bi, qi, h)),
        out_shape=jax.ShapeDtypeStruct((b, s, MOBA_HEADS * HEAD_DIM), BF16),
        scratch_shapes=[pltpu.VMEM((hp, s, LANES), BF16), pltpu.VMEM((hp, n_blk, LANES), F32),
                        pltpu.VMEM((hp, LANES, s), BF16)],
        compiler_params=_cparams(("parallel", "parallel", "arbitrary")),
        name="moba",
    )(z3, z3, z3, cos, sin_s, cos, sin_s)


def _mla_kernel(cq_ref, ckv_ref, kr_ref, qg_ref, kvg_ref, wqn_ref, wqr_ref, wkn_ref, wv_ref,
                cosf_ref, sinf_ref, cosq_ref, sinq_ref, o_ref, kf_ref, vt_ref, *, scale):
    qi = pl.program_id(2)
    tq = cq_ref.shape[1]
    s_len = ckv_ref.shape[1]
    hp = wqn_ref.shape[1] // LANES
    head = lambda hh: slice(hh * LANES, (hh + 1) * LANES)

    @pl.when(qi == 0)
    def _():
        ckv = _rmsnorm(ckv_ref[0], kvg_ref[...]).astype(BF16)
        k_nope = _dot_nn(ckv, wkn_ref[...])
        v = _dot_nn(ckv, wv_ref[...])
        k_rope = _rope_half(kr_ref[0], cosf_ref[...], sinf_ref[...]).astype(BF16)
        for hh in range(hp):
            kf_ref[hh, :, :LANES] = k_nope[:, head(hh)].astype(BF16)
            kf_ref[hh, :, LANES:] = k_rope
            vt_ref[hh] = v[:, head(hh)].T.astype(BF16)

    cq = _rmsnorm(cq_ref[0], qg_ref[...]).astype(BF16)
    q_nope = _dot_nn(cq, wqn_ref[...])
    q_rope = _dot_nn(cq, wqr_ref[...])
    def causal(a, n):
        return (lax.broadcasted_iota(jnp.int32, (n, tq), 0) + a
                <= lax.broadcasted_iota(jnp.int32, (n, tq), 1))

    for n in range(s_len // tq):
        @pl.when(qi == n)
        def _(n=n):
            for hh in range(hp):
                q = jnp.concatenate([q_nope[:, head(hh)],
                                     _rope_half(q_rope[:, head(hh)], cosq_ref[...], sinq_ref[...])], axis=1)
                q_bf = (q * (scale * LOG2E)).astype(BF16)
                s = _dot_nt(kf_ref[hh, 0:(n + 1) * tq], q_bf)
                parts = [(s[0:n * tq], None, None)] if n > 0 else []
                parts.append((s[n * tq:], causal, None))
                e, l = _softmax_parts(parts)
                o_t = _dot_nn(vt_ref[hh, :, 0:(n + 1) * tq], e) / l
                o_ref[0, :, head(hh)] = o_t.T.astype(o_ref.dtype)


def _mla(z3, q_gain, kv_gain, w_uq, w_ukv, layer, cos, sin_s, tq=512, hp=MLA_HEADS):
    b, s, _ = z3.shape
    w = hp * LANES
    ng = MLA_HEADS // hp
    return pl.pallas_call(
        functools.partial(_mla_kernel, scale=(MLA_NOPE + MLA_ROPE) ** -0.5),
        grid=(b, ng, s // tq),
        in_specs=[
            pl.BlockSpec((1, tq, MLA_Q_RANK), lambda bi, h, qi: (bi, qi, OFF_BCQ // MLA_Q_RANK)),
            pl.BlockSpec((1, s, MLA_KV_RANK), lambda bi, h, qi: (bi, 0, OFF_BCKV // MLA_KV_RANK)),
            pl.BlockSpec((1, s, LANES), lambda bi, h, qi: (bi, 0, OFF_BKR // LANES)),
            pl.BlockSpec((1, MLA_Q_RANK), lambda bi, h, qi: (0, 0)),
            pl.BlockSpec((1, MLA_KV_RANK), lambda bi, h, qi: (0, 0)),
            pl.BlockSpec((None, MLA_Q_RANK, w), lambda bi, h, qi: (layer, 0, h)),
            pl.BlockSpec((None, MLA_Q_RANK, w), lambda bi, h, qi: (layer, 0, ng + h)),
            pl.BlockSpec((None, MLA_KV_RANK, w), lambda bi, h, qi: (layer, 0, h)),
            pl.BlockSpec((None, MLA_KV_RANK, w), lambda bi, h, qi: (layer, 0, ng + h)),
            pl.BlockSpec((s, LANES), lambda bi, h, qi: (0, 0)),
            pl.BlockSpec((s, LANES), lambda bi, h, qi: (0, 0)),
            pl.BlockSpec((tq, LANES), lambda bi, h, qi: (qi, 0)),
            pl.BlockSpec((tq, LANES), lambda bi, h, qi: (qi, 0)),
        ],
        out_specs=pl.BlockSpec((1, tq, w), lambda bi, h, qi: (bi, qi, h)),
        out_shape=jax.ShapeDtypeStruct((b, s, MLA_HEADS * MLA_V), BF16),
        scratch_shapes=[pltpu.VMEM((hp, s, 2 * LANES), BF16), pltpu.VMEM((hp, LANES, s), BF16)],
        compiler_params=_cparams(("parallel", "parallel", "arbitrary")),
        name="mla",
    )(z3, z3, z3, q_gain.reshape(1, -1), kv_gain.reshape(1, -1), w_uq, w_uq, w_ukv, w_ukv,
      cos, sin_s, cos, sin_s)


def _dot3_presplit(a, b_hi, b_lo):
    ah, al = _split(a)
    return _dot_nn(ah, b_hi) + _dot_nn(ah, b_lo) + _dot_nn(al, b_hi)


def _cmp_kernel(*refs):
    n_grp = len(refs) - 9
    x_refs = refs[:n_grp]
    w1_ref, w2_ref, pos_ref, cos_ref, sin_ref, o_ref, w1_hi_ref, w1_lo_ref, bias_ref = refs[n_grp:]
    kv = pl.program_id(0)
    n16 = o_ref.shape[3]
    half = NSA_CMP_STRIDE * HEAD_DIM

    @pl.when(pl.program_id(1) == 0)
    def _():
        hi, lo = _split(w1_ref[0])
        w1_hi_ref[...] = hi
        w1_lo_ref[...] = lo
        bias_ref[...] = _dot3_presplit(jnp.broadcast_to(pos_ref[0], (SUBLANES, 2 * half)), hi, lo)

    for gi, x_ref in enumerate(x_refs):
        x = jnp.concatenate(
            [x_ref[0, pl.ds(l, n16, stride=NSA_CMP_STRIDE), :] for l in range(NSA_CMP_STRIDE)], axis=1)
        first = _dot3_presplit(x, w1_hi_ref[:half], w1_lo_ref[:half])
        second = _dot3_presplit(x, w1_hi_ref[half:], w1_lo_ref[half:])
        hid = first + pltpu.roll(second, n16 - 1, axis=0) + bias_ref[0:1]
        y = _dot3_nn(jax.nn.gelu(hid), w2_ref[0])
        roped = _rope_full(y, cos_ref[...], sin_ref[...])
        o_ref[0, 0, gi] = jnp.where(kv == 0, roped, y)


def _nsa_compress(z3, w1, w2, posflat, cos_c, sin_c):
    b, s, _ = z3.shape
    g = NSA_KV_GROUPS
    n16 = s // NSA_CMP_STRIDE
    width = NSA_CMP_STRIDE * HEAD_DIM
    hid = w1.shape[2]
    base = OFF_CKC // LANES
    x_specs = [pl.BlockSpec((1, s, LANES), lambda kv, bi, gi=gi: (bi, 0, base + g * kv + gi)) for gi in range(g)]
    return pl.pallas_call(
        _cmp_kernel,
        grid=(2, b),
        in_specs=x_specs + [
            pl.BlockSpec((1, 2 * width, hid), lambda kv, bi: (kv, 0, 0)),
            pl.BlockSpec((1, hid, HEAD_DIM), lambda kv, bi: (kv, 0, 0)),
            pl.BlockSpec((1, 1, 2 * width), lambda kv, bi: (kv, 0, 0)),
            pl.BlockSpec((n16, HEAD_DIM), lambda kv, bi: (0, 0)),
            pl.BlockSpec((n16, HEAD_DIM), lambda kv, bi: (0, 0)),
        ],
        out_specs=pl.BlockSpec((1, 1, g, n16, HEAD_DIM), lambda kv, bi: (kv, bi, 0, 0, 0)),
        out_shape=jax.ShapeDtypeStruct((2, b, g, n16, HEAD_DIM), F32),
        scratch_shapes=[pltpu.VMEM((2 * width, hid), BF16), pltpu.VMEM((2 * width, hid), BF16),
                        pltpu.VMEM((SUBLANES, hid), F32)],
        compiler_params=_cparams(("parallel", "arbitrary")),
        name="nsa_compress",
    )(*([z3] * g), w1, w2, posflat, cos_c, sin_c)


def _nsa_kernel(q_ref, ks_ref, vs_ref, kw_ref, vw_ref, kc_ref, vc_ref, gate_ref,
                cosf_ref, sinf_ref, cosq_ref, sinq_ref, ovl_ref, o_ref,
                ksr_ref, vst_ref, kw3_ref, vwt3_ref, vct_ref, os_ref, *, n_sel, scale):
    qi = pl.program_id(1)
    tq = q_ref.shape[1]
    s_len = ks_ref.shape[1]
    n_grp = ks_ref.shape[2] // HEAD_DIM
    rep = NSA_REP
    lanes = rep * tq
    n_chunk = s_len // tq
    pad_chunks = NSA_WINDOW // tq
    n_cmp = kc_ref.shape[3]
    grp = lambda gg: slice(gg * HEAD_DIM, (gg + 1) * HEAD_DIM)
    groups = range(n_grp)

    @pl.when(qi == 0)
    def _():
        for gg in groups:
            ksr_ref[gg] = _rope_full(ks_ref[0, :, grp(gg)], cosf_ref[...], sinf_ref[...]).astype(BF16)
            vst_ref[gg] = vs_ref[0, :, grp(gg)].T.astype(BF16)
            kw3_ref[gg, 0:pad_chunks] = jnp.zeros((pad_chunks, tq, HEAD_DIM), BF16)
            vwt3_ref[gg, 0:pad_chunks] = jnp.zeros((pad_chunks, HEAD_DIM, tq), BF16)
            kwr = _rope_full(kw_ref[0, :, grp(gg)], cosf_ref[...], sinf_ref[...]).astype(BF16)
            for c in range(n_chunk):
                kw3_ref[gg, pad_chunks + c] = kwr[c * tq:(c + 1) * tq]
                vwt3_ref[gg, pad_chunks + c] = vw_ref[0, c * tq:(c + 1) * tq, grp(gg)].T.astype(BF16)
            vct_ref[gg] = vc_ref[0, 0, gg].T.astype(BF16)

    cos_q = cosq_ref[...]
    sin_q = sinq_ref[...]
    lane = lax.broadcasted_iota(jnp.int32, (1, lanes), 1)
    t_loc = lane % tq
    t = qi * tq + t_loc
    cur = t[:, 0:tq] // NSA_SEL_BLOCK
    blk = lax.broadcasted_iota(jnp.int32, (n_sel, tq), 0)
    forced = (blk == 0) | (blk == cur) | (blk == cur - 1)
    visible = blk <= cur
    cmp_end = (lax.broadcasted_iota(jnp.int32, (n_cmp, lanes), 0) * NSA_CMP_STRIDE + (NSA_CMP_LEN - 1))
    ovl = ovl_ref[...]

    q_bf, o_c, imp = [], [], []
    for gg in groups:
        q_all = jnp.concatenate(
            [_rope_full(q_ref[0, :, grp(gg * rep + r)], cos_q, sin_q) for r in range(rep)],
            axis=0)
        q_bf.append((q_all * (scale * LOG2E)).astype(BF16))

        s_c = _dot3_nt(kc_ref[0, 0, gg], q_all)
        e_c, l_c = _softmax_t(s_c, cmp_end <= t, scale)
        p_c = e_c / l_c
        o_c.append(_dot_nn(vct_ref[gg], p_c.astype(BF16)))

        p_hi, p_lo = _split(p_c)
        imp3 = _dot_nn(ovl, p_hi) + _dot_nn(ovl, p_lo)
        imp_g = imp3[:, 0:tq]
        for r in range(1, rep):
            imp_g = imp_g + imp3[:, r * tq:(r + 1) * tq]
        imp_g = jnp.where(forced, FORCE_SCORE, imp_g)
        imp.append(jnp.where(visible, imp_g, -jnp.inf))

    unit = tq
    sb = NSA_SEL_BLOCK
    for v in range(s_len // unit):
        @pl.when(qi == v)
        def _(v=v):
            n_keys = (v + 1) * unit
            lo = n_keys - unit

            def causal(base, a, n):
                return lax.broadcasted_iota(jnp.int32, (n, lanes), 0) <= (t - (base + a))

            n_vis = n_keys // sb
            for gg in groups:
                s = _dot_nt(ksr_ref[gg, 0:n_keys], q_bf[gg])
                if n_vis > NSA_SEL_TOPK:
                    rank = _rank_t(imp[gg], n_vis)
                    rank3 = jnp.concatenate([rank] * rep, axis=1)
                    drop = jnp.where(rank3 < NSA_SEL_TOPK, 0.0, NEG_BIG)
                    parts = []
                    for j in range(n_vis):
                        parts.append((s[j * sb:(j + 1) * sb],
                                      functools.partial(causal, j * sb) if j * sb >= lo else None,
                                      drop[j:j + 1, :]))
                else:
                    parts = [(s[0:lo], None, None)] if lo > 0 else []
                    parts.append((s[lo:], functools.partial(causal, lo), None))
                e_s, l_s = _softmax_parts(parts)
                os_ref[gg] = _dot_nn(vst_ref[gg, :, 0:n_keys], e_s) / l_s

    n_win = pad_chunks + 1
    def oldest(a, n):
        return lax.broadcasted_iota(jnp.int32, (n, lanes), 0) + a > t_loc

    def newest(a, n):
        return lax.broadcasted_iota(jnp.int32, (n, lanes), 0) + a <= t_loc

    for gg in groups:
        parts = []
        for c in range(n_win):
            s_w = _dot_nt(kw3_ref[gg, qi + c], q_bf[gg])
            pen = None
            if c < pad_chunks:
                pen = jnp.full((1, lanes), jnp.where(qi + c >= pad_chunks, 0.0, NEG_BIG), F32)
            parts.append((s_w, oldest if c == 0 else (newest if c == n_win - 1 else None), pen))
        e_w, l_w = _softmax_parts(parts)
        o_w = _dot_nn(vwt3_ref[gg, qi], e_w[0:tq])
        for c in range(1, n_win):
            o_w = o_w + _dot_nn(vwt3_ref[gg, qi + c], e_w[c * tq:(c + 1) * tq])
        o_w = o_w / l_w

        g_t = _sigmoid(gate_ref[0, :, grp(gg)].T)

        def gate_row(branch, g_t=g_t):
            return jnp.concatenate([g_t[3 * r + branch:3 * r + branch + 1, :] for r in range(rep)], axis=1)

        out_t = gate_row(0) * o_c[gg] + gate_row(1) * os_ref[gg] + gate_row(2) * o_w
        for r in range(rep):
            o_ref[0, :, grp(gg * rep + r)] = out_t[:, r * tq:(r + 1) * tq].T.astype(o_ref.dtype)


def _nsa(z3, cmp_kv, cos, sin_s, ovl_t, tq=256):
    b, s, _ = z3.shape
    g = NSA_KV_GROUPS
    n_sel = s // NSA_SEL_BLOCK
    n_cmp = cmp_kv.shape[3]
    qw = NSA_HEADS * HEAD_DIM
    kvw = g * HEAD_DIM
    n_chunk_pad = (s + NSA_WINDOW) // tq

    def col(off):
        return lambda bi, qi: (bi, 0, off // kvw)

    return pl.pallas_call(
        functools.partial(_nsa_kernel, n_sel=n_sel, scale=HEAD_DIM ** -0.5),
        grid=(b, s // tq),
        in_specs=[
            pl.BlockSpec((1, tq, qw), lambda bi, qi: (bi, qi, OFF_CQ // qw)),
            pl.BlockSpec((1, s, kvw), col(OFF_CKS)),
            pl.BlockSpec((1, s, kvw), col(OFF_CVS)),
            pl.BlockSpec((1, s, kvw), col(OFF_CKW)),
            pl.BlockSpec((1, s, kvw), col(OFF_CVW)),
            pl.BlockSpec((1, 1, g, n_cmp, HEAD_DIM), lambda bi, qi: (0, bi, 0, 0, 0)),
            pl.BlockSpec((1, 1, g, n_cmp, HEAD_DIM), lambda bi, qi: (1, bi, 0, 0, 0)),
            pl.BlockSpec((1, tq, g * LANES), lambda bi, qi: (bi, qi, OFF_CGATE // (g * LANES))),
            pl.BlockSpec((s, LANES), lambda bi, qi: (0, 0)),
            pl.BlockSpec((s, LANES), lambda bi, qi: (0, 0)),
            pl.BlockSpec((tq, LANES), lambda bi, qi: (qi, 0)),
            pl.BlockSpec((tq, LANES), lambda bi, qi: (qi, 0)),
            pl.BlockSpec((n_sel, n_cmp), lambda bi, qi: (0, 0)),
        ],
        out_specs=pl.BlockSpec((1, tq, qw), lambda bi, qi: (bi, qi, 0)),
        out_shape=jax.ShapeDtypeStruct((b, s, qw), BF16),
        scratch_shapes=[
            pltpu.VMEM((g, s, HEAD_DIM), BF16),
            pltpu.VMEM((g, HEAD_DIM, s), BF16),
            pltpu.VMEM((g, n_chunk_pad, tq, HEAD_DIM), BF16),
            pltpu.VMEM((g, n_chunk_pad, HEAD_DIM, tq), BF16),
            pltpu.VMEM((g, HEAD_DIM, n_cmp), BF16),
            pltpu.VMEM((g, HEAD_DIM, NSA_REP * tq), F32),
        ],
        compiler_params=_cparams(("parallel", "arbitrary")),
        name="nsa",
    )(z3, z3, z3, z3, z3, cmp_kv, cmp_kv, z3, cos, sin_s, cos, sin_s, ovl_t)


def _merge_kernel(ya_ref, yb_ref, yc_ref, ga_ref, gb_ref, gc_ref, wa_ref, wb_ref, wc_ref, o_ref,
                  wa_bf_ref, wb_bf_ref, wc_bf_ref):
    @pl.when(pl.program_id(1) == 0)
    def _():
        wa_bf_ref[...] = wa_ref[...].astype(BF16)
        wb_bf_ref[...] = wb_ref[...].astype(BF16)
        wc_bf_ref[...] = wc_ref[...].astype(BF16)

    m = ga_ref[...].astype(F32) * _dot_nn(ya_ref[...], wa_bf_ref[...])
    m = m + gb_ref[...].astype(F32) * _dot_nn(yb_ref[...], wb_bf_ref[...])
    m = m + gc_ref[...].astype(F32) * _dot_nn(yc_ref[...], wc_bf_ref[...])
    o_ref[...] = m.astype(o_ref.dtype)


def _merge(ya, yb, yc, gates, wa, wb, wc, layer, tm=1024, tn=1024):
    t = ya.shape[0]
    d = wa.shape[2]
    nj = d // tn

    def gate_spec(branch):
        return pl.BlockSpec((tm, tn), lambda j, i: (i, branch * nj + j))

    def y_spec(y):
        return pl.BlockSpec((tm, y.shape[1]), lambda j, i: (i, 0))

    def w_spec(w):
        return pl.BlockSpec((None, w.shape[1], tn), lambda j, i: (layer, 0, j))

    return pl.pallas_call(
        _merge_kernel,
        grid=(nj, t // tm),
        in_specs=[y_spec(ya), y_spec(yb), y_spec(yc), gate_spec(0), gate_spec(1), gate_spec(2),
                  w_spec(wa), w_spec(wb), w_spec(wc)],
        out_specs=pl.BlockSpec((tm, tn), lambda j, i: (i, j)),
        out_shape=jax.ShapeDtypeStruct((t, d), BF16),
        scratch_shapes=[pltpu.VMEM((w.shape[1], tn), BF16) for w in (wa, wb, wc)],
        compiler_params=_cparams(("parallel", "arbitrary")),
        name="merge",
    )(ya, yb, yc, gates, gates, gates, wa, wb, wc)


def _proj_residual_kernel(m_ref, w_ref, x_ref, g_ref, o_ref, h_ref, w_bf_ref):
    @pl.when(pl.program_id(0) == 0)
    def _():
        w_bf_ref[...] = w_ref[...].astype(BF16)

    y = x_ref[...] + _dot_nn(m_ref[...], w_bf_ref[...])
    o_ref[...] = y
    h_ref[...] = _rmsnorm(y, g_ref[...]).astype(h_ref.dtype)


def _proj_residual(m, w, x2d, layer, next_gain, tm=512):
    t, k = m.shape
    d = w.shape[2]
    return pl.pallas_call(
        _proj_residual_kernel,
        grid=(t // tm,),
        in_specs=[
            pl.BlockSpec((tm, k), lambda i: (i, 0)),
            pl.BlockSpec((None, k, d), lambda i: (layer, 0, 0), pipeline_mode=pl.Buffered(1)),
            pl.BlockSpec((tm, d), lambda i: (i, 0)),
            pl.BlockSpec((1, d), lambda i: (0, 0)),
        ],
        out_specs=[pl.BlockSpec((tm, d), lambda i: (i, 0)), pl.BlockSpec((tm, d), lambda i: (i, 0))],
        out_shape=[jax.ShapeDtypeStruct((t, d), F32), jax.ShapeDtypeStruct((t, d), BF16)],
        scratch_shapes=[pltpu.VMEM((k, d), BF16)],
        compiler_params=_cparams(("arbitrary",)),
        name="out_proj",
    )(m, w, x2d, next_gain.reshape(1, d))


def _cast_kernel(w_ref, o_ref):
    o_ref[...] = w_ref[...].astype(o_ref.dtype)


def _cast_bf16(w, layer, steps=8):
    rows, cols = w.shape[1:]
    slab = rows // steps
    assert slab * steps == rows and slab % (2 * SUBLANES) == 0
    return pl.pallas_call(
        _cast_kernel,
        grid=(steps,),
        in_specs=[pl.BlockSpec((None, slab, cols), lambda i: (layer, i, 0))],
        out_specs=pl.BlockSpec((slab, cols), lambda i: (i, 0)),
        out_shape=jax.ShapeDtypeStruct((rows, cols), BF16),
        compiler_params=_cparams(("parallel",)),
        name="cast_bf16",
    )(w)


def _rope_tables(pos, dim, width):
    inv = 1.0 / (ROPE_THETA ** (jnp.arange(0, dim, 2, dtype=F32) / dim))
    ang = pos.astype(F32)[:, None] * inv[None, :]
    cos = jnp.cos(ang)
    sin = jnp.sin(ang)
    cos = jnp.concatenate([cos, cos], axis=-1)
    sin_s = jnp.concatenate([-sin, sin], axis=-1)
    pad = ((0, 0), (0, width - dim))
    return jnp.pad(cos, pad), jnp.pad(sin_s, pad)


def _repack_kernel(srcp_ref, srcq_ref, start_ref, valid_ref, p_ref, q_ref, o_ref, *, n_src_cols):
    j = pl.program_id(1)
    start = start_ref[j]
    valid = valid_ref[j]
    row = lax.broadcasted_iota(jnp.int32, (LANES, LANES), 0)
    col = lax.broadcasted_iota(jnp.int32, (LANES, LANES), 1)
    lane = lax.broadcasted_iota(jnp.int32, (1, LANES), 1)

    def contribution(blk_ref, src_blk):
        take = jnp.where((src_blk * LANES + row == start + col) & (col < valid), 1.0, 0.0).astype(BF16)
        blk = jnp.where(src_blk * LANES + lane < n_src_cols, blk_ref[...], 0.0).astype(BF16)
        return _dot_nn(blk, take)

    o_ref[...] = (contribution(p_ref, srcp_ref[j]) + contribution(q_ref, srcq_ref[j])).astype(o_ref.dtype)


def _repack_tables(pieces, n_out_blocks, n_src_blocks):
    starts, valids = [], []
    for start, width in pieces:
        if width == 0:
            starts.append(0)
            valids.append(0)
        for off in range(0, width, LANES):
            starts.append(start + off)
            valids.append(min(LANES, width - off))
    starts += [0] * (n_out_blocks - len(starts))
    valids += [0] * (n_out_blocks - len(valids))
    assert n_src_blocks >= 2
    slots = [0, 1]
    srcp, srcq = [], []
    for start, valid in zip(starts, valids):
        need = []
        if valid > 0:
            need = list(range(start // LANES, (start + valid - 1) // LANES + 1))
        for blk in need:
            if blk not in slots:
                slots[0 if slots[0] not in need else 1] = blk
        srcp.append(slots[0])
        srcq.append(slots[1])
    return tuple(jnp.asarray(np.asarray(v, np.int32)) for v in (srcp, srcq, starts, valids))


def _repack(w, pieces, n_out_blocks):
    depth, k, n = w.shape
    tables = _repack_tables(pieces, n_out_blocks, -(-n // LANES))
    grid_spec = pltpu.PrefetchScalarGridSpec(
        num_scalar_prefetch=4,
        grid=(depth, n_out_blocks),
        in_specs=[
            pl.BlockSpec((None, k, LANES), lambda d, j, sp, sq, st, va: (d, 0, sp[j])),
            pl.BlockSpec((None, k, LANES), lambda d, j, sp, sq, st, va: (d, 0, sq[j])),
        ],
        out_specs=pl.BlockSpec((None, k, LANES), lambda d, j, sp, sq, st, va: (d, 0, j)),
    )
    return pl.pallas_call(
        functools.partial(_repack_kernel, n_src_cols=n),
        grid_spec=grid_spec,
        out_shape=jax.ShapeDtypeStruct((depth, k, n_out_blocks * LANES), BF16),
        compiler_params=_cparams(("parallel", "arbitrary")),
        name="repack",
    )(*tables, w, w)


def _repack_t_kernel(srcp_ref, srcq_ref, start_ref, valid_ref, p_ref, q_ref, o_ref, *, n_src_cols):
    j = pl.program_id(0)
    start = start_ref[j]
    valid = valid_ref[j]
    depth = o_ref.shape[0]
    rows = LANES * depth
    out_row = lax.broadcasted_iota(jnp.int32, (LANES, rows), 0)
    flat = lax.broadcasted_iota(jnp.int32, (LANES, rows), 1)
    sub = lax.broadcasted_iota(jnp.int32, (rows, 1), 0)

    def flattened(blk_ref, src_blk):
        x = blk_ref[...].reshape(rows, blk_ref.shape[2])
        return jnp.where(src_blk * LANES + sub // depth < n_src_cols, x, 0.0).astype(BF16)

    def take(src_blk, layer):
        hit = (flat == (start + out_row - src_blk * LANES) * depth + layer) & (out_row < valid)
        return jnp.where(hit, 1.0, 0.0).astype(BF16)

    xp, xq = flattened(p_ref, srcp_ref[j]), flattened(q_ref, srcq_ref[j])
    for layer in range(depth):
        o_ref[layer] = (_dot_nn(take(srcp_ref[j], layer), xp)
                        + _dot_nn(take(srcq_ref[j], layer), xq)).astype(o_ref.dtype)


def _repack_t(wt, pieces, n_out_blocks):
    n, depth, k = wt.shape
    tables = _repack_tables(pieces, n_out_blocks, -(-n // LANES))
    grid_spec = pltpu.PrefetchScalarGridSpec(
        num_scalar_prefetch=4,
        grid=(n_out_blocks,),
        in_specs=[
            pl.BlockSpec((LANES, depth, k), lambda j, sp, sq, st, va: (sp[j], 0, 0)),
            pl.BlockSpec((LANES, depth, k), lambda j, sp, sq, st, va: (sq[j], 0, 0)),
        ],
        out_specs=pl.BlockSpec((depth, LANES, k), lambda j, sp, sq, st, va: (0, j, 0)),
    )
    return pl.pallas_call(
        functools.partial(_repack_t_kernel, n_src_cols=n),
        grid_spec=grid_spec,
        out_shape=jax.ShapeDtypeStruct((depth, n_out_blocks * LANES, k), BF16),
        compiler_params=_cparams(("arbitrary",)),
        name="repack_t",
    )(*tables, wt, wt)


def _pack_w_in(w):
    sizes = [MOBA_HEADS * HEAD_DIM] * 3 + [MLA_Q_RANK, MLA_KV_RANK, MLA_ROPE, NSA_HEADS * HEAD_DIM] \
        + [NSA_KV_GROUPS * HEAD_DIM] * 6
    offs = [int(v) for v in np.concatenate([[0], np.cumsum(sizes)])]
    (a_q, a_k, a_v, b_cq, b_ckv, b_kr, c_q, c_kc, c_vc, c_ks, c_vs, c_kw, c_vw) = [
        (offs[i], sizes[i]) for i in range(len(sizes))]
    gate0 = offs[-1]
    per_group = NSA_REP * 3
    gates = [(gate0 + gi * per_group, per_group) for gi in range(NSA_KV_GROUPS)]
    merge = (gate0 + NSA_HEADS * 3, w.shape[2] - gate0 - NSA_HEADS * 3)
    pieces = [c_q, b_ckv, b_cq, a_q, a_k, a_v, c_kc, c_vc, c_ks, c_vs, c_kw, c_vw] + gates + [b_kr]
    used = sum(-(-width // LANES) for _, width in pieces)
    pieces += [(0, 0)] * (Z_COLS // LANES - used) + [merge]
    return _repack_t(jnp.transpose(w, (2, 0, 1)), pieces, (Z_COLS + merge[1]) // LANES)


def _pack_w_uq(w):
    per_head = MLA_NOPE + MLA_ROPE
    pieces = [(h * per_head, MLA_NOPE) for h in range(MLA_HEADS)] \
        + [(h * per_head + MLA_NOPE, MLA_ROPE) for h in range(MLA_HEADS)]
    return _repack(w, pieces, 2 * MLA_HEADS)


def _pack_w_ukv(w):
    per_head = MLA_NOPE + MLA_V
    pieces = [(h * per_head, MLA_NOPE) for h in range(MLA_HEADS)] \
        + [(h * per_head + MLA_NOPE, MLA_V) for h in range(MLA_HEADS)]
    return _repack(w, pieces, 2 * MLA_HEADS)


def kernel(x, ffn1_norm, ffn1_w_gate, ffn1_w_up, ffn1_w_down, mix_norm, w_in, mla_q_norm, mla_w_uq, mla_kv_norm, mla_w_ukv, nsa_cmp_pos_k, nsa_cmp_w1_k, nsa_cmp_w2_k, nsa_cmp_pos_v, nsa_cmp_w1_v, nsa_cmp_w2_v, w_branch_moba, w_branch_mla, w_branch_nsa, w_out, ffn2_norm, ffn2_w_gate, ffn2_w_up, ffn2_w_down, final_norm):
    b, s, d = x.shape
    depth = w_in.shape[0]
    t = b * s
    assert NSA_CMP_LEN == 2 * NSA_CMP_STRIDE and s % MOBA_BLOCK == 0 and s % NSA_SEL_BLOCK == 0

    pos = jnp.arange(s)
    cos_h, sin_h = _rope_tables(pos, HEAD_DIM, LANES)
    cos_r, sin_r = _rope_tables(pos, MLA_ROPE, LANES)
    n16 = s // NSA_CMP_STRIDE
    cmp_end = np.arange(n16) * NSA_CMP_STRIDE + NSA_CMP_LEN - 1
    cos_c, sin_c = _rope_tables(jnp.asarray(cmp_end), HEAD_DIM, LANES)
    n_sel = s // NSA_SEL_BLOCK
    sel_start = np.arange(n_sel) * NSA_SEL_BLOCK
    cmp_start = np.arange(n16) * NSA_CMP_STRIDE
    ovl_t = ((cmp_start[None, :] <= sel_start[:, None] + NSA_SEL_BLOCK - 1)
             & (cmp_end[None, :] >= sel_start[:, None])).astype(np.float32)
    ovl_t = jnp.asarray(ovl_t, BF16)

    w_in_p = _pack_w_in(w_in)
    w_uq_p = _pack_w_uq(mla_w_uq)
    w_ukv_p = _pack_w_ukv(mla_w_ukv)

    xf = x.reshape(t, d)
    for i in range(depth):
        xf = _ffn(xf, ffn1_norm[i], ffn1_w_gate, ffn1_w_up, ffn1_w_down, i)

        z, gates = _in_proj(xf, mix_norm[i], w_in_p, i)
        z3 = z.reshape(b, s, Z_COLS)
        y_a = _moba(z3, cos_h, sin_h)
        y_b = _mla(z3, mla_q_norm[i], mla_kv_norm[i], w_uq_p, w_ukv_p, i, cos_r, sin_r)

        w1 =jnp.stack([nsa_cmp_w1_k[i], nsa_cmp_w1_v[i]])
        w2 = jnp.stack([nsa_cmp_w2_k[i], nsa_cmp_w2_v[i]])
        posflat = jnp.stack([nsa_cmp_pos_k[i].reshape(1, -1), nsa_cmp_pos_v[i].reshape(1, -1)])
        cmp_kv = _nsa_compress(z3, w1, w2, posflat, cos_c, sin_c)
        y_c = _nsa(z3, cmp_kv, cos_h, sin_h, ovl_t)

        merged = _merge(y_a.reshape(t, -1), y_b.reshape(t, -1), y_c.reshape(t, -1), gates,
                        w_branch_moba, w_branch_mla, w_branch_nsa, i)
        xf, h2 = _proj_residual(merged, w_out, xf, i, ffn2_norm[i])

        xf = _ffn(xf, ffn2_norm[i], ffn2_w_gate, ffn2_w_up, ffn2_w_down, i, h=h2)
    return _rms(xf, final_norm, F32).reshape(b, s, d)
```
